```python
import math
import jax, jax.numpy as jnp
from jax import lax
import numpy as np

D_MODEL = 1024
BATCH = 2
SEQ = 8192
DEPTH = 1

HEAD_DIM = 64
N_HEADS_MOBA = 8
N_HEADS_SWA = 8
N_KV_SWA = 2
N_HEADS = N_HEADS_MOBA + N_HEADS_SWA
MIX_WIDTH = N_HEADS * HEAD_DIM
W_MOBA = N_HEADS_MOBA * HEAD_DIM
W_SWA = N_HEADS_SWA * HEAD_DIM
W_SWA_KV = N_KV_SWA * HEAD_DIM
IN_COLS = 3 * W_MOBA + W_SWA + 2 * W_SWA_KV
IN_SPLITS = [W_MOBA, 2 * W_MOBA, 3 * W_MOBA, 3 * W_MOBA + W_SWA, 3 * W_MOBA + W_SWA + W_SWA_KV]
MOBA_BLOCK = 256
MOBA_TOPK = 3
MOBA_Q_CHUNK = 64
SWA_WINDOW = 128
SWA_BLOCK = 128
NUM_BUCKETS = 32
MAX_DISTANCE = 128
D_FF = 2816
RMS_EPS = 1e-6
FFN_RES_WEIGHT = 0.5
NEG = -1e30

kernel_name = "hymba_moba_swa_macaron_layer"


def rms_norm(x, g):
    xf = x.astype(jnp.float32)
    y = xf * lax.rsqrt(jnp.mean(xf * xf, axis=-1, keepdims=True) + RMS_EPS)
    return (y * g.astype(jnp.float32)).astype(x.dtype)


def swiglu(x, w_gate, w_up, w_down):
    return (jax.nn.silu(x @ w_gate) * (x @ w_up)) @ w_down


def t5_bucket(dist):
    n = jnp.maximum(dist, 0)
    max_exact = NUM_BUCKETS // 2
    nf = jnp.maximum(n, 1).astype(jnp.float32)
    large = max_exact + (jnp.log(nf / max_exact) / math.log(MAX_DISTANCE / max_exact)
                         * (NUM_BUCKETS - max_exact)).astype(jnp.int32)
    large = jnp.minimum(large, NUM_BUCKETS - 1)
    return jnp.where(n < max_exact, n, large)


def moba_attention(q, k, v, bias_tab):
    B, T, H, hd = q.shape
    nb = -(-T // MOBA_BLOCK)
    tp = nb * MOBA_BLOCK
    padw = ((0, 0), (0, tp - T), (0, 0), (0, 0))
    q, k, v = [jnp.pad(a, padw).transpose(0, 2, 1, 3) for a in (q, k, v)]
    kb = k.reshape(B, H, nb, MOBA_BLOCK, hd)
    vb = v.reshape(B, H, nb, MOBA_BLOCK, hd)
    scale = hd ** -0.5
    k_mean = jnp.mean(kb.astype(jnp.float32), axis=3)
    gate = jnp.einsum('bhtd,bhnd->bhtn', q.astype(jnp.float32), k_mean)
    q_blk = jnp.arange(tp) // MOBA_BLOCK
    past = jnp.arange(nb)[None, :] < q_blk[:, None]
    gate = jnp.where(past, gate, NEG)
    k_sel = min(MOBA_TOPK, nb)
    _, sel = lax.top_k(gate, k_sel)
    sel_valid = sel < q_blk[:, None]
    offs = jnp.arange(MOBA_BLOCK)
    h_idx = jnp.arange(H)[:, None, None, None]
    gather_blocks = jax.vmap(jax.vmap(lambda blocks, idx: blocks[idx]))

    def chunk(c):
        start = c * MOBA_Q_CHUNK
        qc = lax.dynamic_slice_in_dim(q, start, MOBA_Q_CHUNK, axis=2)
        sc = lax.dynamic_slice_in_dim(sel, start, MOBA_Q_CHUNK, axis=2)
        vc = lax.dynamic_slice_in_dim(sel_valid, start, MOBA_Q_CHUNK, axis=2)
        qpos = start + jnp.arange(MOBA_Q_CHUNK)
        k_g = gather_blocks(kb, sc)
        v_g = gather_blocks(vb, sc)
        s_sel = jnp.einsum('bhcd,bhcjsd->bhcjs', qc, k_g).astype(jnp.float32) * scale
        kpos = sc[..., None] * MOBA_BLOCK + offs
        bias_sel = bias_tab[h_idx, t5_bucket(qpos[:, None, None] - kpos)].astype(jnp.float32)
        s_sel = jnp.where(vc[..., None], s_sel + bias_sel, NEG)
        blk = start // MOBA_BLOCK
        k_own = lax.dynamic_index_in_dim(kb, blk, axis=2, keepdims=False)
        v_own = lax.dynamic_index_in_dim(vb, blk, axis=2, keepdims=False)
        s_own = jnp.einsum('bhcd,bhsd->bhcs', qc, k_own).astype(jnp.float32) * scale
        dist_own = qpos[:, None] - (blk * MOBA_BLOCK + offs)[None, :]
        bias_own = bias_tab[:, t5_bucket(dist_own)].astype(jnp.float32)
        s_own = jnp.where(dist_own >= 0, s_own + bias_own, NEG)
        logits = jnp.concatenate([s_sel.reshape(B, H, MOBA_Q_CHUNK, k_sel * MOBA_BLOCK), s_own], axis=-1)
        p = jax.nn.softmax(logits, axis=-1).astype(v.dtype)
        p_sel = p[..., :k_sel * MOBA_BLOCK].reshape(B, H, MOBA_Q_CHUNK, k_sel, MOBA_BLOCK)
        p_own = p[..., k_sel * MOBA_BLOCK:]
        return (jnp.einsum('bhcjs,bhcjsd->bhcd', p_sel, v_g)
                + jnp.einsum('bhcs,bhsd->bhcd', p_own, v_own))

    outs = lax.map(chunk, jnp.arange(tp // MOBA_Q_CHUNK))
    out = outs.transpose(1, 2, 0, 3, 4).reshape(B, H, tp, hd)[:, :, :T]
    return out.transpose(0, 2, 1, 3)


def swa_attention(q, k, v, sinks, bias_tab):
    B, T, Hq, hd = q.shape
    Hkv = k.shape[2]
    G = Hq // Hkv
    nb = T // SWA_BLOCK
    qb = q.reshape(B, nb, SWA_BLOCK, Hkv, G, hd)
    kb = k.reshape(B, nb, SWA_BLOCK, Hkv, hd)
    vb = v.reshape(B, nb, SWA_BLOCK, Hkv, hd)
    shift = ((0, 0), (1, 0), (0, 0), (0, 0), (0, 0))
    kw = jnp.concatenate([jnp.pad(kb, shift)[:, :-1], kb], axis=2)
    vw = jnp.concatenate([jnp.pad(vb, shift)[:, :-1], vb], axis=2)
    logits = jnp.einsum('bnqkgd,bnskd->bkgnqs', qb, kw).astype(jnp.float32) * (hd ** -0.5)
    qi = jnp.arange(SWA_BLOCK)
    ki = jnp.arange(2 * SWA_BLOCK) - SWA_BLOCK
    dist = qi[:, None] - ki[None, :]
    in_win = (dist >= 0) & (dist < SWA_WINDOW)
    kpos = jnp.arange(nb)[:, None] * SWA_BLOCK + ki[None, :]
    valid = in_win[None] & (kpos[:, None, :] >= 0)
    bias = bias_tab[:, t5_bucket(dist)].astype(jnp.float32).reshape(Hkv, G, 1, SWA_BLOCK, 2 * SWA_BLOCK)
    logits = jnp.where(valid, logits + bias, NEG)
    sink = jnp.broadcast_to(sinks.astype(jnp.float32).reshape(Hkv, G, 1, 1, 1),
                            (B, Hkv, G, nb, SWA_BLOCK, 1))
    p = jax.nn.softmax(jnp.concatenate([logits, sink], axis=-1), axis=-1)[..., :-1]
    out = jnp.einsum('bkgnqs,bnskd->bnqkgd', p.astype(v.dtype), vw)
    return out.reshape(B, T, Hq, hd)


def hybrid_mixer(h, w_in, rel_bias, moba_out_g, swa_sinks, swa_out_g, w_out):
    B, T, _ = h.shape
    proj = h @ w_in
    qa, ka, va, qs, ks, vs = jnp.split(proj, IN_SPLITS, axis=-1)
    bias_tab = rel_bias.T
    oa = moba_attention(qa.reshape(B, T, N_HEADS_MOBA, HEAD_DIM),
                        ka.reshape(B, T, N_HEADS_MOBA, HEAD_DIM),
                        va.reshape(B, T, N_HEADS_MOBA, HEAD_DIM),
                        bias_tab[:N_HEADS_MOBA])
    ob = swa_attention(qs.reshape(B, T, N_HEADS_SWA, HEAD_DIM),
                       ks.reshape(B, T, N_KV_SWA, HEAD_DIM),
                       vs.reshape(B, T, N_KV_SWA, HEAD_DIM),
                       swa_sinks, bias_tab[N_HEADS_MOBA:])
    oa = rms_norm(oa.reshape(B, T, W_MOBA), moba_out_g)
    ob = rms_norm(ob.reshape(B, T, W_SWA), swa_out_g)
    return jnp.concatenate([oa, ob], axis=-1) @ w_out


def setup_inputs(seed: int = 0) -> dict:
    key = jax.random.key(seed)
    ks = jax.random.split(key, 19)
    f32 = jnp.float32

    def gain(k, n):
        return 1.0 + 0.05 * jax.random.normal(k, (DEPTH, n), f32)

    def dense(k, fan_in, fan_out):
        return jax.random.normal(k, (DEPTH, fan_in, fan_out), f32) * fan_in ** -0.5

    return {
        "x": jax.random.normal(ks[0], (BATCH, SEQ, D_MODEL), f32),
        "ffn1_pre_g": gain(ks[1], D_MODEL),
        "ffn1_w_gate": dense(ks[2], D_MODEL, D_FF),
        "ffn1_w_up": dense(ks[3], D_MODEL, D_FF),
        "ffn1_w_down": dense(ks[4], D_FF, D_MODEL),
        "ffn1_post_g": gain(ks[5], D_MODEL),
        "mix_pre_g": gain(ks[6], D_MODEL),
        "w_in": dense(ks[7], D_MODEL, IN_COLS),
        "rel_bias": 0.5 * jax.random.normal(ks[8], (NUM_BUCKETS, N_HEADS), f32),
        "moba_out_g": gain(ks[9], W_MOBA),
        "swa_sinks": 0.5 * jax.random.normal(ks[10], (DEPTH, N_HEADS_SWA), f32),
        "swa_out_g": gain(ks[11], W_SWA),
        "w_out": dense(ks[12], MIX_WIDTH, D_MODEL),
        "mix_post_g": gain(ks[13], D_MODEL),
        "ffn2_pre_g": gain(ks[14], D_MODEL),
        "ffn2_w_gate": dense(ks[15], D_MODEL, D_FF),
        "ffn2_w_up": dense(ks[16], D_MODEL, D_FF),
        "ffn2_w_down": dense(ks[17], D_FF, D_MODEL),
        "ffn2_post_g": gain(ks[18], D_MODEL),
    }


def reference(x, ffn1_pre_g, ffn1_w_gate, ffn1_w_up, ffn1_w_down, ffn1_post_g,
              mix_pre_g, w_in, rel_bias, moba_out_g, swa_sinks, swa_out_g, w_out, mix_post_g,
              ffn2_pre_g, ffn2_w_gate, ffn2_w_up, ffn2_w_down, ffn2_post_g):
    for l in range(DEPTH):
        h = swiglu(rms_norm(x, ffn1_pre_g[l]), ffn1_w_gate[l], ffn1_w_up[l], ffn1_w_down[l])
        x = x + FFN_RES_WEIGHT * rms_norm(h, ffn1_post_g[l])
        h = hybrid_mixer(rms_norm(x, mix_pre_g[l]), w_in[l], rel_bias, moba_out_g[l],
                         swa_sinks[l], swa_out_g[l], w_out[l])
        x = x + rms_norm(h, mix_post_g[l])
        h = swiglu(rms_norm(x, ffn2_pre_g[l]), ffn2_w_gate[l], ffn2_w_up[l], ffn2_w_down[l])
        x = x + FFN_RES_WEIGHT * rms_norm(h, ffn2_post_g[l])
    return x
```

```python
import functools
import math

import jax
import jax.numpy as jnp
from jax import lax
from jax.experimental import pallas as pl
from jax.experimental.pallas import tpu as pltpu

D_MODEL = 1024
HEAD_DIM = 64
N_HEADS_MOBA = 8
N_HEADS_SWA = 8
N_KV_SWA = 2
SWA_GROUP = N_HEADS_SWA // N_KV_SWA
W_MOBA = N_HEADS_MOBA * HEAD_DIM
W_SWA = N_HEADS_SWA * HEAD_DIM
W_SWA_KV = N_KV_SWA * HEAD_DIM
MOBA_BLOCK = 256
MOBA_TOPK = 3
SWA_BLOCK = 128
SWA_WINDOW = 128
NUM_BUCKETS = 32
MAX_DISTANCE = 128
D_FF = 2816
RMS_EPS = 1e-6
FFN_RES_WEIGHT = 0.5
NEG = -1e30
QK_SCALE = HEAD_DIM ** -0.5

TOKEN_TILE = 512
FF_CHUNK = 256
HEADS_PER_STEP = 2
VMEM_LIMIT_BYTES = 56 * 1024 * 1024

_BF16 = jnp.bfloat16
_F32 = jnp.float32


def _dot(a, b):
    return jnp.dot(a, b, preferred_element_type=_F32)


def _dot_nt(a, b):
    return lax.dot_general(a, b, (((1,), (1,)), ((), ())), preferred_element_type=_F32)


def _rms_rows(x, g):
    return x * lax.rsqrt(jnp.mean(x * x, axis=-1, keepdims=True) + RMS_EPS) * g


def _const_spec(shape):
    return pl.BlockSpec(shape, lambda *_: (0,) * len(shape), pipeline_mode=pl.Buffered(1))


def _params(n_axes):
    return pltpu.CompilerParams(
        dimension_semantics=("arbitrary",) * n_axes, vmem_limit_bytes=VMEM_LIMIT_BYTES)


def _ffn_body(x_ref, pre_g_ref, wg_ref, wu_ref, wd_ref, post_g_ref, o_ref, h_ref):
    x = x_ref[...]
    xn = _rms_rows(x, pre_g_ref[...]).astype(_BF16)
    for c in range(D_FF // FF_CHUNK):
        cols = slice(c * FF_CHUNK, (c + 1) * FF_CHUNK)
        gate = _dot(xn, wg_ref[:, cols])
        up = _dot(xn, wu_ref[:, cols])
        h_ref[:, cols] = (jax.nn.silu(gate) * up).astype(_BF16)
    y = _dot(h_ref[...], wd_ref[...])
    o_ref[...] = x + FFN_RES_WEIGHT * _rms_rows(y, post_g_ref[...])


def _ffn(x2d, pre_g, wg, wu, wd, post_g):
    n_tok = x2d.shape[0]
    tile = pl.BlockSpec((TOKEN_TILE, D_MODEL), lambda i: (i, 0))
    return pl.pallas_call(
        _ffn_body,
        grid=(n_tok // TOKEN_TILE,),
        in_specs=[tile, _const_spec((1, D_MODEL)), _const_spec((D_MODEL, D_FF)),
                  _const_spec((D_MODEL, D_FF)), _const_spec((D_FF, D_MODEL)),
                  _const_spec((1, D_MODEL))],
        out_specs=tile,
        out_shape=jax.ShapeDtypeStruct((n_tok, D_MODEL), _F32),
        scratch_shapes=[pltpu.VMEM((TOKEN_TILE, D_FF), _BF16)],
        compiler_params=_params(1),
        name="ffn",
    )(x2d, pre_g, wg, wu, wd, post_g)


_T_QA, _T_VA, _T_QS, _T_VS = 0, W_MOBA, 2 * W_MOBA, 2 * W_MOBA + W_SWA
_T_ROWS = 2 * W_MOBA + W_SWA + W_SWA_KV
_MOBA_PER_TILE = TOKEN_TILE // MOBA_BLOCK
_SWA_PER_TILE = TOKEN_TILE // SWA_BLOCK


def _in_proj_body(x_ref, g_ref, w_rows_ref, w_t_ref,
                  qa_ref, va_ref, qs_ref, vs_ref, ka_ref, ks_ref, kmean_ref):
    xn = _rms_rows(x_ref[0], g_ref[...]).astype(_BF16)
    k_rows = _dot(xn, w_rows_ref[...])
    proj_t = _dot_nt(w_t_ref[...], xn)
    for t in range(_MOBA_PER_TILE):
        rows = slice(t * MOBA_BLOCK, (t + 1) * MOBA_BLOCK)
        k_blk = k_rows[rows, :W_MOBA]
        ka_ref[0, t] = k_blk.astype(_BF16)
        kmean_ref[0, t] = jnp.mean(k_blk, axis=0, keepdims=True)
        qa_ref[0, t] = (proj_t[_T_QA:_T_QA + W_MOBA, rows] * QK_SCALE).astype(_BF16)
        va_ref[0, t] = proj_t[_T_VA:_T_VA + W_MOBA, rows].astype(_BF16)
    for t in range(_SWA_PER_TILE):
        rows = slice(t * SWA_BLOCK, (t + 1) * SWA_BLOCK)
        ks_ref[0, t] = k_rows[rows, W_MOBA:].astype(_BF16)
        qs_ref[0, t] = (proj_t[_T_QS:_T_QS + W_SWA, rows] * QK_SCALE).astype(_BF16)
        vs_ref[0, t] = proj_t[_T_VS:_T_VS + W_SWA_KV, rows].astype(_BF16)


def _in_proj(x, g, w_rows, w_t):
    batch, seq, _ = x.shape
    nb_moba, nb_swa = seq // MOBA_BLOCK, seq // SWA_BLOCK

    def blocked(n_per_tile, rows, cols):
        return pl.BlockSpec((1, n_per_tile, rows, cols), lambda b, i: (b, i, 0, 0))

    return pl.pallas_call(
        _in_proj_body,
        grid=(batch, seq // TOKEN_TILE),
        in_specs=[pl.BlockSpec((1, TOKEN_TILE, D_MODEL), lambda b, i: (b, i, 0)),
                  _const_spec((1, D_MODEL)),
                  _const_spec((D_MODEL, W_MOBA + W_SWA_KV)),
                  _const_spec((_T_ROWS, D_MODEL))],
        out_specs=[blocked(_MOBA_PER_TILE, W_MOBA, MOBA_BLOCK),
                   blocked(_MOBA_PER_TILE, W_MOBA, MOBA_BLOCK),
                   blocked(_SWA_PER_TILE, W_SWA, SWA_BLOCK),
                   blocked(_SWA_PER_TILE, W_SWA_KV, SWA_BLOCK),
                   blocked(_MOBA_PER_TILE, MOBA_BLOCK, W_MOBA),
                   blocked(_SWA_PER_TILE, SWA_BLOCK, W_SWA_KV),
                   blocked(_MOBA_PER_TILE, 1, W_MOBA)],
        out_shape=[jax.ShapeDtypeStruct((batch, nb_moba, W_MOBA, MOBA_BLOCK), _BF16),
                   jax.ShapeDtypeStruct((batch, nb_moba, W_MOBA, MOBA_BLOCK), _BF16),
                   jax.ShapeDtypeStruct((batch, nb_swa, W_SWA, SWA_BLOCK), _BF16),
                   jax.ShapeDtypeStruct((batch, nb_swa, W_SWA_KV, SWA_BLOCK), _BF16),
                   jax.ShapeDtypeStruct((batch, nb_moba, MOBA_BLOCK, W_MOBA), _BF16),
                   jax.ShapeDtypeStruct((batch, nb_swa, SWA_BLOCK, W_SWA_KV), _BF16),
                   jax.ShapeDtypeStruct((batch, nb_moba, 1, W_MOBA), _F32)],
        compiler_params=_params(2),
        name="in_proj",
    )(x, g, w_rows, w_t)


def _t5_bucket(dist):
    n = jnp.maximum(dist, 0)
    max_exact = NUM_BUCKETS // 2
    nf = jnp.maximum(n, 1).astype(_F32)
    large = max_exact + (jnp.log(nf / max_exact) / math.log(MAX_DISTANCE / max_exact)
                         * (NUM_BUCKETS - max_exact)).astype(jnp.int32)
    large = jnp.minimum(large, NUM_BUCKETS - 1)
    return jnp.where(n < max_exact, n, large)


def _bias_lookup(rel_bias_ref, bucket, head):
    val = jnp.full(bucket.shape, rel_bias_ref[NUM_BUCKETS - 1, head], _F32)
    for b in range(NUM_BUCKETS - 2, -1, -1):
        val = jnp.where(bucket == b, rel_bias_ref[b, head], val)
    return val


def _bias_tiles_body(rel_bias_ref, own_ref, prev_ref, swa_ref):
    h = pl.program_id(0)
    key = lax.broadcasted_iota(jnp.int32, (MOBA_BLOCK, MOBA_BLOCK), 0)
    qry = lax.broadcasted_iota(jnp.int32, (MOBA_BLOCK, MOBA_BLOCK), 1)
    dist = qry - key
    own_ref[0] = jnp.where(dist >= 0, _bias_lookup(rel_bias_ref, _t5_bucket(dist), h), NEG)
    prev_ref[0] = _bias_lookup(rel_bias_ref, _t5_bucket(dist + MOBA_BLOCK), h)
    key = lax.broadcasted_iota(jnp.int32, (2 * SWA_BLOCK, SWA_BLOCK), 0) - SWA_BLOCK
    qry = lax.broadcasted_iota(jnp.int32, (2 * SWA_BLOCK, SWA_BLOCK), 1)
    dist = qry - key
    in_win = (dist >= 0) & (dist < SWA_WINDOW)
    swa_ref[0] = jnp.where(
        in_win, _bias_lookup(rel_bias_ref, _t5_bucket(dist), h + N_HEADS_MOBA), NEG)


def _bias_tiles(rel_bias):
    def per_head(rows, cols):
        return pl.BlockSpec((1, rows, cols), lambda h: (h, 0, 0))

    return pl.pallas_call(
        _bias_tiles_body,
        grid=(N_HEADS_MOBA,),
        in_specs=[pl.BlockSpec(memory_space=pltpu.SMEM)],
        out_specs=[per_head(MOBA_BLOCK, MOBA_BLOCK), per_head(MOBA_BLOCK, MOBA_BLOCK),
                   per_head(2 * SWA_BLOCK, SWA_BLOCK)],
        out_shape=[jax.ShapeDtypeStruct((N_HEADS_MOBA, MOBA_BLOCK, MOBA_BLOCK), _F32),
                   jax.ShapeDtypeStruct((N_HEADS_MOBA, MOBA_BLOCK, MOBA_BLOCK), _F32),
                   jax.ShapeDtypeStruct((N_HEADS_SWA, 2 * SWA_BLOCK, SWA_BLOCK), _F32)],
        compiler_params=_params(1),
        name="bias_tiles",
    )(rel_bias)


def _moba_body(rel_bias_ref, q_ref, k_ref, v_ref, kmean_ref, own_ref, prev_ref, o_ref, neg_ref):
    g = pl.program_id(1)
    i = pl.program_id(2)
    nb = k_ref.shape[1]
    q_t = q_ref[0, 0]
    q_row = lax.broadcasted_iota(jnp.int32, q_t.shape, 0)
    km = kmean_ref[0]
    km_hi = km.astype(_BF16)
    km_lo = (km - km_hi.astype(_F32)).astype(_BF16)
    blk = lax.broadcasted_iota(jnp.int32, (nb, MOBA_BLOCK), 0)
    past = blk < i
    prev_j = jnp.maximum(i - 1, 0)

    for r in range(HEADS_PER_STEP):
        head = g * HEADS_PER_STEP + r
        lanes = slice(r * HEAD_DIM, (r + 1) * HEAD_DIM)
        q_pad = jnp.where((q_row >= r * HEAD_DIM) & (q_row < (r + 1) * HEAD_DIM), q_t, 0)

        gate = jnp.where(past, _dot(km_hi, q_pad) + _dot(km_lo, q_pad), NEG)
        sel = jnp.zeros(gate.shape, jnp.bool_)
        for _ in range(MOBA_TOPK):
            top = jnp.max(gate, axis=0, keepdims=True)
            first = jnp.min(jnp.where(gate == top, blk, nb), axis=0, keepdims=True)
            pick = blk == first
            sel = sel | pick
            gate = jnp.where(pick, -jnp.inf, gate)
        neg_ref[...] = jnp.where(sel & past, 0.0, NEG)

        s = _dot(k_ref[0, i], q_pad) + own_ref[r]
        m = jnp.max(s, axis=0, keepdims=True)
        p = jnp.exp(s - m)
        l = jnp.sum(p, axis=0, keepdims=True)
        acc = _dot(v_ref[0, i, lanes, :], p.astype(_BF16))

        def absorb(j, extra, carry):
            m, l, acc = carry
            s = _dot(k_ref[0, j], q_pad) + extra
            m_new = jnp.maximum(m, jnp.max(s, axis=0, keepdims=True))
            alpha = jnp.exp(m - m_new)
            p = jnp.exp(s - m_new)
            l = alpha * l + jnp.sum(p, axis=0, keepdims=True)
            acc = alpha * acc + _dot(v_ref[0, j, lanes, :], p.astype(_BF16))
            return m_new, l, acc

        carry = absorb(prev_j, prev_ref[r] + neg_ref[pl.ds(prev_j, 1), :], (m, l, acc))

        far_bias = rel_bias_ref[NUM_BUCKETS - 1, head]

        def far(j, carry):
            return absorb(j, neg_ref[pl.ds(j, 1), :] + far_bias, carry)

        m, l, acc = lax.fori_loop(0, prev_j, far, carry)
        o_ref[0, 0, lanes, :] = acc / l


def _moba(rel_bias, qa_t, ka, va_t, kmean, bias_own, bias_prev):
    batch, nb = qa_t.shape[0], qa_t.shape[1]
    pair = HEADS_PER_STEP * HEAD_DIM
    return pl.pallas_call(
        _moba_body,
        grid=(batch, N_HEADS_MOBA // HEADS_PER_STEP, nb),
        in_specs=[pl.BlockSpec(memory_space=pltpu.SMEM),
                  pl.BlockSpec((1, 1, pair, MOBA_BLOCK), lambda b, g, i: (b, i, g, 0)),
                  pl.BlockSpec((1, nb, MOBA_BLOCK, pair), lambda b, g, i: (b, 0, 0, g)),
                  pl.BlockSpec((1, nb, pair, MOBA_BLOCK), lambda b, g, i: (b, 0, g, 0)),
                  pl.BlockSpec((1, nb, pair), lambda b, g, i: (b, 0, g)),
                  pl.BlockSpec((HEADS_PER_STEP, MOBA_BLOCK, MOBA_BLOCK), lambda b, g, i: (g, 0, 0)),
                  pl.BlockSpec((HEADS_PER_STEP, MOBA_BLOCK, MOBA_BLOCK), lambda b, g, i: (g, 0, 0))],
        out_specs=pl.BlockSpec((1, 1, pair, MOBA_BLOCK), lambda b, g, i: (b, i, g, 0)),
        out_shape=jax.ShapeDtypeStruct((batch, nb, W_MOBA, MOBA_BLOCK), _F32),
        scratch_shapes=[pltpu.VMEM((nb, MOBA_BLOCK), _F32)],
        compiler_params=_params(3),
        name="moba",
    )(rel_bias, qa_t, ka, va_t, kmean, bias_own, bias_prev)


def _swa_body(sinks_ref, q_ref, k_prev_ref, k_own_ref, v_prev_ref, v_own_ref, bias_ref, o_ref):
    n = pl.program_id(1)
    no_prev = jnp.where(n == 0, NEG, 0.0)
    zeros = jnp.zeros((HEAD_DIM, SWA_BLOCK), _BF16)
    for kv in range(N_KV_SWA):
        heads = range(kv * SWA_GROUP, (kv + 1) * SWA_GROUP)
        kv_rows = slice(kv * HEAD_DIM, (kv + 1) * HEAD_DIM)
        q_pad = jnp.concatenate(
            [jnp.concatenate(
                [q_ref[0, 0, h * HEAD_DIM:(h + 1) * HEAD_DIM, :] if part == kv else zeros
                 for part in range(N_KV_SWA)], axis=0) for h in heads], axis=1)
        s_prev = _dot(k_prev_ref[0, 0], q_pad)
        s_own = _dot(k_own_ref[0, 0], q_pad)
        p_prev, p_own, inv_l = [], [], []
        for slot, h in enumerate(heads):
            cols = slice(slot * SWA_BLOCK, (slot + 1) * SWA_BLOCK)
            sp = s_prev[:, cols] + bias_ref[h, :SWA_BLOCK, :] + no_prev
            so = s_own[:, cols] + bias_ref[h, SWA_BLOCK:, :]
            sink = sinks_ref[h]
            m = jnp.maximum(jnp.maximum(jnp.max(sp, axis=0, keepdims=True),
                                        jnp.max(so, axis=0, keepdims=True)), sink)
            pp = jnp.exp(sp - m)
            po = jnp.exp(so - m)
            l = (jnp.sum(pp, axis=0, keepdims=True) + jnp.sum(po, axis=0, keepdims=True)
                 + jnp.exp(sink - m))
            p_prev.append(pp.astype(_BF16))
            p_own.append(po.astype(_BF16))
            inv_l.append(1.0 / l)
        acc = (_dot(v_prev_ref[0, 0, kv_rows, :], jnp.concatenate(p_prev, axis=1))
               + _dot(v_own_ref[0, 0, kv_rows, :], jnp.concatenate(p_own, axis=1)))
        for slot, h in enumerate(heads):
            cols = slice(slot * SWA_BLOCK, (slot + 1) * SWA_BLOCK)
            o_ref[0, 0, h * HEAD_DIM:(h + 1) * HEAD_DIM, :] = acc[:, cols] * inv_l[slot]


def _swa(sinks, qs_t, ks, vs_t, bias_swa):
    batch, nb = qs_t.shape[0], qs_t.shape[1]

    def own(rows, cols):
        return pl.BlockSpec((1, 1, rows, cols), lambda b, n: (b, n, 0, 0))

    def prev(rows, cols):
        return pl.BlockSpec((1, 1, rows, cols), lambda b, n: (b, jnp.maximum(n - 1, 0), 0, 0))

    return pl.pallas_call(
        _swa_body,
        grid=(batch, nb),
        in_specs=[pl.BlockSpec(memory_space=pltpu.SMEM),
                  own(W_SWA, SWA_BLOCK),
                  prev(SWA_BLOCK, W_SWA_KV), own(SWA_BLOCK, W_SWA_KV),
                  prev(W_SWA_KV, SWA_BLOCK), own(W_SWA_KV, SWA_BLOCK),
                  _const_spec((N_HEADS_SWA, 2 * SWA_BLOCK, SWA_BLOCK))],
        out_specs=own(W_SWA, SWA_BLOCK),
        out_shape=jax.ShapeDtypeStruct((batch, nb, W_SWA, SWA_BLOCK), _F32),
        compiler_params=_params(2),
        name="swa",
    )(sinks, qs_t, ks, ks, vs_t, vs_t, bias_swa)


def _group_norm_rows(o_ref, g_ref):
    rows = []
    for t in range(o_ref.shape[1]):
        o_t = o_ref[0, t]
        scale = lax.rsqrt(jnp.mean(o_t * o_t, axis=0, keepdims=True) + RMS_EPS)
        rows.append(((o_t * scale).T * g_ref[...]).astype(_BF16))
    return jnp.concatenate(rows, axis=0)


def _out_proj_body(x_ref, oa_ref, ob_ref, ga_ref, gb_ref, wa_ref, wb_ref, post_g_ref, o_ref):
    y = (_dot(_group_norm_rows(oa_ref, ga_ref), wa_ref[...])
         + _dot(_group_norm_rows(ob_ref, gb_ref), wb_ref[...]))
    o_ref[0] = x_ref[0] + _rms_rows(y, post_g_ref[...])


def _out_proj(x, oa_t, ob_t, ga, gb, wa, wb, post_g):
    batch, seq, _ = x.shape
    tile = pl.BlockSpec((1, TOKEN_TILE, D_MODEL), lambda b, i: (b, i, 0))
    return pl.pallas_call(
        _out_proj_body,
        grid=(batch, seq // TOKEN_TILE),
        in_specs=[tile,
                  pl.BlockSpec((1, _MOBA_PER_TILE, W_MOBA, MOBA_BLOCK), lambda b, i: (b, i, 0, 0)),
                  pl.BlockSpec((1, _SWA_PER_TILE, W_SWA, SWA_BLOCK), lambda b, i: (b, i, 0, 0)),
                  _const_spec((1, W_MOBA)), _const_spec((1, W_SWA)),
                  _const_spec((W_MOBA, D_MODEL)), _const_spec((W_SWA, D_MODEL)),
                  _const_spec((1, D_MODEL))],
        out_specs=tile,
        out_shape=jax.ShapeDtypeStruct((batch, seq, D_MODEL), _F32),
        compiler_params=_params(2),
        name="out_proj",
    )(x, oa_t, ob_t, ga, gb, wa, wb, post_g)


def _row(v):
    return v.reshape(1, -1)


def _layer(x, ffn1, mix, ffn2, rel_bias, bias_tiles):
    batch, seq, _ = x.shape
    (mix_pre_g, w_in, moba_out_g, swa_sinks, swa_out_g, w_out, mix_post_g) = mix
    bias_own, bias_prev, bias_swa = bias_tiles

    def ffn(x, params):
        pre_g, w_gate, w_up, w_down, post_g = params
        return _ffn(x.reshape(batch * seq, D_MODEL), _row(pre_g), w_gate.astype(_BF16),
                    w_up.astype(_BF16), w_down.astype(_BF16), _row(post_g)).reshape(x.shape)

    x = ffn(x, ffn1)

    qa, ka, va, qs, ks, vs = jnp.split(
        w_in, [W_MOBA, 2 * W_MOBA, 3 * W_MOBA, 3 * W_MOBA + W_SWA, 3 * W_MOBA + W_SWA + W_SWA_KV], axis=1)
    w_rows = jnp.concatenate([ka, ks], axis=1).astype(_BF16)
    w_t = jnp.concatenate([qa, va, qs, vs], axis=1).T.astype(_BF16)
    qa_t, va_t, qs_t, vs_t, ka_b, ks_b, kmean = _in_proj(x, _row(mix_pre_g), w_rows, w_t)

    oa_t = _moba(rel_bias, qa_t, ka_b, va_t, kmean.reshape(batch, seq // MOBA_BLOCK, W_MOBA),
                 bias_own, bias_prev)
    ob_t = _swa(swa_sinks, qs_t, ks_b, vs_t, bias_swa)

    w_out = w_out.astype(_BF16)
    x = _out_proj(x, oa_t, ob_t, _row(moba_out_g), _row(swa_out_g),
                  w_out[:W_MOBA], w_out[W_MOBA:], _row(mix_post_g))
    return ffn(x, ffn2)


def kernel(x, ffn1_pre_g, ffn1_w_gate, ffn1_w_up, ffn1_w_down, ffn1_post_g, mix_pre_g, w_in, rel_bias,
           moba_out_g, swa_sinks, swa_out_g, w_out, mix_post_g, ffn2_pre_g, ffn2_w_gate, ffn2_w_up,
           ffn2_w_down, ffn2_post_g):
    bias_tiles = _bias_tiles(rel_bias)
    for l in range(ffn1_pre_g.shape[0]):
        x = _layer(
            x,
            (ffn1_pre_g[l], ffn1_w_gate[l], ffn1_w_up[l], ffn1_w_down[l], ffn1_post_g[l]),
            (mix_pre_g[l], w_in[l], moba_out_g[l], swa_sinks[l], swa_out_g[l], w_out[l], mix_post_g[l]),
            (ffn2_pre_g[l], ffn2_w_gate[l], ffn2_w_up[l], ffn2_w_down[l], ffn2_post_g[l]),
            rel_bias, bias_tiles)
    return x
```

```python
import functools
import math

import jax
import jax.numpy as jnp
from jax import lax
from jax.experimental import pallas as pl
from jax.experimental.pallas import tpu as pltpu

D_MODEL = 1024
HEAD_DIM = 64
N_HEADS_MOBA = 8
N_HEADS_SWA = 8
N_KV_SWA = 2
SWA_GROUP = N_HEADS_SWA // N_KV_SWA
W_MOBA = N_HEADS_MOBA * HEAD_DIM
W_SWA = N_HEADS_SWA * HEAD_DIM
W_SWA_KV = N_KV_SWA * HEAD_DIM
MOBA_BLOCK = 256
MOBA_TOPK = 3
SWA_BLOCK = 128
SWA_WINDOW = 128
NUM_BUCKETS = 32
MAX_DISTANCE = 128
D_FF = 2816
RMS_EPS = 1e-6
FFN_RES_WEIGHT = 0.5
NEG = -1e30
QK_SCALE = HEAD_DIM ** -0.5

TOKEN_TILE = 512
FF_CHUNK = 256
HEADS_PER_STEP = 2
FAR_UNROLL = 2
VMEM_LIMIT_BYTES = 56 * 1024 * 1024

_BF16 = jnp.bfloat16
_F32 = jnp.float32


def _dot(a, b):
    return jnp.dot(a, b, preferred_element_type=_F32)


def _dot_nt(a, b):
    return lax.dot_general(a, b, (((1,), (1,)), ((), ())), preferred_element_type=_F32)


def _rms_rows(x, g):
    return x * lax.rsqrt(jnp.mean(x * x, axis=-1, keepdims=True) + RMS_EPS) * g


def _const_spec(shape):
    return pl.BlockSpec(shape, lambda *_: (0,) * len(shape), pipeline_mode=pl.Buffered(1))


def _params(n_axes):
    return pltpu.CompilerParams(
        dimension_semantics=("arbitrary",) * n_axes, vmem_limit_bytes=VMEM_LIMIT_BYTES)


def _ffn_body(x_ref, pre_g_ref, wg_ref, wu_ref, wd_ref, post_g_ref, o_ref, h_ref):
    x = x_ref[...]
    xn = _rms_rows(x, pre_g_ref[...]).astype(_BF16)
    for c in range(D_FF // FF_CHUNK):
        cols = slice(c * FF_CHUNK, (c + 1) * FF_CHUNK)
        gate = _dot(xn, wg_ref[:, cols])
        up = _dot(xn, wu_ref[:, cols])
        h_ref[:, cols] = (jax.nn.silu(gate) * up).astype(_BF16)
    y = _dot(h_ref[...], wd_ref[...])
    o_ref[...] = x + FFN_RES_WEIGHT * _rms_rows(y, post_g_ref[...])


def _ffn(x2d, pre_g, wg, wu, wd, post_g):
    n_tok = x2d.shape[0]
    tile = pl.BlockSpec((TOKEN_TILE, D_MODEL), lambda i: (i, 0))
    return pl.pallas_call(
        _ffn_body,
        grid=(n_tok // TOKEN_TILE,),
        in_specs=[tile, _const_spec((1, D_MODEL)), _const_spec((D_MODEL, D_FF)),
                  _const_spec((D_MODEL, D_FF)), _const_spec((D_FF, D_MODEL)),
                  _const_spec((1, D_MODEL))],
        out_specs=tile,
        out_shape=jax.ShapeDtypeStruct((n_tok, D_MODEL), _F32),
        scratch_shapes=[pltpu.VMEM((TOKEN_TILE, D_FF), _BF16)],
        compiler_params=_params(1),
        name="ffn",
    )(x2d, pre_g, wg, wu, wd, post_g)


_T_QA, _T_VA, _T_QS, _T_VS = 0, W_MOBA, 2 * W_MOBA, 2 * W_MOBA + W_SWA
_T_ROWS = 2 * W_MOBA + W_SWA + W_SWA_KV
_MOBA_PER_TILE = TOKEN_TILE // MOBA_BLOCK
_SWA_PER_TILE = TOKEN_TILE // SWA_BLOCK


def _in_proj_body(x_ref, g_ref, w_rows_ref, w_t_ref,
                  qa_ref, va_ref, qs_ref, vs_ref, ka_ref, ks_ref, kmean_ref):
    xn = _rms_rows(x_ref[0], g_ref[...]).astype(_BF16)
    k_rows = _dot(xn, w_rows_ref[...])
    proj_t = _dot_nt(w_t_ref[...], xn)
    for t in range(_MOBA_PER_TILE):
        rows = slice(t * MOBA_BLOCK, (t + 1) * MOBA_BLOCK)
        k_blk = k_rows[rows, :W_MOBA]
        ka_ref[0, t] = k_blk.astype(_BF16)
        kmean_ref[0, t] = jnp.mean(k_blk, axis=0, keepdims=True)
        qa_ref[0, t] = (proj_t[_T_QA:_T_QA + W_MOBA, rows] * QK_SCALE).astype(_BF16)
        va_ref[0, t] = proj_t[_T_VA:_T_VA + W_MOBA, rows].astype(_BF16)
    for t in range(_SWA_PER_TILE):
        rows = slice(t * SWA_BLOCK, (t + 1) * SWA_BLOCK)
        ks_ref[0, t] = k_rows[rows, W_MOBA:].astype(_BF16)
        qs_ref[0, t] = (proj_t[_T_QS:_T_QS + W_SWA, rows] * QK_SCALE).astype(_BF16)
        vs_ref[0, t] = proj_t[_T_VS:_T_VS + W_SWA_KV, rows].astype(_BF16)


def _in_proj(x, g, w_rows, w_t):
    batch, seq, _ = x.shape
    nb_moba, nb_swa = seq // MOBA_BLOCK, seq // SWA_BLOCK

    def blocked(n_per_tile, rows, cols):
        return pl.BlockSpec((1, n_per_tile, rows, cols), lambda b, i: (b, i, 0, 0))

    return pl.pallas_call(
        _in_proj_body,
        grid=(batch, seq // TOKEN_TILE),
        in_specs=[pl.BlockSpec((1, TOKEN_TILE, D_MODEL), lambda b, i: (b, i, 0)),
                  _const_spec((1, D_MODEL)),
                  _const_spec((D_MODEL, W_MOBA + W_SWA_KV)),
                  _const_spec((_T_ROWS, D_MODEL))],
        out_specs=[blocked(_MOBA_PER_TILE, W_MOBA, MOBA_BLOCK),
                   blocked(_MOBA_PER_TILE, W_MOBA, MOBA_BLOCK),
                   blocked(_SWA_PER_TILE, W_SWA, SWA_BLOCK),
                   blocked(_SWA_PER_TILE, W_SWA_KV, SWA_BLOCK),
                   blocked(_MOBA_PER_TILE, MOBA_BLOCK, W_MOBA),
                   blocked(_SWA_PER_TILE, SWA_BLOCK, W_SWA_KV),
                   blocked(_MOBA_PER_TILE, 1, W_MOBA)],
        out_shape=[jax.ShapeDtypeStruct((batch, nb_moba, W_MOBA, MOBA_BLOCK), _BF16),
                   jax.ShapeDtypeStruct((batch, nb_moba, W_MOBA, MOBA_BLOCK), _BF16),
                   jax.ShapeDtypeStruct((batch, nb_swa, W_SWA, SWA_BLOCK), _BF16),
                   jax.ShapeDtypeStruct((batch, nb_swa, W_SWA_KV, SWA_BLOCK), _BF16),
                   jax.ShapeDtypeStruct((batch, nb_moba, MOBA_BLOCK, W_MOBA), _BF16),
                   jax.ShapeDtypeStruct((batch, nb_swa, SWA_BLOCK, W_SWA_KV), _BF16),
                   jax.ShapeDtypeStruct((batch, nb_moba, 1, W_MOBA), _F32)],
        compiler_params=_params(2),
        name="in_proj",
    )(x, g, w_rows, w_t)


def _t5_bucket(dist):
    n = jnp.maximum(dist, 0)
    max_exact = NUM_BUCKETS // 2
    nf = jnp.maximum(n, 1).astype(_F32)
    large = max_exact + (jnp.log(nf / max_exact) / math.log(MAX_DISTANCE / max_exact)
                         * (NUM_BUCKETS - max_exact)).astype(jnp.int32)
    large = jnp.minimum(large, NUM_BUCKETS - 1)
    return jnp.where(n < max_exact, n, large)


def _bias_lookup(rel_bias_ref, bucket, head):
    val = jnp.full(bucket.shape, rel_bias_ref[NUM_BUCKETS - 1, head], _F32)
    for b in range(NUM_BUCKETS - 2, -1, -1):
        val = jnp.where(bucket == b, rel_bias_ref[b, head], val)
    return val


def _bias_tiles_body(rel_bias_ref, own_ref, prev_ref, swa_ref):
    h = pl.program_id(0)
    key = lax.broadcasted_iota(jnp.int32, (MOBA_BLOCK, MOBA_BLOCK), 0)
    qry = lax.broadcasted_iota(jnp.int32, (MOBA_BLOCK, MOBA_BLOCK), 1)
    dist = qry - key
    own_ref[0] = jnp.where(dist >= 0, _bias_lookup(rel_bias_ref, _t5_bucket(dist), h), NEG)
    prev_ref[0] = _bias_lookup(rel_bias_ref, _t5_bucket(dist + MOBA_BLOCK), h)
    key = lax.broadcasted_iota(jnp.int32, (2 * SWA_BLOCK, SWA_BLOCK), 0) - SWA_BLOCK
    qry = lax.broadcasted_iota(jnp.int32, (2 * SWA_BLOCK, SWA_BLOCK), 1)
    dist = qry - key
    in_win = (dist >= 0) & (dist < SWA_WINDOW)
    swa_ref[0] = jnp.where(
        in_win, _bias_lookup(rel_bias_ref, _t5_bucket(dist), h + N_HEADS_MOBA), NEG)


def _bias_tiles(rel_bias):
    def per_head(rows, cols):
        return pl.BlockSpec((1, rows, cols), lambda h: (h, 0, 0))

    return pl.pallas_call(
        _bias_tiles_body,
        grid=(N_HEADS_MOBA,),
        in_specs=[pl.BlockSpec(memory_space=pltpu.SMEM)],
        out_specs=[per_head(MOBA_BLOCK, MOBA_BLOCK), per_head(MOBA_BLOCK, MOBA_BLOCK),
                   per_head(2 * SWA_BLOCK, SWA_BLOCK)],
        out_shape=[jax.ShapeDtypeStruct((N_HEADS_MOBA, MOBA_BLOCK, MOBA_BLOCK), _F32),
                   jax.ShapeDtypeStruct((N_HEADS_MOBA, MOBA_BLOCK, MOBA_BLOCK), _F32),
                   jax.ShapeDtypeStruct((N_HEADS_SWA, 2 * SWA_BLOCK, SWA_BLOCK), _F32)],
        compiler_params=_params(1),
        name="bias_tiles",
    )(rel_bias)


def _moba_body(rel_bias_ref, q_ref, k_ref, v_ref, kmean_ref, own_ref, prev_ref, o_ref, far_ref):
    g = pl.program_id(1)
    i = pl.program_id(2)
    nb = k_ref.shape[1]
    q_t = q_ref[0, 0]
    q_row = lax.broadcasted_iota(jnp.int32, q_t.shape, 0)
    km = kmean_ref[0]
    km_hi = km.astype(_BF16)
    km_lo = (km - km_hi.astype(_F32)).astype(_BF16)
    blk = lax.broadcasted_iota(jnp.int32, (nb, MOBA_BLOCK), 0)
    past = blk < i
    n_far = jnp.maximum(i - 1, 0)

    q_pads, first_blocks = [], []
    for r in range(HEADS_PER_STEP):
        q_pad = jnp.where((q_row >= r * HEAD_DIM) & (q_row < (r + 1) * HEAD_DIM), q_t, 0)
        q_pads.append(q_pad)

        gate = jnp.where(past, _dot(km_hi, q_pad) + _dot(km_lo, q_pad), NEG)
        sel = jnp.zeros(gate.shape, jnp.bool_)
        for _ in range(MOBA_TOPK):
            top = jnp.max(gate, axis=0, keepdims=True)
            first = jnp.min(jnp.where(gate == top, blk, nb), axis=0, keepdims=True)
            pick = blk == first
            sel = sel | pick
            gate = jnp.where(pick, -jnp.inf, gate)
        sel = sel & past

        far_bias = rel_bias_ref[NUM_BUCKETS - 1, g * HEADS_PER_STEP + r]
        far_ref[r] = jnp.where(sel & (blk < i - 1), far_bias, NEG)
        prev_neg = jnp.max(jnp.where(sel & (blk == i - 1), 0.0, NEG), axis=0, keepdims=True)
        first_blocks.append([(i, own_ref[r]), (jnp.maximum(i - 1, 0), prev_ref[r] + prev_neg)])

    def absorb(r, carry, blocks):
        lanes = slice(r * HEAD_DIM, (r + 1) * HEAD_DIM)
        s = [_dot(k_ref[0, j], q_pads[r]) + extra for j, extra in blocks]
        m_new = functools.reduce(jnp.maximum, [jnp.max(x, axis=0, keepdims=True) for x in s])
        if carry is not None:
            m, l, acc = carry
            m_new = jnp.maximum(m, m_new)
        p = [jnp.exp(x - m_new) for x in s]
        l_new = functools.reduce(jnp.add, [jnp.sum(x, axis=0, keepdims=True) for x in p])
        acc_new = functools.reduce(
            jnp.add, [_dot(v_ref[0, j, lanes, :], x.astype(_BF16)) for (j, _), x in zip(blocks, p)])
        if carry is not None:
            alpha = jnp.exp(m - m_new)
            l_new = alpha * l + l_new
            acc_new = alpha * acc + acc_new
        return m_new, l_new, acc_new

    carry = tuple(absorb(r, None, first_blocks[r]) for r in range(HEADS_PER_STEP))

    def far_group(grp, carry):
        out = []
        for r in range(HEADS_PER_STEP):
            blocks = []
            for u in range(FAR_UNROLL):
                j = jnp.minimum(grp * FAR_UNROLL + u, nb - 1)
                blocks.append((j, far_ref[r, pl.ds(j, 1), :]))
            out.append(absorb(r, carry[r], blocks))
        return tuple(out)

    carry = lax.fori_loop(0, (n_far + FAR_UNROLL - 1) // FAR_UNROLL, far_group, carry)
    for r in range(HEADS_PER_STEP):
        _, l, acc = carry[r]
        o_ref[0, 0, r * HEAD_DIM:(r + 1) * HEAD_DIM, :] = acc / l


def _moba(rel_bias, qa_t, ka, va_t, kmean, bias_own, bias_prev):
    batch, nb = qa_t.shape[0], qa_t.shape[1]
    pair = HEADS_PER_STEP * HEAD_DIM
    return pl.pallas_call(
        _moba_body,
        grid=(batch, N_HEADS_MOBA // HEADS_PER_STEP, nb),
        in_specs=[pl.BlockSpec(memory_space=pltpu.SMEM),
                  pl.BlockSpec((1, 1, pair, MOBA_BLOCK), lambda b, g, i: (b, i, g, 0)),
                  pl.BlockSpec((1, nb, MOBA_BLOCK, pair), lambda b, g, i: (b, 0, 0, g)),
                  pl.BlockSpec((1, nb, pair, MOBA_BLOCK), lambda b, g, i: (b, 0, g, 0)),
                  pl.BlockSpec((1, nb, pair), lambda b, g, i: (b, 0, g)),
                  pl.BlockSpec((HEADS_PER_STEP, MOBA_BLOCK, MOBA_BLOCK), lambda b, g, i: (g, 0, 0)),
                  pl.BlockSpec((HEADS_PER_STEP, MOBA_BLOCK, MOBA_BLOCK), lambda b, g, i: (g, 0, 0))],
        out_specs=pl.BlockSpec((1, 1, pair, MOBA_BLOCK), lambda b, g, i: (b, i, g, 0)),
        out_shape=jax.ShapeDtypeStruct((batch, nb, W_MOBA, MOBA_BLOCK), _F32),
        scratch_shapes=[pltpu.VMEM((HEADS_PER_STEP, nb, MOBA_BLOCK), _F32)],
        compiler_params=_params(3),
        name="moba",
    )(rel_bias, qa_t, ka, va_t, kmean, bias_own, bias_prev)


def _swa_body(sinks_ref, q_ref, k_prev_ref, k_own_ref, v_prev_ref, v_own_ref, bias_ref, o_ref):
    n = pl.program_id(1)
    no_prev = jnp.where(n == 0, NEG, 0.0)
    zeros = jnp.zeros((HEAD_DIM, SWA_BLOCK), _BF16)
    for kv in range(N_KV_SWA):
        heads = range(kv * SWA_GROUP, (kv + 1) * SWA_GROUP)
        kv_rows = slice(kv * HEAD_DIM, (kv + 1) * HEAD_DIM)
        q_pad = jnp.concatenate(
            [jnp.concatenate(
                [q_ref[0, 0, h * HEAD_DIM:(h + 1) * HEAD_DIM, :] if part == kv else zeros
                 for part in range(N_KV_SWA)], axis=0) for h in heads], axis=1)
        s_prev = _dot(k_prev_ref[0, 0], q_pad)
        s_own = _dot(k_own_ref[0, 0], q_pad)
        p_prev, p_own, inv_l = [], [], []
        for slot, h in enumerate(heads):
            cols = slice(slot * SWA_BLOCK, (slot + 1) * SWA_BLOCK)
            sp = s_prev[:, cols] + bias_ref[h, :SWA_BLOCK, :] + no_prev
            so = s_own[:, cols] + bias_ref[h, SWA_BLOCK:, :]
            sink = sinks_ref[h]
            m = jnp.maximum(jnp.maximum(jnp.max(sp, axis=0, keepdims=True),
                                        jnp.max(so, axis=0, keepdims=True)), sink)
            pp = jnp.exp(sp - m)
            po = jnp.exp(so - m)
            l = (jnp.sum(pp, axis=0, keepdims=True) + jnp.sum(po, axis=0, keepdims=True)
                 + jnp.exp(sink - m))
            p_prev.append(pp.astype(_BF16))
            p_own.append(po.astype(_BF16))
            inv_l.append(1.0 / l)
        acc = (_dot(v_prev_ref[0, 0, kv_rows, :], jnp.concatenate(p_prev, axis=1))
               + _dot(v_own_ref[0, 0, kv_rows, :], jnp.concatenate(p_own, axis=1)))
        for slot, h in enumerate(heads):
            cols = slice(slot * SWA_BLOCK, (slot + 1) * SWA_BLOCK)
            o_ref[0, 0, h * HEAD_DIM:(h + 1) * HEAD_DIM, :] = acc[:, cols] * inv_l[slot]


def _swa(sinks, qs_t, ks, vs_t, bias_swa):
    batch, nb = qs_t.shape[0], qs_t.shape[1]

    def own(rows, cols):
        return pl.BlockSpec((1, 1, rows, cols), lambda b, n: (b, n, 0, 0))

    def prev(rows, cols):
        return pl.BlockSpec((1, 1, rows, cols), lambda b, n: (b, jnp.maximum(n - 1, 0), 0, 0))

    return pl.pallas_call(
        _swa_body,
        grid=(batch, nb),
        in_specs=[pl.BlockSpec(memory_space=pltpu.SMEM),
                  own(W_SWA, SWA_BLOCK),
                  prev(SWA_BLOCK, W_SWA_KV), own(SWA_BLOCK, W_SWA_KV),
                  prev(W_SWA_KV, SWA_BLOCK), own(W_SWA_KV, SWA_BLOCK),
                  _const_spec((N_HEADS_SWA, 2 * SWA_BLOCK, SWA_BLOCK))],
        out_specs=own(W_SWA, SWA_BLOCK),
        out_shape=jax.ShapeDtypeStruct((batch, nb, W_SWA, SWA_BLOCK), _F32),
        compiler_params=_params(2),
        name="swa",
    )(sinks, qs_t, ks, ks, vs_t, vs_t, bias_swa)


def _group_norm_rows(o_ref, g_ref):
    rows = []
    for t in range(o_ref.shape[1]):
        o_t = o_ref[0, t]
        scale = lax.rsqrt(jnp.mean(o_t * o_t, axis=0, keepdims=True) + RMS_EPS)
        rows.append(((o_t * scale).T * g_ref[...]).astype(_BF16))
    return jnp.concatenate(rows, axis=0)


def _out_proj_body(x_ref, oa_ref, ob_ref, ga_ref, gb_ref, wa_ref, wb_ref, post_g_ref, o_ref):
    y = (_dot(_group_norm_rows(oa_ref, ga_ref), wa_ref[...])
         + _dot(_group_norm_rows(ob_ref, gb_ref), wb_ref[...]))
    o_ref[0] = x_ref[0] + _rms_rows(y, post_g_ref[...])


def _out_proj(x, oa_t, ob_t, ga, gb, wa, wb, post_g):
    batch, seq, _ = x.shape
    tile = pl.BlockSpec((1, TOKEN_TILE, D_MODEL), lambda b, i: (b, i, 0))
    return pl.pallas_call(
        _out_proj_body,
        grid=(batch, seq // TOKEN_TILE),
        in_specs=[tile,
                  pl.BlockSpec((1, _MOBA_PER_TILE, W_MOBA, MOBA_BLOCK), lambda b, i: (b, i, 0, 0)),
                  pl.BlockSpec((1, _SWA_PER_TILE, W_SWA, SWA_BLOCK), lambda b, i: (b, i, 0, 0)),
                  _const_spec((1, W_MOBA)), _const_spec((1, W_SWA)),
                  _const_spec((W_MOBA, D_MODEL)), _const_spec((W_SWA, D_MODEL)),
                  _const_spec((1, D_MODEL))],
        out_specs=tile,
        out_shape=jax.ShapeDtypeStruct((batch, seq, D_MODEL), _F32),
        compiler_params=_params(2),
        name="out_proj",
    )(x, oa_t, ob_t, ga, gb, wa, wb, post_g)


def _row(v):
    return v.reshape(1, -1)


def _layer(x, ffn1, mix, ffn2, rel_bias, bias_tiles):
    batch, seq, _ = x.shape
    (mix_pre_g, w_in, moba_out_g, swa_sinks, swa_out_g, w_out, mix_post_g) = mix
    bias_own, bias_prev, bias_swa = bias_tiles

    def ffn(x, params):
        pre_g, w_gate, w_up, w_down, post_g = params
        return _ffn(x.reshape(batch * seq, D_MODEL), _row(pre_g), w_gate.astype(_BF16),
                    w_up.astype(_BF16), w_down.astype(_BF16), _row(post_g)).reshape(x.shape)

    x = ffn(x, ffn1)

    qa, ka, va, qs, ks, vs = jnp.split(
        w_in, [W_MOBA, 2 * W_MOBA, 3 * W_MOBA, 3 * W_MOBA + W_SWA, 3 * W_MOBA + W_SWA + W_SWA_KV], axis=1)
    w_rows = jnp.concatenate([ka, ks], axis=1).astype(_BF16)
    w_t = jnp.concatenate([qa, va, qs, vs], axis=1).T.astype(_BF16)
    qa_t, va_t, qs_t, vs_t, ka_b, ks_b, kmean = _in_proj(x, _row(mix_pre_g), w_rows, w_t)

    oa_t = _moba(rel_bias, qa_t, ka_b, va_t, kmean.reshape(batch, seq // MOBA_BLOCK, W_MOBA),
                 bias_own, bias_prev)
    ob_t = _swa(swa_sinks, qs_t, ks_b, vs_t, bias_swa)

    w_out = w_out.astype(_BF16)
    x = _out_proj(x, oa_t, ob_t, _row(moba_out_g), _row(swa_out_g),
                  w_out[:W_MOBA], w_out[W_MOBA:], _row(mix_post_g))
    return ffn(x, ffn2)


def kernel(x, ffn1_pre_g, ffn1_w_gate, ffn1_w_up, ffn1_w_down, ffn1_post_g, mix_pre_g, w_in, rel_bias,
           moba_out_g, swa_sinks, swa_out_g, w_out, mix_post_g, ffn2_pre_g, ffn2_w_gate, ffn2_w_up,
           ffn2_w_down, ffn2_post_g):
    bias_tiles = _bias_tiles(rel_bias)
    for l in range(ffn1_pre_g.shape[0]):
        x = _layer(
            x,
            (ffn1_pre_g[l], ffn1_w_gate[l], ffn1_w_up[l], ffn1_w_down[l], ffn1_post_g[l]),
            (mix_pre_g[l], w_in[l], moba_out_g[l], swa_sinks[l], swa_out_g[l], w_out[l], mix_post_g[l]),
            (ffn2_pre_g[l], ffn2_w_gate[l], ffn2_w_up[l], ffn2_w_down[l], ffn2_post_g[l]),
            rel_bias, bias_tiles)
    return x
```

```python
import functools
import math

import jax
import jax.numpy as jnp
from jax import lax
from jax.experimental import pallas as pl
from jax.experimental.pallas import tpu as pltpu

D_MODEL = 1024
HEAD_DIM = 64
N_HEADS_MOBA = 8
N_HEADS_SWA = 8
N_KV_SWA = 2
SWA_GROUP = N_HEADS_SWA // N_KV_SWA
W_MOBA = N_HEADS_MOBA * HEAD_DIM
W_SWA = N_HEADS_SWA * HEAD_DIM
W_SWA_KV = N_KV_SWA * HEAD_DIM
MOBA_BLOCK = 256
MOBA_TOPK = 3
SWA_BLOCK = 128
SWA_WINDOW = 128
NUM_BUCKETS = 32
MAX_DISTANCE = 128
D_FF = 2816
RMS_EPS = 1e-6
FFN_RES_WEIGHT = 0.5
NEG = -1e30
QK_SCALE = HEAD_DIM ** -0.5

TOKEN_TILE = 512
FF_CHUNK = 256
HEADS_PER_STEP = 2
FAR_UNROLL = 2
VMEM_LIMIT_BYTES = 56 * 1024 * 1024

_BF16 = jnp.bfloat16
_F32 = jnp.float32


def _dot(a, b):
    return jnp.dot(a, b, preferred_element_type=_F32)


def _dot_nt(a, b):
    return lax.dot_general(a, b, (((1,), (1,)), ((), ())), preferred_element_type=_F32)


def _rms_rows(x, g):
    return x * lax.rsqrt(jnp.mean(x * x, axis=-1, keepdims=True) + RMS_EPS) * g


def _const_spec(shape):
    return pl.BlockSpec(shape, lambda *_: (0,) * len(shape), pipeline_mode=pl.Buffered(1))


def _params(n_axes):
    return pltpu.CompilerParams(
        dimension_semantics=("arbitrary",) * n_axes, vmem_limit_bytes=VMEM_LIMIT_BYTES)


def _ffn_body(x_ref, pre_g_ref, wg_ref, wu_ref, wd_ref, post_g_ref, o_ref, h_ref):
    x = x_ref[...]
    xn = _rms_rows(x, pre_g_ref[...]).astype(_BF16)
    for c in range(D_FF // FF_CHUNK):
        cols = slice(c * FF_CHUNK, (c + 1) * FF_CHUNK)
        gate = _dot(xn, wg_ref[:, cols])
        up = _dot(xn, wu_ref[:, cols])
        h_ref[:, cols] = (jax.nn.silu(gate) * up).astype(_BF16)
    y = _dot(h_ref[...], wd_ref[...])
    o_ref[...] = x + FFN_RES_WEIGHT * _rms_rows(y, post_g_ref[...])


def _ffn(x2d, pre_g, wg, wu, wd, post_g):
    n_tok = x2d.shape[0]
    tile = pl.BlockSpec((TOKEN_TILE, D_MODEL), lambda i: (i, 0))
    return pl.pallas_call(
        _ffn_body,
        grid=(n_tok // TOKEN_TILE,),
        in_specs=[tile, _const_spec((1, D_MODEL)), _const_spec((D_MODEL, D_FF)),
                  _const_spec((D_MODEL, D_FF)), _const_spec((D_FF, D_MODEL)),
                  _const_spec((1, D_MODEL))],
        out_specs=tile,
        out_shape=jax.ShapeDtypeStruct((n_tok, D_MODEL), _F32),
        scratch_shapes=[pltpu.VMEM((TOKEN_TILE, D_FF), _BF16)],
        compiler_params=_params(1),
        name="ffn",
    )(x2d, pre_g, wg, wu, wd, post_g)


_T_QA, _T_VA, _T_QS, _T_VS = 0, W_MOBA, 2 * W_MOBA, 2 * W_MOBA + W_SWA
_T_ROWS = 2 * W_MOBA + W_SWA + W_SWA_KV
_MOBA_PER_TILE = TOKEN_TILE // MOBA_BLOCK
_SWA_PER_TILE = TOKEN_TILE // SWA_BLOCK


def _in_proj_body(x_ref, g_ref, w_rows_ref, w_t_ref,
                  qa_ref, va_ref, qs_ref, vs_ref, ka_ref, ks_ref, kmean_ref):
    xn = _rms_rows(x_ref[0], g_ref[...]).astype(_BF16)
    k_rows = _dot(xn, w_rows_ref[...])
    proj_t = _dot_nt(w_t_ref[...], xn)
    for t in range(_MOBA_PER_TILE):
        rows = slice(t * MOBA_BLOCK, (t + 1) * MOBA_BLOCK)
        k_blk = k_rows[rows, :W_MOBA]
        ka_ref[0, t] = k_blk.astype(_BF16)
        kmean_ref[0, t] = jnp.mean(k_blk, axis=0, keepdims=True)
        qa_ref[0, t] = (proj_t[_T_QA:_T_QA + W_MOBA, rows] * QK_SCALE).astype(_BF16)
        va_ref[0, t] = proj_t[_T_VA:_T_VA + W_MOBA, rows].astype(_BF16)
    for t in range(_SWA_PER_TILE):
        rows = slice(t * SWA_BLOCK, (t + 1) * SWA_BLOCK)
        ks_ref[0, t] = k_rows[rows, W_MOBA:].astype(_BF16)
        qs_ref[0, t] = (proj_t[_T_QS:_T_QS + W_SWA, rows] * QK_SCALE).astype(_BF16)
        vs_ref[0, t] = proj_t[_T_VS:_T_VS + W_SWA_KV, rows].astype(_BF16)


def _in_proj(x, g, w_rows, w_t):
    batch, seq, _ = x.shape
    nb_moba, nb_swa = seq // MOBA_BLOCK, seq // SWA_BLOCK

    def blocked(n_per_tile, rows, cols):
        return pl.BlockSpec((1, n_per_tile, rows, cols), lambda b, i: (b, i, 0, 0))

    return pl.pallas_call(
        _in_proj_body,
        grid=(batch, seq // TOKEN_TILE),
        in_specs=[pl.BlockSpec((1, TOKEN_TILE, D_MODEL), lambda b, i: (b, i, 0)),
                  _const_spec((1, D_MODEL)),
                  _const_spec((D_MODEL, W_MOBA + W_SWA_KV)),
                  _const_spec((_T_ROWS, D_MODEL))],
        out_specs=[blocked(_MOBA_PER_TILE, W_MOBA, MOBA_BLOCK),
                   blocked(_MOBA_PER_TILE, W_MOBA, MOBA_BLOCK),
                   blocked(_SWA_PER_TILE, W_SWA, SWA_BLOCK),
                   blocked(_SWA_PER_TILE, W_SWA_KV, SWA_BLOCK),
                   blocked(_MOBA_PER_TILE, MOBA_BLOCK, W_MOBA),
                   blocked(_SWA_PER_TILE, SWA_BLOCK, W_SWA_KV),
                   blocked(_MOBA_PER_TILE, 1, W_MOBA)],
        out_shape=[jax.ShapeDtypeStruct((batch, nb_moba, W_MOBA, MOBA_BLOCK), _BF16),
                   jax.ShapeDtypeStruct((batch, nb_moba, W_MOBA, MOBA_BLOCK), _BF16),
                   jax.ShapeDtypeStruct((batch, nb_swa, W_SWA, SWA_BLOCK), _BF16),
                   jax.ShapeDtypeStruct((batch, nb_swa, W_SWA_KV, SWA_BLOCK), _BF16),
                   jax.ShapeDtypeStruct((batch, nb_moba, MOBA_BLOCK, W_MOBA), _BF16),
                   jax.ShapeDtypeStruct((batch, nb_swa, SWA_BLOCK, W_SWA_KV), _BF16),
                   jax.ShapeDtypeStruct((batch, nb_moba, 1, W_MOBA), _F32)],
        compiler_params=_params(2),
        name="in_proj",
    )(x, g, w_rows, w_t)


def _t5_bucket(dist):
    n = jnp.maximum(dist, 0)
    max_exact = NUM_BUCKETS // 2
    nf = jnp.maximum(n, 1).astype(_F32)
    large = max_exact + (jnp.log(nf / max_exact) / math.log(MAX_DISTANCE / max_exact)
                         * (NUM_BUCKETS - max_exact)).astype(jnp.int32)
    large = jnp.minimum(large, NUM_BUCKETS - 1)
    return jnp.where(n < max_exact, n, large)


def _bias_lookup(rel_bias_ref, bucket, head):
    val = jnp.full(bucket.shape, rel_bias_ref[NUM_BUCKETS - 1, head], _F32)
    for b in range(NUM_BUCKETS - 2, -1, -1):
        val = jnp.where(bucket == b, rel_bias_ref[b, head], val)
    return val


def _bias_tiles_body(rel_bias_ref, own_ref, prev_ref, swa_ref):
    h = pl.program_id(0)
    key = lax.broadcasted_iota(jnp.int32, (MOBA_BLOCK, MOBA_BLOCK), 0)
    qry = lax.broadcasted_iota(jnp.int32, (MOBA_BLOCK, MOBA_BLOCK), 1)
    dist = qry - key
    own_ref[0] = jnp.where(dist >= 0, _bias_lookup(rel_bias_ref, _t5_bucket(dist), h), NEG)
    prev_ref[0] = _bias_lookup(rel_bias_ref, _t5_bucket(dist + MOBA_BLOCK), h)
    key = lax.broadcasted_iota(jnp.int32, (2 * SWA_BLOCK, SWA_BLOCK), 0) - SWA_BLOCK
    qry = lax.broadcasted_iota(jnp.int32, (2 * SWA_BLOCK, SWA_BLOCK), 1)
    dist = qry - key
    in_win = (dist >= 0) & (dist < SWA_WINDOW)
    swa_ref[0] = jnp.where(
        in_win, _bias_lookup(rel_bias_ref, _t5_bucket(dist), h + N_HEADS_MOBA), NEG)


def _bias_tiles(rel_bias):
    def per_head(rows, cols):
        return pl.BlockSpec((1, rows, cols), lambda h: (h, 0, 0))

    return pl.pallas_call(
        _bias_tiles_body,
        grid=(N_HEADS_MOBA,),
        in_specs=[pl.BlockSpec(memory_space=pltpu.SMEM)],
        out_specs=[per_head(MOBA_BLOCK, MOBA_BLOCK), per_head(MOBA_BLOCK, MOBA_BLOCK),
                   per_head(2 * SWA_BLOCK, SWA_BLOCK)],
        out_shape=[jax.ShapeDtypeStruct((N_HEADS_MOBA, MOBA_BLOCK, MOBA_BLOCK), _F32),
                   jax.ShapeDtypeStruct((N_HEADS_MOBA, MOBA_BLOCK, MOBA_BLOCK), _F32),
                   jax.ShapeDtypeStruct((N_HEADS_SWA, 2 * SWA_BLOCK, SWA_BLOCK), _F32)],
        compiler_params=_params(1),
        name="bias_tiles",
    )(rel_bias)


def _moba_body(rel_bias_ref, q_ref, k_ref, v_ref, kmean_ref, own_ref, prev_ref, o_ref, far_ref, s_ref):
    g = pl.program_id(1)
    i = pl.program_id(2)
    nb = k_ref.shape[1]
    q_t = q_ref[0, 0]
    q_row = lax.broadcasted_iota(jnp.int32, q_t.shape, 0)
    km = kmean_ref[0]
    km_hi = km.astype(_BF16)
    km_lo = (km - km_hi.astype(_F32)).astype(_BF16)
    blk = lax.broadcasted_iota(jnp.int32, (nb, MOBA_BLOCK), 0)
    past = blk < i
    n_far = jnp.maximum(i - 1, 0)

    q_pads, prev_extra = [], []
    for r in range(HEADS_PER_STEP):
        q_pad = jnp.where((q_row >= r * HEAD_DIM) & (q_row < (r + 1) * HEAD_DIM), q_t, 0)
        q_pads.append(q_pad)

        gate = jnp.where(past, _dot(km_hi, q_pad) + _dot(km_lo, q_pad), NEG)
        sel = jnp.zeros(gate.shape, jnp.bool_)
        for _ in range(MOBA_TOPK):
            top = jnp.max(gate, axis=0, keepdims=True)
            first = jnp.min(jnp.where(gate == top, blk, nb), axis=0, keepdims=True)
            pick = blk == first
            sel = sel | pick
            gate = jnp.where(pick, -jnp.inf, gate)
        sel = sel & past

        far_bias = rel_bias_ref[NUM_BUCKETS - 1, g * HEADS_PER_STEP + r]
        far_ref[r] = jnp.where(sel & (blk < i - 1), far_bias, NEG)
        prev_neg = jnp.max(jnp.where(sel & (blk == i - 1), 0.0, NEG), axis=0, keepdims=True)
        prev_extra.append(prev_ref[r] + prev_neg)

    def score(blocks):
        tops = []
        for r in range(HEADS_PER_STEP):
            top = []
            for u, (j, extra) in enumerate(blocks):
                s = _dot(k_ref[0, j], q_pads[r]) + extra[r]
                s_ref[r, u] = s
                top.append(jnp.max(s, axis=0, keepdims=True))
            tops.append(functools.reduce(jnp.maximum, top))
        return tuple(tops)

    def absorb(carry, js, tops):
        out = []
        for r in range(HEADS_PER_STEP):
            lanes = slice(r * HEAD_DIM, (r + 1) * HEAD_DIM)
            m, l, acc = carry[r]
            m_new = jnp.maximum(m, tops[r])
            alpha = jnp.exp(m - m_new)
            p = [jnp.exp(s_ref[r, u] - m_new) for u in range(len(js))]
            l_new = alpha * l + functools.reduce(jnp.add, [jnp.sum(x, axis=0, keepdims=True) for x in p])
            pv = functools.reduce(
                jnp.add, [_dot(v_ref[0, j, lanes, :], x.astype(_BF16)) for j, x in zip(js, p)])
            out.append((m_new, l_new, alpha * acc + pv))
        return tuple(out)

    def far_blocks(grp):
        blocks = []
        for u in range(FAR_UNROLL):
            j = jnp.minimum(grp * FAR_UNROLL + u, nb - 1)
            blocks.append((j, [far_ref[r, pl.ds(j, 1), :] for r in range(HEADS_PER_STEP)]))
        return blocks

    first = [(i, [own_ref[r] for r in range(HEADS_PER_STEP)]),
             (jnp.maximum(i - 1, 0), [prev_extra[r] for r in range(HEADS_PER_STEP)])]
    assert len(first) == FAR_UNROLL
    row = (jnp.full((1, MOBA_BLOCK), -jnp.inf, _F32), jnp.zeros((1, MOBA_BLOCK), _F32))
    carry = tuple(row + (jnp.zeros((HEAD_DIM, MOBA_BLOCK), _F32),) for _ in range(HEADS_PER_STEP))

    def step(grp, state):
        carry, js, tops = state
        carry = absorb(carry, js, tops)
        blocks = far_blocks(grp)
        return carry, tuple(j for j, _ in blocks), score(blocks)

    state = (carry, tuple(j for j, _ in first), score(first))
    carry, js, tops = lax.fori_loop(0, (n_far + FAR_UNROLL - 1) // FAR_UNROLL, step, state)
    carry = absorb(carry, js, tops)
    for r in range(HEADS_PER_STEP):
        _, l, acc = carry[r]
        o_ref[0, 0, r * HEAD_DIM:(r + 1) * HEAD_DIM, :] = acc / l


def _moba(rel_bias, qa_t, ka, va_t, kmean, bias_own, bias_prev):
    batch, nb = qa_t.shape[0], qa_t.shape[1]
    pair = HEADS_PER_STEP * HEAD_DIM
    return pl.pallas_call(
        _moba_body,
        grid=(batch, N_HEADS_MOBA // HEADS_PER_STEP, nb),
        in_specs=[pl.BlockSpec(memory_space=pltpu.SMEM),
                  pl.BlockSpec((1, 1, pair, MOBA_BLOCK), lambda b, g, i: (b, i, g, 0)),
                  pl.BlockSpec((1, nb, MOBA_BLOCK, pair), lambda b, g, i: (b, 0, 0, g)),
                  pl.BlockSpec((1, nb, pair, MOBA_BLOCK), lambda b, g, i: (b, 0, g, 0)),
                  pl.BlockSpec((1, nb, pair), lambda b, g, i: (b, 0, g)),
                  pl.BlockSpec((HEADS_PER_STEP, MOBA_BLOCK, MOBA_BLOCK), lambda b, g, i: (g, 0, 0)),
                  pl.BlockSpec((HEADS_PER_STEP, MOBA_BLOCK, MOBA_BLOCK), lambda b, g, i: (g, 0, 0))],
        out_specs=pl.BlockSpec((1, 1, pair, MOBA_BLOCK), lambda b, g, i: (b, i, g, 0)),
        out_shape=jax.ShapeDtypeStruct((batch, nb, W_MOBA, MOBA_BLOCK), _F32),
        scratch_shapes=[pltpu.VMEM((HEADS_PER_STEP, nb, MOBA_BLOCK), _F32),
                        pltpu.VMEM((HEADS_PER_STEP, FAR_UNROLL, MOBA_BLOCK, MOBA_BLOCK), _F32)],
        compiler_params=_params(3),
        name="moba",
    )(rel_bias, qa_t, ka, va_t, kmean, bias_own, bias_prev)


def _swa_body(sinks_ref, q_ref, k_prev_ref, k_own_ref, v_prev_ref, v_own_ref, bias_ref, o_ref):
    n = pl.program_id(1)
    no_prev = jnp.where(n == 0, NEG, 0.0)
    zeros = jnp.zeros((HEAD_DIM, SWA_BLOCK), _BF16)
    for kv in range(N_KV_SWA):
        heads = range(kv * SWA_GROUP, (kv + 1) * SWA_GROUP)
        kv_rows = slice(kv * HEAD_DIM, (kv + 1) * HEAD_DIM)
        q_pad = jnp.concatenate(
            [jnp.concatenate(
                [q_ref[0, 0, h * HEAD_DIM:(h + 1) * HEAD_DIM, :] if part == kv else zeros
                 for part in range(N_KV_SWA)], axis=0) for h in heads], axis=1)
        s_prev = _dot(k_prev_ref[0, 0], q_pad)
        s_own = _dot(k_own_ref[0, 0], q_pad)
        p_prev, p_own, inv_l = [], [], []
        for slot, h in enumerate(heads):
            cols = slice(slot * SWA_BLOCK, (slot + 1) * SWA_BLOCK)
            sp = s_prev[:, cols] + bias_ref[h, :SWA_BLOCK, :] + no_prev
            so = s_own[:, cols] + bias_ref[h, SWA_BLOCK:, :]
            sink = sinks_ref[h]
            m = jnp.maximum(jnp.maximum(jnp.max(sp, axis=0, keepdims=True),
                                        jnp.max(so, axis=0, keepdims=True)), sink)
            pp = jnp.exp(sp - m)
            po = jnp.exp(so - m)
            l = (jnp.sum(pp, axis=0, keepdims=True) + jnp.sum(po, axis=0, keepdims=True)
                 + jnp.exp(sink - m))
            p_prev.append(pp.astype(_BF16))
            p_own.append(po.astype(_BF16))
            inv_l.append(1.0 / l)
        acc = (_dot(v_prev_ref[0, 0, kv_rows, :], jnp.concatenate(p_prev, axis=1))
               + _dot(v_own_ref[0, 0, kv_rows, :], jnp.concatenate(p_own, axis=1)))
        for slot, h in enumerate(heads):
            cols = slice(slot * SWA_BLOCK, (slot + 1) * SWA_BLOCK)
            o_ref[0, 0, h * HEAD_DIM:(h + 1) * HEAD_DIM, :] = acc[:, cols] * inv_l[slot]


def _swa(sinks, qs_t, ks, vs_t, bias_swa):
    batch, nb = qs_t.shape[0], qs_t.shape[1]

    def own(rows, cols):
        return pl.BlockSpec((1, 1, rows, cols), lambda b, n: (b, n, 0, 0))

    def prev(rows, cols):
        return pl.BlockSpec((1, 1, rows, cols), lambda b, n: (b, jnp.maximum(n - 1, 0), 0, 0))

    return pl.pallas_call(
        _swa_body,
        grid=(batch, nb),
        in_specs=[pl.BlockSpec(memory_space=pltpu.SMEM),
                  own(W_SWA, SWA_BLOCK),
                  prev(SWA_BLOCK, W_SWA_KV), own(SWA_BLOCK, W_SWA_KV),
                  prev(W_SWA_KV, SWA_BLOCK), own(W_SWA_KV, SWA_BLOCK),
                  _const_spec((N_HEADS_SWA, 2 * SWA_BLOCK, SWA_BLOCK))],
        out_specs=own(W_SWA, SWA_BLOCK),
        out_shape=jax.ShapeDtypeStruct((batch, nb, W_SWA, SWA_BLOCK), _F32),
        compiler_params=_params(2),
        name="swa",
    )(sinks, qs_t, ks, ks, vs_t, vs_t, bias_swa)


def _group_norm_rows(o_ref, g_ref):
    rows = []
    for t in range(o_ref.shape[1]):
        o_t = o_ref[0, t]
        scale = lax.rsqrt(jnp.mean(o_t * o_t, axis=0, keepdims=True) + RMS_EPS)
        rows.append(((o_t * scale).T * g_ref[...]).astype(_BF16))
    return jnp.concatenate(rows, axis=0)


def _out_proj_body(x_ref, oa_ref, ob_ref, ga_ref, gb_ref, wa_ref, wb_ref, post_g_ref, o_ref):
    y = (_dot(_group_norm_rows(oa_ref, ga_ref), wa_ref[...])
         + _dot(_group_norm_rows(ob_ref, gb_ref), wb_ref[...]))
    o_ref[0] = x_ref[0] + _rms_rows(y, post_g_ref[...])


def _out_proj(x, oa_t, ob_t, ga, gb, wa, wb, post_g):
    batch, seq, _ = x.shape
    tile = pl.BlockSpec((1, TOKEN_TILE, D_MODEL), lambda b, i: (b, i, 0))
    return pl.pallas_call(
        _out_proj_body,
        grid=(batch, seq // TOKEN_TILE),
        in_specs=[tile,
                  pl.BlockSpec((1, _MOBA_PER_TILE, W_MOBA, MOBA_BLOCK), lambda b, i: (b, i, 0, 0)),
                  pl.BlockSpec((1, _SWA_PER_TILE, W_SWA, SWA_BLOCK), lambda b, i: (b, i, 0, 0)),
                  _const_spec((1, W_MOBA)), _const_spec((1, W_SWA)),
                  _const_spec((W_MOBA, D_MODEL)), _const_spec((W_SWA, D_MODEL)),
                  _const_spec((1, D_MODEL))],
        out_specs=tile,
        out_shape=jax.ShapeDtypeStruct((batch, seq, D_MODEL), _F32),
        compiler_params=_params(2),
        name="out_proj",
    )(x, oa_t, ob_t, ga, gb, wa, wb, post_g)


def _row(v):
    return v.reshape(1, -1)


def _layer(x, ffn1, mix, ffn2, rel_bias, bias_tiles):
    batch, seq, _ = x.shape
    (mix_pre_g, w_in, moba_out_g, swa_sinks, swa_out_g, w_out, mix_post_g) = mix
    bias_own, bias_prev, bias_swa = bias_tiles

    def ffn(x, params):
        pre_g, w_gate, w_up, w_down, post_g = params
        return _ffn(x.reshape(batch * seq, D_MODEL), _row(pre_g), w_gate.astype(_BF16),
                    w_up.astype(_BF16), w_down.astype(_BF16), _row(post_g)).reshape(x.shape)

    x = ffn(x, ffn1)

    qa, ka, va, qs, ks, vs = jnp.split(
        w_in, [W_MOBA, 2 * W_MOBA, 3 * W_MOBA, 3 * W_MOBA + W_SWA, 3 * W_MOBA + W_SWA + W_SWA_KV], axis=1)
    w_rows = jnp.concatenate([ka, ks], axis=1).astype(_BF16)
    w_t = jnp.concatenate([qa, va, qs, vs], axis=1).T.astype(_BF16)
    qa_t, va_t, qs_t, vs_t, ka_b, ks_b, kmean = _in_proj(x, _row(mix_pre_g), w_rows, w_t)

    oa_t = _moba(rel_bias, qa_t, ka_b, va_t, kmean.reshape(batch, seq // MOBA_BLOCK, W_MOBA),
                 bias_own, bias_prev)
    ob_t = _swa(swa_sinks, qs_t, ks_b, vs_t, bias_swa)

    w_out = w_out.astype(_BF16)
    x = _out_proj(x, oa_t, ob_t, _row(moba_out_g), _row(swa_out_g),
                  w_out[:W_MOBA], w_out[W_MOBA:], _row(mix_post_g))
    return ffn(x, ffn2)


def kernel(x, ffn1_pre_g, ffn1_w_gate, ffn1_w_up, ffn1_w_down, ffn1_post_g, mix_pre_g, w_in, rel_bias,
           moba_out_g, swa_sinks, swa_out_g, w_out, mix_post_g, ffn2_pre_g, ffn2_w_gate, ffn2_w_up,
           ffn2_w_down, ffn2_post_g):
    bias_tiles = _bias_tiles(rel_bias)
    for l in range(ffn1_pre_g.shape[0]):
        x = _layer(
            x,
            (ffn1_pre_g[l], ffn1_w_gate[l], ffn1_w_up[l], ffn1_w_down[l], ffn1_post_g[l]),
            (mix_pre_g[l], w_in[l], moba_out_g[l], swa_sinks[l], swa_out_g[l], w_out[l], mix_post_g[l]),
            (ffn2_pre_g[l], ffn2_w_gate[l], ffn2_w_up[l], ffn2_w_down[l], ffn2_post_g[l]),
            rel_bias, bias_tiles)
    return x
```

```python
import functools
import math

import jax
import jax.numpy as jnp
from jax import lax
from jax.experimental import pallas as pl
from jax.experimental.pallas import tpu as pltpu

D_MODEL = 1024
HEAD_DIM = 64
N_HEADS_MOBA = 8
N_HEADS_SWA = 8
N_KV_SWA = 2
SWA_GROUP = N_HEADS_SWA // N_KV_SWA
W_MOBA = N_HEADS_MOBA * HEAD_DIM
W_SWA = N_HEADS_SWA * HEAD_DIM
W_SWA_KV = N_KV_SWA * HEAD_DIM
MOBA_BLOCK = 256
MOBA_TOPK = 3
SWA_BLOCK = 128
SWA_WINDOW = 128
NUM_BUCKETS = 32
MAX_DISTANCE = 128
D_FF = 2816
RMS_EPS = 1e-6
FFN_RES_WEIGHT = 0.5
NEG = -1e30
QK_SCALE = HEAD_DIM ** -0.5
LOG2E = math.log2(math.e)
Q_FOLD = QK_SCALE * LOG2E
SUM_ROWS = 16

TOKEN_TILE = 512
FF_CHUNK = 256
HEADS_PER_STEP = 2
FAR_UNROLL = 2
VMEM_LIMIT_BYTES = 56 * 1024 * 1024

_BF16 = jnp.bfloat16
_F32 = jnp.float32


def _dot(a, b):
    return jnp.dot(a, b, preferred_element_type=_F32)


def _dot_nt(a, b):
    return lax.dot_general(a, b, (((1,), (1,)), ((), ())), preferred_element_type=_F32)


def _rms_rows(x, g):
    return x * lax.rsqrt(jnp.mean(x * x, axis=-1, keepdims=True) + RMS_EPS) * g


def _const_spec(shape):
    return pl.BlockSpec(shape, lambda *_: (0,) * len(shape), pipeline_mode=pl.Buffered(1))


def _params(n_axes):
    return pltpu.CompilerParams(
        dimension_semantics=("arbitrary",) * n_axes, vmem_limit_bytes=VMEM_LIMIT_BYTES)


def _ffn_body(x_ref, pre_g_ref, wg_ref, wu_ref, wd_ref, post_g_ref, o_ref, h_ref):
    x = x_ref[...]
    xn = _rms_rows(x, pre_g_ref[...]).astype(_BF16)
    for c in range(D_FF // FF_CHUNK):
        cols = slice(c * FF_CHUNK, (c + 1) * FF_CHUNK)
        gate = _dot(xn, wg_ref[:, cols])
        up = _dot(xn, wu_ref[:, cols])
        h_ref[:, cols] = (jax.nn.silu(gate) * up).astype(_BF16)
    y = _dot(h_ref[...], wd_ref[...])
    o_ref[...] = x + FFN_RES_WEIGHT * _rms_rows(y, post_g_ref[...])


def _ffn(x2d, pre_g, wg, wu, wd, post_g):
    n_tok = x2d.shape[0]
    tile = pl.BlockSpec((TOKEN_TILE, D_MODEL), lambda i: (i, 0))
    return pl.pallas_call(
        _ffn_body,
        grid=(n_tok // TOKEN_TILE,),
        in_specs=[tile, _const_spec((1, D_MODEL)), _const_spec((D_MODEL, D_FF)),
                  _const_spec((D_MODEL, D_FF)), _const_spec((D_FF, D_MODEL)),
                  _const_spec((1, D_MODEL))],
        out_specs=tile,
        out_shape=jax.ShapeDtypeStruct((n_tok, D_MODEL), _F32),
        scratch_shapes=[pltpu.VMEM((TOKEN_TILE, D_FF), _BF16)],
        compiler_params=_params(1),
        name="ffn",
    )(x2d, pre_g, wg, wu, wd, post_g)


_T_QA, _T_VA, _T_QS, _T_VS = 0, W_MOBA, 2 * W_MOBA, 2 * W_MOBA + W_SWA
_T_ROWS = 2 * W_MOBA + W_SWA + W_SWA_KV
_MOBA_PER_TILE = TOKEN_TILE // MOBA_BLOCK
_SWA_PER_TILE = TOKEN_TILE // SWA_BLOCK


def _in_proj_body(x_ref, g_ref, w_rows_ref, w_t_ref,
                  qa_ref, va_ref, qs_ref, vs_ref, ka_ref, ks_ref, kmean_ref):
    xn = _rms_rows(x_ref[0], g_ref[...]).astype(_BF16)
    k_rows = _dot(xn, w_rows_ref[...])
    proj_t = _dot_nt(w_t_ref[...], xn)
    for t in range(_MOBA_PER_TILE):
        rows = slice(t * MOBA_BLOCK, (t + 1) * MOBA_BLOCK)
        k_blk = k_rows[rows, :W_MOBA]
        ka_ref[0, t] = k_blk.astype(_BF16)
        kmean_ref[0, t] = jnp.mean(k_blk, axis=0, keepdims=True)
        qa_ref[0, t] = (proj_t[_T_QA:_T_QA + W_MOBA, rows] * Q_FOLD).astype(_BF16)
        va_ref[0, t] = proj_t[_T_VA:_T_VA + W_MOBA, rows].astype(_BF16)
    for t in range(_SWA_PER_TILE):
        rows = slice(t * SWA_BLOCK, (t + 1) * SWA_BLOCK)
        ks_ref[0, t] = k_rows[rows, W_MOBA:].astype(_BF16)
        qs_ref[0, t] = (proj_t[_T_QS:_T_QS + W_SWA, rows] * Q_FOLD).astype(_BF16)
        vs_ref[0, t] = proj_t[_T_VS:_T_VS + W_SWA_KV, rows].astype(_BF16)


def _in_proj(x, g, w_rows, w_t):
    batch, seq, _ = x.shape
    nb_moba, nb_swa = seq // MOBA_BLOCK, seq // SWA_BLOCK

    def blocked(n_per_tile, rows, cols):
        return pl.BlockSpec((1, n_per_tile, rows, cols), lambda b, i: (b, i, 0, 0))

    return pl.pallas_call(
        _in_proj_body,
        grid=(batch, seq // TOKEN_TILE),
        in_specs=[pl.BlockSpec((1, TOKEN_TILE, D_MODEL), lambda b, i: (b, i, 0)),
                  _const_spec((1, D_MODEL)),
                  _const_spec((D_MODEL, W_MOBA + W_SWA_KV)),
                  _const_spec((_T_ROWS, D_MODEL))],
        out_specs=[blocked(_MOBA_PER_TILE, W_MOBA, MOBA_BLOCK),
                   blocked(_MOBA_PER_TILE, W_MOBA, MOBA_BLOCK),
                   blocked(_SWA_PER_TILE, W_SWA, SWA_BLOCK),
                   blocked(_SWA_PER_TILE, W_SWA_KV, SWA_BLOCK),
                   blocked(_MOBA_PER_TILE, MOBA_BLOCK, W_MOBA),
                   blocked(_SWA_PER_TILE, SWA_BLOCK, W_SWA_KV),
                   blocked(_MOBA_PER_TILE, 1, W_MOBA)],
        out_shape=[jax.ShapeDtypeStruct((batch, nb_moba, W_MOBA, MOBA_BLOCK), _BF16),
                   jax.ShapeDtypeStruct((batch, nb_moba, W_MOBA, MOBA_BLOCK), _BF16),
                   jax.ShapeDtypeStruct((batch, nb_swa, W_SWA, SWA_BLOCK), _BF16),
                   jax.ShapeDtypeStruct((batch, nb_swa, W_SWA_KV, SWA_BLOCK), _BF16),
                   jax.ShapeDtypeStruct((batch, nb_moba, MOBA_BLOCK, W_MOBA), _BF16),
                   jax.ShapeDtypeStruct((batch, nb_swa, SWA_BLOCK, W_SWA_KV), _BF16),
                   jax.ShapeDtypeStruct((batch, nb_moba, 1, W_MOBA), _F32)],
        compiler_params=_params(2),
        name="in_proj",
    )(x, g, w_rows, w_t)


def _t5_bucket(dist):
    n = jnp.maximum(dist, 0)
    max_exact = NUM_BUCKETS // 2
    nf = jnp.maximum(n, 1).astype(_F32)
    large = max_exact + (jnp.log(nf / max_exact) / math.log(MAX_DISTANCE / max_exact)
                         * (NUM_BUCKETS - max_exact)).astype(jnp.int32)
    large = jnp.minimum(large, NUM_BUCKETS - 1)
    return jnp.where(n < max_exact, n, large)


def _bias_lookup(rel_bias_ref, bucket, head):
    val = jnp.full(bucket.shape, rel_bias_ref[NUM_BUCKETS - 1, head], _F32)
    for b in range(NUM_BUCKETS - 2, -1, -1):
        val = jnp.where(bucket == b, rel_bias_ref[b, head], val)
    return val


def _bias_tiles_body(rel_bias_ref, own_ref, prev_ref, swa_ref):
    h = pl.program_id(0)
    key = lax.broadcasted_iota(jnp.int32, (MOBA_BLOCK, MOBA_BLOCK), 0)
    qry = lax.broadcasted_iota(jnp.int32, (MOBA_BLOCK, MOBA_BLOCK), 1)
    dist = qry - key
    own_ref[0] = jnp.where(dist >= 0, _bias_lookup(rel_bias_ref, _t5_bucket(dist), h) * LOG2E, NEG)
    prev_ref[0] = _bias_lookup(rel_bias_ref, _t5_bucket(dist + MOBA_BLOCK), h) * LOG2E
    key = lax.broadcasted_iota(jnp.int32, (2 * SWA_BLOCK, SWA_BLOCK), 0) - SWA_BLOCK
    qry = lax.broadcasted_iota(jnp.int32, (2 * SWA_BLOCK, SWA_BLOCK), 1)
    dist = qry - key
    in_win = (dist >= 0) & (dist < SWA_WINDOW)
    swa_ref[0] = jnp.where(
        in_win, _bias_lookup(rel_bias_ref, _t5_bucket(dist), h + N_HEADS_MOBA) * LOG2E, NEG)


def _bias_tiles(rel_bias):
    def per_head(rows, cols):
        return pl.BlockSpec((1, rows, cols), lambda h: (h, 0, 0))

    return pl.pallas_call(
        _bias_tiles_body,
        grid=(N_HEADS_MOBA,),
        in_specs=[pl.BlockSpec(memory_space=pltpu.SMEM)],
        out_specs=[per_head(MOBA_BLOCK, MOBA_BLOCK), per_head(MOBA_BLOCK, MOBA_BLOCK),
                   per_head(2 * SWA_BLOCK, SWA_BLOCK)],
        out_shape=[jax.ShapeDtypeStruct((N_HEADS_MOBA, MOBA_BLOCK, MOBA_BLOCK), _F32),
                   jax.ShapeDtypeStruct((N_HEADS_MOBA, MOBA_BLOCK, MOBA_BLOCK), _F32),
                   jax.ShapeDtypeStruct((N_HEADS_SWA, 2 * SWA_BLOCK, SWA_BLOCK), _F32)],
        compiler_params=_params(1),
        name="bias_tiles",
    )(rel_bias)


def _moba_body(rel_bias_ref, q_ref, k_ref, v_ref, kmean_ref, own_ref, prev_ref, o_ref, far_ref, s_ref):
    g = pl.program_id(1)
    i = pl.program_id(2)
    nb = k_ref.shape[1]
    q_t = q_ref[0, 0]
    q_row = lax.broadcasted_iota(jnp.int32, q_t.shape, 0)
    km = kmean_ref[0]
    km_hi = km.astype(_BF16)
    km_lo = (km - km_hi.astype(_F32)).astype(_BF16)
    blk = lax.broadcasted_iota(jnp.int32, (nb, MOBA_BLOCK), 0)
    past = blk < i
    n_far = jnp.maximum(i - 1, 0)

    q_pads, prev_extra = [], []
    for r in range(HEADS_PER_STEP):
        q_pad = jnp.where((q_row >= r * HEAD_DIM) & (q_row < (r + 1) * HEAD_DIM), q_t, 0)
        q_pads.append(q_pad)

        gate = jnp.where(past, _dot(km_hi, q_pad) + _dot(km_lo, q_pad), NEG)
        sel = jnp.zeros(gate.shape, jnp.bool_)
        for _ in range(MOBA_TOPK):
            top = jnp.max(gate, axis=0, keepdims=True)
            first = jnp.min(jnp.where(gate == top, blk, nb), axis=0, keepdims=True)
            pick = blk == first
            sel = sel | pick
            gate = jnp.where(pick, -jnp.inf, gate)
        sel = sel & past

        far_bias = rel_bias_ref[NUM_BUCKETS - 1, g * HEADS_PER_STEP + r] * LOG2E
        far_ref[r] = jnp.where(sel & (blk < i - 1), far_bias, NEG)
        prev_neg = jnp.max(jnp.where(sel & (blk == i - 1), 0.0, NEG), axis=0, keepdims=True)
        prev_extra.append(prev_ref[r] + prev_neg)

    def score(slot, blocks):
        tops = []
        for r in range(HEADS_PER_STEP):
            top = []
            for u, (j, extra) in enumerate(blocks):
                s = _dot(k_ref[0, j], q_pads[r]) + extra[r]
                s_ref[slot, r, u] = s
                top.append(jnp.max(s, axis=0, keepdims=True))
            tops.append(functools.reduce(jnp.maximum, top))
        return slot, tuple(j for j, _ in blocks), tuple(tops)

    ones = jnp.ones((SUM_ROWS, MOBA_BLOCK), _BF16)

    def absorb(carry, scored):
        slot, js, tops = scored
        out = []
        for r in range(HEADS_PER_STEP):
            lanes = slice(r * HEAD_DIM, (r + 1) * HEAD_DIM)
            m, acc = carry[r]
            m_new = jnp.maximum(m, tops[r])
            pv = functools.reduce(jnp.add, [
                _dot(jnp.concatenate([v_ref[0, j, lanes, :], ones], axis=0),
                     jnp.exp2(s_ref[slot, r, u] - m_new).astype(_BF16)) for u, j in enumerate(js)])
            out.append((m_new, jnp.exp2(m - m_new) * acc + pv))
        return tuple(out)

    def far_blocks(grp):
        blocks = []
        for u in range(FAR_UNROLL):
            j = jnp.minimum(grp * FAR_UNROLL + u, nb - 1)
            blocks.append((j, [far_ref[r, pl.ds(j, 1), :] for r in range(HEADS_PER_STEP)]))
        return blocks

    near = [(i, [own_ref[r] for r in range(HEADS_PER_STEP)]),
            (jnp.maximum(i - 1, 0), [prev_extra[r] for r in range(HEADS_PER_STEP)])]
    assert len(near) % FAR_UNROLL == 0
    carry = tuple((jnp.full((1, MOBA_BLOCK), -jnp.inf, _F32),
                   jnp.zeros((HEAD_DIM + SUM_ROWS, MOBA_BLOCK), _F32)) for _ in range(HEADS_PER_STEP))
    pending, slot = None, 0
    for c in range(0, len(near), FAR_UNROLL):
        scored = score(slot, near[c:c + FAR_UNROLL])
        if pending is not None:
            carry = absorb(carry, pending)
        pending, slot = scored, 1 - slot
    pending_slot = pending[0]

    def step(pair, state):
        carry, js, tops = state
        pending = (pending_slot, js, tops)
        for half in range(2):
            scored = score(1 - pending[0], far_blocks(2 * pair + half))
            carry = absorb(carry, pending)
            pending = scored
        return carry, pending[1], pending[2]

    n_groups = (n_far + FAR_UNROLL - 1) // FAR_UNROLL
    carry, js, tops = lax.fori_loop(0, (n_groups + 1) // 2, step, (carry, pending[1], pending[2]))
    carry = absorb(carry, (pending_slot, js, tops))
    for r in range(HEADS_PER_STEP):
        _, acc = carry[r]
        o_ref[0, 0, r * HEAD_DIM:(r + 1) * HEAD_DIM, :] = acc[:HEAD_DIM] / acc[HEAD_DIM:HEAD_DIM + 1]


def _moba(rel_bias, qa_t, ka, va_t, kmean, bias_own, bias_prev):
    batch, nb = qa_t.shape[0], qa_t.shape[1]
    pair = HEADS_PER_STEP * HEAD_DIM
    return pl.pallas_call(
        _moba_body,
        grid=(batch, N_HEADS_MOBA // HEADS_PER_STEP, nb),
        in_specs=[pl.BlockSpec(memory_space=pltpu.SMEM),
                  pl.BlockSpec((1, 1, pair, MOBA_BLOCK), lambda b, g, i: (b, i, g, 0)),
                  pl.BlockSpec((1, nb, MOBA_BLOCK, pair), lambda b, g, i: (b, 0, 0, g)),
                  pl.BlockSpec((1, nb, pair, MOBA_BLOCK), lambda b, g, i: (b, 0, g, 0)),
                  pl.BlockSpec((1, nb, pair), lambda b, g, i: (b, 0, g)),
                  pl.BlockSpec((HEADS_PER_STEP, MOBA_BLOCK, MOBA_BLOCK), lambda b, g, i: (g, 0, 0)),
                  pl.BlockSpec((HEADS_PER_STEP, MOBA_BLOCK, MOBA_BLOCK), lambda b, g, i: (g, 0, 0))],
        out_specs=pl.BlockSpec((1, 1, pair, MOBA_BLOCK), lambda b, g, i: (b, i, g, 0)),
        out_shape=jax.ShapeDtypeStruct((batch, nb, W_MOBA, MOBA_BLOCK), _F32),
        scratch_shapes=[pltpu.VMEM((HEADS_PER_STEP, nb, MOBA_BLOCK), _F32),
                        pltpu.VMEM((2, HEADS_PER_STEP, FAR_UNROLL, MOBA_BLOCK, MOBA_BLOCK), _F32)],
        compiler_params=_params(3),
        name="moba",
    )(rel_bias, qa_t, ka, va_t, kmean, bias_own, bias_prev)


def _swa_body(sinks_ref, q_ref, k_prev_ref, k_own_ref, v_prev_ref, v_own_ref, bias_ref, o_ref):
    n = pl.program_id(1)
    no_prev = jnp.where(n == 0, NEG, 0.0)
    zeros = jnp.zeros((HEAD_DIM, SWA_BLOCK), _BF16)
    for kv in range(N_KV_SWA):
        heads = range(kv * SWA_GROUP, (kv + 1) * SWA_GROUP)
        kv_rows = slice(kv * HEAD_DIM, (kv + 1) * HEAD_DIM)
        q_pad = jnp.concatenate(
            [jnp.concatenate(
                [q_ref[0, 0, h * HEAD_DIM:(h + 1) * HEAD_DIM, :] if part == kv else zeros
                 for part in range(N_KV_SWA)], axis=0) for h in heads], axis=1)
        s_prev = _dot(k_prev_ref[0, 0], q_pad)
        s_own = _dot(k_own_ref[0, 0], q_pad)
        p_prev, p_own, inv_l = [], [], []
        for slot, h in enumerate(heads):
            cols = slice(slot * SWA_BLOCK, (slot + 1) * SWA_BLOCK)
            sp = s_prev[:, cols] + bias_ref[h, :SWA_BLOCK, :] + no_prev
            so = s_own[:, cols] + bias_ref[h, SWA_BLOCK:, :]
            sink = sinks_ref[h] * LOG2E
            m = jnp.maximum(jnp.maximum(jnp.max(sp, axis=0, keepdims=True),
                                        jnp.max(so, axis=0, keepdims=True)), sink)
            pp = jnp.exp2(sp - m)
            po = jnp.exp2(so - m)
            l = (jnp.sum(pp, axis=0, keepdims=True) + jnp.sum(po, axis=0, keepdims=True)
                 + jnp.exp2(sink - m))
            p_prev.append(pp.astype(_BF16))
            p_own.append(po.astype(_BF16))
            inv_l.append(1.0 / l)
        acc = (_dot(v_prev_ref[0, 0, kv_rows, :], jnp.concatenate(p_prev, axis=1))
               + _dot(v_own_ref[0, 0, kv_rows, :], jnp.concatenate(p_own, axis=1)))
        for slot, h in enumerate(heads):
            cols = slice(slot * SWA_BLOCK, (slot + 1) * SWA_BLOCK)
            o_ref[0, 0, h * HEAD_DIM:(h + 1) * HEAD_DIM, :] = acc[:, cols] * inv_l[slot]


def _swa(sinks, qs_t, ks, vs_t, bias_swa):
    batch, nb = qs_t.shape[0], qs_t.shape[1]

    def own(rows, cols):
        return pl.BlockSpec((1, 1, rows, cols), lambda b, n: (b, n, 0, 0))

    def prev(rows, cols):
        return pl.BlockSpec((1, 1, rows, cols), lambda b, n: (b, jnp.maximum(n - 1, 0), 0, 0))

    return pl.pallas_call(
        _swa_body,
        grid=(batch, nb),
        in_specs=[pl.BlockSpec(memory_space=pltpu.SMEM),
                  own(W_SWA, SWA_BLOCK),
                  prev(SWA_BLOCK, W_SWA_KV), own(SWA_BLOCK, W_SWA_KV),
                  prev(W_SWA_KV, SWA_BLOCK), own(W_SWA_KV, SWA_BLOCK),
                  _const_spec((N_HEADS_SWA, 2 * SWA_BLOCK, SWA_BLOCK))],
        out_specs=own(W_SWA, SWA_BLOCK),
        out_shape=jax.ShapeDtypeStruct((batch, nb, W_SWA, SWA_BLOCK), _F32),
        compiler_params=_params(2),
        name="swa",
    )(sinks, qs_t, ks, ks, vs_t, vs_t, bias_swa)


def _group_norm_rows(o_ref, g_ref):
    rows = []
    for t in range(o_ref.shape[1]):
        o_t = o_ref[0, t]
        scale = lax.rsqrt(jnp.mean(o_t * o_t, axis=0, keepdims=True) + RMS_EPS)
        rows.append(((o_t * scale).T * g_ref[...]).astype(_BF16))
    return jnp.concatenate(rows, axis=0)


def _out_proj_body(x_ref, oa_ref, ob_ref, ga_ref, gb_ref, wa_ref, wb_ref, post_g_ref, o_ref):
    y = (_dot(_group_norm_rows(oa_ref, ga_ref), wa_ref[...])
         + _dot(_group_norm_rows(ob_ref, gb_ref), wb_ref[...]))
    o_ref[0] = x_ref[0] + _rms_rows(y, post_g_ref[...])


def _out_proj(x, oa_t, ob_t, ga, gb, wa, wb, post_g):
    batch, seq, _ = x.shape
    tile = pl.BlockSpec((1, TOKEN_TILE, D_MODEL), lambda b, i: (b, i, 0))
    return pl.pallas_call(
        _out_proj_body,
        grid=(batch, seq // TOKEN_TILE),
        in_specs=[tile,
                  pl.BlockSpec((1, _MOBA_PER_TILE, W_MOBA, MOBA_BLOCK), lambda b, i: (b, i, 0, 0)),
                  pl.BlockSpec((1, _SWA_PER_TILE, W_SWA, SWA_BLOCK), lambda b, i: (b, i, 0, 0)),
                  _const_spec((1, W_MOBA)), _const_spec((1, W_SWA)),
                  _const_spec((W_MOBA, D_MODEL)), _const_spec((W_SWA, D_MODEL)),
                  _const_spec((1, D_MODEL))],
        out_specs=tile,
        out_shape=jax.ShapeDtypeStruct((batch, seq, D_MODEL), _F32),
        compiler_params=_params(2),
        name="out_proj",
    )(x, oa_t, ob_t, ga, gb, wa, wb, post_g)


def _row(v):
    return v.reshape(1, -1)


def _layer(x, ffn1, mix, ffn2, rel_bias, bias_tiles):
    batch, seq, _ = x.shape
    (mix_pre_g, w_in, moba_out_g, swa_sinks, swa_out_g, w_out, mix_post_g) = mix
    bias_own, bias_prev, bias_swa = bias_tiles

    def ffn(x, params):
        pre_g, w_gate, w_up, w_down, post_g = params
        return _ffn(x.reshape(batch * seq, D_MODEL), _row(pre_g), w_gate.astype(_BF16),
                    w_up.astype(_BF16), w_down.astype(_BF16), _row(post_g)).reshape(x.shape)

    x = ffn(x, ffn1)

    qa, ka, va, qs, ks, vs = jnp.split(
        w_in, [W_MOBA, 2 * W_MOBA, 3 * W_MOBA, 3 * W_MOBA + W_SWA, 3 * W_MOBA + W_SWA + W_SWA_KV], axis=1)
    w_rows = jnp.concatenate([ka, ks], axis=1).astype(_BF16)
    w_t = jnp.concatenate([qa, va, qs, vs], axis=1).T.astype(_BF16)
    qa_t, va_t, qs_t, vs_t, ka_b, ks_b, kmean = _in_proj(x, _row(mix_pre_g), w_rows, w_t)

    oa_t = _moba(rel_bias, qa_t, ka_b, va_t, kmean.reshape(batch, seq // MOBA_BLOCK, W_MOBA),
                 bias_own, bias_prev)
    ob_t = _swa(swa_sinks, qs_t, ks_b, vs_t, bias_swa)

    w_out = w_out.astype(_BF16)
    x = _out_proj(x, oa_t, ob_t, _row(moba_out_g), _row(swa_out_g),
                  w_out[:W_MOBA], w_out[W_MOBA:], _row(mix_post_g))
    return ffn(x, ffn2)


def kernel(x, ffn1_pre_g, ffn1_w_gate, ffn1_w_up, ffn1_w_down, ffn1_post_g, mix_pre_g, w_in, rel_bias,
           moba_out_g, swa_sinks, swa_out_g, w_out, mix_post_g, ffn2_pre_g, ffn2_w_gate, ffn2_w_up,
           ffn2_w_down, ffn2_post_g):
    bias_tiles = _bias_tiles(rel_bias)
    for l in range(ffn1_pre_g.shape[0]):
        x = _layer(
            x,
            (ffn1_pre_g[l], ffn1_w_gate[l], ffn1_w_up[l], ffn1_w_down[l], ffn1_post_g[l]),
            (mix_pre_g[l], w_in[l], moba_out_g[l], swa_sinks[l], swa_out_g[l], w_out[l], mix_post_g[l]),
            (ffn2_pre_g[l], ffn2_w_gate[l], ffn2_w_up[l], ffn2_w_down[l], ffn2_post_g[l]),
            rel_bias, bias_tiles)
    return x
```

```python
import functools
import math

import jax
import jax.numpy as jnp
from jax import lax
from jax.experimental import pallas as pl
from jax.experimental.pallas import tpu as pltpu

D_MODEL = 1024
HEAD_DIM = 64
N_HEADS_MOBA = 8
N_HEADS_SWA = 8
N_KV_SWA = 2
SWA_GROUP = N_HEADS_SWA // N_KV_SWA
W_MOBA = N_HEADS_MOBA * HEAD_DIM
W_SWA = N_HEADS_SWA * HEAD_DIM
W_SWA_KV = N_KV_SWA * HEAD_DIM
MOBA_BLOCK = 256
MOBA_TOPK = 3
SWA_BLOCK = 128
SWA_WINDOW = 128
NUM_BUCKETS = 32
MAX_DISTANCE = 128
D_FF = 2816
RMS_EPS = 1e-6
FFN_RES_WEIGHT = 0.5
NEG = -1e30
QK_SCALE = HEAD_DIM ** -0.5
LOG2E = math.log2(math.e)
Q_FOLD = QK_SCALE * LOG2E
SUM_ROWS = 16

TOKEN_TILE = 512
FF_CHUNK = 256
HEADS_PER_STEP = 2
FAR_UNROLL = 2
SWA_Q_PER_STEP = 4
VMEM_LIMIT_BYTES = 56 * 1024 * 1024

_BF16 = jnp.bfloat16
_F32 = jnp.float32


def _dot(a, b):
    return jnp.dot(a, b, preferred_element_type=_F32)


def _dot_nt(a, b):
    return lax.dot_general(a, b, (((1,), (1,)), ((), ())), preferred_element_type=_F32)


def _rms_rows(x, g):
    return x * lax.rsqrt(jnp.mean(x * x, axis=-1, keepdims=True) + RMS_EPS) * g


def _const_spec(shape):
    return pl.BlockSpec(shape, lambda *_: (0,) * len(shape), pipeline_mode=pl.Buffered(1))


def _params(n_axes):
    return pltpu.CompilerParams(
        dimension_semantics=("arbitrary",) * n_axes, vmem_limit_bytes=VMEM_LIMIT_BYTES)


def _ffn_tile(x, pre_g_ref, wg_ref, wu_ref, wd_ref, post_g_ref, h_ref):
    xn = _rms_rows(x, pre_g_ref[...]).astype(_BF16)
    for c in range(D_FF // FF_CHUNK):
        cols = slice(c * FF_CHUNK, (c + 1) * FF_CHUNK)
        gate = _dot(xn, wg_ref[:, cols])
        up = _dot(xn, wu_ref[:, cols])
        h_ref[:, cols] = (jax.nn.silu(gate) * up).astype(_BF16)
    y = _dot(h_ref[...], wd_ref[...])
    return x + FFN_RES_WEIGHT * _rms_rows(y, post_g_ref[...])


def _ffn_body(x_ref, pre_g_ref, wg_ref, wu_ref, wd_ref, post_g_ref, o_ref, h_ref):
    o_ref[...] = _ffn_tile(x_ref[...], pre_g_ref, wg_ref, wu_ref, wd_ref, post_g_ref, h_ref)


_TOKEN_TILE_SPEC = pl.BlockSpec((TOKEN_TILE, D_MODEL), lambda i: (i, 0))
_FFN_WEIGHT_SPECS = [_const_spec((1, D_MODEL)), _const_spec((D_MODEL, D_FF)), _const_spec((D_MODEL, D_FF)),
                     _const_spec((D_FF, D_MODEL)), _const_spec((1, D_MODEL))]


def _ffn(x2d, ffn_params):
    n_tok = x2d.shape[0]
    return pl.pallas_call(
        _ffn_body,
        grid=(n_tok // TOKEN_TILE,),
        in_specs=[_TOKEN_TILE_SPEC] + _FFN_WEIGHT_SPECS,
        out_specs=_TOKEN_TILE_SPEC,
        out_shape=jax.ShapeDtypeStruct((n_tok, D_MODEL), _F32),
        scratch_shapes=[pltpu.VMEM((TOKEN_TILE, D_FF), _BF16)],
        compiler_params=_params(1),
        name="ffn",
    )(x2d, *ffn_params)


_T_QA, _T_VA, _T_QS, _T_VS = 0, W_MOBA, 2 * W_MOBA, 2 * W_MOBA + W_SWA
_T_ROWS = 2 * W_MOBA + W_SWA + W_SWA_KV
_MOBA_PER_TILE = TOKEN_TILE // MOBA_BLOCK
_SWA_PER_TILE = TOKEN_TILE // SWA_BLOCK


def _in_proj_body(x_ref, g_ref, w_rows_ref, w_t_ref,
                  qa_ref, va_ref, qs_ref, vs_ref, ka_ref, ks_ref, kmean_ref):
    xn = _rms_rows(x_ref[0], g_ref[...]).astype(_BF16)
    k_rows = _dot(xn, w_rows_ref[...])
    proj_t = _dot_nt(w_t_ref[...], xn)
    for t in range(_MOBA_PER_TILE):
        rows = slice(t * MOBA_BLOCK, (t + 1) * MOBA_BLOCK)
        k_blk = k_rows[rows, :W_MOBA]
        ka_ref[0, t] = k_blk.astype(_BF16)
        kmean_ref[0, t] = jnp.mean(k_blk, axis=0, keepdims=True)
        qa_ref[0, t] = (proj_t[_T_QA:_T_QA + W_MOBA, rows] * Q_FOLD).astype(_BF16)
        va_ref[0, t] = proj_t[_T_VA:_T_VA + W_MOBA, rows].astype(_BF16)
    for t in range(_SWA_PER_TILE):
        rows = slice(t * SWA_BLOCK, (t + 1) * SWA_BLOCK)
        ks_ref[0, t] = k_rows[rows, W_MOBA:].astype(_BF16)
        qs_ref[0, t] = (proj_t[_T_QS:_T_QS + W_SWA, rows] * Q_FOLD).astype(_BF16)
        vs_ref[0, t] = proj_t[_T_VS:_T_VS + W_SWA_KV, rows].astype(_BF16)


def _in_proj(x, g, w_rows, w_t):
    batch, seq, _ = x.shape
    nb_moba, nb_swa = seq // MOBA_BLOCK, seq // SWA_BLOCK

    def blocked(n_per_tile, rows, cols):
        return pl.BlockSpec((1, n_per_tile, rows, cols), lambda b, i: (b, i, 0, 0))

    return pl.pallas_call(
        _in_proj_body,
        grid=(batch, seq // TOKEN_TILE),
        in_specs=[pl.BlockSpec((1, TOKEN_TILE, D_MODEL), lambda b, i: (b, i, 0)),
                  _const_spec((1, D_MODEL)),
                  _const_spec((D_MODEL, W_MOBA + W_SWA_KV)),
                  _const_spec((_T_ROWS, D_MODEL))],
        out_specs=[blocked(_MOBA_PER_TILE, W_MOBA, MOBA_BLOCK),
                   blocked(_MOBA_PER_TILE, W_MOBA, MOBA_BLOCK),
                   blocked(_SWA_PER_TILE, W_SWA, SWA_BLOCK),
                   blocked(_SWA_PER_TILE, W_SWA_KV, SWA_BLOCK),
                   blocked(_MOBA_PER_TILE, MOBA_BLOCK, W_MOBA),
                   blocked(_SWA_PER_TILE, SWA_BLOCK, W_SWA_KV),
                   blocked(_MOBA_PER_TILE, 1, W_MOBA)],
        out_shape=[jax.ShapeDtypeStruct((batch, nb_moba, W_MOBA, MOBA_BLOCK), _BF16),
                   jax.ShapeDtypeStruct((batch, nb_moba, W_MOBA, MOBA_BLOCK), _BF16),
                   jax.ShapeDtypeStruct((batch, nb_swa, W_SWA, SWA_BLOCK), _BF16),
                   jax.ShapeDtypeStruct((batch, nb_swa, W_SWA_KV, SWA_BLOCK), _BF16),
                   jax.ShapeDtypeStruct((batch, nb_moba, MOBA_BLOCK, W_MOBA), _BF16),
                   jax.ShapeDtypeStruct((batch, nb_swa, SWA_BLOCK, W_SWA_KV), _BF16),
                   jax.ShapeDtypeStruct((batch, nb_moba, 1, W_MOBA), _F32)],
        compiler_params=_params(2),
        name="in_proj",
    )(x, g, w_rows, w_t)


def _t5_bucket(dist):
    n = jnp.maximum(dist, 0)
    max_exact = NUM_BUCKETS // 2
    nf = jnp.maximum(n, 1).astype(_F32)
    large = max_exact + (jnp.log(nf / max_exact) / math.log(MAX_DISTANCE / max_exact)
                         * (NUM_BUCKETS - max_exact)).astype(jnp.int32)
    large = jnp.minimum(large, NUM_BUCKETS - 1)
    return jnp.where(n < max_exact, n, large)


def _bias_lookup(rel_bias_ref, bucket, head):
    val = jnp.full(bucket.shape, rel_bias_ref[NUM_BUCKETS - 1, head], _F32)
    for b in range(NUM_BUCKETS - 2, -1, -1):
        val = jnp.where(bucket == b, rel_bias_ref[b, head], val)
    return val


def _bias_tiles_body(rel_bias_ref, own_ref, prev_ref, swa_ref):
    h = pl.program_id(0)
    key = lax.broadcasted_iota(jnp.int32, (MOBA_BLOCK, MOBA_BLOCK), 0)
    qry = lax.broadcasted_iota(jnp.int32, (MOBA_BLOCK, MOBA_BLOCK), 1)
    dist = qry - key
    own_ref[0] = jnp.where(dist >= 0, _bias_lookup(rel_bias_ref, _t5_bucket(dist), h) * LOG2E, NEG)
    prev_ref[0] = _bias_lookup(rel_bias_ref, _t5_bucket(dist + MOBA_BLOCK), h) * LOG2E
    key = lax.broadcasted_iota(jnp.int32, (2 * SWA_BLOCK, SWA_BLOCK), 0) - SWA_BLOCK
    qry = lax.broadcasted_iota(jnp.int32, (2 * SWA_BLOCK, SWA_BLOCK), 1)
    dist = qry - key
    in_win = (dist >= 0) & (dist < SWA_WINDOW)
    swa_ref[0] = jnp.where(
        in_win, _bias_lookup(rel_bias_ref, _t5_bucket(dist), h + N_HEADS_MOBA) * LOG2E, NEG)


def _bias_tiles(rel_bias):
    def per_head(rows, cols):
        return pl.BlockSpec((1, rows, cols), lambda h: (h, 0, 0))

    return pl.pallas_call(
        _bias_tiles_body,
        grid=(N_HEADS_MOBA,),
        in_specs=[pl.BlockSpec(memory_space=pltpu.SMEM)],
        out_specs=[per_head(MOBA_BLOCK, MOBA_BLOCK), per_head(MOBA_BLOCK, MOBA_BLOCK),
                   per_head(2 * SWA_BLOCK, SWA_BLOCK)],
        out_shape=[jax.ShapeDtypeStruct((N_HEADS_MOBA, MOBA_BLOCK, MOBA_BLOCK), _F32),
                   jax.ShapeDtypeStruct((N_HEADS_MOBA, MOBA_BLOCK, MOBA_BLOCK), _F32),
                   jax.ShapeDtypeStruct((N_HEADS_SWA, 2 * SWA_BLOCK, SWA_BLOCK), _F32)],
        compiler_params=_params(1),
        name="bias_tiles",
    )(rel_bias)


def _moba_body(rel_bias_ref, q_ref, k_ref, v_ref, kmean_ref, own_ref, prev_ref, o_ref, far_ref, s_ref):
    g = pl.program_id(1)
    i = pl.program_id(2)
    nb = k_ref.shape[1]
    q_t = q_ref[0, 0]
    q_row = lax.broadcasted_iota(jnp.int32, q_t.shape, 0)
    km = kmean_ref[0]
    km_hi = km.astype(_BF16)
    km_lo = (km - km_hi.astype(_F32)).astype(_BF16)
    blk = lax.broadcasted_iota(jnp.int32, (nb, MOBA_BLOCK), 0)
    past = blk < i
    n_far = jnp.maximum(i - 1, 0)

    q_pads, prev_extra = [], []
    for r in range(HEADS_PER_STEP):
        q_pad = jnp.where((q_row >= r * HEAD_DIM) & (q_row < (r + 1) * HEAD_DIM), q_t, 0)
        q_pads.append(q_pad)

        gate = jnp.where(past, _dot(km_hi, q_pad) + _dot(km_lo, q_pad), NEG)
        sel = jnp.zeros(gate.shape, jnp.bool_)
        for _ in range(MOBA_TOPK):
            top = jnp.max(gate, axis=0, keepdims=True)
            first = jnp.min(jnp.where(gate == top, blk, nb), axis=0, keepdims=True)
            pick = blk == first
            sel = sel | pick
            gate = jnp.where(pick, -jnp.inf, gate)
        sel = sel & past

        far_bias = rel_bias_ref[NUM_BUCKETS - 1, g * HEADS_PER_STEP + r] * LOG2E
        far_ref[r] = jnp.where(sel & (blk < i - 1), far_bias, NEG)
        prev_neg = jnp.max(jnp.where(sel & (blk == i - 1), 0.0, NEG), axis=0, keepdims=True)
        prev_extra.append(prev_ref[r] + prev_neg)

    def score(slot, blocks):
        tops = []
        for r in range(HEADS_PER_STEP):
            top = []
            for u, (j, extra) in enumerate(blocks):
                s = _dot(k_ref[0, j], q_pads[r]) + extra[r]
                s_ref[slot, r, u] = s
                top.append(jnp.max(s, axis=0, keepdims=True))
            tops.append(functools.reduce(jnp.maximum, top))
        return slot, tuple(j for j, _ in blocks), tuple(tops)

    ones = jnp.ones((SUM_ROWS, MOBA_BLOCK), _BF16)

    def absorb(carry, scored):
        slot, js, tops = scored
        out = []
        for r in range(HEADS_PER_STEP):
            lanes = slice(r * HEAD_DIM, (r + 1) * HEAD_DIM)
            m, acc = carry[r]
            m_new = jnp.maximum(m, tops[r])
            pv = functools.reduce(jnp.add, [
                _dot(jnp.concatenate([v_ref[0, j, lanes, :], ones], axis=0),
                     jnp.exp2(s_ref[slot, r, u] - m_new).astype(_BF16)) for u, j in enumerate(js)])
            out.append((m_new, jnp.exp2(m - m_new) * acc + pv))
        return tuple(out)

    def far_blocks(grp):
        blocks = []
        for u in range(FAR_UNROLL):
            j = jnp.minimum(grp * FAR_UNROLL + u, nb - 1)
            blocks.append((j, [far_ref[r, pl.ds(j, 1), :] for r in range(HEADS_PER_STEP)]))
        return blocks

    near = [(i, [own_ref[r] for r in range(HEADS_PER_STEP)]),
            (jnp.maximum(i - 1, 0), [prev_extra[r] for r in range(HEADS_PER_STEP)])]
    assert len(near) % FAR_UNROLL == 0
    carry = tuple((jnp.full((1, MOBA_BLOCK), -jnp.inf, _F32),
                   jnp.zeros((HEAD_DIM + SUM_ROWS, MOBA_BLOCK), _F32)) for _ in range(HEADS_PER_STEP))
    pending, slot = None, 0
    for c in range(0, len(near), FAR_UNROLL):
        scored = score(slot, near[c:c + FAR_UNROLL])
        if pending is not None:
            carry = absorb(carry, pending)
        pending, slot = scored, 1 - slot
    pending_slot = pending[0]

    def step(pair, state):
        carry, js, tops = state
        pending = (pending_slot, js, tops)
        for half in range(2):
            scored = score(1 - pending[0], far_blocks(2 * pair + half))
            carry = absorb(carry, pending)
            pending = scored
        return carry, pending[1], pending[2]

    n_groups = (n_far + FAR_UNROLL - 1) // FAR_UNROLL
    carry, js, tops = lax.fori_loop(0, (n_groups + 1) // 2, step, (carry, pending[1], pending[2]))
    carry = absorb(carry, (pending_slot, js, tops))
    for r in range(HEADS_PER_STEP):
        _, acc = carry[r]
        o_ref[0, 0, r * HEAD_DIM:(r + 1) * HEAD_DIM, :] = acc[:HEAD_DIM] / acc[HEAD_DIM:HEAD_DIM + 1]


def _moba(rel_bias, qa_t, ka, va_t, kmean, bias_own, bias_prev):
    batch, nb = qa_t.shape[0], qa_t.shape[1]
    pair = HEADS_PER_STEP * HEAD_DIM
    return pl.pallas_call(
        _moba_body,
        grid=(batch, N_HEADS_MOBA // HEADS_PER_STEP, nb),
        in_specs=[pl.BlockSpec(memory_space=pltpu.SMEM),
                  pl.BlockSpec((1, 1, pair, MOBA_BLOCK), lambda b, g, i: (b, i, g, 0)),
                  pl.BlockSpec((1, nb, MOBA_BLOCK, pair), lambda b, g, i: (b, 0, 0, g)),
                  pl.BlockSpec((1, nb, pair, MOBA_BLOCK), lambda b, g, i: (b, 0, g, 0)),
                  pl.BlockSpec((1, nb, pair), lambda b, g, i: (b, 0, g)),
                  pl.BlockSpec((HEADS_PER_STEP, MOBA_BLOCK, MOBA_BLOCK), lambda b, g, i: (g, 0, 0)),
                  pl.BlockSpec((HEADS_PER_STEP, MOBA_BLOCK, MOBA_BLOCK), lambda b, g, i: (g, 0, 0))],
        out_specs=pl.BlockSpec((1, 1, pair, MOBA_BLOCK), lambda b, g, i: (b, i, g, 0)),
        out_shape=jax.ShapeDtypeStruct((batch, nb, W_MOBA, MOBA_BLOCK), _F32),
        scratch_shapes=[pltpu.VMEM((HEADS_PER_STEP, nb, MOBA_BLOCK), _F32),
                        pltpu.VMEM((2, HEADS_PER_STEP, FAR_UNROLL, MOBA_BLOCK, MOBA_BLOCK), _F32)],
        compiler_params=_params(3),
        name="moba",
    )(rel_bias, qa_t, ka, va_t, kmean, bias_own, bias_prev)


def _swa_body(sinks_ref, q_ref, k_prev_ref, k_cur_ref, v_prev_ref, v_cur_ref, bias_ref, o_ref, s_ref):
    no_prev = jnp.where(pl.program_id(1) == 0, NEG, 0.0)
    zeros = jnp.zeros((HEAD_DIM, SWA_BLOCK), _BF16)
    ones = jnp.ones((SUM_ROWS, SWA_BLOCK), _BF16)
    k_blocks = [k_prev_ref[0, 0]] + [k_cur_ref[0, t] for t in range(SWA_Q_PER_STEP)]
    v_blocks = [v_prev_ref[0, 0]] + [v_cur_ref[0, t] for t in range(SWA_Q_PER_STEP)]
    chains = [(t, kv) for t in range(SWA_Q_PER_STEP) for kv in range(N_KV_SWA)]

    for t, kv in chains:
        q_pad = jnp.concatenate(
            [jnp.concatenate(
                [q_ref[0, t, h * HEAD_DIM:(h + 1) * HEAD_DIM, :] if part == kv else zeros
                 for part in range(N_KV_SWA)], axis=0)
             for h in range(kv * SWA_GROUP, (kv + 1) * SWA_GROUP)], axis=1)
        s_ref[t, kv, 0] = _dot(k_blocks[t], q_pad)
        s_ref[t, kv, 1] = _dot(k_blocks[t + 1], q_pad)

    for t, kv in chains:
        heads = range(kv * SWA_GROUP, (kv + 1) * SWA_GROUP)
        kv_rows = slice(kv * HEAD_DIM, (kv + 1) * HEAD_DIM)
        p_prev, p_own, sink_p = [], [], []
        for slot, h in enumerate(heads):
            cols = slice(slot * SWA_BLOCK, (slot + 1) * SWA_BLOCK)
            sp = s_ref[t, kv, 0, :, cols] + bias_ref[h, :SWA_BLOCK, :]
            if t == 0:
                sp = sp + no_prev
            so = s_ref[t, kv, 1, :, cols] + bias_ref[h, SWA_BLOCK:, :]
            sink = sinks_ref[h] * LOG2E
            m = jnp.maximum(jnp.maximum(jnp.max(sp, axis=0, keepdims=True),
                                        jnp.max(so, axis=0, keepdims=True)), sink)
            p_prev.append(jnp.exp2(sp - m).astype(_BF16))
            p_own.append(jnp.exp2(so - m).astype(_BF16))
            sink_p.append(jnp.exp2(sink - m))
        acc = (_dot(jnp.concatenate([v_blocks[t][kv_rows, :], ones], axis=0), jnp.concatenate(p_prev, axis=1))
               + _dot(jnp.concatenate([v_blocks[t + 1][kv_rows, :], ones], axis=0),
                      jnp.concatenate(p_own, axis=1)))
        for slot, h in enumerate(heads):
            cols = slice(slot * SWA_BLOCK, (slot + 1) * SWA_BLOCK)
            l = acc[HEAD_DIM:HEAD_DIM + 1, cols] + sink_p[slot]
            o_ref[0, t, h * HEAD_DIM:(h + 1) * HEAD_DIM, :] = acc[:HEAD_DIM, cols] / l


def _swa(sinks, qs_t, ks, vs_t, bias_swa):
    batch, nb = qs_t.shape[0], qs_t.shape[1]

    def cur(rows, cols):
        return pl.BlockSpec((1, SWA_Q_PER_STEP, rows, cols), lambda b, n: (b, n, 0, 0))

    def prev(rows, cols):
        return pl.BlockSpec((1, 1, rows, cols),
                            lambda b, n: (b, jnp.maximum(n * SWA_Q_PER_STEP - 1, 0), 0, 0))

    return pl.pallas_call(
        _swa_body,
        grid=(batch, nb // SWA_Q_PER_STEP),
        in_specs=[pl.BlockSpec(memory_space=pltpu.SMEM),
                  cur(W_SWA, SWA_BLOCK),
                  prev(SWA_BLOCK, W_SWA_KV), cur(SWA_BLOCK, W_SWA_KV),
                  prev(W_SWA_KV, SWA_BLOCK), cur(W_SWA_KV, SWA_BLOCK),
                  _const_spec((N_HEADS_SWA, 2 * SWA_BLOCK, SWA_BLOCK))],
        out_specs=cur(W_SWA, SWA_BLOCK),
        out_shape=jax.ShapeDtypeStruct((batch, nb, W_SWA, SWA_BLOCK), _F32),
        scratch_shapes=[pltpu.VMEM((SWA_Q_PER_STEP, N_KV_SWA, 2, SWA_BLOCK, SWA_GROUP * SWA_BLOCK), _F32)],
        compiler_params=_params(2),
        name="swa",
    )(sinks, qs_t, ks, ks, vs_t, vs_t, bias_swa)


def _group_norm_rows(o_ref, g_ref):
    rows = []
    for t in range(o_ref.shape[0]):
        o_t = o_ref[t]
        scale = lax.rsqrt(jnp.mean(o_t * o_t, axis=0, keepdims=True) + RMS_EPS)
        rows.append(((o_t * scale).T * g_ref[...]).astype(_BF16))
    return jnp.concatenate(rows, axis=0)


def _mix_ffn_body(x_ref, oa_ref, ob_ref, ga_ref, gb_ref, wa_ref, wb_ref, mix_post_g_ref,
                  pre_g_ref, wg_ref, wu_ref, wd_ref, post_g_ref, o_ref, h_ref):
    y = (_dot(_group_norm_rows(oa_ref, ga_ref), wa_ref[...])
         + _dot(_group_norm_rows(ob_ref, gb_ref), wb_ref[...]))
    x = x_ref[...] + _rms_rows(y, mix_post_g_ref[...])
    o_ref[...] = _ffn_tile(x, pre_g_ref, wg_ref, wu_ref, wd_ref, post_g_ref, h_ref)


def _mix_ffn(x2d, oa_t, ob_t, ga, gb, wa, wb, mix_post_g, ffn_params):
    n_tok = x2d.shape[0]
    return pl.pallas_call(
        _mix_ffn_body,
        grid=(n_tok // TOKEN_TILE,),
        in_specs=[_TOKEN_TILE_SPEC,
                  pl.BlockSpec((_MOBA_PER_TILE, W_MOBA, MOBA_BLOCK), lambda i: (i, 0, 0)),
                  pl.BlockSpec((_SWA_PER_TILE, W_SWA, SWA_BLOCK), lambda i: (i, 0, 0)),
                  _const_spec((1, W_MOBA)), _const_spec((1, W_SWA)),
                  _const_spec((W_MOBA, D_MODEL)), _const_spec((W_SWA, D_MODEL)),
                  _const_spec((1, D_MODEL))] + _FFN_WEIGHT_SPECS,
        out_specs=_TOKEN_TILE_SPEC,
        out_shape=jax.ShapeDtypeStruct((n_tok, D_MODEL), _F32),
        scratch_shapes=[pltpu.VMEM((TOKEN_TILE, D_FF), _BF16)],
        compiler_params=_params(1),
        name="mix_ffn",
    )(x2d, oa_t, ob_t, ga, gb, wa, wb, mix_post_g, *ffn_params)


def _row(v):
    return v.reshape(1, -1)


def _layer(x, ffn1, mix, ffn2, rel_bias, bias_tiles):
    batch, seq, _ = x.shape
    (mix_pre_g, w_in, moba_out_g, swa_sinks, swa_out_g, w_out, mix_post_g) = mix
    bias_own, bias_prev, bias_swa = bias_tiles

    def ffn_params(params):
        pre_g, w_gate, w_up, w_down, post_g = params
        return (_row(pre_g), w_gate.astype(_BF16), w_up.astype(_BF16), w_down.astype(_BF16), _row(post_g))

    x = _ffn(x.reshape(batch * seq, D_MODEL), ffn_params(ffn1))

    qa, ka, va, qs, ks, vs = jnp.split(
        w_in, [W_MOBA, 2 * W_MOBA, 3 * W_MOBA, 3 * W_MOBA + W_SWA, 3 * W_MOBA + W_SWA + W_SWA_KV], axis=1)
    w_rows = jnp.concatenate([ka, ks], axis=1).astype(_BF16)
    w_t = jnp.concatenate([qa, va, qs, vs], axis=1).T.astype(_BF16)
    qa_t, va_t, qs_t, vs_t, ka_b, ks_b, kmean = _in_proj(
        x.reshape(batch, seq, D_MODEL), _row(mix_pre_g), w_rows, w_t)

    oa_t = _moba(rel_bias, qa_t, ka_b, va_t, kmean.reshape(batch, seq // MOBA_BLOCK, W_MOBA),
                 bias_own, bias_prev)
    ob_t = _swa(swa_sinks, qs_t, ks_b, vs_t, bias_swa)

    w_out = w_out.astype(_BF16)
    x = _mix_ffn(x, oa_t.reshape(-1, W_MOBA, MOBA_BLOCK), ob_t.reshape(-1, W_SWA, SWA_BLOCK),
                 _row(moba_out_g), _row(swa_out_g), w_out[:W_MOBA], w_out[W_MOBA:], _row(mix_post_g),
                 ffn_params(ffn2))
    return x.reshape(batch, seq, D_MODEL)


def kernel(x, ffn1_pre_g, ffn1_w_gate, ffn1_w_up, ffn1_w_down, ffn1_post_g, mix_pre_g, w_in, rel_bias,
           moba_out_g, swa_sinks, swa_out_g, w_out, mix_post_g, ffn2_pre_g, ffn2_w_gate, ffn2_w_up,
           ffn2_w_down, ffn2_post_g):
    bias_tiles = _bias_tiles(rel_bias)
    for l in range(ffn1_pre_g.shape[0]):
        x = _layer(
            x,
            (ffn1_pre_g[l], ffn1_w_gate[l], ffn1_w_up[l], ffn1_w_down[l], ffn1_post_g[l]),
            (mix_pre_g[l], w_in[l], moba_out_g[l], swa_sinks[l], swa_out_g[l], w_out[l], mix_post_g[l]),
            (ffn2_pre_g[l], ffn2_w_gate[l], ffn2_w_up[l], ffn2_w_down[l], ffn2_post_g[l]),
            rel_bias, bias_tiles)
    return x
```

```python
import functools
import math

import jax
import jax.numpy as jnp
from jax import lax
from jax.experimental import pallas as pl
from jax.experimental.pallas import tpu as pltpu

D_MODEL = 1024
HEAD_DIM = 64
N_HEADS_MOBA = 8
N_HEADS_SWA = 8
N_KV_SWA = 2
SWA_GROUP = N_HEADS_SWA // N_KV_SWA
W_MOBA = N_HEADS_MOBA * HEAD_DIM
W_SWA = N_HEADS_SWA * HEAD_DIM
W_SWA_KV = N_KV_SWA * HEAD_DIM
MOBA_BLOCK = 256
MOBA_TOPK = 3
SWA_BLOCK = 128
SWA_WINDOW = 128
NUM_BUCKETS = 32
MAX_DISTANCE = 128
D_FF = 2816
RMS_EPS = 1e-6
FFN_RES_WEIGHT = 0.5
NEG = -1e30
QK_SCALE = HEAD_DIM ** -0.5
LOG2E = math.log2(math.e)
Q_FOLD = QK_SCALE * LOG2E
SUM_ROWS = 16

TOKEN_TILE = 512
FF_CHUNK = 256
HEADS_PER_STEP = 2
FAR_UNROLL = 1
PIPE_DEPTH = 2
GROUPS_PER_ITER = 4
SWA_Q_PER_STEP = 4
VMEM_LIMIT_BYTES = 56 * 1024 * 1024

_BF16 = jnp.bfloat16
_F32 = jnp.float32


def _dot(a, b):
    return jnp.dot(a, b, preferred_element_type=_F32)


def _dot_nt(a, b):
    return lax.dot_general(a, b, (((1,), (1,)), ((), ())), preferred_element_type=_F32)


def _rms_rows(x, g):
    return x * lax.rsqrt(jnp.mean(x * x, axis=-1, keepdims=True) + RMS_EPS) * g


def _const_spec(shape):
    return pl.BlockSpec(shape, lambda *_: (0,) * len(shape), pipeline_mode=pl.Buffered(1))


def _params(n_axes):
    return pltpu.CompilerParams(
        dimension_semantics=("arbitrary",) * n_axes, vmem_limit_bytes=VMEM_LIMIT_BYTES)


def _ffn_tile(x, pre_g_ref, wg_ref, wu_ref, wd_ref, post_g_ref, h_ref):
    xn = _rms_rows(x, pre_g_ref[...]).astype(_BF16)
    for c in range(D_FF // FF_CHUNK):
        cols = slice(c * FF_CHUNK, (c + 1) * FF_CHUNK)
        gate = _dot(xn, wg_ref[:, cols])
        up = _dot(xn, wu_ref[:, cols])
        h_ref[:, cols] = (jax.nn.silu(gate) * up).astype(_BF16)
    y = _dot(h_ref[...], wd_ref[...])
    return x + FFN_RES_WEIGHT * _rms_rows(y, post_g_ref[...])


def _ffn_body(x_ref, pre_g_ref, wg_ref, wu_ref, wd_ref, post_g_ref, o_ref, h_ref):
    o_ref[...] = _ffn_tile(x_ref[...], pre_g_ref, wg_ref, wu_ref, wd_ref, post_g_ref, h_ref)


_TOKEN_TILE_SPEC = pl.BlockSpec((TOKEN_TILE, D_MODEL), lambda i: (i, 0))
_FFN_WEIGHT_SPECS = [_const_spec((1, D_MODEL)), _const_spec((D_MODEL, D_FF)), _const_spec((D_MODEL, D_FF)),
                     _const_spec((D_FF, D_MODEL)), _const_spec((1, D_MODEL))]


def _ffn(x2d, ffn_params):
    n_tok = x2d.shape[0]
    return pl.pallas_call(
        _ffn_body,
        grid=(n_tok // TOKEN_TILE,),
        in_specs=[_TOKEN_TILE_SPEC] + _FFN_WEIGHT_SPECS,
        out_specs=_TOKEN_TILE_SPEC,
        out_shape=jax.ShapeDtypeStruct((n_tok, D_MODEL), _F32),
        scratch_shapes=[pltpu.VMEM((TOKEN_TILE, D_FF), _BF16)],
        compiler_params=_params(1),
        name="ffn",
    )(x2d, *ffn_params)


_T_QA, _T_VA, _T_QS, _T_VS = 0, W_MOBA, 2 * W_MOBA, 2 * W_MOBA + W_SWA
_T_ROWS = 2 * W_MOBA + W_SWA + W_SWA_KV
_MOBA_PER_TILE = TOKEN_TILE // MOBA_BLOCK
_SWA_PER_TILE = TOKEN_TILE // SWA_BLOCK


def _in_proj_body(x_ref, g_ref, w_rows_ref, w_t_ref,
                  qa_ref, va_ref, qs_ref, vs_ref, ka_ref, ks_ref, kmean_ref):
    xn = _rms_rows(x_ref[0], g_ref[...]).astype(_BF16)
    k_rows = _dot(xn, w_rows_ref[...])
    proj_t = _dot_nt(w_t_ref[...], xn)
    for t in range(_MOBA_PER_TILE):
        rows = slice(t * MOBA_BLOCK, (t + 1) * MOBA_BLOCK)
        k_blk = k_rows[rows, :W_MOBA]
        ka_ref[0, t] = k_blk.astype(_BF16)
        kmean_ref[0, t] = jnp.mean(k_blk, axis=0, keepdims=True)
        qa_ref[0, t] = (proj_t[_T_QA:_T_QA + W_MOBA, rows] * Q_FOLD).astype(_BF16)
        va_ref[0, t] = proj_t[_T_VA:_T_VA + W_MOBA, rows].astype(_BF16)
    for t in range(_SWA_PER_TILE):
        rows = slice(t * SWA_BLOCK, (t + 1) * SWA_BLOCK)
        ks_ref[0, t] = k_rows[rows, W_MOBA:].astype(_BF16)
        qs_ref[0, t] = (proj_t[_T_QS:_T_QS + W_SWA, rows] * Q_FOLD).astype(_BF16)
        vs_ref[0, t] = proj_t[_T_VS:_T_VS + W_SWA_KV, rows].astype(_BF16)


def _in_proj(x, g, w_rows, w_t):
    batch, seq, _ = x.shape
    nb_moba, nb_swa = seq // MOBA_BLOCK, seq // SWA_BLOCK

    def blocked(n_per_tile, rows, cols):
        return pl.BlockSpec((1, n_per_tile, rows, cols), lambda b, i: (b, i, 0, 0))

    return pl.pallas_call(
        _in_proj_body,
        grid=(batch, seq // TOKEN_TILE),
        in_specs=[pl.BlockSpec((1, TOKEN_TILE, D_MODEL), lambda b, i: (b, i, 0)),
                  _const_spec((1, D_MODEL)),
                  _const_spec((D_MODEL, W_MOBA + W_SWA_KV)),
                  _const_spec((_T_ROWS, D_MODEL))],
        out_specs=[blocked(_MOBA_PER_TILE, W_MOBA, MOBA_BLOCK),
                   blocked(_MOBA_PER_TILE, W_MOBA, MOBA_BLOCK),
                   blocked(_SWA_PER_TILE, W_SWA, SWA_BLOCK),
                   blocked(_SWA_PER_TILE, W_SWA_KV, SWA_BLOCK),
                   blocked(_MOBA_PER_TILE, MOBA_BLOCK, W_MOBA),
                   blocked(_SWA_PER_TILE, SWA_BLOCK, W_SWA_KV),
                   blocked(_MOBA_PER_TILE, 1, W_MOBA)],
        out_shape=[jax.ShapeDtypeStruct((batch, nb_moba, W_MOBA, MOBA_BLOCK), _BF16),
                   jax.ShapeDtypeStruct((batch, nb_moba, W_MOBA, MOBA_BLOCK), _BF16),
                   jax.ShapeDtypeStruct((batch, nb_swa, W_SWA, SWA_BLOCK), _BF16),
                   jax.ShapeDtypeStruct((batch, nb_swa, W_SWA_KV, SWA_BLOCK), _BF16),
                   jax.ShapeDtypeStruct((batch, nb_moba, MOBA_BLOCK, W_MOBA), _BF16),
                   jax.ShapeDtypeStruct((batch, nb_swa, SWA_BLOCK, W_SWA_KV), _BF16),
                   jax.ShapeDtypeStruct((batch, nb_moba, 1, W_MOBA), _F32)],
        compiler_params=_params(2),
        name="in_proj",
    )(x, g, w_rows, w_t)


def _t5_bucket(dist):
    n = jnp.maximum(dist, 0)
    max_exact = NUM_BUCKETS // 2
    nf = jnp.maximum(n, 1).astype(_F32)
    large = max_exact + (jnp.log(nf / max_exact) / math.log(MAX_DISTANCE / max_exact)
                         * (NUM_BUCKETS - max_exact)).astype(jnp.int32)
    large = jnp.minimum(large, NUM_BUCKETS - 1)
    return jnp.where(n < max_exact, n, large)


def _bias_lookup(rel_bias_ref, bucket, head):
    val = jnp.full(bucket.shape, rel_bias_ref[NUM_BUCKETS - 1, head], _F32)
    for b in range(NUM_BUCKETS - 2, -1, -1):
        val = jnp.where(bucket == b, rel_bias_ref[b, head], val)
    return val


def _bias_tiles_body(rel_bias_ref, own_ref, prev_ref, swa_ref):
    h = pl.program_id(0)
    key = lax.broadcasted_iota(jnp.int32, (MOBA_BLOCK, MOBA_BLOCK), 0)
    qry = lax.broadcasted_iota(jnp.int32, (MOBA_BLOCK, MOBA_BLOCK), 1)
    dist = qry - key
    own_ref[0] = jnp.where(dist >= 0, _bias_lookup(rel_bias_ref, _t5_bucket(dist), h) * LOG2E, NEG)
    prev_ref[0] = _bias_lookup(rel_bias_ref, _t5_bucket(dist + MOBA_BLOCK), h) * LOG2E
    key = lax.broadcasted_iota(jnp.int32, (2 * SWA_BLOCK, SWA_BLOCK), 0) - SWA_BLOCK
    qry = lax.broadcasted_iota(jnp.int32, (2 * SWA_BLOCK, SWA_BLOCK), 1)
    dist = qry - key
    in_win = (dist >= 0) & (dist < SWA_WINDOW)
    swa_ref[0] = jnp.where(
        in_win, _bias_lookup(rel_bias_ref, _t5_bucket(dist), h + N_HEADS_MOBA) * LOG2E, NEG)


def _bias_tiles(rel_bias):
    def per_head(rows, cols):
        return pl.BlockSpec((1, rows, cols), lambda h: (h, 0, 0))

    return pl.pallas_call(
        _bias_tiles_body,
        grid=(N_HEADS_MOBA,),
        in_specs=[pl.BlockSpec(memory_space=pltpu.SMEM)],
        out_specs=[per_head(MOBA_BLOCK, MOBA_BLOCK), per_head(MOBA_BLOCK, MOBA_BLOCK),
                   per_head(2 * SWA_BLOCK, SWA_BLOCK)],
        out_shape=[jax.ShapeDtypeStruct((N_HEADS_MOBA, MOBA_BLOCK, MOBA_BLOCK), _F32),
                   jax.ShapeDtypeStruct((N_HEADS_MOBA, MOBA_BLOCK, MOBA_BLOCK), _F32),
                   jax.ShapeDtypeStruct((N_HEADS_SWA, 2 * SWA_BLOCK, SWA_BLOCK), _F32)],
        compiler_params=_params(1),
        name="bias_tiles",
    )(rel_bias)


def _moba_body(rel_bias_ref, q_ref, k_ref, v_ref, kmean_ref, own_ref, prev_ref, o_ref,
               far_ref, qpad_ref, s_ref):
    g = pl.program_id(1)
    nb = k_ref.shape[1]
    q_row = lax.broadcasted_iota(jnp.int32, (HEADS_PER_STEP * HEAD_DIM, MOBA_BLOCK), 0)
    km = kmean_ref[0]
    km_hi = km.astype(_BF16)
    km_lo = (km - km_hi.astype(_F32)).astype(_BF16)
    blk = lax.broadcasted_iota(jnp.int32, (nb, MOBA_BLOCK), 0)
    ones = jnp.ones((SUM_ROWS, MOBA_BLOCK), _BF16)
    ring_slots = range(GROUPS_PER_ITER)
    tail_slots = ring_slots[-PIPE_DEPTH:]
    near_slots = range(GROUPS_PER_ITER, GROUPS_PER_ITER + PIPE_DEPTH)
    assert FAR_UNROLL == 1 and PIPE_DEPTH == 2 and GROUPS_PER_ITER >= 2 * PIPE_DEPTH

    def prepare(i):
        q_t = q_ref[0, i]
        past = blk < i
        own, prev = [], []
        for r in range(HEADS_PER_STEP):
            q_pad = jnp.where((q_row >= r * HEAD_DIM) & (q_row < (r + 1) * HEAD_DIM), q_t, 0)
            qpad_ref[r] = q_pad

            gate = jnp.where(past, _dot(km_hi, q_pad) + _dot(km_lo, q_pad), NEG)
            sel = jnp.zeros(gate.shape, jnp.bool_)
            for _ in range(MOBA_TOPK):
                top = jnp.max(gate, axis=0, keepdims=True)
                first = jnp.min(jnp.where(gate == top, blk, nb), axis=0, keepdims=True)
                pick = blk == first
                sel = sel | pick
                gate = jnp.where(pick, -jnp.inf, gate)
            sel = sel & past

            far_bias = rel_bias_ref[NUM_BUCKETS - 1, g * HEADS_PER_STEP + r] * LOG2E
            far_ref[r] = jnp.where(sel & (blk < i - 1), far_bias, NEG)
            prev_neg = jnp.max(jnp.where(sel & (blk == i - 1), 0.0, NEG), axis=0, keepdims=True)
            own.append(own_ref[r])
            prev.append(prev_ref[r] + prev_neg)
        return [(i, own)], [(jnp.maximum(i - 1, 0), prev)]

    def score(slot, blocks):
        tops = []
        for r in range(HEADS_PER_STEP):
            top = []
            for u, (j, extra) in enumerate(blocks):
                s = _dot(k_ref[0, j], qpad_ref[r]) + extra[r]
                s_ref[slot, r, u] = s
                top.append(jnp.max(s, axis=0, keepdims=True))
            tops.append(functools.reduce(jnp.maximum, top))
        return slot, tuple(j for j, _ in blocks), tuple(tops)

    def absorb(carry, scored):
        slot, js, tops = scored
        out = []
        for r in range(HEADS_PER_STEP):
            lanes = slice(r * HEAD_DIM, (r + 1) * HEAD_DIM)
            m, acc = carry[r]
            m_new = jnp.maximum(m, tops[r])
            pv = functools.reduce(jnp.add, [
                _dot(jnp.concatenate([v_ref[0, j, lanes, :], ones], axis=0),
                     jnp.exp2(s_ref[slot, r, u] - m_new).astype(_BF16)) for u, j in enumerate(js)])
            out.append((m_new, jnp.exp2(m - m_new) * acc + pv))
        return tuple(out)

    def far_blocks(grp):
        j = jnp.minimum(grp, nb - 1)
        return [(j, [far_ref[r, pl.ds(j, 1), :] for r in range(HEADS_PER_STEP)])]

    def tile(i, near_tops):
        carry = tuple((jnp.full((1, MOBA_BLOCK), -jnp.inf, _F32),
                       jnp.zeros((HEAD_DIM + SUM_ROWS, MOBA_BLOCK), _F32)) for _ in range(HEADS_PER_STEP))
        near_js = ((i,), (jnp.maximum(i - 1, 0),))
        ring = {slot: (slot, js, tops) for slot, js, tops in zip(near_slots, near_js, near_tops)}
        for c, slot in enumerate(tail_slots):
            ring[slot] = score(slot, far_blocks(c))
            carry = absorb(carry, ring[near_slots[c]])

        def step(it, state):
            carry, pending = state
            ring = {slot: (slot,) + p for slot, p in zip(tail_slots, pending)}
            for k in ring_slots:
                scored = score(k, far_blocks(PIPE_DEPTH + it * GROUPS_PER_ITER + k))
                carry = absorb(carry, ring[(k - PIPE_DEPTH) % GROUPS_PER_ITER])
                ring[k] = scored
            return carry, tuple(ring[slot][1:] for slot in tail_slots)

        n_far = jnp.maximum(i - 1, 0)
        n_iters = (jnp.maximum(n_far - PIPE_DEPTH, 0) + GROUPS_PER_ITER - 1) // GROUPS_PER_ITER
        carry, pending = lax.fori_loop(
            0, n_iters, step, (carry, tuple(ring[slot][1:] for slot in tail_slots)))

        near_next = [score(slot, blocks)
                     for slot, blocks in zip(near_slots, prepare(jnp.minimum(i + 1, nb - 1)))]
        for slot, p in zip(tail_slots, pending):
            carry = absorb(carry, (slot,) + p)
        for r in range(HEADS_PER_STEP):
            _, acc = carry[r]
            o_ref[0, i, r * HEAD_DIM:(r + 1) * HEAD_DIM, :] = acc[:HEAD_DIM] / acc[HEAD_DIM:HEAD_DIM + 1]
        return tuple(tops for _, _, tops in near_next)

    near_first = [score(slot, blocks) for slot, blocks in zip(near_slots, prepare(0))]
    lax.fori_loop(0, nb, tile, tuple(tops for _, _, tops in near_first))


def _moba(rel_bias, qa_t, ka, va_t, kmean, bias_own, bias_prev):
    batch, nb = qa_t.shape[0], qa_t.shape[1]
    pair = HEADS_PER_STEP * HEAD_DIM
    q_spec = pl.BlockSpec((1, nb, pair, MOBA_BLOCK), lambda b, g: (b, 0, g, 0))
    bias_spec = pl.BlockSpec((HEADS_PER_STEP, MOBA_BLOCK, MOBA_BLOCK), lambda b, g: (g, 0, 0))
    return pl.pallas_call(
        _moba_body,
        grid=(batch, N_HEADS_MOBA // HEADS_PER_STEP),
        in_specs=[pl.BlockSpec(memory_space=pltpu.SMEM),
                  q_spec,
                  pl.BlockSpec((1, nb, MOBA_BLOCK, pair), lambda b, g: (b, 0, 0, g)),
                  q_spec,
                  pl.BlockSpec((1, nb, pair), lambda b, g: (b, 0, g)),
                  bias_spec, bias_spec],
        out_specs=q_spec,
        out_shape=jax.ShapeDtypeStruct((batch, nb, W_MOBA, MOBA_BLOCK), _F32),
        scratch_shapes=[pltpu.VMEM((HEADS_PER_STEP, nb, MOBA_BLOCK), _F32),
                        pltpu.VMEM((HEADS_PER_STEP, pair, MOBA_BLOCK), _BF16),
                        pltpu.VMEM((GROUPS_PER_ITER + PIPE_DEPTH, HEADS_PER_STEP, FAR_UNROLL,
                                    MOBA_BLOCK, MOBA_BLOCK), _F32)],
        compiler_params=_params(2),
        name="moba",
    )(rel_bias, qa_t, ka, va_t, kmean, bias_own, bias_prev)


def _swa_body(sinks_ref, q_ref, k_prev_ref, k_cur_ref, v_prev_ref, v_cur_ref, bias_ref, o_ref, s_ref):
    no_prev = jnp.where(pl.program_id(1) == 0, NEG, 0.0)
    zeros = jnp.zeros((HEAD_DIM, SWA_BLOCK), _BF16)
    ones = jnp.ones((SUM_ROWS, SWA_BLOCK), _BF16)
    k_blocks = [k_prev_ref[0, 0]] + [k_cur_ref[0, t] for t in range(SWA_Q_PER_STEP)]
    v_blocks = [v_prev_ref[0, 0]] + [v_cur_ref[0, t] for t in range(SWA_Q_PER_STEP)]
    chains = [(t, kv) for t in range(SWA_Q_PER_STEP) for kv in range(N_KV_SWA)]

    for t, kv in chains:
        q_pad = jnp.concatenate(
            [jnp.concatenate(
                [q_ref[0, t, h * HEAD_DIM:(h + 1) * HEAD_DIM, :] if part == kv else zeros
                 for part in range(N_KV_SWA)], axis=0)
             for h in range(kv * SWA_GROUP, (kv + 1) * SWA_GROUP)], axis=1)
        s_ref[t, kv, 0] = _dot(k_blocks[t], q_pad)
        s_ref[t, kv, 1] = _dot(k_blocks[t + 1], q_pad)

    for t, kv in chains:
        heads = range(kv * SWA_GROUP, (kv + 1) * SWA_GROUP)
        kv_rows = slice(kv * HEAD_DIM, (kv + 1) * HEAD_DIM)
        p_prev, p_own, sink_p = [], [], []
        for slot, h in enumerate(heads):
            cols = slice(slot * SWA_BLOCK, (slot + 1) * SWA_BLOCK)
            sp = s_ref[t, kv, 0, :, cols] + bias_ref[h, :SWA_BLOCK, :]
            if t == 0:
                sp = sp + no_prev
            so = s_ref[t, kv, 1, :, cols] + bias_ref[h, SWA_BLOCK:, :]
            sink = sinks_ref[h] * LOG2E
            m = jnp.maximum(jnp.maximum(jnp.max(sp, axis=0, keepdims=True),
                                        jnp.max(so, axis=0, keepdims=True)), sink)
            p_prev.append(jnp.exp2(sp - m).astype(_BF16))
            p_own.append(jnp.exp2(so - m).astype(_BF16))
            sink_p.append(jnp.exp2(sink - m))
        acc = (_dot(jnp.concatenate([v_blocks[t][kv_rows, :], ones], axis=0), jnp.concatenate(p_prev, axis=1))
               + _dot(jnp.concatenate([v_blocks[t + 1][kv_rows, :], ones], axis=0),
                      jnp.concatenate(p_own, axis=1)))
        for slot, h in enumerate(heads):
            cols = slice(slot * SWA_BLOCK, (slot + 1) * SWA_BLOCK)
            l = acc[HEAD_DIM:HEAD_DIM + 1, cols] + sink_p[slot]
            o_ref[0, t, h * HEAD_DIM:(h + 1) * HEAD_DIM, :] = acc[:HEAD_DIM, cols] / l


def _swa(sinks, qs_t, ks, vs_t, bias_swa):
    batch, nb = qs_t.shape[0], qs_t.shape[1]

    def cur(rows, cols):
        return pl.BlockSpec((1, SWA_Q_PER_STEP, rows, cols), lambda b, n: (b, n, 0, 0))

    def prev(rows, cols):
        return pl.BlockSpec((1, 1, rows, cols),
                            lambda b, n: (b, jnp.maximum(n * SWA_Q_PER_STEP - 1, 0), 0, 0))

    return pl.pallas_call(
        _swa_body,
        grid=(batch, nb // SWA_Q_PER_STEP),
        in_specs=[pl.BlockSpec(memory_space=pltpu.SMEM),
                  cur(W_SWA, SWA_BLOCK),
                  prev(SWA_BLOCK, W_SWA_KV), cur(SWA_BLOCK, W_SWA_KV),
                  prev(W_SWA_KV, SWA_BLOCK), cur(W_SWA_KV, SWA_BLOCK),
                  _const_spec((N_HEADS_SWA, 2 * SWA_BLOCK, SWA_BLOCK))],
        out_specs=cur(W_SWA, SWA_BLOCK),
        out_shape=jax.ShapeDtypeStruct((batch, nb, W_SWA, SWA_BLOCK), _F32),
        scratch_shapes=[pltpu.VMEM((SWA_Q_PER_STEP, N_KV_SWA, 2, SWA_BLOCK, SWA_GROUP * SWA_BLOCK), _F32)],
        compiler_params=_params(2),
        name="swa",
    )(sinks, qs_t, ks, ks, vs_t, vs_t, bias_swa)


def _group_norm_rows(o_ref, g_ref):
    rows = []
    for t in range(o_ref.shape[0]):
        o_t = o_ref[t]
        scale = lax.rsqrt(jnp.mean(o_t * o_t, axis=0, keepdims=True) + RMS_EPS)
        rows.append(((o_t * scale).T * g_ref[...]).astype(_BF16))
    return jnp.concatenate(rows, axis=0)


def _mix_ffn_body(x_ref, oa_ref, ob_ref, ga_ref, gb_ref, wa_ref, wb_ref, mix_post_g_ref,
                  pre_g_ref, wg_ref, wu_ref, wd_ref, post_g_ref, o_ref, h_ref):
    y = (_dot(_group_norm_rows(oa_ref, ga_ref), wa_ref[...])
         + _dot(_group_norm_rows(ob_ref, gb_ref), wb_ref[...]))
    x = x_ref[...] + _rms_rows(y, mix_post_g_ref[...])
    o_ref[...] = _ffn_tile(x, pre_g_ref, wg_ref, wu_ref, wd_ref, post_g_ref, h_ref)


def _mix_ffn(x2d, oa_t, ob_t, ga, gb, wa, wb, mix_post_g, ffn_params):
    n_tok = x2d.shape[0]
    return pl.pallas_call(
        _mix_ffn_body,
        grid=(n_tok // TOKEN_TILE,),
        in_specs=[_TOKEN_TILE_SPEC,
                  pl.BlockSpec((_MOBA_PER_TILE, W_MOBA, MOBA_BLOCK), lambda i: (i, 0, 0)),
                  pl.BlockSpec((_SWA_PER_TILE, W_SWA, SWA_BLOCK), lambda i: (i, 0, 0)),
                  _const_spec((1, W_MOBA)), _const_spec((1, W_SWA)),
                  _const_spec((W_MOBA, D_MODEL)), _const_spec((W_SWA, D_MODEL)),
                  _const_spec((1, D_MODEL))] + _FFN_WEIGHT_SPECS,
        out_specs=_TOKEN_TILE_SPEC,
        out_shape=jax.ShapeDtypeStruct((n_tok, D_MODEL), _F32),
        scratch_shapes=[pltpu.VMEM((TOKEN_TILE, D_FF), _BF16)],
        compiler_params=_params(1),
        name="mix_ffn",
    )(x2d, oa_t, ob_t, ga, gb, wa, wb, mix_post_g, *ffn_params)


def _row(v):
    return v.reshape(1, -1)


def _layer(x, ffn1, mix, ffn2, rel_bias, bias_tiles):
    batch, seq, _ = x.shape
    (mix_pre_g, w_in, moba_out_g, swa_sinks, swa_out_g, w_out, mix_post_g) = mix
    bias_own, bias_prev, bias_swa = bias_tiles

    def ffn_params(params):
        pre_g, w_gate, w_up, w_down, post_g = params
        return (_row(pre_g), w_gate.astype(_BF16), w_up.astype(_BF16), w_down.astype(_BF16), _row(post_g))

    x = _ffn(x.reshape(batch * seq, D_MODEL), ffn_params(ffn1))

    qa, ka, va, qs, ks, vs = jnp.split(
        w_in, [W_MOBA, 2 * W_MOBA, 3 * W_MOBA, 3 * W_MOBA + W_SWA, 3 * W_MOBA + W_SWA + W_SWA_KV], axis=1)
    w_rows = jnp.concatenate([ka, ks], axis=1).astype(_BF16)
    w_t = jnp.concatenate([qa, va, qs, vs], axis=1).T.astype(_BF16)
    qa_t, va_t, qs_t, vs_t, ka_b, ks_b, kmean = _in_proj(
        x.reshape(batch, seq, D_MODEL), _row(mix_pre_g), w_rows, w_t)

    oa_t = _moba(rel_bias, qa_t, ka_b, va_t, kmean.reshape(batch, seq // MOBA_BLOCK, W_MOBA),
                 bias_own, bias_prev)
    ob_t = _swa(swa_sinks, qs_t, ks_b, vs_t, bias_swa)

    w_out = w_out.astype(_BF16)
    x = _mix_ffn(x, oa_t.reshape(-1, W_MOBA, MOBA_BLOCK), ob_t.reshape(-1, W_SWA, SWA_BLOCK),
                 _row(moba_out_g), _row(swa_out_g), w_out[:W_MOBA], w_out[W_MOBA:], _row(mix_post_g),
                 ffn_params(ffn2))
    return x.reshape(batch, seq, D_MODEL)


def kernel(x, ffn1_pre_g, ffn1_w_gate, ffn1_w_up, ffn1_w_down, ffn1_post_g, mix_pre_g, w_in, rel_bias,
           moba_out_g, swa_sinks, swa_out_g, w_out, mix_post_g, ffn2_pre_g, ffn2_w_gate, ffn2_w_up,
           ffn2_w_down, ffn2_post_g):
    bias_tiles = _bias_tiles(rel_bias)
    for l in range(ffn1_pre_g.shape[0]):
        x = _layer(
            x,
            (ffn1_pre_g[l], ffn1_w_gate[l], ffn1_w_up[l], ffn1_w_down[l], ffn1_post_g[l]),
            (mix_pre_g[l], w_in[l], moba_out_g[l], swa_sinks[l], swa_out_g[l], w_out[l], mix_post_g[l]),
            (ffn2_pre_g[l], ffn2_w_gate[l], ffn2_w_up[l], ffn2_w_down[l], ffn2_post_g[l]),
            rel_bias, bias_tiles)
    return x
```

```python
import functools
import math

import jax
import jax.numpy as jnp
from jax import lax
from jax.experimental import pallas as pl
from jax.experimental.pallas import tpu as pltpu

D_MODEL = 1024
HEAD_DIM = 64
N_HEADS_MOBA = 8
N_HEADS_SWA = 8
N_KV_SWA = 2
SWA_GROUP = N_HEADS_SWA // N_KV_SWA
W_MOBA = N_HEADS_MOBA * HEAD_DIM
W_SWA = N_HEADS_SWA * HEAD_DIM
W_SWA_KV = N_KV_SWA * HEAD_DIM
MOBA_BLOCK = 256
MOBA_TOPK = 3
SWA_BLOCK = 128
SWA_WINDOW = 128
NUM_BUCKETS = 32
MAX_DISTANCE = 128
D_FF = 2816
RMS_EPS = 1e-6
FFN_RES_WEIGHT = 0.5
NEG = -1e30
QK_SCALE = HEAD_DIM ** -0.5
LOG2E = math.log2(math.e)
Q_FOLD = QK_SCALE * LOG2E
SUM_ROWS = 16

TOKEN_TILE = 512
FF_CHUNK = 256
HEADS_PER_STEP = 4
HEADS_PER_KEY_TILE = 2
PIPE_DEPTH = 2
GROUPS_PER_ITER = 4
SWA_Q_PER_STEP = 4
VMEM_LIMIT_BYTES = 56 * 1024 * 1024

_BF16 = jnp.bfloat16
_F32 = jnp.float32


def _dot(a, b):
    return jnp.dot(a, b, preferred_element_type=_F32)


def _dot_nt(a, b):
    return lax.dot_general(a, b, (((1,), (1,)), ((), ())), preferred_element_type=_F32)


def _rms_rows(x, g):
    return x * lax.rsqrt(jnp.mean(x * x, axis=-1, keepdims=True) + RMS_EPS) * g


def _const_spec(shape):
    return pl.BlockSpec(shape, lambda *_: (0,) * len(shape), pipeline_mode=pl.Buffered(1))


def _params(n_axes):
    return pltpu.CompilerParams(
        dimension_semantics=("arbitrary",) * n_axes, vmem_limit_bytes=VMEM_LIMIT_BYTES)


def _ffn_tile(x, pre_g_ref, wg_ref, wu_ref, wd_ref, post_g_ref, h_ref):
    xn = _rms_rows(x, pre_g_ref[...]).astype(_BF16)
    for c in range(D_FF // FF_CHUNK):
        cols = slice(c * FF_CHUNK, (c + 1) * FF_CHUNK)
        gate = _dot(xn, wg_ref[:, cols])
        up = _dot(xn, wu_ref[:, cols])
        h_ref[:, cols] = (jax.nn.silu(gate) * up).astype(_BF16)
    y = _dot(h_ref[...], wd_ref[...])
    return x + FFN_RES_WEIGHT * _rms_rows(y, post_g_ref[...])


def _ffn_body(x_ref, pre_g_ref, wg_ref, wu_ref, wd_ref, post_g_ref, o_ref, h_ref):
    o_ref[...] = _ffn_tile(x_ref[...], pre_g_ref, wg_ref, wu_ref, wd_ref, post_g_ref, h_ref)


_TOKEN_TILE_SPEC = pl.BlockSpec((TOKEN_TILE, D_MODEL), lambda i: (i, 0))
_FFN_WEIGHT_SPECS = [_const_spec((1, D_MODEL)), _const_spec((D_MODEL, D_FF)), _const_spec((D_MODEL, D_FF)),
                     _const_spec((D_FF, D_MODEL)), _const_spec((1, D_MODEL))]


def _ffn(x2d, ffn_params):
    n_tok = x2d.shape[0]
    return pl.pallas_call(
        _ffn_body,
        grid=(n_tok // TOKEN_TILE,),
        in_specs=[_TOKEN_TILE_SPEC] + _FFN_WEIGHT_SPECS,
        out_specs=_TOKEN_TILE_SPEC,
        out_shape=jax.ShapeDtypeStruct((n_tok, D_MODEL), _F32),
        scratch_shapes=[pltpu.VMEM((TOKEN_TILE, D_FF), _BF16)],
        compiler_params=_params(1),
        name="ffn",
    )(x2d, *ffn_params)


_T_QA, _T_VA, _T_QS, _T_VS = 0, W_MOBA, 2 * W_MOBA, 2 * W_MOBA + W_SWA
_T_ROWS = 2 * W_MOBA + W_SWA + W_SWA_KV
_MOBA_PER_TILE = TOKEN_TILE // MOBA_BLOCK
_SWA_PER_TILE = TOKEN_TILE // SWA_BLOCK


def _in_proj_body(x_ref, g_ref, w_rows_ref, w_t_ref,
                  qa_ref, va_ref, qs_ref, vs_ref, ka_ref, ks_ref, kmean_ref):
    xn = _rms_rows(x_ref[0], g_ref[...]).astype(_BF16)
    k_rows = _dot(xn, w_rows_ref[...])
    proj_t = _dot_nt(w_t_ref[...], xn)
    for t in range(_MOBA_PER_TILE):
        rows = slice(t * MOBA_BLOCK, (t + 1) * MOBA_BLOCK)
        k_blk = k_rows[rows, :W_MOBA]
        ka_ref[0, t] = k_blk.astype(_BF16)
        kmean_ref[0, t] = jnp.mean(k_blk, axis=0, keepdims=True)
        qa_ref[0, t] = (proj_t[_T_QA:_T_QA + W_MOBA, rows] * Q_FOLD).astype(_BF16)
        va_ref[0, t] = proj_t[_T_VA:_T_VA + W_MOBA, rows].astype(_BF16)
    for t in range(_SWA_PER_TILE):
        rows = slice(t * SWA_BLOCK, (t + 1) * SWA_BLOCK)
        ks_ref[0, t] = k_rows[rows, W_MOBA:].astype(_BF16)
        qs_ref[0, t] = (proj_t[_T_QS:_T_QS + W_SWA, rows] * Q_FOLD).astype(_BF16)
        vs_ref[0, t] = proj_t[_T_VS:_T_VS + W_SWA_KV, rows].astype(_BF16)


def _in_proj(x, g, w_rows, w_t):
    batch, seq, _ = x.shape
    nb_moba, nb_swa = seq // MOBA_BLOCK, seq // SWA_BLOCK

    def blocked(n_per_tile, rows, cols):
        return pl.BlockSpec((1, n_per_tile, rows, cols), lambda b, i: (b, i, 0, 0))

    return pl.pallas_call(
        _in_proj_body,
        grid=(batch, seq // TOKEN_TILE),
        in_specs=[pl.BlockSpec((1, TOKEN_TILE, D_MODEL), lambda b, i: (b, i, 0)),
                  _const_spec((1, D_MODEL)),
                  _const_spec((D_MODEL, W_MOBA + W_SWA_KV)),
                  _const_spec((_T_ROWS, D_MODEL))],
        out_specs=[blocked(_MOBA_PER_TILE, W_MOBA, MOBA_BLOCK),
                   blocked(_MOBA_PER_TILE, W_MOBA, MOBA_BLOCK),
                   blocked(_SWA_PER_TILE, W_SWA, SWA_BLOCK),
                   blocked(_SWA_PER_TILE, W_SWA_KV, SWA_BLOCK),
                   blocked(_MOBA_PER_TILE, MOBA_BLOCK, W_MOBA),
                   blocked(_SWA_PER_TILE, SWA_BLOCK, W_SWA_KV),
                   blocked(_MOBA_PER_TILE, 1, W_MOBA)],
        out_shape=[jax.ShapeDtypeStruct((batch, nb_moba, W_MOBA, MOBA_BLOCK), _BF16),
                   jax.ShapeDtypeStruct((batch, nb_moba, W_MOBA, MOBA_BLOCK), _BF16),
                   jax.ShapeDtypeStruct((batch, nb_swa, W_SWA, SWA_BLOCK), _BF16),
                   jax.ShapeDtypeStruct((batch, nb_swa, W_SWA_KV, SWA_BLOCK), _BF16),
                   jax.ShapeDtypeStruct((batch, nb_moba, MOBA_BLOCK, W_MOBA), _BF16),
                   jax.ShapeDtypeStruct((batch, nb_swa, SWA_BLOCK, W_SWA_KV), _BF16),
                   jax.ShapeDtypeStruct((batch, nb_moba, 1, W_MOBA), _F32)],
        compiler_params=_params(2),
        name="in_proj",
    )(x, g, w_rows, w_t)


def _t5_bucket(dist):
    n = jnp.maximum(dist, 0)
    max_exact = NUM_BUCKETS // 2
    nf = jnp.maximum(n, 1).astype(_F32)
    large = max_exact + (jnp.log(nf / max_exact) / math.log(MAX_DISTANCE / max_exact)
                         * (NUM_BUCKETS - max_exact)).astype(jnp.int32)
    large = jnp.minimum(large, NUM_BUCKETS - 1)
    return jnp.where(n < max_exact, n, large)


def _bias_lookup(rel_bias_ref, bucket, head):
    val = jnp.full(bucket.shape, rel_bias_ref[NUM_BUCKETS - 1, head], _F32)
    for b in range(NUM_BUCKETS - 2, -1, -1):
        val = jnp.where(bucket == b, rel_bias_ref[b, head], val)
    return val


def _bias_tiles_body(rel_bias_ref, own_ref, prev_ref, swa_ref):
    h = pl.program_id(0)
    key = lax.broadcasted_iota(jnp.int32, (MOBA_BLOCK, MOBA_BLOCK), 0)
    qry = lax.broadcasted_iota(jnp.int32, (MOBA_BLOCK, MOBA_BLOCK), 1)
    dist = qry - key
    own_ref[0] = jnp.where(dist >= 0, _bias_lookup(rel_bias_ref, _t5_bucket(dist), h) * LOG2E, NEG)
    prev_ref[0] = _bias_lookup(rel_bias_ref, _t5_bucket(dist + MOBA_BLOCK), h) * LOG2E
    key = lax.broadcasted_iota(jnp.int32, (2 * SWA_BLOCK, SWA_BLOCK), 0) - SWA_BLOCK
    qry = lax.broadcasted_iota(jnp.int32, (2 * SWA_BLOCK, SWA_BLOCK), 1)
    dist = qry - key
    in_win = (dist >= 0) & (dist < SWA_WINDOW)
    swa_ref[0] = jnp.where(
        in_win, _bias_lookup(rel_bias_ref, _t5_bucket(dist), h + N_HEADS_MOBA) * LOG2E, NEG)


def _bias_tiles(rel_bias):
    def per_head(rows, cols):
        return pl.BlockSpec((1, rows, cols), lambda h: (h, 0, 0))

    return pl.pallas_call(
        _bias_tiles_body,
        grid=(N_HEADS_MOBA,),
        in_specs=[pl.BlockSpec(memory_space=pltpu.SMEM)],
        out_specs=[per_head(MOBA_BLOCK, MOBA_BLOCK), per_head(MOBA_BLOCK, MOBA_BLOCK),
                   per_head(2 * SWA_BLOCK, SWA_BLOCK)],
        out_shape=[jax.ShapeDtypeStruct((N_HEADS_MOBA, MOBA_BLOCK, MOBA_BLOCK), _F32),
                   jax.ShapeDtypeStruct((N_HEADS_MOBA, MOBA_BLOCK, MOBA_BLOCK), _F32),
                   jax.ShapeDtypeStruct((N_HEADS_SWA, 2 * SWA_BLOCK, SWA_BLOCK), _F32)],
        compiler_params=_params(1),
        name="bias_tiles",
    )(rel_bias)


def _moba_body(rel_bias_ref, q_ref, k_ref, v_ref, kmean_ref, own_ref, prev_ref, o_ref,
               far_ref, qpad_ref, s_ref):
    g = pl.program_id(1)
    nb = k_ref.shape[1]
    key_lanes = HEADS_PER_KEY_TILE * HEAD_DIM
    q_row = lax.broadcasted_iota(jnp.int32, (key_lanes, MOBA_BLOCK), 0)
    km = kmean_ref[0]
    km_hi = km.astype(_BF16)
    km_lo = (km - km_hi.astype(_F32)).astype(_BF16)
    blk = lax.broadcasted_iota(jnp.int32, (nb, MOBA_BLOCK), 0)
    ones = jnp.ones((SUM_ROWS, MOBA_BLOCK), _BF16)
    ring_slots = range(GROUPS_PER_ITER)
    tail_slots = ring_slots[-PIPE_DEPTH:]
    near_slots = range(GROUPS_PER_ITER, GROUPS_PER_ITER + PIPE_DEPTH)
    assert PIPE_DEPTH == 2 and GROUPS_PER_ITER >= 2 * PIPE_DEPTH

    def key_tile(r):
        t, sub = divmod(r, HEADS_PER_KEY_TILE)
        return slice(t * key_lanes, (t + 1) * key_lanes), sub

    def prepare(i):
        past = blk < i
        own, prev = [], []
        for r in range(HEADS_PER_STEP):
            tile_lanes, sub = key_tile(r)
            q_t = q_ref[0, i, tile_lanes, :]
            q_pad = jnp.where((q_row >= sub * HEAD_DIM) & (q_row < (sub + 1) * HEAD_DIM), q_t, 0)
            qpad_ref[r] = q_pad

            gate = jnp.where(
                past, _dot(km_hi[:, tile_lanes], q_pad) + _dot(km_lo[:, tile_lanes], q_pad), NEG)
            sel = jnp.zeros(gate.shape, jnp.bool_)
            for _ in range(MOBA_TOPK):
                top = jnp.max(gate, axis=0, keepdims=True)
                first = jnp.min(jnp.where(gate == top, blk, nb), axis=0, keepdims=True)
                pick = blk == first
                sel = sel | pick
                gate = jnp.where(pick, -jnp.inf, gate)
            sel = sel & past

            far_bias = rel_bias_ref[NUM_BUCKETS - 1, g * HEADS_PER_STEP + r] * LOG2E
            far_ref[r] = jnp.where(sel & (blk < i - 1), far_bias, NEG)
            prev_neg = jnp.max(jnp.where(sel & (blk == i - 1), 0.0, NEG), axis=0, keepdims=True)
            own.append(own_ref[r])
            prev.append(prev_ref[r] + prev_neg)
        return own, prev

    def score(slot, j, tile_bias=None, row_bias=None):
        tops, shifts = [], []
        for r in range(HEADS_PER_STEP):
            s = _dot(k_ref[0, j, :, key_tile(r)[0]], qpad_ref[r])
            if tile_bias is not None:
                s = s + tile_bias[r]
            s_ref[slot, r] = s
            top = jnp.max(s, axis=0, keepdims=True)
            shift = jnp.zeros_like(top) if row_bias is None else row_bias[r]
            tops.append(top + shift)
            shifts.append(shift)
        return slot, j, tuple(tops), tuple(shifts)

    def absorb(carry, scored):
        slot, j, tops, shifts = scored
        out = []
        for r in range(HEADS_PER_STEP):
            lanes = slice(r * HEAD_DIM, (r + 1) * HEAD_DIM)
            m, acc = carry[r]
            m_new = jnp.maximum(m, tops[r])
            p = jnp.exp2(s_ref[slot, r] + (shifts[r] - m_new)).astype(_BF16)
            pv = _dot(jnp.concatenate([v_ref[0, j, lanes, :], ones], axis=0), p)
            out.append((m_new, jnp.exp2(m - m_new) * acc + pv))
        return tuple(out)

    def score_far(slot, grp):
        j = jnp.minimum(grp, nb - 1)
        return score(slot, j, row_bias=[far_ref[r, pl.ds(j, 1), :] for r in range(HEADS_PER_STEP)])

    def score_near(i):
        own, prev = prepare(i)
        return (score(near_slots[0], i, tile_bias=own)[2],
                score(near_slots[1], jnp.maximum(i - 1, 0), tile_bias=prev)[2])

    def tile(i, near_tops):
        carry = tuple((jnp.full((1, MOBA_BLOCK), -jnp.inf, _F32),
                       jnp.zeros((HEAD_DIM + SUM_ROWS, MOBA_BLOCK), _F32)) for _ in range(HEADS_PER_STEP))
        no_shift = tuple(jnp.zeros((1, MOBA_BLOCK), _F32) for _ in range(HEADS_PER_STEP))
        near_js = (i, jnp.maximum(i - 1, 0))
        ring = {slot: (slot, j, tops, no_shift) for slot, j, tops in zip(near_slots, near_js, near_tops)}
        for c, slot in enumerate(tail_slots):
            ring[slot] = score_far(slot, c)
            carry = absorb(carry, ring[near_slots[c]])

        def step(it, state):
            carry, pending = state
            ring = {slot: (slot,) + p for slot, p in zip(tail_slots, pending)}
            for k in ring_slots:
                scored = score_far(k, PIPE_DEPTH + it * GROUPS_PER_ITER + k)
                carry = absorb(carry, ring[(k - PIPE_DEPTH) % GROUPS_PER_ITER])
                ring[k] = scored
            return carry, tuple(ring[slot][1:] for slot in tail_slots)

        n_far = jnp.maximum(i - 1, 0)
        n_iters = (jnp.maximum(n_far - PIPE_DEPTH, 0) + GROUPS_PER_ITER - 1) // GROUPS_PER_ITER
        carry, pending = lax.fori_loop(
            0, n_iters, step, (carry, tuple(ring[slot][1:] for slot in tail_slots)))

        near_next = score_near(jnp.minimum(i + 1, nb - 1))
        for slot, p in zip(tail_slots, pending):
            carry = absorb(carry, (slot,) + p)
        for r in range(HEADS_PER_STEP):
            _, acc = carry[r]
            o_ref[0, i, r * HEAD_DIM:(r + 1) * HEAD_DIM, :] = acc[:HEAD_DIM] / acc[HEAD_DIM:HEAD_DIM + 1]
        return near_next

    lax.fori_loop(0, nb, tile, score_near(0))


def _moba(rel_bias, qa_t, ka, va_t, kmean, bias_own, bias_prev):
    batch, nb = qa_t.shape[0], qa_t.shape[1]
    width = HEADS_PER_STEP * HEAD_DIM
    q_spec = pl.BlockSpec((1, nb, width, MOBA_BLOCK), lambda b, g: (b, 0, g, 0))
    bias_spec = pl.BlockSpec((HEADS_PER_STEP, MOBA_BLOCK, MOBA_BLOCK), lambda b, g: (g, 0, 0))
    return pl.pallas_call(
        _moba_body,
        grid=(batch, N_HEADS_MOBA // HEADS_PER_STEP),
        in_specs=[pl.BlockSpec(memory_space=pltpu.SMEM),
                  q_spec,
                  pl.BlockSpec((1, nb, MOBA_BLOCK, width), lambda b, g: (b, 0, 0, g)),
                  q_spec,
                  pl.BlockSpec((1, nb, width), lambda b, g: (b, 0, g)),
                  bias_spec, bias_spec],
        out_specs=q_spec,
        out_shape=jax.ShapeDtypeStruct((batch, nb, W_MOBA, MOBA_BLOCK), _F32),
        scratch_shapes=[pltpu.VMEM((HEADS_PER_STEP, nb, MOBA_BLOCK), _F32),
                        pltpu.VMEM((HEADS_PER_STEP, HEADS_PER_KEY_TILE * HEAD_DIM, MOBA_BLOCK), _BF16),
                        pltpu.VMEM((GROUPS_PER_ITER + PIPE_DEPTH, HEADS_PER_STEP, MOBA_BLOCK, MOBA_BLOCK), _F32)],
        compiler_params=_params(2),
        name="moba",
    )(rel_bias, qa_t, ka, va_t, kmean, bias_own, bias_prev)


def _swa_body(sinks_ref, q_ref, k_prev_ref, k_cur_ref, v_prev_ref, v_cur_ref, bias_ref, o_ref, s_ref):
    no_prev = jnp.where(pl.program_id(1) == 0, NEG, 0.0)
    zeros = jnp.zeros((HEAD_DIM, SWA_BLOCK), _BF16)
    ones = jnp.ones((SUM_ROWS, SWA_BLOCK), _BF16)
    k_blocks = [k_prev_ref[0, 0]] + [k_cur_ref[0, t] for t in range(SWA_Q_PER_STEP)]
    v_blocks = [v_prev_ref[0, 0]] + [v_cur_ref[0, t] for t in range(SWA_Q_PER_STEP)]
    chains = [(t, kv) for t in range(SWA_Q_PER_STEP) for kv in range(N_KV_SWA)]

    for t, kv in chains:
        q_pad = jnp.concatenate(
            [jnp.concatenate(
                [q_ref[0, t, h * HEAD_DIM:(h + 1) * HEAD_DIM, :] if part == kv else zeros
                 for part in range(N_KV_SWA)], axis=0)
             for h in range(kv * SWA_GROUP, (kv + 1) * SWA_GROUP)], axis=1)
        s_ref[t, kv, 0] = _dot(k_blocks[t], q_pad)
        s_ref[t, kv, 1] = _dot(k_blocks[t + 1], q_pad)

    for t, kv in chains:
        heads = range(kv * SWA_GROUP, (kv + 1) * SWA_GROUP)
        kv_rows = slice(kv * HEAD_DIM, (kv + 1) * HEAD_DIM)
        p_prev, p_own, sink_p = [], [], []
        for slot, h in enumerate(heads):
            cols = slice(slot * SWA_BLOCK, (slot + 1) * SWA_BLOCK)
            sp = s_ref[t, kv, 0, :, cols] + bias_ref[h, :SWA_BLOCK, :]
            if t == 0:
                sp = sp + no_prev
            so = s_ref[t, kv, 1, :, cols] + bias_ref[h, SWA_BLOCK:, :]
            sink = sinks_ref[h] * LOG2E
            m = jnp.maximum(jnp.maximum(jnp.max(sp, axis=0, keepdims=True),
                                        jnp.max(so, axis=0, keepdims=True)), sink)
            p_prev.append(jnp.exp2(sp - m).astype(_BF16))
            p_own.append(jnp.exp2(so - m).astype(_BF16))
            sink_p.append(jnp.exp2(sink - m))
        acc = (_dot(jnp.concatenate([v_blocks[t][kv_rows, :], ones], axis=0), jnp.concatenate(p_prev, axis=1))
               + _dot(jnp.concatenate([v_blocks[t + 1][kv_rows, :], ones], axis=0),
                      jnp.concatenate(p_own, axis=1)))
        for slot, h in enumerate(heads):
            cols = slice(slot * SWA_BLOCK, (slot + 1) * SWA_BLOCK)
            l = acc[HEAD_DIM:HEAD_DIM + 1, cols] + sink_p[slot]
            o_ref[0, t, h * HEAD_DIM:(h + 1) * HEAD_DIM, :] = acc[:HEAD_DIM, cols] / l


def _swa(sinks, qs_t, ks, vs_t, bias_swa):
    batch, nb = qs_t.shape[0], qs_t.shape[1]

    def cur(rows, cols):
        return pl.BlockSpec((1, SWA_Q_PER_STEP, rows, cols), lambda b, n: (b, n, 0, 0))

    def prev(rows, cols):
        return pl.BlockSpec((1, 1, rows, cols),
                            lambda b, n: (b, jnp.maximum(n * SWA_Q_PER_STEP - 1, 0), 0, 0))

    return pl.pallas_call(
        _swa_body,
        grid=(batch, nb // SWA_Q_PER_STEP),
        in_specs=[pl.BlockSpec(memory_space=pltpu.SMEM),
                  cur(W_SWA, SWA_BLOCK),
                  prev(SWA_BLOCK, W_SWA_KV), cur(SWA_BLOCK, W_SWA_KV),
                  prev(W_SWA_KV, SWA_BLOCK), cur(W_SWA_KV, SWA_BLOCK),
                  _const_spec((N_HEADS_SWA, 2 * SWA_BLOCK, SWA_BLOCK))],
        out_specs=cur(W_SWA, SWA_BLOCK),
        out_shape=jax.ShapeDtypeStruct((batch, nb, W_SWA, SWA_BLOCK), _F32),
        scratch_shapes=[pltpu.VMEM((SWA_Q_PER_STEP, N_KV_SWA, 2, SWA_BLOCK, SWA_GROUP * SWA_BLOCK), _F32)],
        compiler_params=_params(2),
        name="swa",
    )(sinks, qs_t, ks, ks, vs_t, vs_t, bias_swa)


def _group_norm_rows(o_ref, g_ref):
    rows = []
    for t in range(o_ref.shape[0]):
        o_t = o_ref[t]
        scale = lax.rsqrt(jnp.mean(o_t * o_t, axis=0, keepdims=True) + RMS_EPS)
        rows.append(((o_t * scale).T * g_ref[...]).astype(_BF16))
    return jnp.concatenate(rows, axis=0)


def _mix_ffn_body(x_ref, oa_ref, ob_ref, ga_ref, gb_ref, wa_ref, wb_ref, mix_post_g_ref,
                  pre_g_ref, wg_ref, wu_ref, wd_ref, post_g_ref, o_ref, h_ref):
    y = (_dot(_group_norm_rows(oa_ref, ga_ref), wa_ref[...])
         + _dot(_group_norm_rows(ob_ref, gb_ref), wb_ref[...]))
    x = x_ref[...] + _rms_rows(y, mix_post_g_ref[...])
    o_ref[...] = _ffn_tile(x, pre_g_ref, wg_ref, wu_ref, wd_ref, post_g_ref, h_ref)


def _mix_ffn(x2d, oa_t, ob_t, ga, gb, wa, wb, mix_post_g, ffn_params):
    n_tok = x2d.shape[0]
    return pl.pallas_call(
        _mix_ffn_body,
        grid=(n_tok // TOKEN_TILE,),
        in_specs=[_TOKEN_TILE_SPEC,
                  pl.BlockSpec((_MOBA_PER_TILE, W_MOBA, MOBA_BLOCK), lambda i: (i, 0, 0)),
                  pl.BlockSpec((_SWA_PER_TILE, W_SWA, SWA_BLOCK), lambda i: (i, 0, 0)),
                  _const_spec((1, W_MOBA)), _const_spec((1, W_SWA)),
                  _const_spec((W_MOBA, D_MODEL)), _const_spec((W_SWA, D_MODEL)),
                  _const_spec((1, D_MODEL))] + _FFN_WEIGHT_SPECS,
        out_specs=_TOKEN_TILE_SPEC,
        out_shape=jax.ShapeDtypeStruct((n_tok, D_MODEL), _F32),
        scratch_shapes=[pltpu.VMEM((TOKEN_TILE, D_FF), _BF16)],
        compiler_params=_params(1),
        name="mix_ffn",
    )(x2d, oa_t, ob_t, ga, gb, wa, wb, mix_post_g, *ffn_params)


def _row(v):
    return v.reshape(1, -1)


def _layer(x, ffn1, mix, ffn2, rel_bias, bias_tiles):
    batch, seq, _ = x.shape
    (mix_pre_g, w_in, moba_out_g, swa_sinks, swa_out_g, w_out, mix_post_g) = mix
    bias_own, bias_prev, bias_swa = bias_tiles

    def ffn_params(params):
        pre_g, w_gate, w_up, w_down, post_g = params
        return (_row(pre_g), w_gate.astype(_BF16), w_up.astype(_BF16), w_down.astype(_BF16), _row(post_g))

    x = _ffn(x.reshape(batch * seq, D_MODEL), ffn_params(ffn1))

    qa, ka, va, qs, ks, vs = jnp.split(
        w_in, [W_MOBA, 2 * W_MOBA, 3 * W_MOBA, 3 * W_MOBA + W_SWA, 3 * W_MOBA + W_SWA + W_SWA_KV], axis=1)
    w_rows = jnp.concatenate([ka, ks], axis=1).astype(_BF16)
    w_t = jnp.concatenate([qa, va, qs, vs], axis=1).T.astype(_BF16)
    qa_t, va_t, qs_t, vs_t, ka_b, ks_b, kmean = _in_proj(
        x.reshape(batch, seq, D_MODEL), _row(mix_pre_g), w_rows, w_t)

    oa_t = _moba(rel_bias, qa_t, ka_b, va_t, kmean.reshape(batch, seq // MOBA_BLOCK, W_MOBA),
                 bias_own, bias_prev)
    ob_t = _swa(swa_sinks, qs_t, ks_b, vs_t, bias_swa)

    w_out = w_out.astype(_BF16)
    x = _mix_ffn(x, oa_t.reshape(-1, W_MOBA, MOBA_BLOCK), ob_t.reshape(-1, W_SWA, SWA_BLOCK),
                 _row(moba_out_g), _row(swa_out_g), w_out[:W_MOBA], w_out[W_MOBA:], _row(mix_post_g),
                 ffn_params(ffn2))
    return x.reshape(batch, seq, D_MODEL)


def kernel(x, ffn1_pre_g, ffn1_w_gate, ffn1_w_up, ffn1_w_down, ffn1_post_g, mix_pre_g, w_in, rel_bias,
           moba_out_g, swa_sinks, swa_out_g, w_out, mix_post_g, ffn2_pre_g, ffn2_w_gate, ffn2_w_up,
           ffn2_w_down, ffn2_post_g):
    bias_tiles = _bias_tiles(rel_bias)
    for l in range(ffn1_pre_g.shape[0]):
        x = _layer(
            x,
            (ffn1_pre_g[l], ffn1_w_gate[l], ffn1_w_up[l], ffn1_w_down[l], ffn1_post_g[l]),
            (mix_pre_g[l], w_in[l], moba_out_g[l], swa_sinks[l], swa_out_g[l], w_out[l], mix_post_g[l]),
            (ffn2_pre_g[l], ffn2_w_gate[l], ffn2_w_up[l], ffn2_w_down[l], ffn2_post_g[l]),
            rel_bias, bias_tiles)
    return x
```

```python
import functools
import math

import jax
import jax.numpy as jnp
from jax import lax
from jax.experimental import pallas as pl
from jax.experimental.pallas import tpu as pltpu

D_MODEL = 1024
HEAD_DIM = 64
N_HEADS_MOBA = 8
N_HEADS_SWA = 8
N_KV_SWA = 2
SWA_GROUP = N_HEADS_SWA // N_KV_SWA
W_MOBA = N_HEADS_MOBA * HEAD_DIM
W_SWA = N_HEADS_SWA * HEAD_DIM
W_SWA_KV = N_KV_SWA * HEAD_DIM
MOBA_BLOCK = 256
MOBA_TOPK = 3
SWA_BLOCK = 128
SWA_WINDOW = 128
NUM_BUCKETS = 32
MAX_DISTANCE = 128
D_FF = 2816
RMS_EPS = 1e-6
FFN_RES_WEIGHT = 0.5
NEG = -1e30
QK_SCALE = HEAD_DIM ** -0.5
LOG2E = math.log2(math.e)
Q_FOLD = QK_SCALE * LOG2E
SUM_ROWS = 16

TOKEN_TILE = 1024
FF_CHUNK = 256
HEADS_PER_STEP = 4
HEADS_PER_KEY_TILE = 2
PIPE_DEPTH = 2
GROUPS_PER_ITER = 4
SCORE_PAD_ROWS = 8
SWA_Q_PER_STEP = 4
VMEM_LIMIT_BYTES = 56 * 1024 * 1024

_BF16 = jnp.bfloat16
_F32 = jnp.float32


def _dot(a, b):
    return jnp.dot(a, b, preferred_element_type=_F32)


def _dot_nt(a, b):
    return lax.dot_general(a, b, (((1,), (1,)), ((), ())), preferred_element_type=_F32)


def _rms_rows(x, g):
    return x * lax.rsqrt(jnp.mean(x * x, axis=-1, keepdims=True) + RMS_EPS) * g


def _const_spec(shape):
    return pl.BlockSpec(shape, lambda *_: (0,) * len(shape), pipeline_mode=pl.Buffered(1))


def _params(n_axes):
    return pltpu.CompilerParams(
        dimension_semantics=("arbitrary",) * n_axes, vmem_limit_bytes=VMEM_LIMIT_BYTES)


def _ffn_tile(x, pre_g_ref, wg_ref, wu_ref, wd_ref, post_g_ref, h_ref):
    xn = _rms_rows(x, pre_g_ref[...]).astype(_BF16)
    for c in range(D_FF // FF_CHUNK):
        cols = slice(c * FF_CHUNK, (c + 1) * FF_CHUNK)
        gate = _dot(xn, wg_ref[:, cols])
        up = _dot(xn, wu_ref[:, cols])
        h_ref[:, cols] = (jax.nn.silu(gate) * up).astype(_BF16)
    y = _dot(h_ref[...], wd_ref[...])
    return x + FFN_RES_WEIGHT * _rms_rows(y, post_g_ref[...])


def _ffn_body(x_ref, pre_g_ref, wg_ref, wu_ref, wd_ref, post_g_ref, o_ref, h_ref):
    o_ref[...] = _ffn_tile(x_ref[...], pre_g_ref, wg_ref, wu_ref, wd_ref, post_g_ref, h_ref)


_TOKEN_TILE_SPEC = pl.BlockSpec((TOKEN_TILE, D_MODEL), lambda i: (i, 0))
_FFN_WEIGHT_SPECS = [_const_spec((1, D_MODEL)), _const_spec((D_MODEL, D_FF)), _const_spec((D_MODEL, D_FF)),
                     _const_spec((D_FF, D_MODEL)), _const_spec((1, D_MODEL))]


def _ffn(x2d, ffn_params):
    n_tok = x2d.shape[0]
    return pl.pallas_call(
        _ffn_body,
        grid=(n_tok // TOKEN_TILE,),
        in_specs=[_TOKEN_TILE_SPEC] + _FFN_WEIGHT_SPECS,
        out_specs=_TOKEN_TILE_SPEC,
        out_shape=jax.ShapeDtypeStruct((n_tok, D_MODEL), _F32),
        scratch_shapes=[pltpu.VMEM((TOKEN_TILE, D_FF), _BF16)],
        compiler_params=_params(1),
        name="ffn",
    )(x2d, *ffn_params)


_T_QA, _T_VA, _T_QS, _T_VS = 0, W_MOBA, 2 * W_MOBA, 2 * W_MOBA + W_SWA
_T_ROWS = 2 * W_MOBA + W_SWA + W_SWA_KV
_MOBA_PER_TILE = TOKEN_TILE // MOBA_BLOCK
_SWA_PER_TILE = TOKEN_TILE // SWA_BLOCK


def _in_proj_body(x_ref, g_ref, w_rows_ref, w_t_ref,
                  qa_ref, va_ref, qs_ref, vs_ref, ka_ref, ks_ref, kmean_ref):
    xn = _rms_rows(x_ref[0], g_ref[...]).astype(_BF16)
    k_rows = _dot(xn, w_rows_ref[...])
    proj_t = _dot_nt(w_t_ref[...], xn)
    for t in range(_MOBA_PER_TILE):
        rows = slice(t * MOBA_BLOCK, (t + 1) * MOBA_BLOCK)
        k_blk = k_rows[rows, :W_MOBA]
        ka_ref[0, t] = k_blk.astype(_BF16)
        kmean_ref[0, t] = jnp.mean(k_blk, axis=0, keepdims=True)
        qa_ref[0, t] = (proj_t[_T_QA:_T_QA + W_MOBA, rows] * Q_FOLD).astype(_BF16)
        va_ref[0, t] = proj_t[_T_VA:_T_VA + W_MOBA, rows].astype(_BF16)
    for t in range(_SWA_PER_TILE):
        rows = slice(t * SWA_BLOCK, (t + 1) * SWA_BLOCK)
        ks_ref[0, t] = k_rows[rows, W_MOBA:].astype(_BF16)
        qs_ref[0, t] = (proj_t[_T_QS:_T_QS + W_SWA, rows] * Q_FOLD).astype(_BF16)
        vs_ref[0, t] = proj_t[_T_VS:_T_VS + W_SWA_KV, rows].astype(_BF16)


def _in_proj(x, g, w_rows, w_t):
    batch, seq, _ = x.shape
    nb_moba, nb_swa = seq // MOBA_BLOCK, seq // SWA_BLOCK

    def blocked(n_per_tile, rows, cols):
        return pl.BlockSpec((1, n_per_tile, rows, cols), lambda b, i: (b, i, 0, 0))

    return pl.pallas_call(
        _in_proj_body,
        grid=(batch, seq // TOKEN_TILE),
        in_specs=[pl.BlockSpec((1, TOKEN_TILE, D_MODEL), lambda b, i: (b, i, 0)),
                  _const_spec((1, D_MODEL)),
                  _const_spec((D_MODEL, W_MOBA + W_SWA_KV)),
                  _const_spec((_T_ROWS, D_MODEL))],
        out_specs=[blocked(_MOBA_PER_TILE, W_MOBA, MOBA_BLOCK),
                   blocked(_MOBA_PER_TILE, W_MOBA, MOBA_BLOCK),
                   blocked(_SWA_PER_TILE, W_SWA, SWA_BLOCK),
                   blocked(_SWA_PER_TILE, W_SWA_KV, SWA_BLOCK),
                   blocked(_MOBA_PER_TILE, MOBA_BLOCK, W_MOBA),
                   blocked(_SWA_PER_TILE, SWA_BLOCK, W_SWA_KV),
                   blocked(_MOBA_PER_TILE, 1, W_MOBA)],
        out_shape=[jax.ShapeDtypeStruct((batch, nb_moba, W_MOBA, MOBA_BLOCK), _BF16),
                   jax.ShapeDtypeStruct((batch, nb_moba, W_MOBA, MOBA_BLOCK), _BF16),
                   jax.ShapeDtypeStruct((batch, nb_swa, W_SWA, SWA_BLOCK), _BF16),
                   jax.ShapeDtypeStruct((batch, nb_swa, W_SWA_KV, SWA_BLOCK), _BF16),
                   jax.ShapeDtypeStruct((batch, nb_moba, MOBA_BLOCK, W_MOBA), _BF16),
                   jax.ShapeDtypeStruct((batch, nb_swa, SWA_BLOCK, W_SWA_KV), _BF16),
                   jax.ShapeDtypeStruct((batch, nb_moba, 1, W_MOBA), _F32)],
        compiler_params=_params(2),
        name="in_proj",
    )(x, g, w_rows, w_t)


def _t5_bucket(dist):
    n = jnp.maximum(dist, 0)
    max_exact = NUM_BUCKETS // 2
    nf = jnp.maximum(n, 1).astype(_F32)
    large = max_exact + (jnp.log(nf / max_exact) / math.log(MAX_DISTANCE / max_exact)
                         * (NUM_BUCKETS - max_exact)).astype(jnp.int32)
    large = jnp.minimum(large, NUM_BUCKETS - 1)
    return jnp.where(n < max_exact, n, large)


def _bias_lookup(rel_bias_ref, bucket, head):
    val = jnp.full(bucket.shape, rel_bias_ref[NUM_BUCKETS - 1, head], _F32)
    for b in range(NUM_BUCKETS - 2, -1, -1):
        val = jnp.where(bucket == b, rel_bias_ref[b, head], val)
    return val


def _bias_tiles_body(rel_bias_ref, own_ref, prev_ref, swa_ref):
    h = pl.program_id(0)
    key = lax.broadcasted_iota(jnp.int32, (MOBA_BLOCK, MOBA_BLOCK), 0)
    qry = lax.broadcasted_iota(jnp.int32, (MOBA_BLOCK, MOBA_BLOCK), 1)
    dist = qry - key
    own_ref[0] = jnp.where(dist >= 0, _bias_lookup(rel_bias_ref, _t5_bucket(dist), h) * LOG2E, NEG)
    prev_ref[0] = _bias_lookup(rel_bias_ref, _t5_bucket(dist + MOBA_BLOCK), h) * LOG2E
    key = lax.broadcasted_iota(jnp.int32, (2 * SWA_BLOCK, SWA_BLOCK), 0) - SWA_BLOCK
    qry = lax.broadcasted_iota(jnp.int32, (2 * SWA_BLOCK, SWA_BLOCK), 1)
    dist = qry - key
    in_win = (dist >= 0) & (dist < SWA_WINDOW)
    swa_ref[0] = jnp.where(
        in_win, _bias_lookup(rel_bias_ref, _t5_bucket(dist), h + N_HEADS_MOBA) * LOG2E, NEG)


def _bias_tiles(rel_bias):
    def per_head(rows, cols):
        return pl.BlockSpec((1, rows, cols), lambda h: (h, 0, 0))

    return pl.pallas_call(
        _bias_tiles_body,
        grid=(N_HEADS_MOBA,),
        in_specs=[pl.BlockSpec(memory_space=pltpu.SMEM)],
        out_specs=[per_head(MOBA_BLOCK, MOBA_BLOCK), per_head(MOBA_BLOCK, MOBA_BLOCK),
                   per_head(2 * SWA_BLOCK, SWA_BLOCK)],
        out_shape=[jax.ShapeDtypeStruct((N_HEADS_MOBA, MOBA_BLOCK, MOBA_BLOCK), _F32),
                   jax.ShapeDtypeStruct((N_HEADS_MOBA, MOBA_BLOCK, MOBA_BLOCK), _F32),
                   jax.ShapeDtypeStruct((N_HEADS_SWA, 2 * SWA_BLOCK, SWA_BLOCK), _F32)],
        compiler_params=_params(1),
        name="bias_tiles",
    )(rel_bias)


def _moba_body(rel_bias_ref, q_ref, k_ref, v_ref, kmean_ref, own_ref, prev_ref, o_ref,
               far_ref, qpad_ref, s_ref):
    g = pl.program_id(1)
    nb = k_ref.shape[1]
    key_lanes = HEADS_PER_KEY_TILE * HEAD_DIM
    q_row = lax.broadcasted_iota(jnp.int32, (key_lanes, MOBA_BLOCK), 0)
    km = kmean_ref[0]
    km_hi = km.astype(_BF16)
    km_lo = (km - km_hi.astype(_F32)).astype(_BF16)
    blk = lax.broadcasted_iota(jnp.int32, (nb, MOBA_BLOCK), 0)
    ones = jnp.ones((SUM_ROWS, MOBA_BLOCK), _BF16)
    ring_slots = range(GROUPS_PER_ITER)
    tail_slots = ring_slots[-PIPE_DEPTH:]
    near_slots = range(GROUPS_PER_ITER, GROUPS_PER_ITER + PIPE_DEPTH)
    assert PIPE_DEPTH == 2 and GROUPS_PER_ITER >= 2 * PIPE_DEPTH

    def key_tile(r):
        t, sub = divmod(r, HEADS_PER_KEY_TILE)
        return slice(t * key_lanes, (t + 1) * key_lanes), sub

    def prepare(i):
        past = blk < i
        own, prev = [], []
        for r in range(HEADS_PER_STEP):
            tile_lanes, sub = key_tile(r)
            q_t = q_ref[0, i, tile_lanes, :]
            q_pad = jnp.where((q_row >= sub * HEAD_DIM) & (q_row < (sub + 1) * HEAD_DIM), q_t, 0)
            qpad_ref[r] = q_pad

            gate = jnp.where(
                past, _dot(km_hi[:, tile_lanes], q_pad) + _dot(km_lo[:, tile_lanes], q_pad), NEG)
            sel = jnp.zeros(gate.shape, jnp.bool_)
            for _ in range(MOBA_TOPK):
                top = jnp.max(gate, axis=0, keepdims=True)
                first = jnp.min(jnp.where(gate == top, blk, nb), axis=0, keepdims=True)
                pick = blk == first
                sel = sel | pick
                gate = jnp.where(pick, -jnp.inf, gate)
            sel = sel & past

            far_bias = rel_bias_ref[NUM_BUCKETS - 1, g * HEADS_PER_STEP + r] * LOG2E
            far_ref[r] = jnp.where(sel & (blk < i - 1), far_bias, NEG)
            prev_neg = jnp.max(jnp.where(sel & (blk == i - 1), 0.0, NEG), axis=0, keepdims=True)
            own.append(own_ref[r])
            prev.append(prev_ref[r] + prev_neg)
        return own, prev

    def score(slot, j, tile_bias=None, row_bias=None):
        tops, shifts = [], []
        for r in range(HEADS_PER_STEP):
            s = _dot(k_ref[0, j, :, key_tile(r)[0]], qpad_ref[r])
            if tile_bias is not None:
                s = s + tile_bias[r]
            s_ref[slot, r, :MOBA_BLOCK] = s
            top = jnp.max(s, axis=0, keepdims=True)
            shift = jnp.zeros_like(top) if row_bias is None else row_bias[r]
            tops.append(top + shift)
            shifts.append(shift)
        return slot, j, tuple(tops), tuple(shifts)

    def absorb(carry, scored):
        slot, j, tops, shifts = scored
        out = []
        for r in range(HEADS_PER_STEP):
            lanes = slice(r * HEAD_DIM, (r + 1) * HEAD_DIM)
            m, acc = carry[r]
            m_new = jnp.maximum(m, tops[r])
            p = jnp.exp2(s_ref[slot, r, :MOBA_BLOCK] + (shifts[r] - m_new)).astype(_BF16)
            pv = _dot(jnp.concatenate([v_ref[0, j, lanes, :], ones], axis=0), p)
            out.append((m_new, jnp.exp2(m - m_new) * acc + pv))
        return tuple(out)

    def score_far(slot, grp):
        j = jnp.minimum(grp, nb - 1)
        return score(slot, j, row_bias=[far_ref[r, pl.ds(j, 1), :] for r in range(HEADS_PER_STEP)])

    def score_near(i):
        own, prev = prepare(i)
        return (score(near_slots[0], i, tile_bias=own)[2],
                score(near_slots[1], jnp.maximum(i - 1, 0), tile_bias=prev)[2])

    def tile(i, near_tops):
        carry = tuple((jnp.full((1, MOBA_BLOCK), -jnp.inf, _F32),
                       jnp.zeros((HEAD_DIM + SUM_ROWS, MOBA_BLOCK), _F32)) for _ in range(HEADS_PER_STEP))
        no_shift = tuple(jnp.zeros((1, MOBA_BLOCK), _F32) for _ in range(HEADS_PER_STEP))
        near_js = (i, jnp.maximum(i - 1, 0))
        ring = {slot: (slot, j, tops, no_shift) for slot, j, tops in zip(near_slots, near_js, near_tops)}
        for c, slot in enumerate(tail_slots):
            ring[slot] = score_far(slot, c)
            carry = absorb(carry, ring[near_slots[c]])

        def step(it, state):
            carry, pending = state
            ring = {slot: (slot,) + p for slot, p in zip(tail_slots, pending)}
            for k in ring_slots:
                scored = score_far(k, PIPE_DEPTH + it * GROUPS_PER_ITER + k)
                carry = absorb(carry, ring[(k - PIPE_DEPTH) % GROUPS_PER_ITER])
                ring[k] = scored
            return carry, tuple(ring[slot][1:] for slot in tail_slots)

        n_far = jnp.maximum(i - 1, 0)
        n_iters = (jnp.maximum(n_far - PIPE_DEPTH, 0) + GROUPS_PER_ITER - 1) // GROUPS_PER_ITER
        carry, pending = lax.fori_loop(
            0, n_iters, step, (carry, tuple(ring[slot][1:] for slot in tail_slots)))

        near_next = score_near(jnp.minimum(i + 1, nb - 1))
        for slot, p in zip(tail_slots, pending):
            carry = absorb(carry, (slot,) + p)
        for r in range(HEADS_PER_STEP):
            _, acc = carry[r]
            o_ref[0, i, r * HEAD_DIM:(r + 1) * HEAD_DIM, :] = acc[:HEAD_DIM] / acc[HEAD_DIM:HEAD_DIM + 1]
        return near_next

    lax.fori_loop(0, nb, tile, score_near(0))


def _moba(rel_bias, qa_t, ka, va_t, kmean, bias_own, bias_prev):
    batch, nb = qa_t.shape[0], qa_t.shape[1]
    width = HEADS_PER_STEP * HEAD_DIM
    q_spec = pl.BlockSpec((1, nb, width, MOBA_BLOCK), lambda b, g: (b, 0, g, 0))
    bias_spec = pl.BlockSpec((HEADS_PER_STEP, MOBA_BLOCK, MOBA_BLOCK), lambda b, g: (g, 0, 0))
    return pl.pallas_call(
        _moba_body,
        grid=(batch, N_HEADS_MOBA // HEADS_PER_STEP),
        in_specs=[pl.BlockSpec(memory_space=pltpu.SMEM),
                  q_spec,
                  pl.BlockSpec((1, nb, MOBA_BLOCK, width), lambda b, g: (b, 0, 0, g)),
                  q_spec,
                  pl.BlockSpec((1, nb, width), lambda b, g: (b, 0, g)),
                  bias_spec, bias_spec],
        out_specs=q_spec,
        out_shape=jax.ShapeDtypeStruct((batch, nb, W_MOBA, MOBA_BLOCK), _F32),
        scratch_shapes=[pltpu.VMEM((HEADS_PER_STEP, nb, MOBA_BLOCK), _F32),
                        pltpu.VMEM((HEADS_PER_STEP, HEADS_PER_KEY_TILE * HEAD_DIM, MOBA_BLOCK), _BF16),
                        pltpu.VMEM((GROUPS_PER_ITER + PIPE_DEPTH, HEADS_PER_STEP,
                                    MOBA_BLOCK + SCORE_PAD_ROWS, MOBA_BLOCK), _F32)],
        compiler_params=_params(2),
        name="moba",
    )(rel_bias, qa_t, ka, va_t, kmean, bias_own, bias_prev)


def _swa_body(sinks_ref, q_ref, k_prev_ref, k_cur_ref, v_prev_ref, v_cur_ref, bias_ref, o_ref, s_ref):
    no_prev = jnp.where(pl.program_id(1) == 0, NEG, 0.0)
    zeros = jnp.zeros((HEAD_DIM, SWA_BLOCK), _BF16)
    ones = jnp.ones((SUM_ROWS, SWA_BLOCK), _BF16)
    k_blocks = [k_prev_ref[0, 0]] + [k_cur_ref[0, t] for t in range(SWA_Q_PER_STEP)]
    v_blocks = [v_prev_ref[0, 0]] + [v_cur_ref[0, t] for t in range(SWA_Q_PER_STEP)]
    chains = [(t, kv) for t in range(SWA_Q_PER_STEP) for kv in range(N_KV_SWA)]

    for t, kv in chains:
        q_pad = jnp.concatenate(
            [jnp.concatenate(
                [q_ref[0, t, h * HEAD_DIM:(h + 1) * HEAD_DIM, :] if part == kv else zeros
                 for part in range(N_KV_SWA)], axis=0)
             for h in range(kv * SWA_GROUP, (kv + 1) * SWA_GROUP)], axis=1)
        s_ref[t, kv, 0] = _dot(k_blocks[t], q_pad)
        s_ref[t, kv, 1] = _dot(k_blocks[t + 1], q_pad)

    for t, kv in chains:
        heads = range(kv * SWA_GROUP, (kv + 1) * SWA_GROUP)
        kv_rows = slice(kv * HEAD_DIM, (kv + 1) * HEAD_DIM)
        p_prev, p_own, sink_p = [], [], []
        for slot, h in enumerate(heads):
            cols = slice(slot * SWA_BLOCK, (slot + 1) * SWA_BLOCK)
            sp = s_ref[t, kv, 0, :, cols] + bias_ref[h, :SWA_BLOCK, :]
            if t == 0:
                sp = sp + no_prev
            so = s_ref[t, kv, 1, :, cols] + bias_ref[h, SWA_BLOCK:, :]
            sink = sinks_ref[h] * LOG2E
            m = jnp.maximum(jnp.maximum(jnp.max(sp, axis=0, keepdims=True),
                                        jnp.max(so, axis=0, keepdims=True)), sink)
            p_prev.append(jnp.exp2(sp - m).astype(_BF16))
            p_own.append(jnp.exp2(so - m).astype(_BF16))
            sink_p.append(jnp.exp2(sink - m))
        acc = (_dot(jnp.concatenate([v_blocks[t][kv_rows, :], ones], axis=0), jnp.concatenate(p_prev, axis=1))
               + _dot(jnp.concatenate([v_blocks[t + 1][kv_rows, :], ones], axis=0),
                      jnp.concatenate(p_own, axis=1)))
        for slot, h in enumerate(heads):
            cols = slice(slot * SWA_BLOCK, (slot + 1) * SWA_BLOCK)
            l = acc[HEAD_DIM:HEAD_DIM + 1, cols] + sink_p[slot]
            o_ref[0, t, h * HEAD_DIM:(h + 1) * HEAD_DIM, :] = acc[:HEAD_DIM, cols] / l


def _swa(sinks, qs_t, ks, vs_t, bias_swa):
    batch, nb = qs_t.shape[0], qs_t.shape[1]

    def cur(rows, cols):
        return pl.BlockSpec((1, SWA_Q_PER_STEP, rows, cols), lambda b, n: (b, n, 0, 0))

    def prev(rows, cols):
        return pl.BlockSpec((1, 1, rows, cols),
                            lambda b, n: (b, jnp.maximum(n * SWA_Q_PER_STEP - 1, 0), 0, 0))

    return pl.pallas_call(
        _swa_body,
        grid=(batch, nb // SWA_Q_PER_STEP),
        in_specs=[pl.BlockSpec(memory_space=pltpu.SMEM),
                  cur(W_SWA, SWA_BLOCK),
                  prev(SWA_BLOCK, W_SWA_KV), cur(SWA_BLOCK, W_SWA_KV),
                  prev(W_SWA_KV, SWA_BLOCK), cur(W_SWA_KV, SWA_BLOCK),
                  _const_spec((N_HEADS_SWA, 2 * SWA_BLOCK, SWA_BLOCK))],
        out_specs=cur(W_SWA, SWA_BLOCK),
        out_shape=jax.ShapeDtypeStruct((batch, nb, W_SWA, SWA_BLOCK), _F32),
        scratch_shapes=[pltpu.VMEM((SWA_Q_PER_STEP, N_KV_SWA, 2, SWA_BLOCK, SWA_GROUP * SWA_BLOCK), _F32)],
        compiler_params=_params(2),
        name="swa",
    )(sinks, qs_t, ks, ks, vs_t, vs_t, bias_swa)


def _group_norm_rows(o_ref, g_ref):
    rows = []
    for t in range(o_ref.shape[0]):
        o_t = o_ref[t]
        scale = lax.rsqrt(jnp.mean(o_t * o_t, axis=0, keepdims=True) + RMS_EPS)
        rows.append(((o_t * scale).T * g_ref[...]).astype(_BF16))
    return jnp.concatenate(rows, axis=0)


def _mix_ffn_body(x_ref, oa_ref, ob_ref, ga_ref, gb_ref, wa_ref, wb_ref, mix_post_g_ref,
                  pre_g_ref, wg_ref, wu_ref, wd_ref, post_g_ref, o_ref, h_ref):
    y = (_dot(_group_norm_rows(oa_ref, ga_ref), wa_ref[...])
         + _dot(_group_norm_rows(ob_ref, gb_ref), wb_ref[...]))
    x = x_ref[...] + _rms_rows(y, mix_post_g_ref[...])
    o_ref[...] = _ffn_tile(x, pre_g_ref, wg_ref, wu_ref, wd_ref, post_g_ref, h_ref)


def _mix_ffn(x2d, oa_t, ob_t, ga, gb, wa, wb, mix_post_g, ffn_params):
    n_tok = x2d.shape[0]
    return pl.pallas_call(
        _mix_ffn_body,
        grid=(n_tok // TOKEN_TILE,),
        in_specs=[_TOKEN_TILE_SPEC,
                  pl.BlockSpec((_MOBA_PER_TILE, W_MOBA, MOBA_BLOCK), lambda i: (i, 0, 0)),
                  pl.BlockSpec((_SWA_PER_TILE, W_SWA, SWA_BLOCK), lambda i: (i, 0, 0)),
                  _const_spec((1, W_MOBA)), _const_spec((1, W_SWA)),
                  _const_spec((W_MOBA, D_MODEL)), _const_spec((W_SWA, D_MODEL)),
                  _const_spec((1, D_MODEL))] + _FFN_WEIGHT_SPECS,
        out_specs=_TOKEN_TILE_SPEC,
        out_shape=jax.ShapeDtypeStruct((n_tok, D_MODEL), _F32),
        scratch_shapes=[pltpu.VMEM((TOKEN_TILE, D_FF), _BF16)],
        compiler_params=_params(1),
        name="mix_ffn",
    )(x2d, oa_t, ob_t, ga, gb, wa, wb, mix_post_g, *ffn_params)


def _row(v):
    return v.reshape(1, -1)


def _layer(x, ffn1, mix, ffn2, rel_bias, bias_tiles):
    batch, seq, _ = x.shape
    (mix_pre_g, w_in, moba_out_g, swa_sinks, swa_out_g, w_out, mix_post_g) = mix
    bias_own, bias_prev, bias_swa = bias_tiles

    def ffn_params(params):
        pre_g, w_gate, w_up, w_down, post_g = params
        return (_row(pre_g), w_gate.astype(_BF16), w_up.astype(_BF16), w_down.astype(_BF16), _row(post_g))

    x = _ffn(x.reshape(batch * seq, D_MODEL), ffn_params(ffn1))

    qa, ka, va, qs, ks, vs = jnp.split(
        w_in, [W_MOBA, 2 * W_MOBA, 3 * W_MOBA, 3 * W_MOBA + W_SWA, 3 * W_MOBA + W_SWA + W_SWA_KV], axis=1)
    w_rows = jnp.concatenate([ka, ks], axis=1).astype(_BF16)
    w_t = jnp.concatenate([qa, va, qs, vs], axis=1).T.astype(_BF16)
    qa_t, va_t, qs_t, vs_t, ka_b, ks_b, kmean = _in_proj(
        x.reshape(batch, seq, D_MODEL), _row(mix_pre_g), w_rows, w_t)

    oa_t = _moba(rel_bias, qa_t, ka_b, va_t, kmean.reshape(batch, seq // MOBA_BLOCK, W_MOBA),
                 bias_own, bias_prev)
    ob_t = _swa(swa_sinks, qs_t, ks_b, vs_t, bias_swa)

    w_out = w_out.astype(_BF16)
    x = _mix_ffn(x, oa_t.reshape(-1, W_MOBA, MOBA_BLOCK), ob_t.reshape(-1, W_SWA, SWA_BLOCK),
                 _row(moba_out_g), _row(swa_out_g), w_out[:W_MOBA], w_out[W_MOBA:], _row(mix_post_g),
                 ffn_params(ffn2))
    return x.reshape(batch, seq, D_MODEL)


def kernel(x, ffn1_pre_g, ffn1_w_gate, ffn1_w_up, ffn1_w_down, ffn1_post_g, mix_pre_g, w_in, rel_bias,
           moba_out_g, swa_sinks, swa_out_g, w_out, mix_post_g, ffn2_pre_g, ffn2_w_gate, ffn2_w_up,
           ffn2_w_down, ffn2_post_g):
    bias_tiles = _bias_tiles(rel_bias)
    for l in range(ffn1_pre_g.shape[0]):
        x = _layer(
            x,
            (ffn1_pre_g[l], ffn1_w_gate[l], ffn1_w_up[l], ffn1_w_down[l], ffn1_post_g[l]),
            (mix_pre_g[l], w_in[l], moba_out_g[l], swa_sinks[l], swa_out_g[l], w_out[l], mix_post_g[l]),
            (ffn2_pre_g[l], ffn2_w_gate[l], ffn2_w_up[l], ffn2_w_down[l], ffn2_post_g[l]),
            rel_bias, bias_tiles)
    return x
```

```python
import functools
import math

import jax
import jax.numpy as jnp
from jax import lax
from jax.experimental import pallas as pl
from jax.experimental.pallas import tpu as pltpu

D_MODEL = 1024
HEAD_DIM = 64
N_HEADS_MOBA = 8
N_HEADS_SWA = 8
N_KV_SWA = 2
SWA_GROUP = N_HEADS_SWA // N_KV_SWA
W_MOBA = N_HEADS_MOBA * HEAD_DIM
W_SWA = N_HEADS_SWA * HEAD_DIM
W_SWA_KV = N_KV_SWA * HEAD_DIM
MOBA_BLOCK = 256
MOBA_TOPK = 3
SWA_BLOCK = 128
SWA_WINDOW = 128
NUM_BUCKETS = 32
MAX_DISTANCE = 128
D_FF = 2816
RMS_EPS = 1e-6
FFN_RES_WEIGHT = 0.5
NEG = -1e30
QK_SCALE = HEAD_DIM ** -0.5
LOG2E = math.log2(math.e)
Q_FOLD = QK_SCALE * LOG2E
SUM_ROWS = 16

TOKEN_TILE = 1024
FF_CHUNK = 256
FFN_ROW_PARTS = 2
HEADS_PER_STEP = 4
HEADS_PER_KEY_TILE = 2
PIPE_DEPTH = 2
GROUPS_PER_ITER = 4
SWA_Q_PER_STEP = 4
VMEM_LIMIT_BYTES = 56 * 1024 * 1024

_BF16 = jnp.bfloat16
_F32 = jnp.float32


def _dot(a, b):
    return jnp.dot(a, b, preferred_element_type=_F32)


def _dot_nt(a, b):
    return lax.dot_general(a, b, (((1,), (1,)), ((), ())), preferred_element_type=_F32)


def _rms_rows(x, g):
    return x * lax.rsqrt(jnp.mean(x * x, axis=-1, keepdims=True) + RMS_EPS) * g


def _const_spec(shape):
    return pl.BlockSpec(shape, lambda *_: (0,) * len(shape), pipeline_mode=pl.Buffered(1))


def _params(n_axes):
    return pltpu.CompilerParams(
        dimension_semantics=("arbitrary",) * n_axes, vmem_limit_bytes=VMEM_LIMIT_BYTES)


def _ffn_tile(x_rows, pre_g_ref, wg_ref, wu_ref, wd_ref, post_g_ref, o_ref, h_ref):
    part = TOKEN_TILE // FFN_ROW_PARTS
    for p in range(FFN_ROW_PARTS):
        rows = slice(p * part, (p + 1) * part)
        x = x_rows(rows)
        xn = _rms_rows(x, pre_g_ref[...]).astype(_BF16)
        for c in range(D_FF // FF_CHUNK):
            cols = slice(c * FF_CHUNK, (c + 1) * FF_CHUNK)
            gate = _dot(xn, wg_ref[:, cols])
            up = _dot(xn, wu_ref[:, cols])
            h_ref[rows, cols] = (jax.nn.silu(gate) * up).astype(_BF16)
        y = _dot(h_ref[rows, :], wd_ref[...])
        o_ref[rows, :] = x + FFN_RES_WEIGHT * _rms_rows(y, post_g_ref[...])


def _ffn_body(x_ref, pre_g_ref, wg_ref, wu_ref, wd_ref, post_g_ref, o_ref, h_ref):
    _ffn_tile(lambda rows: x_ref[rows, :], pre_g_ref, wg_ref, wu_ref, wd_ref, post_g_ref, o_ref, h_ref)


_TOKEN_TILE_SPEC = pl.BlockSpec((TOKEN_TILE, D_MODEL), lambda i: (i, 0))
_FFN_WEIGHT_SPECS = [_const_spec((1, D_MODEL)), _const_spec((D_MODEL, D_FF)), _const_spec((D_MODEL, D_FF)),
                     _const_spec((D_FF, D_MODEL)), _const_spec((1, D_MODEL))]


def _ffn(x2d, ffn_params):
    n_tok = x2d.shape[0]
    return pl.pallas_call(
        _ffn_body,
        grid=(n_tok // TOKEN_TILE,),
        in_specs=[_TOKEN_TILE_SPEC] + _FFN_WEIGHT_SPECS,
        out_specs=_TOKEN_TILE_SPEC,
        out_shape=jax.ShapeDtypeStruct((n_tok, D_MODEL), _F32),
        scratch_shapes=[pltpu.VMEM((TOKEN_TILE, D_FF), _BF16)],
        compiler_params=_params(1),
        name="ffn",
    )(x2d, *ffn_params)


_T_QA, _T_VA, _T_QS, _T_VS = 0, W_MOBA, 2 * W_MOBA, 2 * W_MOBA + W_SWA
_T_ROWS = 2 * W_MOBA + W_SWA + W_SWA_KV
_MOBA_PER_TILE = TOKEN_TILE // MOBA_BLOCK
_SWA_PER_TILE = TOKEN_TILE // SWA_BLOCK


def _in_proj_body(x_ref, g_ref, w_rows_ref, w_t_ref,
                  qa_ref, va_ref, qs_ref, vs_ref, ka_ref, ks_ref, kmean_ref):
    xn = _rms_rows(x_ref[0], g_ref[...]).astype(_BF16)
    k_rows = _dot(xn, w_rows_ref[...])
    proj_t = _dot_nt(w_t_ref[...], xn)
    for t in range(_MOBA_PER_TILE):
        rows = slice(t * MOBA_BLOCK, (t + 1) * MOBA_BLOCK)
        k_blk = k_rows[rows, :W_MOBA]
        ka_ref[0, t] = k_blk.astype(_BF16)
        kmean_ref[0, t] = jnp.mean(k_blk, axis=0, keepdims=True)
        qa_ref[0, t] = (proj_t[_T_QA:_T_QA + W_MOBA, rows] * Q_FOLD).astype(_BF16)
        va_ref[0, t] = proj_t[_T_VA:_T_VA + W_MOBA, rows].astype(_BF16)
    for t in range(_SWA_PER_TILE):
        rows = slice(t * SWA_BLOCK, (t + 1) * SWA_BLOCK)
        ks_ref[0, t] = k_rows[rows, W_MOBA:].astype(_BF16)
        qs_ref[0, t] = (proj_t[_T_QS:_T_QS + W_SWA, rows] * Q_FOLD).astype(_BF16)
        vs_ref[0, t] = proj_t[_T_VS:_T_VS + W_SWA_KV, rows].astype(_BF16)


def _in_proj(x, g, w_rows, w_t):
    batch, seq, _ = x.shape
    nb_moba, nb_swa = seq // MOBA_BLOCK, seq // SWA_BLOCK

    def blocked(n_per_tile, rows, cols):
        return pl.BlockSpec((1, n_per_tile, rows, cols), lambda b, i: (b, i, 0, 0))

    return pl.pallas_call(
        _in_proj_body,
        grid=(batch, seq // TOKEN_TILE),
        in_specs=[pl.BlockSpec((1, TOKEN_TILE, D_MODEL), lambda b, i: (b, i, 0)),
                  _const_spec((1, D_MODEL)),
                  _const_spec((D_MODEL, W_MOBA + W_SWA_KV)),
                  _const_spec((_T_ROWS, D_MODEL))],
        out_specs=[blocked(_MOBA_PER_TILE, W_MOBA, MOBA_BLOCK),
                   blocked(_MOBA_PER_TILE, W_MOBA, MOBA_BLOCK),
                   blocked(_SWA_PER_TILE, W_SWA, SWA_BLOCK),
                   blocked(_SWA_PER_TILE, W_SWA_KV, SWA_BLOCK),
                   blocked(_MOBA_PER_TILE, MOBA_BLOCK, W_MOBA),
                   blocked(_SWA_PER_TILE, SWA_BLOCK, W_SWA_KV),
                   blocked(_MOBA_PER_TILE, 1, W_MOBA)],
        out_shape=[jax.ShapeDtypeStruct((batch, nb_moba, W_MOBA, MOBA_BLOCK), _BF16),
                   jax.ShapeDtypeStruct((batch, nb_moba, W_MOBA, MOBA_BLOCK), _BF16),
                   jax.ShapeDtypeStruct((batch, nb_swa, W_SWA, SWA_BLOCK), _BF16),
                   jax.ShapeDtypeStruct((batch, nb_swa, W_SWA_KV, SWA_BLOCK), _BF16),
                   jax.ShapeDtypeStruct((batch, nb_moba, MOBA_BLOCK, W_MOBA), _BF16),
                   jax.ShapeDtypeStruct((batch, nb_swa, SWA_BLOCK, W_SWA_KV), _BF16),
                   jax.ShapeDtypeStruct((batch, nb_moba, 1, W_MOBA), _F32)],
        compiler_params=_params(2),
        name="in_proj",
    )(x, g, w_rows, w_t)


def _t5_bucket(dist):
    n = jnp.maximum(dist, 0)
    max_exact = NUM_BUCKETS // 2
    nf = jnp.maximum(n, 1).astype(_F32)
    large = max_exact + (jnp.log(nf / max_exact) / math.log(MAX_DISTANCE / max_exact)
                         * (NUM_BUCKETS - max_exact)).astype(jnp.int32)
    large = jnp.minimum(large, NUM_BUCKETS - 1)
    return jnp.where(n < max_exact, n, large)


def _bias_lookup(rel_bias_ref, bucket, head):
    val = jnp.full(bucket.shape, rel_bias_ref[NUM_BUCKETS - 1, head], _F32)
    for b in range(NUM_BUCKETS - 2, -1, -1):
        val = jnp.where(bucket == b, rel_bias_ref[b, head], val)
    return val


def _bias_tiles_body(rel_bias_ref, own_ref, prev_ref, swa_ref):
    h = pl.program_id(0)
    key = lax.broadcasted_iota(jnp.int32, (MOBA_BLOCK, MOBA_BLOCK), 0)
    qry = lax.broadcasted_iota(jnp.int32, (MOBA_BLOCK, MOBA_BLOCK), 1)
    dist = qry - key
    own_ref[0] = jnp.where(dist >= 0, _bias_lookup(rel_bias_ref, _t5_bucket(dist), h) * LOG2E, NEG)
    prev_ref[0] = _bias_lookup(rel_bias_ref, _t5_bucket(dist + MOBA_BLOCK), h) * LOG2E
    key = lax.broadcasted_iota(jnp.int32, (2 * SWA_BLOCK, SWA_BLOCK), 0) - SWA_BLOCK
    qry = lax.broadcasted_iota(jnp.int32, (2 * SWA_BLOCK, SWA_BLOCK), 1)
    dist = qry - key
    in_win = (dist >= 0) & (dist < SWA_WINDOW)
    swa_ref[0] = jnp.where(
        in_win, _bias_lookup(rel_bias_ref, _t5_bucket(dist), h + N_HEADS_MOBA) * LOG2E, NEG)


def _bias_tiles(rel_bias):
    def per_head(rows, cols):
        return pl.BlockSpec((1, rows, cols), lambda h: (h, 0, 0))

    return pl.pallas_call(
        _bias_tiles_body,
        grid=(N_HEADS_MOBA,),
        in_specs=[pl.BlockSpec(memory_space=pltpu.SMEM)],
        out_specs=[per_head(MOBA_BLOCK, MOBA_BLOCK), per_head(MOBA_BLOCK, MOBA_BLOCK),
                   per_head(2 * SWA_BLOCK, SWA_BLOCK)],
        out_shape=[jax.ShapeDtypeStruct((N_HEADS_MOBA, MOBA_BLOCK, MOBA_BLOCK), _F32),
                   jax.ShapeDtypeStruct((N_HEADS_MOBA, MOBA_BLOCK, MOBA_BLOCK), _F32),
                   jax.ShapeDtypeStruct((N_HEADS_SWA, 2 * SWA_BLOCK, SWA_BLOCK), _F32)],
        compiler_params=_params(1),
        name="bias_tiles",
    )(rel_bias)


def _moba_body(rel_bias_ref, q_ref, k_ref, v_ref, kmean_ref, own_ref, prev_ref, o_ref,
               far_ref, qpad_ref, s_ref):
    g = pl.program_id(1)
    nb = k_ref.shape[1]
    key_lanes = HEADS_PER_KEY_TILE * HEAD_DIM
    q_row = lax.broadcasted_iota(jnp.int32, (key_lanes, MOBA_BLOCK), 0)
    km = kmean_ref[0]
    km_hi = km.astype(_BF16)
    km_lo = (km - km_hi.astype(_F32)).astype(_BF16)
    blk = lax.broadcasted_iota(jnp.int32, (nb, MOBA_BLOCK), 0)
    ones = jnp.ones((SUM_ROWS, MOBA_BLOCK), _BF16)
    ring_slots = range(GROUPS_PER_ITER)
    tail_slots = ring_slots[-PIPE_DEPTH:]
    near_slots = range(GROUPS_PER_ITER, GROUPS_PER_ITER + PIPE_DEPTH)
    assert PIPE_DEPTH == 2 and GROUPS_PER_ITER >= 2 * PIPE_DEPTH

    def key_tile(r):
        t, sub = divmod(r, HEADS_PER_KEY_TILE)
        return slice(t * key_lanes, (t + 1) * key_lanes), sub

    def prepare(i):
        past = blk < i
        own, prev = [], []
        for r in range(HEADS_PER_STEP):
            tile_lanes, sub = key_tile(r)
            q_t = q_ref[0, i, tile_lanes, :]
            q_pad = jnp.where((q_row >= sub * HEAD_DIM) & (q_row < (sub + 1) * HEAD_DIM), q_t, 0)
            qpad_ref[r] = q_pad

            gate = jnp.where(
                past, _dot(km_hi[:, tile_lanes], q_pad) + _dot(km_lo[:, tile_lanes], q_pad), NEG)
            sel = jnp.zeros(gate.shape, jnp.bool_)
            for _ in range(MOBA_TOPK):
                top = jnp.max(gate, axis=0, keepdims=True)
                first = jnp.min(jnp.where(gate == top, blk, nb), axis=0, keepdims=True)
                pick = blk == first
                sel = sel | pick
                gate = jnp.where(pick, -jnp.inf, gate)
            sel = sel & past

            far_bias = rel_bias_ref[NUM_BUCKETS - 1, g * HEADS_PER_STEP + r] * LOG2E
            far_ref[r] = jnp.where(sel & (blk < i - 1), far_bias, NEG)
            prev_neg = jnp.max(jnp.where(sel & (blk == i - 1), 0.0, NEG), axis=0, keepdims=True)
            own.append(own_ref[r])
            prev.append(prev_ref[r] + prev_neg)
        return own, prev

    def score(slot, j, tile_bias=None, row_bias=None):
        tops, shifts = [], []
        for r in range(HEADS_PER_STEP):
            s = _dot(k_ref[0, j, :, key_tile(r)[0]], qpad_ref[r])
            if tile_bias is not None:
                s = s + tile_bias[r]
            s_ref[slot, r] = s
            top = jnp.max(s, axis=0, keepdims=True)
            shift = jnp.zeros_like(top) if row_bias is None else row_bias[r]
            tops.append(top + shift)
            shifts.append(shift)
        return slot, j, tuple(tops), tuple(shifts)

    def absorb(carry, scored):
        slot, j, tops, shifts = scored
        out = []
        for r in range(HEADS_PER_STEP):
            lanes = slice(r * HEAD_DIM, (r + 1) * HEAD_DIM)
            m, acc = carry[r]
            m_new = jnp.maximum(m, tops[r])
            p = jnp.exp2(s_ref[slot, r] + (shifts[r] - m_new)).astype(_BF16)
            pv = _dot(jnp.concatenate([v_ref[0, j, lanes, :], ones], axis=0), p)
            out.append((m_new, jnp.exp2(m - m_new) * acc + pv))
        return tuple(out)

    def score_far(slot, grp):
        j = jnp.minimum(grp, nb - 1)
        return score(slot, j, row_bias=[far_ref[r, pl.ds(j, 1), :] for r in range(HEADS_PER_STEP)])

    def score_near(i):
        own, prev = prepare(i)
        return (score(near_slots[0], i, tile_bias=own)[2],
                score(near_slots[1], jnp.maximum(i - 1, 0), tile_bias=prev)[2])

    def start(i, near_tops):
        carry = tuple((jnp.full((1, MOBA_BLOCK), -jnp.inf, _F32),
                       jnp.zeros((HEAD_DIM + SUM_ROWS, MOBA_BLOCK), _F32)) for _ in range(HEADS_PER_STEP))
        no_shift = tuple(jnp.zeros((1, MOBA_BLOCK), _F32) for _ in range(HEADS_PER_STEP))
        near_js = (i, jnp.maximum(i - 1, 0))
        ring = {slot: (slot, j, tops, no_shift) for slot, j, tops in zip(near_slots, near_js, near_tops)}
        for c, slot in enumerate(tail_slots):
            ring[slot] = score_far(slot, c)
            carry = absorb(carry, ring[near_slots[c]])
        return carry, tuple(ring[slot][1:] for slot in tail_slots)

    def tile(i, state, last=False):
        def step(it, state):
            carry, pending = state
            ring = {slot: (slot,) + p for slot, p in zip(tail_slots, pending)}
            for k in ring_slots:
                scored = score_far(k, PIPE_DEPTH + it * GROUPS_PER_ITER + k)
                carry = absorb(carry, ring[(k - PIPE_DEPTH) % GROUPS_PER_ITER])
                ring[k] = scored
            return carry, tuple(ring[slot][1:] for slot in tail_slots)

        n_far = jnp.maximum(i - 1, 0)
        n_iters = (jnp.maximum(n_far - PIPE_DEPTH, 0) + GROUPS_PER_ITER - 1) // GROUPS_PER_ITER
        carry, pending = lax.fori_loop(0, n_iters, step, state)

        near_next = None if last else score_near(i + 1)
        for slot, p in zip(tail_slots, pending):
            carry = absorb(carry, (slot,) + p)
        for r in range(HEADS_PER_STEP):
            _, acc = carry[r]
            o_ref[0, i, r * HEAD_DIM:(r + 1) * HEAD_DIM, :] = acc[:HEAD_DIM] / acc[HEAD_DIM:HEAD_DIM + 1]
        return None if last else start(i + 1, near_next)

    state = lax.fori_loop(0, nb - 1, tile, start(0, score_near(0)))
    tile(nb - 1, state, last=True)


def _moba(rel_bias, qa_t, ka, va_t, kmean, bias_own, bias_prev):
    batch, nb = qa_t.shape[0], qa_t.shape[1]
    width = HEADS_PER_STEP * HEAD_DIM
    q_spec = pl.BlockSpec((1, nb, width, MOBA_BLOCK), lambda b, g: (b, 0, g, 0))
    bias_spec = pl.BlockSpec((HEADS_PER_STEP, MOBA_BLOCK, MOBA_BLOCK), lambda b, g: (g, 0, 0))
    return pl.pallas_call(
        _moba_body,
        grid=(batch, N_HEADS_MOBA // HEADS_PER_STEP),
        in_specs=[pl.BlockSpec(memory_space=pltpu.SMEM),
                  q_spec,
                  pl.BlockSpec((1, nb, MOBA_BLOCK, width), lambda b, g: (b, 0, 0, g)),
                  q_spec,
                  pl.BlockSpec((1, nb, width), lambda b, g: (b, 0, g)),
                  bias_spec, bias_spec],
        out_specs=q_spec,
        out_shape=jax.ShapeDtypeStruct((batch, nb, W_MOBA, MOBA_BLOCK), _F32),
        scratch_shapes=[pltpu.VMEM((HEADS_PER_STEP, nb, MOBA_BLOCK), _F32),
                        pltpu.VMEM((HEADS_PER_STEP, HEADS_PER_KEY_TILE * HEAD_DIM, MOBA_BLOCK), _BF16),
                        pltpu.VMEM((GROUPS_PER_ITER + PIPE_DEPTH, HEADS_PER_STEP, MOBA_BLOCK, MOBA_BLOCK), _F32)],
        compiler_params=_params(2),
        name="moba",
    )(rel_bias, qa_t, ka, va_t, kmean, bias_own, bias_prev)


def _swa_body(sinks_ref, q_ref, k_prev_ref, k_cur_ref, v_prev_ref, v_cur_ref, bias_ref, o_ref, s_ref):
    no_prev = jnp.where(pl.program_id(1) == 0, NEG, 0.0)
    zeros = jnp.zeros((HEAD_DIM, SWA_BLOCK), _BF16)
    ones = jnp.ones((SUM_ROWS, SWA_BLOCK), _BF16)
    k_blocks = [k_prev_ref[0, 0]] + [k_cur_ref[0, t] for t in range(SWA_Q_PER_STEP)]
    v_blocks = [v_prev_ref[0, 0]] + [v_cur_ref[0, t] for t in range(SWA_Q_PER_STEP)]
    chains = [(t, kv) for t in range(SWA_Q_PER_STEP) for kv in range(N_KV_SWA)]

    for t, kv in chains:
        q_pad = jnp.concatenate(
            [jnp.concatenate(
                [q_ref[0, t, h * HEAD_DIM:(h + 1) * HEAD_DIM, :] if part == kv else zeros
                 for part in range(N_KV_SWA)], axis=0)
             for h in range(kv * SWA_GROUP, (kv + 1) * SWA_GROUP)], axis=1)
        s_ref[t, kv, 0] = _dot(k_blocks[t], q_pad)
        s_ref[t, kv, 1] = _dot(k_blocks[t + 1], q_pad)

    for t, kv in chains:
        heads = range(kv * SWA_GROUP, (kv + 1) * SWA_GROUP)
        kv_rows = slice(kv * HEAD_DIM, (kv + 1) * HEAD_DIM)
        p_prev, p_own, sink_p = [], [], []
        for slot, h in enumerate(heads):
            cols = slice(slot * SWA_BLOCK, (slot + 1) * SWA_BLOCK)
            sp = s_ref[t, kv, 0, :, cols] + bias_ref[h, :SWA_BLOCK, :]
            if t == 0:
                sp = sp + no_prev
            so = s_ref[t, kv, 1, :, cols] + bias_ref[h, SWA_BLOCK:, :]
            sink = sinks_ref[h] * LOG2E
            m = jnp.maximum(jnp.maximum(jnp.max(sp, axis=0, keepdims=True),
                                        jnp.max(so, axis=0, keepdims=True)), sink)
            p_prev.append(jnp.exp2(sp - m).astype(_BF16))
            p_own.append(jnp.exp2(so - m).astype(_BF16))
            sink_p.append(jnp.exp2(sink - m))
        acc = (_dot(jnp.concatenate([v_blocks[t][kv_rows, :], ones], axis=0), jnp.concatenate(p_prev, axis=1))
               + _dot(jnp.concatenate([v_blocks[t + 1][kv_rows, :], ones], axis=0),
                      jnp.concatenate(p_own, axis=1)))
        for slot, h in enumerate(heads):
            cols = slice(slot * SWA_BLOCK, (slot + 1) * SWA_BLOCK)
            l = acc[HEAD_DIM:HEAD_DIM + 1, cols] + sink_p[slot]
            o_ref[0, t, h * HEAD_DIM:(h + 1) * HEAD_DIM, :] = acc[:HEAD_DIM, cols] / l


def _swa(sinks, qs_t, ks, vs_t, bias_swa):
    batch, nb = qs_t.shape[0], qs_t.shape[1]

    def cur(rows, cols):
        return pl.BlockSpec((1, SWA_Q_PER_STEP, rows, cols), lambda b, n: (b, n, 0, 0))

    def prev(rows, cols):
        return pl.BlockSpec((1, 1, rows, cols),
                            lambda b, n: (b, jnp.maximum(n * SWA_Q_PER_STEP - 1, 0), 0, 0))

    return pl.pallas_call(
        _swa_body,
        grid=(batch, nb // SWA_Q_PER_STEP),
        in_specs=[pl.BlockSpec(memory_space=pltpu.SMEM),
                  cur(W_SWA, SWA_BLOCK),
                  prev(SWA_BLOCK, W_SWA_KV), cur(SWA_BLOCK, W_SWA_KV),
                  prev(W_SWA_KV, SWA_BLOCK), cur(W_SWA_KV, SWA_BLOCK),
                  _const_spec((N_HEADS_SWA, 2 * SWA_BLOCK, SWA_BLOCK))],
        out_specs=cur(W_SWA, SWA_BLOCK),
        out_shape=jax.ShapeDtypeStruct((batch, nb, W_SWA, SWA_BLOCK), _F32),
        scratch_shapes=[pltpu.VMEM((SWA_Q_PER_STEP, N_KV_SWA, 2, SWA_BLOCK, SWA_GROUP * SWA_BLOCK), _F32)],
        compiler_params=_params(2),
        name="swa",
    )(sinks, qs_t, ks, ks, vs_t, vs_t, bias_swa)


def _group_norm_rows(o_ref, g_ref):
    rows = []
    for t in range(o_ref.shape[0]):
        o_t = o_ref[t]
        scale = lax.rsqrt(jnp.mean(o_t * o_t, axis=0, keepdims=True) + RMS_EPS)
        rows.append(((o_t * scale).T * g_ref[...]).astype(_BF16))
    return jnp.concatenate(rows, axis=0)


def _mix_ffn_body(x_ref, oa_ref, ob_ref, ga_ref, gb_ref, wa_ref, wb_ref, mix_post_g_ref,
                  pre_g_ref, wg_ref, wu_ref, wd_ref, post_g_ref, o_ref, h_ref):
    y = (_dot(_group_norm_rows(oa_ref, ga_ref), wa_ref[...])
         + _dot(_group_norm_rows(ob_ref, gb_ref), wb_ref[...]))
    x = x_ref[...] + _rms_rows(y, mix_post_g_ref[...])
    _ffn_tile(lambda rows: x[rows, :], pre_g_ref, wg_ref, wu_ref, wd_ref, post_g_ref, o_ref, h_ref)


def _mix_ffn(x2d, oa_t, ob_t, ga, gb, wa, wb, mix_post_g, ffn_params):
    n_tok = x2d.shape[0]
    return pl.pallas_call(
        _mix_ffn_body,
        grid=(n_tok // TOKEN_TILE,),
        in_specs=[_TOKEN_TILE_SPEC,
                  pl.BlockSpec((_MOBA_PER_TILE, W_MOBA, MOBA_BLOCK), lambda i: (i, 0, 0)),
                  pl.BlockSpec((_SWA_PER_TILE, W_SWA, SWA_BLOCK), lambda i: (i, 0, 0)),
                  _const_spec((1, W_MOBA)), _const_spec((1, W_SWA)),
                  _const_spec((W_MOBA, D_MODEL)), _const_spec((W_SWA, D_MODEL)),
                  _const_spec((1, D_MODEL))] + _FFN_WEIGHT_SPECS,
        out_specs=_TOKEN_TILE_SPEC,
        out_shape=jax.ShapeDtypeStruct((n_tok, D_MODEL), _F32),
        scratch_shapes=[pltpu.VMEM((TOKEN_TILE, D_FF), _BF16)],
        compiler_params=_params(1),
        name="mix_ffn",
    )(x2d, oa_t, ob_t, ga, gb, wa, wb, mix_post_g, *ffn_params)


def _row(v):
    return v.reshape(1, -1)


def _layer(x, ffn1, mix, ffn2, rel_bias, bias_tiles):
    batch, seq, _ = x.shape
    (mix_pre_g, w_in, moba_out_g, swa_sinks, swa_out_g, w_out, mix_post_g) = mix
    bias_own, bias_prev, bias_swa = bias_tiles

    def ffn_params(params):
        pre_g, w_gate, w_up, w_down, post_g = params
        return (_row(pre_g), w_gate.astype(_BF16), w_up.astype(_BF16), w_down.astype(_BF16), _row(post_g))

    x = _ffn(x.reshape(batch * seq, D_MODEL), ffn_params(ffn1))

    qa, ka, va, qs, ks, vs = jnp.split(
        w_in, [W_MOBA, 2 * W_MOBA, 3 * W_MOBA, 3 * W_MOBA + W_SWA, 3 * W_MOBA + W_SWA + W_SWA_KV], axis=1)
    w_rows = jnp.concatenate([ka, ks], axis=1).astype(_BF16)
    w_t = jnp.concatenate([qa, va, qs, vs], axis=1).T.astype(_BF16)
    qa_t, va_t, qs_t, vs_t, ka_b, ks_b, kmean = _in_proj(
        x.reshape(batch, seq, D_MODEL), _row(mix_pre_g), w_rows, w_t)

    oa_t = _moba(rel_bias, qa_t, ka_b, va_t, kmean.reshape(batch, seq // MOBA_BLOCK, W_MOBA),
                 bias_own, bias_prev)
    ob_t = _swa(swa_sinks, qs_t, ks_b, vs_t, bias_swa)

    w_out = w_out.astype(_BF16)
    x = _mix_ffn(x, oa_t.reshape(-1, W_MOBA, MOBA_BLOCK), ob_t.reshape(-1, W_SWA, SWA_BLOCK),
                 _row(moba_out_g), _row(swa_out_g), w_out[:W_MOBA], w_out[W_MOBA:], _row(mix_post_g),
                 ffn_params(ffn2))
    return x.reshape(batch, seq, D_MODEL)


def kernel(x, ffn1_pre_g, ffn1_w_gate, ffn1_w_up, ffn1_w_down, ffn1_post_g, mix_pre_g, w_in, rel_bias,
           moba_out_g, swa_sinks, swa_out_g, w_out, mix_post_g, ffn2_pre_g, ffn2_w_gate, ffn2_w_up,
           ffn2_w_down, ffn2_post_g):
    bias_tiles = _bias_tiles(rel_bias)
    for l in range(ffn1_pre_g.shape[0]):
        x = _layer(
            x,
            (ffn1_pre_g[l], ffn1_w_gate[l], ffn1_w_up[l], ffn1_w_down[l], ffn1_post_g[l]),
            (mix_pre_g[l], w_in[l], moba_out_g[l], swa_sinks[l], swa_out_g[l], w_out[l], mix_post_g[l]),
            (ffn2_pre_g[l], ffn2_w_gate[l], ffn2_w_up[l], ffn2_w_down[l], ffn2_post_g[l]),
            rel_bias, bias_tiles)
    return x
```

```python
import functools
import math

import jax
import jax.numpy as jnp
from jax import lax
from jax.experimental import pallas as pl
from jax.experimental.pallas import tpu as pltpu

D_MODEL = 1024
HEAD_DIM = 64
N_HEADS_MOBA = 8
N_HEADS_SWA = 8
N_KV_SWA = 2
SWA_GROUP = N_HEADS_SWA // N_KV_SWA
W_MOBA = N_HEADS_MOBA * HEAD_DIM
W_SWA = N_HEADS_SWA * HEAD_DIM
W_SWA_KV = N_KV_SWA * HEAD_DIM
MOBA_BLOCK = 256
MOBA_TOPK = 3
SWA_BLOCK = 128
SWA_WINDOW = 128
NUM_BUCKETS = 32
MAX_DISTANCE = 128
D_FF = 2816
RMS_EPS = 1e-6
FFN_RES_WEIGHT = 0.5
NEG = -1e30
QK_SCALE = HEAD_DIM ** -0.5
LOG2E = math.log2(math.e)
Q_FOLD = QK_SCALE * LOG2E
SUM_ROWS = 16

TOKEN_TILE = 1024
FF_CHUNK = 256
FFN_ROW_PARTS = 2
HEADS_PER_STEP = 4
HEADS_PER_KEY_TILE = 2
PIPE_DEPTH = 2
GROUPS_PER_ITER = 4
SWA_Q_PER_STEP = 4
VMEM_LIMIT_BYTES = 56 * 1024 * 1024

_BF16 = jnp.bfloat16
_F32 = jnp.float32


def _dot(a, b):
    return jnp.dot(a, b, preferred_element_type=_F32)


def _dot_nt(a, b):
    return lax.dot_general(a, b, (((1,), (1,)), ((), ())), preferred_element_type=_F32)


def _rms_rows(x, g):
    return x * lax.rsqrt(jnp.mean(x * x, axis=-1, keepdims=True) + RMS_EPS) * g


def _const_spec(shape):
    return pl.BlockSpec(shape, lambda *_: (0,) * len(shape), pipeline_mode=pl.Buffered(1))


def _params(n_axes):
    return pltpu.CompilerParams(
        dimension_semantics=("arbitrary",) * n_axes, vmem_limit_bytes=VMEM_LIMIT_BYTES)


def _ffn_tile(x_rows, pre_g_ref, wg_ref, wu_ref, wd_ref, post_g_ref, o_ref, h_ref):
    part = TOKEN_TILE // FFN_ROW_PARTS
    for p in range(FFN_ROW_PARTS):
        rows = slice(p * part, (p + 1) * part)
        x = x_rows(rows)
        xn = _rms_rows(x, pre_g_ref[...]).astype(_BF16)
        for c in range(D_FF // FF_CHUNK):
            cols = slice(c * FF_CHUNK, (c + 1) * FF_CHUNK)
            gate = _dot(xn, wg_ref[:, cols])
            up = _dot(xn, wu_ref[:, cols])
            h_ref[rows, cols] = (jax.nn.silu(gate) * up).astype(_BF16)
        y = _dot(h_ref[rows, :], wd_ref[...])
        o_ref[rows, :] = x + FFN_RES_WEIGHT * _rms_rows(y, post_g_ref[...])


def _ffn_body(x_ref, pre_g_ref, wg_ref, wu_ref, wd_ref, post_g_ref, o_ref, h_ref):
    _ffn_tile(lambda rows: x_ref[rows, :], pre_g_ref, wg_ref, wu_ref, wd_ref, post_g_ref, o_ref, h_ref)


_TOKEN_TILE_SPEC = pl.BlockSpec((TOKEN_TILE, D_MODEL), lambda i: (i, 0))
_FFN_WEIGHT_SPECS = [_const_spec((1, D_MODEL)), _const_spec((D_MODEL, D_FF)), _const_spec((D_MODEL, D_FF)),
                     _const_spec((D_FF, D_MODEL)), _const_spec((1, D_MODEL))]


def _ffn(x2d, ffn_params):
    n_tok = x2d.shape[0]
    return pl.pallas_call(
        _ffn_body,
        grid=(n_tok // TOKEN_TILE,),
        in_specs=[_TOKEN_TILE_SPEC] + _FFN_WEIGHT_SPECS,
        out_specs=_TOKEN_TILE_SPEC,
        out_shape=jax.ShapeDtypeStruct((n_tok, D_MODEL), _F32),
        scratch_shapes=[pltpu.VMEM((TOKEN_TILE, D_FF), _BF16)],
        compiler_params=_params(1),
        name="ffn",
    )(x2d, *ffn_params)


_T_QA, _T_VA, _T_QS, _T_VS = 0, W_MOBA, 2 * W_MOBA, 2 * W_MOBA + W_SWA
_T_ROWS = 2 * W_MOBA + W_SWA + W_SWA_KV
_MOBA_PER_TILE = TOKEN_TILE // MOBA_BLOCK
_SWA_PER_TILE = TOKEN_TILE // SWA_BLOCK


def _in_proj_body(x_ref, g_ref, w_rows_ref, w_t_ref,
                  qa_ref, va_ref, qs_ref, vs_ref, ka_ref, ks_ref, kmean_ref):
    xn = _rms_rows(x_ref[0], g_ref[...]).astype(_BF16)
    k_rows = _dot(xn, w_rows_ref[...])
    proj_t = _dot_nt(w_t_ref[...], xn)
    for t in range(_MOBA_PER_TILE):
        rows = slice(t * MOBA_BLOCK, (t + 1) * MOBA_BLOCK)
        k_blk = k_rows[rows, :W_MOBA]
        ka_ref[0, t] = k_blk.astype(_BF16)
        kmean_ref[0, t] = jnp.mean(k_blk, axis=0, keepdims=True)
        qa_ref[0, t] = (proj_t[_T_QA:_T_QA + W_MOBA, rows] * Q_FOLD).astype(_BF16)
        va_ref[0, t] = proj_t[_T_VA:_T_VA + W_MOBA, rows].astype(_BF16)
    for t in range(_SWA_PER_TILE):
        rows = slice(t * SWA_BLOCK, (t + 1) * SWA_BLOCK)
        ks_ref[0, t] = k_rows[rows, W_MOBA:].astype(_BF16)
        qs_ref[0, t] = (proj_t[_T_QS:_T_QS + W_SWA, rows] * Q_FOLD).astype(_BF16)
        vs_ref[0, t] = proj_t[_T_VS:_T_VS + W_SWA_KV, rows].astype(_BF16)


def _in_proj(x, g, w_rows, w_t):
    batch, seq, _ = x.shape
    nb_moba, nb_swa = seq // MOBA_BLOCK, seq // SWA_BLOCK

    def blocked(n_per_tile, rows, cols):
        return pl.BlockSpec((1, n_per_tile, rows, cols), lambda b, i: (b, i, 0, 0))

    return pl.pallas_call(
        _in_proj_body,
        grid=(batch, seq // TOKEN_TILE),
        in_specs=[pl.BlockSpec((1, TOKEN_TILE, D_MODEL), lambda b, i: (b, i, 0)),
                  _const_spec((1, D_MODEL)),
                  _const_spec((D_MODEL, W_MOBA + W_SWA_KV)),
                  _const_spec((_T_ROWS, D_MODEL))],
        out_specs=[blocked(_MOBA_PER_TILE, W_MOBA, MOBA_BLOCK),
                   blocked(_MOBA_PER_TILE, W_MOBA, MOBA_BLOCK),
                   blocked(_SWA_PER_TILE, W_SWA, SWA_BLOCK),
                   blocked(_SWA_PER_TILE, W_SWA_KV, SWA_BLOCK),
                   blocked(_MOBA_PER_TILE, MOBA_BLOCK, W_MOBA),
                   blocked(_SWA_PER_TILE, SWA_BLOCK, W_SWA_KV),
                   blocked(_MOBA_PER_TILE, 1, W_MOBA)],
        out_shape=[jax.ShapeDtypeStruct((batch, nb_moba, W_MOBA, MOBA_BLOCK), _BF16),
                   jax.ShapeDtypeStruct((batch, nb_moba, W_MOBA, MOBA_BLOCK), _BF16),
                   jax.ShapeDtypeStruct((batch, nb_swa, W_SWA, SWA_BLOCK), _BF16),
                   jax.ShapeDtypeStruct((batch, nb_swa, W_SWA_KV, SWA_BLOCK), _BF16),
                   jax.ShapeDtypeStruct((batch, nb_moba, MOBA_BLOCK, W_MOBA), _BF16),
                   jax.ShapeDtypeStruct((batch, nb_swa, SWA_BLOCK, W_SWA_KV), _BF16),
                   jax.ShapeDtypeStruct((batch, nb_moba, 1, W_MOBA), _F32)],
        compiler_params=_params(2),
        name="in_proj",
    )(x, g, w_rows, w_t)


def _t5_bucket(dist):
    n = jnp.maximum(dist, 0)
    max_exact = NUM_BUCKETS // 2
    nf = jnp.maximum(n, 1).astype(_F32)
    large = max_exact + jnp.floor(jnp.log(nf / max_exact) / math.log(MAX_DISTANCE / max_exact)
                                  * (NUM_BUCKETS - max_exact))
    large = jnp.minimum(large, float(NUM_BUCKETS - 1))
    return jnp.where(n < max_exact, n.astype(_F32), large)


def _bias_lookup(rel_bias_ref, bucket, head):
    val = jnp.full(bucket.shape, rel_bias_ref[NUM_BUCKETS - 1, head], _F32)
    for b in range(NUM_BUCKETS - 2, -1, -1):
        val = jnp.where(bucket == float(b), rel_bias_ref[b, head], val)
    return val


def _bias_tiles_body(rel_bias_ref, own_ref, prev_ref, swa_ref):
    h = pl.program_id(0)
    key = lax.broadcasted_iota(jnp.int32, (MOBA_BLOCK, MOBA_BLOCK), 0)
    qry = lax.broadcasted_iota(jnp.int32, (MOBA_BLOCK, MOBA_BLOCK), 1)
    dist = qry - key
    own_ref[0] = jnp.where(dist >= 0, _bias_lookup(rel_bias_ref, _t5_bucket(dist), h) * LOG2E, NEG)
    prev_ref[0] = _bias_lookup(rel_bias_ref, _t5_bucket(dist + MOBA_BLOCK), h) * LOG2E
    key = lax.broadcasted_iota(jnp.int32, (2 * SWA_BLOCK, SWA_BLOCK), 0) - SWA_BLOCK
    qry = lax.broadcasted_iota(jnp.int32, (2 * SWA_BLOCK, SWA_BLOCK), 1)
    dist = qry - key
    in_win = (dist >= 0) & (dist < SWA_WINDOW)
    swa_ref[0] = jnp.where(
        in_win, _bias_lookup(rel_bias_ref, _t5_bucket(dist), h + N_HEADS_MOBA) * LOG2E, NEG)


def _bias_tiles(rel_bias):
    def per_head(rows, cols):
        return pl.BlockSpec((1, rows, cols), lambda h: (h, 0, 0))

    return pl.pallas_call(
        _bias_tiles_body,
        grid=(N_HEADS_MOBA,),
        in_specs=[pl.BlockSpec(memory_space=pltpu.SMEM)],
        out_specs=[per_head(MOBA_BLOCK, MOBA_BLOCK), per_head(MOBA_BLOCK, MOBA_BLOCK),
                   per_head(2 * SWA_BLOCK, SWA_BLOCK)],
        out_shape=[jax.ShapeDtypeStruct((N_HEADS_MOBA, MOBA_BLOCK, MOBA_BLOCK), _F32),
                   jax.ShapeDtypeStruct((N_HEADS_MOBA, MOBA_BLOCK, MOBA_BLOCK), _F32),
                   jax.ShapeDtypeStruct((N_HEADS_SWA, 2 * SWA_BLOCK, SWA_BLOCK), _F32)],
        compiler_params=_params(1),
        name="bias_tiles",
    )(rel_bias)


def _moba_body(rel_bias_ref, q_ref, k_ref, v_ref, kmean_ref, own_ref, prev_ref, o_ref,
               far_ref, qpad_ref, s_ref):
    g = pl.program_id(1)
    nb = k_ref.shape[1]
    key_lanes = HEADS_PER_KEY_TILE * HEAD_DIM
    q_row = lax.broadcasted_iota(jnp.int32, (key_lanes, MOBA_BLOCK), 0)
    km = kmean_ref[0]
    km_hi = km.astype(_BF16)
    km_lo = (km - km_hi.astype(_F32)).astype(_BF16)
    blk = lax.broadcasted_iota(jnp.int32, (nb, MOBA_BLOCK), 0)
    ones = jnp.ones((SUM_ROWS, MOBA_BLOCK), _BF16)
    ring_slots = range(GROUPS_PER_ITER)
    tail_slots = ring_slots[-PIPE_DEPTH:]
    near_slots = range(GROUPS_PER_ITER, GROUPS_PER_ITER + PIPE_DEPTH)
    assert PIPE_DEPTH == 2 and GROUPS_PER_ITER >= 2 * PIPE_DEPTH

    def key_tile(r):
        t, sub = divmod(r, HEADS_PER_KEY_TILE)
        return slice(t * key_lanes, (t + 1) * key_lanes), sub

    def prepare(i):
        past = blk < i
        own, prev = [], []
        for r in range(HEADS_PER_STEP):
            tile_lanes, sub = key_tile(r)
            q_t = q_ref[0, i, tile_lanes, :]
            q_pad = jnp.where((q_row >= sub * HEAD_DIM) & (q_row < (sub + 1) * HEAD_DIM), q_t, 0)
            qpad_ref[r] = q_pad

            gate = jnp.where(
                past, _dot(km_hi[:, tile_lanes], q_pad) + _dot(km_lo[:, tile_lanes], q_pad), NEG)
            sel = jnp.zeros(gate.shape, jnp.bool_)
            for _ in range(MOBA_TOPK):
                top = jnp.max(gate, axis=0, keepdims=True)
                first = jnp.min(jnp.where(gate == top, blk, nb), axis=0, keepdims=True)
                pick = blk == first
                sel = sel | pick
                gate = jnp.where(pick, -jnp.inf, gate)
            sel = sel & past

            far_bias = rel_bias_ref[NUM_BUCKETS - 1, g * HEADS_PER_STEP + r] * LOG2E
            far_ref[r] = jnp.where(sel & (blk < i - 1), far_bias, NEG)
            prev_neg = jnp.max(jnp.where(sel & (blk == i - 1), 0.0, NEG), axis=0, keepdims=True)
            own.append(own_ref[r])
            prev.append(prev_ref[r] + prev_neg)
        return own, prev

    def score(slot, j, tile_bias=None, row_bias=None):
        tops, shifts = [], []
        for r in range(HEADS_PER_STEP):
            s = _dot(k_ref[0, j, :, key_tile(r)[0]], qpad_ref[r])
            if tile_bias is not None:
                s = s + tile_bias[r]
            s_ref[slot, r] = s
            top = jnp.max(s, axis=0, keepdims=True)
            shift = jnp.zeros_like(top) if row_bias is None else row_bias[r]
            tops.append(top + shift)
            shifts.append(shift)
        return slot, j, tuple(tops), tuple(shifts)

    def absorb(carry, scored):
        slot, j, tops, shifts = scored
        out = []
        for r in range(HEADS_PER_STEP):
            lanes = slice(r * HEAD_DIM, (r + 1) * HEAD_DIM)
            m, acc = carry[r]
            m_new = jnp.maximum(m, tops[r])
            p = jnp.exp2(s_ref[slot, r] + (shifts[r] - m_new)).astype(_BF16)
            pv = _dot(jnp.concatenate([v_ref[0, j, lanes, :], ones], axis=0), p)
            out.append((m_new, jnp.exp2(m - m_new) * acc + pv))
        return tuple(out)

    def score_far(slot, grp):
        j = jnp.minimum(grp, nb - 1)
        return score(slot, j, row_bias=[far_ref[r, pl.ds(j, 1), :] for r in range(HEADS_PER_STEP)])

    def score_near(i):
        own, prev = prepare(i)
        return (score(near_slots[0], i, tile_bias=own)[2],
                score(near_slots[1], jnp.maximum(i - 1, 0), tile_bias=prev)[2])

    def start(i, near_tops):
        carry = tuple((jnp.full((1, MOBA_BLOCK), -jnp.inf, _F32),
                       jnp.zeros((HEAD_DIM + SUM_ROWS, MOBA_BLOCK), _F32)) for _ in range(HEADS_PER_STEP))
        near_js = (i, jnp.maximum(i - 1, 0))
        no_shift = tuple(jnp.zeros((1, MOBA_BLOCK), _F32) for _ in range(HEADS_PER_STEP))
        ring = {slot: (slot, j, tops, no_shift) for slot, j, tops in zip(near_slots, near_js, near_tops)}
        for c, slot in enumerate(tail_slots):
            ring[slot] = score_far(slot, c)
            carry = absorb(carry, ring[near_slots[c]])
        return carry, tuple(ring[slot][1:] for slot in tail_slots)

    def tile(i, state, last=False):
        def step(it, state):
            carry, pending = state
            ring = {slot: (slot,) + p for slot, p in zip(tail_slots, pending)}
            for k in ring_slots:
                scored = score_far(k, PIPE_DEPTH + it * GROUPS_PER_ITER + k)
                carry = absorb(carry, ring[(k - PIPE_DEPTH) % GROUPS_PER_ITER])
                ring[k] = scored
            return carry, tuple(ring[slot][1:] for slot in tail_slots)

        n_far = jnp.maximum(i - 1, 0)
        n_iters = (jnp.maximum(n_far - PIPE_DEPTH, 0) + GROUPS_PER_ITER - 1) // GROUPS_PER_ITER
        carry, pending = lax.fori_loop(0, n_iters, step, state)

        near_next = None if last else score_near(i + 1)
        for slot, p in zip(tail_slots, pending):
            carry = absorb(carry, (slot,) + p)
        for r in range(HEADS_PER_STEP):
            _, acc = carry[r]
            o_ref[0, i, r * HEAD_DIM:(r + 1) * HEAD_DIM, :] = acc[:HEAD_DIM] / acc[HEAD_DIM:HEAD_DIM + 1]
        return None if last else start(i + 1, near_next)

    state = lax.fori_loop(0, nb - 1, tile, start(0, score_near(0)))
    tile(nb - 1, state, last=True)


def _moba(rel_bias, qa_t, ka, va_t, kmean, bias_own, bias_prev):
    batch, nb = qa_t.shape[0], qa_t.shape[1]
    width = HEADS_PER_STEP * HEAD_DIM
    q_spec = pl.BlockSpec((1, nb, width, MOBA_BLOCK), lambda b, g: (b, 0, g, 0))
    bias_spec = pl.BlockSpec((HEADS_PER_STEP, MOBA_BLOCK, MOBA_BLOCK), lambda b, g: (g, 0, 0))
    return pl.pallas_call(
        _moba_body,
        grid=(batch, N_HEADS_MOBA // HEADS_PER_STEP),
        in_specs=[pl.BlockSpec(memory_space=pltpu.SMEM),
                  q_spec,
                  pl.BlockSpec((1, nb, MOBA_BLOCK, width), lambda b, g: (b, 0, 0, g)),
                  q_spec,
                  pl.BlockSpec((1, nb, width), lambda b, g: (b, 0, g)),
                  bias_spec, bias_spec],
        out_specs=q_spec,
        out_shape=jax.ShapeDtypeStruct((batch, nb, W_MOBA, MOBA_BLOCK), _F32),
        scratch_shapes=[pltpu.VMEM((HEADS_PER_STEP, nb, MOBA_BLOCK), _F32),
                        pltpu.VMEM((HEADS_PER_STEP, HEADS_PER_KEY_TILE * HEAD_DIM, MOBA_BLOCK), _BF16),
                        pltpu.VMEM((GROUPS_PER_ITER + PIPE_DEPTH, HEADS_PER_STEP, MOBA_BLOCK, MOBA_BLOCK), _F32)],
        compiler_params=_params(2),
        name="moba",
    )(rel_bias, qa_t, ka, va_t, kmean, bias_own, bias_prev)


def _swa_body(sinks_ref, q_ref, k_prev_ref, k_cur_ref, v_prev_ref, v_cur_ref, bias_ref, o_ref, s_ref):
    no_prev = jnp.where(pl.program_id(1) == 0, NEG, 0.0)
    window_row = lax.broadcasted_iota(jnp.int32, (2 * SWA_BLOCK, SWA_BLOCK), 0)
    first_window_mask = jnp.where(window_row < SWA_BLOCK, no_prev, 0.0)
    zeros = jnp.zeros((HEAD_DIM, SWA_BLOCK), _BF16)
    ones = jnp.ones((SUM_ROWS, 2 * SWA_BLOCK), _BF16)
    k_blocks = [k_prev_ref[0, 0]] + [k_cur_ref[0, t] for t in range(SWA_Q_PER_STEP)]
    v_blocks = [v_prev_ref[0, 0]] + [v_cur_ref[0, t] for t in range(SWA_Q_PER_STEP)]
    chains = [(t, kv) for t in range(SWA_Q_PER_STEP) for kv in range(N_KV_SWA)]

    for t, kv in chains:
        q_pad = jnp.concatenate(
            [jnp.concatenate(
                [q_ref[0, t, h * HEAD_DIM:(h + 1) * HEAD_DIM, :] if part == kv else zeros
                 for part in range(N_KV_SWA)], axis=0)
             for h in range(kv * SWA_GROUP, (kv + 1) * SWA_GROUP)], axis=1)
        window_keys = jnp.concatenate([k_blocks[t], k_blocks[t + 1]], axis=0)
        s = _dot(window_keys, q_pad)
        for slot in range(SWA_GROUP):
            cols = slice(slot * SWA_BLOCK, (slot + 1) * SWA_BLOCK)
            biased = s[:, cols] + bias_ref[kv * SWA_GROUP + slot]
            s_ref[t, kv, :, cols] = biased + first_window_mask if t == 0 else biased

    for t, kv in chains:
        kv_rows = slice(kv * HEAD_DIM, (kv + 1) * HEAD_DIM)
        window_values = jnp.concatenate(
            [jnp.concatenate([v_blocks[t][kv_rows, :], v_blocks[t + 1][kv_rows, :]], axis=1), ones], axis=0)
        p, sink_p = [], []
        for slot in range(SWA_GROUP):
            cols = slice(slot * SWA_BLOCK, (slot + 1) * SWA_BLOCK)
            sink = sinks_ref[kv * SWA_GROUP + slot] * LOG2E
            m = jnp.maximum(jnp.max(s_ref[t, kv, :, cols], axis=0, keepdims=True), sink)
            p.append(jnp.exp2(s_ref[t, kv, :, cols] - m).astype(_BF16))
            sink_p.append(jnp.exp2(sink - m))
        acc = _dot(window_values, jnp.concatenate(p, axis=1))
        for slot in range(SWA_GROUP):
            h = kv * SWA_GROUP + slot
            cols = slice(slot * SWA_BLOCK, (slot + 1) * SWA_BLOCK)
            l = acc[HEAD_DIM:HEAD_DIM + 1, cols] + sink_p[slot]
            o_ref[0, t, h * HEAD_DIM:(h + 1) * HEAD_DIM, :] = acc[:HEAD_DIM, cols] / l


def _swa(sinks, qs_t, ks, vs_t, bias_swa):
    batch, nb = qs_t.shape[0], qs_t.shape[1]

    def cur(rows, cols):
        return pl.BlockSpec((1, SWA_Q_PER_STEP, rows, cols), lambda b, n: (b, n, 0, 0))

    def prev(rows, cols):
        return pl.BlockSpec((1, 1, rows, cols),
                            lambda b, n: (b, jnp.maximum(n * SWA_Q_PER_STEP - 1, 0), 0, 0))

    return pl.pallas_call(
        _swa_body,
        grid=(batch, nb // SWA_Q_PER_STEP),
        in_specs=[pl.BlockSpec(memory_space=pltpu.SMEM),
                  cur(W_SWA, SWA_BLOCK),
                  prev(SWA_BLOCK, W_SWA_KV), cur(SWA_BLOCK, W_SWA_KV),
                  prev(W_SWA_KV, SWA_BLOCK), cur(W_SWA_KV, SWA_BLOCK),
                  _const_spec((N_HEADS_SWA, 2 * SWA_BLOCK, SWA_BLOCK))],
        out_specs=cur(W_SWA, SWA_BLOCK),
        out_shape=jax.ShapeDtypeStruct((batch, nb, W_SWA, SWA_BLOCK), _F32),
        scratch_shapes=[pltpu.VMEM((SWA_Q_PER_STEP, N_KV_SWA, 2 * SWA_BLOCK, SWA_GROUP * SWA_BLOCK), _F32)],
        compiler_params=_params(2),
        name="swa",
    )(sinks, qs_t, ks, ks, vs_t, vs_t, bias_swa)


def _group_norm_rows(o_ref, g_ref):
    rows = []
    for t in range(o_ref.shape[0]):
        o_t = o_ref[t]
        scale = lax.rsqrt(jnp.mean(o_t * o_t, axis=0, keepdims=True) + RMS_EPS)
        rows.append(((o_t * scale).T * g_ref[...]).astype(_BF16))
    return jnp.concatenate(rows, axis=0)


def _mix_ffn_body(x_ref, oa_ref, ob_ref, ga_ref, gb_ref, wa_ref, wb_ref, mix_post_g_ref,
                  pre_g_ref, wg_ref, wu_ref, wd_ref, post_g_ref, o_ref, h_ref):
    y = (_dot(_group_norm_rows(oa_ref, ga_ref), wa_ref[...])
         + _dot(_group_norm_rows(ob_ref, gb_ref), wb_ref[...]))
    x = x_ref[...] + _rms_rows(y, mix_post_g_ref[...])
    _ffn_tile(lambda rows: x[rows, :], pre_g_ref, wg_ref, wu_ref, wd_ref, post_g_ref, o_ref, h_ref)


def _mix_ffn(x2d, oa_t, ob_t, ga, gb, wa, wb, mix_post_g, ffn_params):
    n_tok = x2d.shape[0]
    return pl.pallas_call(
        _mix_ffn_body,
        grid=(n_tok // TOKEN_TILE,),
        in_specs=[_TOKEN_TILE_SPEC,
                  pl.BlockSpec((_MOBA_PER_TILE, W_MOBA, MOBA_BLOCK), lambda i: (i, 0, 0)),
                  pl.BlockSpec((_SWA_PER_TILE, W_SWA, SWA_BLOCK), lambda i: (i, 0, 0)),
                  _const_spec((1, W_MOBA)), _const_spec((1, W_SWA)),
                  _const_spec((W_MOBA, D_MODEL)), _const_spec((W_SWA, D_MODEL)),
                  _const_spec((1, D_MODEL))] + _FFN_WEIGHT_SPECS,
        out_specs=_TOKEN_TILE_SPEC,
        out_shape=jax.ShapeDtypeStruct((n_tok, D_MODEL), _F32),
        scratch_shapes=[pltpu.VMEM((TOKEN_TILE, D_FF), _BF16)],
        compiler_params=_params(1),
        name="mix_ffn",
    )(x2d, oa_t, ob_t, ga, gb, wa, wb, mix_post_g, *ffn_params)


def _row(v):
    return v.reshape(1, -1)


def _layer(x, ffn1, mix, ffn2, rel_bias, bias_tiles):
    batch, seq, _ = x.shape
    (mix_pre_g, w_in, moba_out_g, swa_sinks, swa_out_g, w_out, mix_post_g) = mix
    bias_own, bias_prev, bias_swa = bias_tiles

    def ffn_params(params):
        pre_g, w_gate, w_up, w_down, post_g = params
        return (_row(pre_g), w_gate.astype(_BF16), w_up.astype(_BF16), w_down.astype(_BF16), _row(post_g))

    x = _ffn(x.reshape(batch * seq, D_MODEL), ffn_params(ffn1))

    qa, ka, va, qs, ks, vs = jnp.split(
        w_in, [W_MOBA, 2 * W_MOBA, 3 * W_MOBA, 3 * W_MOBA + W_SWA, 3 * W_MOBA + W_SWA + W_SWA_KV], axis=1)
    w_rows = jnp.concatenate([ka, ks], axis=1).astype(_BF16)
    w_t = jnp.concatenate([qa, va, qs, vs], axis=1).T.astype(_BF16)
    qa_t, va_t, qs_t, vs_t, ka_b, ks_b, kmean = _in_proj(
        x.reshape(batch, seq, D_MODEL), _row(mix_pre_g), w_rows, w_t)

    oa_t = _moba(rel_bias, qa_t, ka_b, va_t, kmean.reshape(batch, seq // MOBA_BLOCK, W_MOBA),
                 bias_own, bias_prev)
    ob_t = _swa(swa_sinks, qs_t, ks_b, vs_t, bias_swa)

    w_out = w_out.astype(_BF16)
    x = _mix_ffn(x, oa_t.reshape(-1, W_MOBA, MOBA_BLOCK), ob_t.reshape(-1, W_SWA, SWA_BLOCK),
                 _row(moba_out_g), _row(swa_out_g), w_out[:W_MOBA], w_out[W_MOBA:], _row(mix_post_g),
                 ffn_params(ffn2))
    return x.reshape(batch, seq, D_MODEL)


def kernel(x, ffn1_pre_g, ffn1_w_gate, ffn1_w_up, ffn1_w_down, ffn1_post_g, mix_pre_g, w_in, rel_bias,
           moba_out_g, swa_sinks, swa_out_g, w_out, mix_post_g, ffn2_pre_g, ffn2_w_gate, ffn2_w_up,
           ffn2_w_down, ffn2_post_g):
    bias_tiles = _bias_tiles(rel_bias)
    for l in range(ffn1_pre_g.shape[0]):
        x = _layer(
            x,
            (ffn1_pre_g[l], ffn1_w_gate[l], ffn1_w_up[l], ffn1_w_down[l], ffn1_post_g[l]),
            (mix_pre_g[l], w_in[l], moba_out_g[l], swa_sinks[l], swa_out_g[l], w_out[l], mix_post_g[l]),
            (ffn2_pre_g[l], ffn2_w_gate[l], ffn2_w_up[l], ffn2_w_down[l], ffn2_post_g[l]),
            rel_bias, bias_tiles)
    return x
```

```python
import functools
import math

import jax
import jax.numpy as jnp
from jax import lax
from jax.experimental import pallas as pl
from jax.experimental.pallas import tpu as pltpu

D_MODEL = 1024
HEAD_DIM = 64
N_HEADS_MOBA = 8
N_HEADS_SWA = 8
N_KV_SWA = 2
SWA_GROUP = N_HEADS_SWA // N_KV_SWA
W_MOBA = N_HEADS_MOBA * HEAD_DIM
W_SWA = N_HEADS_SWA * HEAD_DIM
W_SWA_KV = N_KV_SWA * HEAD_DIM
MOBA_BLOCK = 256
MOBA_TOPK = 3
SWA_BLOCK = 128
SWA_WINDOW = 128
NUM_BUCKETS = 32
MAX_DISTANCE = 128
D_FF = 2816
RMS_EPS = 1e-6
FFN_RES_WEIGHT = 0.5
NEG = -1e30
QK_SCALE = HEAD_DIM ** -0.5
LOG2E = math.log2(math.e)
Q_FOLD = QK_SCALE * LOG2E
SUM_ROWS = 16

TOKEN_TILE = 1024
FF_CHUNK = 256
FFN_ROW_PARTS = 2
HEADS_PER_STEP = 4
HEADS_PER_KEY_TILE = 2
PIPE_DEPTH = 2
GROUPS_PER_ITER = 4
SWA_Q_PER_STEP = 4
VMEM_LIMIT_BYTES = 56 * 1024 * 1024

_BF16 = jnp.bfloat16
_F32 = jnp.float32


def _dot(a, b):
    return jnp.dot(a, b, preferred_element_type=_F32)


def _dot_nt(a, b):
    return lax.dot_general(a, b, (((1,), (1,)), ((), ())), preferred_element_type=_F32)


def _rms_rows(x, g):
    return x * lax.rsqrt(jnp.mean(x * x, axis=-1, keepdims=True) + RMS_EPS) * g


def _const_spec(shape):
    return pl.BlockSpec(shape, lambda *_: (0,) * len(shape), pipeline_mode=pl.Buffered(1))


def _params(n_axes):
    return pltpu.CompilerParams(
        dimension_semantics=("arbitrary",) * n_axes, vmem_limit_bytes=VMEM_LIMIT_BYTES)


def _ffn_tile(x_rows, pre_g_ref, wg_ref, wu_ref, wd_ref, post_g_ref, o_ref, h_ref):
    part = TOKEN_TILE // FFN_ROW_PARTS
    for p in range(FFN_ROW_PARTS):
        rows = slice(p * part, (p + 1) * part)
        x = x_rows(rows)
        xn = _rms_rows(x, pre_g_ref[...]).astype(_BF16)
        for c in range(D_FF // FF_CHUNK):
            cols = slice(c * FF_CHUNK, (c + 1) * FF_CHUNK)
            gate = _dot(xn, wg_ref[:, cols])
            up = _dot(xn, wu_ref[:, cols])
            h_ref[rows, cols] = (jax.nn.silu(gate) * up).astype(_BF16)
        y = _dot(h_ref[rows, :], wd_ref[...])
        o_ref[rows, :] = x + FFN_RES_WEIGHT * _rms_rows(y, post_g_ref[...])


def _ffn_body(x_ref, pre_g_ref, wg_ref, wu_ref, wd_ref, post_g_ref, o_ref, h_ref):
    _ffn_tile(lambda rows: x_ref[rows, :], pre_g_ref, wg_ref, wu_ref, wd_ref, post_g_ref, o_ref, h_ref)


_TOKEN_TILE_SPEC = pl.BlockSpec((TOKEN_TILE, D_MODEL), lambda i: (i, 0))
_FFN_WEIGHT_SPECS = [_const_spec((1, D_MODEL)), _const_spec((D_MODEL, D_FF)), _const_spec((D_MODEL, D_FF)),
                     _const_spec((D_FF, D_MODEL)), _const_spec((1, D_MODEL))]


def _ffn(x2d, ffn_params):
    n_tok = x2d.shape[0]
    return pl.pallas_call(
        _ffn_body,
        grid=(n_tok // TOKEN_TILE,),
        in_specs=[_TOKEN_TILE_SPEC] + _FFN_WEIGHT_SPECS,
        out_specs=_TOKEN_TILE_SPEC,
        out_shape=jax.ShapeDtypeStruct((n_tok, D_MODEL), _F32),
        scratch_shapes=[pltpu.VMEM((TOKEN_TILE, D_FF), _BF16)],
        compiler_params=_params(1),
        name="ffn",
    )(x2d, *ffn_params)


_T_QA, _T_VA, _T_QS, _T_VS = 0, W_MOBA, 2 * W_MOBA, 2 * W_MOBA + W_SWA
_T_ROWS = 2 * W_MOBA + W_SWA + W_SWA_KV
_MOBA_PER_TILE = TOKEN_TILE // MOBA_BLOCK
_SWA_PER_TILE = TOKEN_TILE // SWA_BLOCK


def _in_proj_body(x_ref, g_ref, w_rows_ref, w_t_ref,
                  qa_ref, va_ref, qs_ref, vs_ref, ka_ref, ks_ref, kmean_ref):
    swa_per_moba = MOBA_BLOCK // SWA_BLOCK
    for t in range(_MOBA_PER_TILE):
        xn = _rms_rows(x_ref[0, t * MOBA_BLOCK:(t + 1) * MOBA_BLOCK, :], g_ref[...]).astype(_BF16)
        k_rows = _dot(xn, w_rows_ref[...])
        proj_t = _dot_nt(w_t_ref[...], xn)
        k_blk = k_rows[:, :W_MOBA]
        ka_ref[0, t] = k_blk.astype(_BF16)
        kmean_ref[0, t] = jnp.mean(k_blk, axis=0, keepdims=True)
        qa_ref[0, t] = (proj_t[_T_QA:_T_QA + W_MOBA, :] * Q_FOLD).astype(_BF16)
        va_ref[0, t] = proj_t[_T_VA:_T_VA + W_MOBA, :].astype(_BF16)
        for u in range(swa_per_moba):
            rows = slice(u * SWA_BLOCK, (u + 1) * SWA_BLOCK)
            ks_ref[0, t * swa_per_moba + u] = k_rows[rows, W_MOBA:].astype(_BF16)
            qs_ref[0, t * swa_per_moba + u] = (proj_t[_T_QS:_T_QS + W_SWA, rows] * Q_FOLD).astype(_BF16)
            vs_ref[0, t * swa_per_moba + u] = proj_t[_T_VS:_T_VS + W_SWA_KV, rows].astype(_BF16)


def _in_proj(x, g, w_rows, w_t):
    batch, seq, _ = x.shape
    nb_moba, nb_swa = seq // MOBA_BLOCK, seq // SWA_BLOCK

    def blocked(n_per_tile, rows, cols):
        return pl.BlockSpec((1, n_per_tile, rows, cols), lambda b, i: (b, i, 0, 0))

    return pl.pallas_call(
        _in_proj_body,
        grid=(batch, seq // TOKEN_TILE),
        in_specs=[pl.BlockSpec((1, TOKEN_TILE, D_MODEL), lambda b, i: (b, i, 0)),
                  _const_spec((1, D_MODEL)),
                  _const_spec((D_MODEL, W_MOBA + W_SWA_KV)),
                  _const_spec((_T_ROWS, D_MODEL))],
        out_specs=[blocked(_MOBA_PER_TILE, W_MOBA, MOBA_BLOCK),
                   blocked(_MOBA_PER_TILE, W_MOBA, MOBA_BLOCK),
                   blocked(_SWA_PER_TILE, W_SWA, SWA_BLOCK),
                   blocked(_SWA_PER_TILE, W_SWA_KV, SWA_BLOCK),
                   blocked(_MOBA_PER_TILE, MOBA_BLOCK, W_MOBA),
                   blocked(_SWA_PER_TILE, SWA_BLOCK, W_SWA_KV),
                   blocked(_MOBA_PER_TILE, 1, W_MOBA)],
        out_shape=[jax.ShapeDtypeStruct((batch, nb_moba, W_MOBA, MOBA_BLOCK), _BF16),
                   jax.ShapeDtypeStruct((batch, nb_moba, W_MOBA, MOBA_BLOCK), _BF16),
                   jax.ShapeDtypeStruct((batch, nb_swa, W_SWA, SWA_BLOCK), _BF16),
                   jax.ShapeDtypeStruct((batch, nb_swa, W_SWA_KV, SWA_BLOCK), _BF16),
                   jax.ShapeDtypeStruct((batch, nb_moba, MOBA_BLOCK, W_MOBA), _BF16),
                   jax.ShapeDtypeStruct((batch, nb_swa, SWA_BLOCK, W_SWA_KV), _BF16),
                   jax.ShapeDtypeStruct((batch, nb_moba, 1, W_MOBA), _F32)],
        compiler_params=_params(2),
        name="in_proj",
    )(x, g, w_rows, w_t)


def _t5_bucket(dist):
    n = jnp.maximum(dist, 0)
    max_exact = NUM_BUCKETS // 2
    nf = jnp.maximum(n, 1).astype(_F32)
    large = max_exact + jnp.floor(jnp.log(nf / max_exact) / math.log(MAX_DISTANCE / max_exact)
                                  * (NUM_BUCKETS - max_exact))
    large = jnp.minimum(large, float(NUM_BUCKETS - 1))
    return jnp.where(n < max_exact, n.astype(_F32), large)


def _bias_lookup(rel_bias_ref, bucket, head):
    val = jnp.full(bucket.shape, rel_bias_ref[NUM_BUCKETS - 1, head], _F32)
    for b in range(NUM_BUCKETS - 2, -1, -1):
        val = jnp.where(bucket == float(b), rel_bias_ref[b, head], val)
    return val


def _bias_tiles_body(rel_bias_ref, own_ref, prev_ref, swa_ref):
    assert MOBA_BLOCK == 2 * SWA_BLOCK and SWA_WINDOW == SWA_BLOCK and MAX_DISTANCE <= SWA_BLOCK
    h = pl.program_id(0)
    q = SWA_BLOCK
    key = lax.broadcasted_iota(jnp.int32, (q, q), 0)
    qry = lax.broadcasted_iota(jnp.int32, (q, q), 1)
    dist = qry - key

    def quadrants(head):
        near = jnp.where(dist >= 0, _bias_lookup(rel_bias_ref, _t5_bucket(dist), head) * LOG2E, NEG)
        nxt = _bias_lookup(rel_bias_ref, _t5_bucket(dist + q), head) * LOG2E
        return near, nxt

    near, nxt = quadrants(h)
    far = jnp.full((q, q), rel_bias_ref[NUM_BUCKETS - 1, h] * LOG2E, _F32)
    masked = jnp.full((q, q), NEG, _F32)
    own_ref[0, :q, :q] = near
    own_ref[0, :q, q:] = nxt
    own_ref[0, q:, :q] = masked
    own_ref[0, q:, q:] = near
    prev_ref[0, :q, :q] = far
    prev_ref[0, :q, q:] = far
    prev_ref[0, q:, :q] = nxt
    prev_ref[0, q:, q:] = far
    near, nxt = quadrants(h + N_HEADS_MOBA)
    swa_ref[0, :q, :] = jnp.where(dist < 0, nxt, NEG)
    swa_ref[0, q:, :] = near


def _bias_tiles(rel_bias):
    def per_head(rows, cols):
        return pl.BlockSpec((1, rows, cols), lambda h: (h, 0, 0))

    return pl.pallas_call(
        _bias_tiles_body,
        grid=(N_HEADS_MOBA,),
        in_specs=[pl.BlockSpec(memory_space=pltpu.SMEM)],
        out_specs=[per_head(MOBA_BLOCK, MOBA_BLOCK), per_head(MOBA_BLOCK, MOBA_BLOCK),
                   per_head(2 * SWA_BLOCK, SWA_BLOCK)],
        out_shape=[jax.ShapeDtypeStruct((N_HEADS_MOBA, MOBA_BLOCK, MOBA_BLOCK), _F32),
                   jax.ShapeDtypeStruct((N_HEADS_MOBA, MOBA_BLOCK, MOBA_BLOCK), _F32),
                   jax.ShapeDtypeStruct((N_HEADS_SWA, 2 * SWA_BLOCK, SWA_BLOCK), _F32)],
        compiler_params=_params(1),
        name="bias_tiles",
    )(rel_bias)


def _moba_body(rel_bias_ref, q_ref, k_ref, v_ref, kmean_ref, own_ref, prev_ref, o_ref,
               far_ref, qpad_ref, s_ref):
    g = pl.program_id(1)
    nb = k_ref.shape[1]
    key_lanes = HEADS_PER_KEY_TILE * HEAD_DIM
    q_row = lax.broadcasted_iota(jnp.int32, (key_lanes, MOBA_BLOCK), 0)
    km = kmean_ref[0]
    km_hi = km.astype(_BF16)
    km_lo = (km - km_hi.astype(_F32)).astype(_BF16)
    blk = lax.broadcasted_iota(jnp.int32, (nb, MOBA_BLOCK), 0)
    ones = jnp.ones((SUM_ROWS, MOBA_BLOCK), _BF16)
    ring_slots = range(GROUPS_PER_ITER)
    tail_slots = ring_slots[-PIPE_DEPTH:]
    near_slots = range(GROUPS_PER_ITER, GROUPS_PER_ITER + PIPE_DEPTH)
    assert PIPE_DEPTH == 2 and GROUPS_PER_ITER >= 2 * PIPE_DEPTH

    def key_tile(r):
        t, sub = divmod(r, HEADS_PER_KEY_TILE)
        return slice(t * key_lanes, (t + 1) * key_lanes), sub

    def prepare(i):
        past = blk < i
        own, prev = [], []
        for r in range(HEADS_PER_STEP):
            tile_lanes, sub = key_tile(r)
            q_t = q_ref[0, i, tile_lanes, :]
            q_pad = jnp.where((q_row >= sub * HEAD_DIM) & (q_row < (sub + 1) * HEAD_DIM), q_t, 0)
            qpad_ref[r] = q_pad

            gate = jnp.where(
                past, _dot(km_hi[:, tile_lanes], q_pad) + _dot(km_lo[:, tile_lanes], q_pad), NEG)
            sel = jnp.zeros(gate.shape, jnp.bool_)
            for _ in range(MOBA_TOPK):
                top = jnp.max(gate, axis=0, keepdims=True)
                first = jnp.min(jnp.where(gate == top, blk, nb), axis=0, keepdims=True)
                pick = blk == first
                sel = sel | pick
                gate = jnp.where(pick, -jnp.inf, gate)
            sel = sel & past

            far_bias = rel_bias_ref[NUM_BUCKETS - 1, g * HEADS_PER_STEP + r] * LOG2E
            far_ref[r] = jnp.where(sel & (blk < i - 1), far_bias, NEG)
            prev_neg = jnp.max(jnp.where(sel & (blk == i - 1), 0.0, NEG), axis=0, keepdims=True)
            own.append(own_ref[r])
            prev.append(prev_ref[r] + prev_neg)
        return own, prev

    def score(slot, j, tile_bias=None, row_bias=None):
        tops, shifts = [], []
        for r in range(HEADS_PER_STEP):
            s = _dot(k_ref[0, j, :, key_tile(r)[0]], qpad_ref[r])
            if tile_bias is not None:
                s = s + tile_bias[r]
            s_ref[slot, r] = s
            top = jnp.max(s, axis=0, keepdims=True)
            shift = jnp.zeros_like(top) if row_bias is None else row_bias[r]
            tops.append(top + shift)
            shifts.append(shift)
        return slot, j, tuple(tops), tuple(shifts)

    def absorb(carry, scored):
        slot, j, tops, shifts = scored
        out = []
        for r in range(HEADS_PER_STEP):
            lanes = slice(r * HEAD_DIM, (r + 1) * HEAD_DIM)
            m, acc = carry[r]
            m_new = jnp.maximum(m, tops[r])
            p = jnp.exp2(s_ref[slot, r] + (shifts[r] - m_new)).astype(_BF16)
            pv = _dot(jnp.concatenate([v_ref[0, j, lanes, :], ones], axis=0), p)
            out.append((m_new, jnp.exp2(m - m_new) * acc + pv))
        return tuple(out)

    def score_far(slot, grp):
        j = jnp.minimum(grp, nb - 1)
        return score(slot, j, row_bias=[far_ref[r, pl.ds(j, 1), :] for r in range(HEADS_PER_STEP)])

    def score_near(i):
        own, prev = prepare(i)
        return (score(near_slots[0], i, tile_bias=own)[2],
                score(near_slots[1], jnp.maximum(i - 1, 0), tile_bias=prev)[2])

    def start(i, near_tops):
        carry = tuple((jnp.full((1, MOBA_BLOCK), -jnp.inf, _F32),
                       jnp.zeros((HEAD_DIM + SUM_ROWS, MOBA_BLOCK), _F32)) for _ in range(HEADS_PER_STEP))
        near_js = (i, jnp.maximum(i - 1, 0))
        no_shift = tuple(jnp.zeros((1, MOBA_BLOCK), _F32) for _ in range(HEADS_PER_STEP))
        ring = {slot: (slot, j, tops, no_shift) for slot, j, tops in zip(near_slots, near_js, near_tops)}
        for c, slot in enumerate(tail_slots):
            ring[slot] = score_far(slot, c)
            carry = absorb(carry, ring[near_slots[c]])
        return carry, tuple(ring[slot][1:] for slot in tail_slots)

    def tile(i, state, last=False):
        def step(it, state):
            carry, pending = state
            ring = {slot: (slot,) + p for slot, p in zip(tail_slots, pending)}
            for k in ring_slots:
                scored = score_far(k, PIPE_DEPTH + it * GROUPS_PER_ITER + k)
                carry = absorb(carry, ring[(k - PIPE_DEPTH) % GROUPS_PER_ITER])
                ring[k] = scored
            return carry, tuple(ring[slot][1:] for slot in tail_slots)

        n_far = jnp.maximum(i - 1, 0)
        n_iters = (jnp.maximum(n_far - PIPE_DEPTH, 0) + GROUPS_PER_ITER - 1) // GROUPS_PER_ITER
        carry, pending = lax.fori_loop(0, n_iters, step, state)

        near_next = None if last else score_near(i + 1)
        for slot, p in zip(tail_slots, pending):
            carry = absorb(carry, (slot,) + p)
        for r in range(HEADS_PER_STEP):
            _, acc = carry[r]
            o_ref[0, i, r * HEAD_DIM:(r + 1) * HEAD_DIM, :] = acc[:HEAD_DIM] / acc[HEAD_DIM:HEAD_DIM + 1]
        return None if last else start(i + 1, near_next)

    state = lax.fori_loop(0, nb - 1, tile, start(0, score_near(0)))
    tile(nb - 1, state, last=True)


def _moba(rel_bias, qa_t, ka, va_t, kmean, bias_own, bias_prev):
    batch, nb = qa_t.shape[0], qa_t.shape[1]
    width = HEADS_PER_STEP * HEAD_DIM
    q_spec = pl.BlockSpec((1, nb, width, MOBA_BLOCK), lambda b, g: (b, 0, g, 0))
    bias_spec = pl.BlockSpec((HEADS_PER_STEP, MOBA_BLOCK, MOBA_BLOCK), lambda b, g: (g, 0, 0))
    return pl.pallas_call(
        _moba_body,
        grid=(batch, N_HEADS_MOBA // HEADS_PER_STEP),
        in_specs=[pl.BlockSpec(memory_space=pltpu.SMEM),
                  q_spec,
                  pl.BlockSpec((1, nb, MOBA_BLOCK, width), lambda b, g: (b, 0, 0, g)),
                  q_spec,
                  pl.BlockSpec((1, nb, width), lambda b, g: (b, 0, g)),
                  bias_spec, bias_spec],
        out_specs=q_spec,
        out_shape=jax.ShapeDtypeStruct((batch, nb, W_MOBA, MOBA_BLOCK), _F32),
        scratch_shapes=[pltpu.VMEM((HEADS_PER_STEP, nb, MOBA_BLOCK), _F32),
                        pltpu.VMEM((HEADS_PER_STEP, HEADS_PER_KEY_TILE * HEAD_DIM, MOBA_BLOCK), _BF16),
                        pltpu.VMEM((GROUPS_PER_ITER + PIPE_DEPTH, HEADS_PER_STEP, MOBA_BLOCK, MOBA_BLOCK), _F32)],
        compiler_params=_params(2),
        name="moba",
    )(rel_bias, qa_t, ka, va_t, kmean, bias_own, bias_prev)


def _swa_body(sinks_ref, q_ref, k_prev_ref, k_cur_ref, v_prev_ref, v_cur_ref, bias_ref, o_ref, s_ref):
    no_prev = jnp.where(pl.program_id(1) == 0, NEG, 0.0)
    window_row = lax.broadcasted_iota(jnp.int32, (2 * SWA_BLOCK, SWA_BLOCK), 0)
    first_window_mask = jnp.where(window_row < SWA_BLOCK, no_prev, 0.0)
    zeros = jnp.zeros((HEAD_DIM, SWA_BLOCK), _BF16)
    ones = jnp.ones((SUM_ROWS, 2 * SWA_BLOCK), _BF16)
    k_blocks = [k_prev_ref[0, 0]] + [k_cur_ref[0, t] for t in range(SWA_Q_PER_STEP)]
    v_blocks = [v_prev_ref[0, 0]] + [v_cur_ref[0, t] for t in range(SWA_Q_PER_STEP)]
    chains = [(t, kv) for t in range(SWA_Q_PER_STEP) for kv in range(N_KV_SWA)]

    for t, kv in chains:
        q_pad = jnp.concatenate(
            [jnp.concatenate(
                [q_ref[0, t, h * HEAD_DIM:(h + 1) * HEAD_DIM, :] if part == kv else zeros
                 for part in range(N_KV_SWA)], axis=0)
             for h in range(kv * SWA_GROUP, (kv + 1) * SWA_GROUP)], axis=1)
        window_keys = jnp.concatenate([k_blocks[t], k_blocks[t + 1]], axis=0)
        s = _dot(window_keys, q_pad)
        for slot in range(SWA_GROUP):
            cols = slice(slot * SWA_BLOCK, (slot + 1) * SWA_BLOCK)
            biased = s[:, cols] + bias_ref[kv * SWA_GROUP + slot]
            s_ref[t, kv, :, cols] = biased + first_window_mask if t == 0 else biased

    for t, kv in chains:
        kv_rows = slice(kv * HEAD_DIM, (kv + 1) * HEAD_DIM)
        window_values = jnp.concatenate(
            [jnp.concatenate([v_blocks[t][kv_rows, :], v_blocks[t + 1][kv_rows, :]], axis=1), ones], axis=0)
        p, sink_p = [], []
        for slot in range(SWA_GROUP):
            cols = slice(slot * SWA_BLOCK, (slot + 1) * SWA_BLOCK)
            sink = sinks_ref[kv * SWA_GROUP + slot] * LOG2E
            m = jnp.maximum(jnp.max(s_ref[t, kv, :, cols], axis=0, keepdims=True), sink)
            p.append(jnp.exp2(s_ref[t, kv, :, cols] - m).astype(_BF16))
            sink_p.append(jnp.exp2(sink - m))
        acc = _dot(window_values, jnp.concatenate(p, axis=1))
        for slot in range(SWA_GROUP):
            h = kv * SWA_GROUP + slot
            cols = slice(slot * SWA_BLOCK, (slot + 1) * SWA_BLOCK)
            l = acc[HEAD_DIM:HEAD_DIM + 1, cols] + sink_p[slot]
            o_ref[0, t, h * HEAD_DIM:(h + 1) * HEAD_DIM, :] = acc[:HEAD_DIM, cols] / l


def _swa(sinks, qs_t, ks, vs_t, bias_swa):
    batch, nb = qs_t.shape[0], qs_t.shape[1]

    def cur(rows, cols):
        return pl.BlockSpec((1, SWA_Q_PER_STEP, rows, cols), lambda b, n: (b, n, 0, 0))

    def prev(rows, cols):
        return pl.BlockSpec((1, 1, rows, cols),
                            lambda b, n: (b, jnp.maximum(n * SWA_Q_PER_STEP - 1, 0), 0, 0))

    return pl.pallas_call(
        _swa_body,
        grid=(batch, nb // SWA_Q_PER_STEP),
        in_specs=[pl.BlockSpec(memory_space=pltpu.SMEM),
                  cur(W_SWA, SWA_BLOCK),
                  prev(SWA_BLOCK, W_SWA_KV), cur(SWA_BLOCK, W_SWA_KV),
                  prev(W_SWA_KV, SWA_BLOCK), cur(W_SWA_KV, SWA_BLOCK),
                  _const_spec((N_HEADS_SWA, 2 * SWA_BLOCK, SWA_BLOCK))],
        out_specs=cur(W_SWA, SWA_BLOCK),
        out_shape=jax.ShapeDtypeStruct((batch, nb, W_SWA, SWA_BLOCK), _F32),
        scratch_shapes=[pltpu.VMEM((SWA_Q_PER_STEP, N_KV_SWA, 2 * SWA_BLOCK, SWA_GROUP * SWA_BLOCK), _F32)],
        compiler_params=_params(2),
        name="swa",
    )(sinks, qs_t, ks, ks, vs_t, vs_t, bias_swa)


def _group_norm_rows(o_ref, g_ref):
    rows = []
    for t in range(o_ref.shape[0]):
        o_t = o_ref[t]
        scale = lax.rsqrt(jnp.mean(o_t * o_t, axis=0, keepdims=True) + RMS_EPS)
        rows.append(((o_t * scale).T * g_ref[...]).astype(_BF16))
    return jnp.concatenate(rows, axis=0)


def _mix_ffn_body(x_ref, oa_ref, ob_ref, ga_ref, gb_ref, w_out_ref, mix_post_g_ref,
                  pre_g_ref, wg_ref, wu_ref, wd_ref, post_g_ref, o_ref, h_ref):
    y = (_dot(_group_norm_rows(oa_ref, ga_ref), w_out_ref[:W_MOBA, :])
         + _dot(_group_norm_rows(ob_ref, gb_ref), w_out_ref[W_MOBA:, :]))
    x = x_ref[...] + _rms_rows(y, mix_post_g_ref[...])
    _ffn_tile(lambda rows: x[rows, :], pre_g_ref, wg_ref, wu_ref, wd_ref, post_g_ref, o_ref, h_ref)


def _mix_ffn(x2d, oa_t, ob_t, ga, gb, w_out, mix_post_g, ffn_params):
    n_tok = x2d.shape[0]
    return pl.pallas_call(
        _mix_ffn_body,
        grid=(n_tok // TOKEN_TILE,),
        in_specs=[_TOKEN_TILE_SPEC,
                  pl.BlockSpec((_MOBA_PER_TILE, W_MOBA, MOBA_BLOCK), lambda i: (i, 0, 0)),
                  pl.BlockSpec((_SWA_PER_TILE, W_SWA, SWA_BLOCK), lambda i: (i, 0, 0)),
                  _const_spec((1, W_MOBA)), _const_spec((1, W_SWA)),
                  _const_spec((W_MOBA + W_SWA, D_MODEL)),
                  _const_spec((1, D_MODEL))] + _FFN_WEIGHT_SPECS,
        out_specs=_TOKEN_TILE_SPEC,
        out_shape=jax.ShapeDtypeStruct((n_tok, D_MODEL), _F32),
        scratch_shapes=[pltpu.VMEM((TOKEN_TILE, D_FF), _BF16)],
        compiler_params=_params(1),
        name="mix_ffn",
    )(x2d, oa_t, ob_t, ga, gb, w_out, mix_post_g, *ffn_params)


def _row(v):
    return v.reshape(1, -1)


def _layer(x, ffn1, mix, ffn2, rel_bias, bias_tiles):
    batch, seq, _ = x.shape
    (mix_pre_g, w_in, moba_out_g, swa_sinks, swa_out_g, w_out, mix_post_g) = mix
    bias_own, bias_prev, bias_swa = bias_tiles

    def ffn_params(params):
        pre_g, w_gate, w_up, w_down, post_g = params
        return (_row(pre_g), w_gate.astype(_BF16), w_up.astype(_BF16), w_down.astype(_BF16), _row(post_g))

    x = _ffn(x.reshape(batch * seq, D_MODEL), ffn_params(ffn1))

    qa, ka, va, qs, ks, vs = jnp.split(
        w_in, [W_MOBA, 2 * W_MOBA, 3 * W_MOBA, 3 * W_MOBA + W_SWA, 3 * W_MOBA + W_SWA + W_SWA_KV], axis=1)
    w_rows = jnp.concatenate([ka, ks], axis=1).astype(_BF16)
    w_t = jnp.concatenate([qa, va, qs, vs], axis=1).T.astype(_BF16)
    qa_t, va_t, qs_t, vs_t, ka_b, ks_b, kmean = _in_proj(
        x.reshape(batch, seq, D_MODEL), _row(mix_pre_g), w_rows, w_t)

    oa_t = _moba(rel_bias, qa_t, ka_b, va_t, kmean.reshape(batch, seq // MOBA_BLOCK, W_MOBA),
                 bias_own, bias_prev)
    ob_t = _swa(swa_sinks, qs_t, ks_b, vs_t, bias_swa)

    x = _mix_ffn(x, oa_t.reshape(-1, W_MOBA, MOBA_BLOCK), ob_t.reshape(-1, W_SWA, SWA_BLOCK),
                 _row(moba_out_g), _row(swa_out_g), w_out.astype(_BF16), _row(mix_post_g),
                 ffn_params(ffn2))
    return x.reshape(batch, seq, D_MODEL)


def kernel(x, ffn1_pre_g, ffn1_w_gate, ffn1_w_up, ffn1_w_down, ffn1_post_g, mix_pre_g, w_in, rel_bias,
           moba_out_g, swa_sinks, swa_out_g, w_out, mix_post_g, ffn2_pre_g, ffn2_w_gate, ffn2_w_up,
           ffn2_w_down, ffn2_post_g):
    bias_tiles = _bias_tiles(rel_bias)
    for l in range(ffn1_pre_g.shape[0]):
        x = _layer(
            x,
            (ffn1_pre_g[l], ffn1_w_gate[l], ffn1_w_up[l], ffn1_w_down[l], ffn1_post_g[l]),
            (mix_pre_g[l], w_in[l], moba_out_g[l], swa_sinks[l], swa_out_g[l], w_out[l], mix_post_g[l]),
            (ffn2_pre_g[l], ffn2_w_gate[l], ffn2_w_up[l], ffn2_w_down[l], ffn2_post_g[l]),
            rel_bias, bias_tiles)
    return x
```

```python
import functools
import math

import jax
import jax.numpy as jnp
from jax import lax
from jax.experimental import pallas as pl
from jax.experimental.pallas import tpu as pltpu

D_MODEL = 1024
HEAD_DIM = 64
N_HEADS_MOBA = 8
N_HEADS_SWA = 8
N_KV_SWA = 2
SWA_GROUP = N_HEADS_SWA // N_KV_SWA
W_MOBA = N_HEADS_MOBA * HEAD_DIM
W_SWA = N_HEADS_SWA * HEAD_DIM
W_SWA_KV = N_KV_SWA * HEAD_DIM
MOBA_BLOCK = 256
MOBA_TOPK = 3
SWA_BLOCK = 128
SWA_WINDOW = 128
NUM_BUCKETS = 32
MAX_DISTANCE = 128
D_FF = 2816
RMS_EPS = 1e-6
FFN_RES_WEIGHT = 0.5
NEG = -1e30
QK_SCALE = HEAD_DIM ** -0.5
LOG2E = math.log2(math.e)
Q_FOLD = QK_SCALE * LOG2E
SUM_ROWS = 16

TOKEN_TILE = 1024
FF_CHUNK = 256
FFN_ROW_PARTS = 2
CAST_CHUNKS = 16
HEADS_PER_STEP = 4
HEADS_PER_KEY_TILE = 2
PIPE_DEPTH = 2
GROUPS_PER_ITER = 4
SWA_Q_PER_STEP = 4
VMEM_LIMIT_BYTES = 56 * 1024 * 1024

_BF16 = jnp.bfloat16
_F32 = jnp.float32


def _dot(a, b):
    return jnp.dot(a, b, preferred_element_type=_F32)


def _dot_nt(a, b):
    return lax.dot_general(a, b, (((1,), (1,)), ((), ())), preferred_element_type=_F32)


def _rms_rows(x, g):
    return x * lax.rsqrt(jnp.mean(x * x, axis=-1, keepdims=True) + RMS_EPS) * g


def _const_spec(shape):
    return pl.BlockSpec(shape, lambda *_: (0,) * len(shape), pipeline_mode=pl.Buffered(1))


def _params(n_axes):
    return pltpu.CompilerParams(
        dimension_semantics=("arbitrary",) * n_axes, vmem_limit_bytes=VMEM_LIMIT_BYTES)


def _ffn_tile(x_rows, pre_g_ref, wg_ref, wu_ref, wd_ref, post_g_ref, o_ref, h_ref):
    part = TOKEN_TILE // FFN_ROW_PARTS
    for p in range(FFN_ROW_PARTS):
        rows = slice(p * part, (p + 1) * part)
        x = x_rows(rows)
        xn = _rms_rows(x, pre_g_ref[...]).astype(_BF16)
        for c in range(D_FF // FF_CHUNK):
            cols = slice(c * FF_CHUNK, (c + 1) * FF_CHUNK)
            gate = _dot(xn, wg_ref[:, cols])
            up = _dot(xn, wu_ref[:, cols])
            h_ref[rows, cols] = (jax.nn.silu(gate) * up).astype(_BF16)
        y = _dot(h_ref[rows, :], wd_ref[...])
        o_ref[rows, :] = x + FFN_RES_WEIGHT * _rms_rows(y, post_g_ref[...])


def _cast_weight(w_hbm, w_bf, stage, sem):
    rows = stage.shape[1]
    n_chunks = w_hbm.shape[0] // rows

    def copy(k):
        return pltpu.make_async_copy(w_hbm.at[pl.ds(k * rows, rows), :], stage.at[k % 2], sem.at[k % 2])

    copy(0).start()
    for k in range(n_chunks):
        if k + 1 < n_chunks:
            copy(k + 1).start()
        copy(k).wait()
        w_bf[k * rows:(k + 1) * rows, :] = stage[k % 2].astype(_BF16)


def _ffn_weights(w_hbm, scratch):
    wg_bf, wu_bf, wd_bf, stage_in, stage_down, sem = scratch

    @pl.when(pl.program_id(0) == 0)
    def _():
        _cast_weight(w_hbm[0], wg_bf, stage_in, sem)
        _cast_weight(w_hbm[1], wu_bf, stage_in, sem)
        _cast_weight(w_hbm[2], wd_bf, stage_down, sem)

    return wg_bf, wu_bf, wd_bf


def _ffn_body(x_ref, pre_g_ref, wg_hbm, wu_hbm, wd_hbm, post_g_ref, o_ref, h_ref, *weight_scratch):
    wg_ref, wu_ref, wd_ref = _ffn_weights((wg_hbm, wu_hbm, wd_hbm), weight_scratch)
    _ffn_tile(lambda rows: x_ref[rows, :], pre_g_ref, wg_ref, wu_ref, wd_ref, post_g_ref, o_ref, h_ref)


_TOKEN_TILE_SPEC = pl.BlockSpec((TOKEN_TILE, D_MODEL), lambda i: (i, 0))
_HBM_SPEC = pl.BlockSpec(memory_space=pl.ANY)
_FFN_WEIGHT_SPECS = [_const_spec((1, D_MODEL)), _HBM_SPEC, _HBM_SPEC, _HBM_SPEC, _const_spec((1, D_MODEL))]
_FFN_SCRATCH = [
    pltpu.VMEM((TOKEN_TILE, D_FF), _BF16),
    pltpu.VMEM((D_MODEL, D_FF), _BF16),
    pltpu.VMEM((D_MODEL, D_FF), _BF16),
    pltpu.VMEM((D_FF, D_MODEL), _BF16),
    pltpu.VMEM((2, D_MODEL // CAST_CHUNKS, D_FF), _F32),
    pltpu.VMEM((2, D_FF // CAST_CHUNKS, D_MODEL), _F32),
    pltpu.SemaphoreType.DMA((2,)),
]


def _ffn(x2d, ffn_params):
    n_tok = x2d.shape[0]
    return pl.pallas_call(
        _ffn_body,
        grid=(n_tok // TOKEN_TILE,),
        in_specs=[_TOKEN_TILE_SPEC] + _FFN_WEIGHT_SPECS,
        out_specs=_TOKEN_TILE_SPEC,
        out_shape=jax.ShapeDtypeStruct((n_tok, D_MODEL), _F32),
        scratch_shapes=_FFN_SCRATCH,
        compiler_params=_params(1),
        name="ffn",
    )(x2d, *ffn_params)


_T_QA, _T_VA, _T_QS, _T_VS = 0, W_MOBA, 2 * W_MOBA, 2 * W_MOBA + W_SWA
_T_ROWS = 2 * W_MOBA + W_SWA + W_SWA_KV
_MOBA_PER_TILE = TOKEN_TILE // MOBA_BLOCK
_SWA_PER_TILE = TOKEN_TILE // SWA_BLOCK


def _in_proj_body(x_ref, g_ref, w_rows_ref, w_t_ref,
                  qa_ref, va_ref, qs_ref, vs_ref, ka_ref, ks_ref, kmean_ref):
    swa_per_moba = MOBA_BLOCK // SWA_BLOCK
    for t in range(_MOBA_PER_TILE):
        xn = _rms_rows(x_ref[0, t * MOBA_BLOCK:(t + 1) * MOBA_BLOCK, :], g_ref[...]).astype(_BF16)
        k_rows = _dot(xn, w_rows_ref[...])
        proj_t = _dot_nt(w_t_ref[...], xn)
        k_blk = k_rows[:, :W_MOBA]
        ka_ref[0, t] = k_blk.astype(_BF16)
        kmean_ref[0, t] = jnp.mean(k_blk, axis=0, keepdims=True)
        qa_ref[0, t] = (proj_t[_T_QA:_T_QA + W_MOBA, :] * Q_FOLD).astype(_BF16)
        va_ref[0, t] = proj_t[_T_VA:_T_VA + W_MOBA, :].astype(_BF16)
        for u in range(swa_per_moba):
            rows = slice(u * SWA_BLOCK, (u + 1) * SWA_BLOCK)
            ks_ref[0, t * swa_per_moba + u] = k_rows[rows, W_MOBA:].astype(_BF16)
            qs_ref[0, t * swa_per_moba + u] = (proj_t[_T_QS:_T_QS + W_SWA, rows] * Q_FOLD).astype(_BF16)
            vs_ref[0, t * swa_per_moba + u] = proj_t[_T_VS:_T_VS + W_SWA_KV, rows].astype(_BF16)


def _in_proj(x, g, w_rows, w_t):
    batch, seq, _ = x.shape
    nb_moba, nb_swa = seq // MOBA_BLOCK, seq // SWA_BLOCK

    def blocked(n_per_tile, rows, cols):
        return pl.BlockSpec((1, n_per_tile, rows, cols), lambda b, i: (b, i, 0, 0))

    return pl.pallas_call(
        _in_proj_body,
        grid=(batch, seq // TOKEN_TILE),
        in_specs=[pl.BlockSpec((1, TOKEN_TILE, D_MODEL), lambda b, i: (b, i, 0)),
                  _const_spec((1, D_MODEL)),
                  _const_spec((D_MODEL, W_MOBA + W_SWA_KV)),
                  _const_spec((_T_ROWS, D_MODEL))],
        out_specs=[blocked(_MOBA_PER_TILE, W_MOBA, MOBA_BLOCK),
                   blocked(_MOBA_PER_TILE, W_MOBA, MOBA_BLOCK),
                   blocked(_SWA_PER_TILE, W_SWA, SWA_BLOCK),
                   blocked(_SWA_PER_TILE, W_SWA_KV, SWA_BLOCK),
                   blocked(_MOBA_PER_TILE, MOBA_BLOCK, W_MOBA),
                   blocked(_SWA_PER_TILE, SWA_BLOCK, W_SWA_KV),
                   blocked(_MOBA_PER_TILE, 1, W_MOBA)],
        out_shape=[jax.ShapeDtypeStruct((batch, nb_moba, W_MOBA, MOBA_BLOCK), _BF16),
                   jax.ShapeDtypeStruct((batch, nb_moba, W_MOBA, MOBA_BLOCK), _BF16),
                   jax.ShapeDtypeStruct((batch, nb_swa, W_SWA, SWA_BLOCK), _BF16),
                   jax.ShapeDtypeStruct((batch, nb_swa, W_SWA_KV, SWA_BLOCK), _BF16),
                   jax.ShapeDtypeStruct((batch, nb_moba, MOBA_BLOCK, W_MOBA), _BF16),
                   jax.ShapeDtypeStruct((batch, nb_swa, SWA_BLOCK, W_SWA_KV), _BF16),
                   jax.ShapeDtypeStruct((batch, nb_moba, 1, W_MOBA), _F32)],
        compiler_params=_params(2),
        name="in_proj",
    )(x, g, w_rows, w_t)


def _t5_bucket(dist):
    n = jnp.maximum(dist, 0)
    max_exact = NUM_BUCKETS // 2
    nf = jnp.maximum(n, 1).astype(_F32)
    large = max_exact + jnp.floor(jnp.log(nf / max_exact) / math.log(MAX_DISTANCE / max_exact)
                                  * (NUM_BUCKETS - max_exact))
    large = jnp.minimum(large, float(NUM_BUCKETS - 1))
    return jnp.where(n < max_exact, n.astype(_F32), large)


def _bias_lookup(rel_bias_ref, bucket, head):
    val = jnp.full(bucket.shape, rel_bias_ref[NUM_BUCKETS - 1, head], _F32)
    for b in range(NUM_BUCKETS - 2, -1, -1):
        val = jnp.where(bucket == float(b), rel_bias_ref[b, head], val)
    return val


def _bias_tiles_body(rel_bias_ref, own_ref, prev_ref, swa_ref):
    assert MOBA_BLOCK == 2 * SWA_BLOCK and SWA_WINDOW == SWA_BLOCK and MAX_DISTANCE <= SWA_BLOCK
    h = pl.program_id(0)
    q = SWA_BLOCK
    key = lax.broadcasted_iota(jnp.int32, (q, q), 0)
    qry = lax.broadcasted_iota(jnp.int32, (q, q), 1)
    dist = qry - key

    def quadrants(head):
        near = jnp.where(dist >= 0, _bias_lookup(rel_bias_ref, _t5_bucket(dist), head) * LOG2E, NEG)
        nxt = _bias_lookup(rel_bias_ref, _t5_bucket(dist + q), head) * LOG2E
        return near, nxt

    near, nxt = quadrants(h)
    far = jnp.full((q, q), rel_bias_ref[NUM_BUCKETS - 1, h] * LOG2E, _F32)
    masked = jnp.full((q, q), NEG, _F32)
    own_ref[0, :q, :q] = near
    own_ref[0, :q, q:] = nxt
    own_ref[0, q:, :q] = masked
    own_ref[0, q:, q:] = near
    prev_ref[0, :q, :q] = far
    prev_ref[0, :q, q:] = far
    prev_ref[0, q:, :q] = nxt
    prev_ref[0, q:, q:] = far
    near, nxt = quadrants(h + N_HEADS_MOBA)
    swa_ref[0, :q, :] = jnp.where(dist < 0, nxt, NEG)
    swa_ref[0, q:, :] = near


def _bias_tiles(rel_bias):
    def per_head(rows, cols):
        return pl.BlockSpec((1, rows, cols), lambda h: (h, 0, 0))

    return pl.pallas_call(
        _bias_tiles_body,
        grid=(N_HEADS_MOBA,),
        in_specs=[pl.BlockSpec(memory_space=pltpu.SMEM)],
        out_specs=[per_head(MOBA_BLOCK, MOBA_BLOCK), per_head(MOBA_BLOCK, MOBA_BLOCK),
                   per_head(2 * SWA_BLOCK, SWA_BLOCK)],
        out_shape=[jax.ShapeDtypeStruct((N_HEADS_MOBA, MOBA_BLOCK, MOBA_BLOCK), _F32),
                   jax.ShapeDtypeStruct((N_HEADS_MOBA, MOBA_BLOCK, MOBA_BLOCK), _F32),
                   jax.ShapeDtypeStruct((N_HEADS_SWA, 2 * SWA_BLOCK, SWA_BLOCK), _F32)],
        compiler_params=_params(1),
        name="bias_tiles",
    )(rel_bias)


def _moba_body(rel_bias_ref, q_ref, k_ref, v_ref, kmean_ref, own_ref, prev_ref, o_ref,
               far_ref, qpad_ref, s_ref):
    g = pl.program_id(1)
    nb = k_ref.shape[1]
    key_lanes = HEADS_PER_KEY_TILE * HEAD_DIM
    q_row = lax.broadcasted_iota(jnp.int32, (key_lanes, MOBA_BLOCK), 0)
    km = kmean_ref[0]
    km_hi = km.astype(_BF16)
    km_lo = (km - km_hi.astype(_F32)).astype(_BF16)
    blk = lax.broadcasted_iota(jnp.int32, (nb, MOBA_BLOCK), 0)
    ones = jnp.ones((SUM_ROWS, MOBA_BLOCK), _BF16)
    ring_slots = range(GROUPS_PER_ITER)
    tail_slots = ring_slots[-PIPE_DEPTH:]
    near_slots = range(GROUPS_PER_ITER, GROUPS_PER_ITER + PIPE_DEPTH)
    assert PIPE_DEPTH == 2 and GROUPS_PER_ITER >= 2 * PIPE_DEPTH

    def key_tile(r):
        t, sub = divmod(r, HEADS_PER_KEY_TILE)
        return slice(t * key_lanes, (t + 1) * key_lanes), sub

    def prepare(i):
        past = blk < i
        own, prev = [], []
        for r in range(HEADS_PER_STEP):
            tile_lanes, sub = key_tile(r)
            q_t = q_ref[0, i, tile_lanes, :]
            q_pad = jnp.where((q_row >= sub * HEAD_DIM) & (q_row < (sub + 1) * HEAD_DIM), q_t, 0)
            qpad_ref[r] = q_pad

            gate = jnp.where(
                past, _dot(km_hi[:, tile_lanes], q_pad) + _dot(km_lo[:, tile_lanes], q_pad), NEG)
            sel = jnp.zeros(gate.shape, jnp.bool_)
            for _ in range(MOBA_TOPK):
                top = jnp.max(gate, axis=0, keepdims=True)
                first = jnp.min(jnp.where(gate == top, blk, nb), axis=0, keepdims=True)
                pick = blk == first
                sel = sel | pick
                gate = jnp.where(pick, -jnp.inf, gate)
            sel = sel & past

            far_bias = rel_bias_ref[NUM_BUCKETS - 1, g * HEADS_PER_STEP + r] * LOG2E
            far_ref[r] = jnp.where(sel & (blk < i - 1), far_bias, NEG)
            prev_neg = jnp.max(jnp.where(sel & (blk == i - 1), 0.0, NEG), axis=0, keepdims=True)
            own.append(own_ref[r])
            prev.append(prev_ref[r] + prev_neg)
        return own, prev

    def score(slot, j, tile_bias=None, row_bias=None):
        tops, shifts = [], []
        for r in range(HEADS_PER_STEP):
            s = _dot(k_ref[0, j, :, key_tile(r)[0]], qpad_ref[r])
            if tile_bias is not None:
                s = s + tile_bias[r]
            s_ref[slot, r] = s
            top = jnp.max(s, axis=0, keepdims=True)
            shift = jnp.zeros_like(top) if row_bias is None else row_bias[r]
            tops.append(top + shift)
            shifts.append(shift)
        return slot, j, tuple(tops), tuple(shifts)

    def absorb(carry, scored):
        slot, j, tops, shifts = scored
        out = []
        for r in range(HEADS_PER_STEP):
            lanes = slice(r * HEAD_DIM, (r + 1) * HEAD_DIM)
            m, acc = carry[r]
            m_new = jnp.maximum(m, tops[r])
            p = jnp.exp2(s_ref[slot, r] + (shifts[r] - m_new)).astype(_BF16)
            pv = _dot(jnp.concatenate([v_ref[0, j, lanes, :], ones], axis=0), p)
            out.append((m_new, jnp.exp2(m - m_new) * acc + pv))
        return tuple(out)

    def score_far(slot, grp):
        j = jnp.minimum(grp, nb - 1)
        return score(slot, j, row_bias=[far_ref[r, pl.ds(j, 1), :] for r in range(HEADS_PER_STEP)])

    def score_near(i):
        own, prev = prepare(i)
        return (score(near_slots[0], i, tile_bias=own)[2],
                score(near_slots[1], jnp.maximum(i - 1, 0), tile_bias=prev)[2])

    def start(i, near_tops):
        carry = tuple((jnp.full((1, MOBA_BLOCK), -jnp.inf, _F32),
                       jnp.zeros((HEAD_DIM + SUM_ROWS, MOBA_BLOCK), _F32)) for _ in range(HEADS_PER_STEP))
        near_js = (i, jnp.maximum(i - 1, 0))
        no_shift = tuple(jnp.zeros((1, MOBA_BLOCK), _F32) for _ in range(HEADS_PER_STEP))
        ring = {slot: (slot, j, tops, no_shift) for slot, j, tops in zip(near_slots, near_js, near_tops)}
        for c, slot in enumerate(tail_slots):
            ring[slot] = score_far(slot, c)
            carry = absorb(carry, ring[near_slots[c]])
        return carry, tuple(ring[slot][1:] for slot in tail_slots)

    def tile(i, state, last=False):
        def step(it, state):
            carry, pending = state
            ring = {slot: (slot,) + p for slot, p in zip(tail_slots, pending)}
            for k in ring_slots:
                scored = score_far(k, PIPE_DEPTH + it * GROUPS_PER_ITER + k)
                carry = absorb(carry, ring[(k - PIPE_DEPTH) % GROUPS_PER_ITER])
                ring[k] = scored
            return carry, tuple(ring[slot][1:] for slot in tail_slots)

        n_far = jnp.maximum(i - 1, 0)
        n_iters = (jnp.maximum(n_far - PIPE_DEPTH, 0) + GROUPS_PER_ITER - 1) // GROUPS_PER_ITER
        carry, pending = lax.fori_loop(0, n_iters, step, state)

        near_next = None if last else score_near(i + 1)
        for slot, p in zip(tail_slots, pending):
            carry = absorb(carry, (slot,) + p)
        for r in range(HEADS_PER_STEP):
            _, acc = carry[r]
            o_ref[0, i, r * HEAD_DIM:(r + 1) * HEAD_DIM, :] = acc[:HEAD_DIM] / acc[HEAD_DIM:HEAD_DIM + 1]
        return None if last else start(i + 1, near_next)

    state = lax.fori_loop(0, nb - 1, tile, start(0, score_near(0)))
    tile(nb - 1, state, last=True)


def _moba(rel_bias, qa_t, ka, va_t, kmean, bias_own, bias_prev):
    batch, nb = qa_t.shape[0], qa_t.shape[1]
    width = HEADS_PER_STEP * HEAD_DIM
    q_spec = pl.BlockSpec((1, nb, width, MOBA_BLOCK), lambda b, g: (b, 0, g, 0))
    bias_spec = pl.BlockSpec((HEADS_PER_STEP, MOBA_BLOCK, MOBA_BLOCK), lambda b, g: (g, 0, 0))
    return pl.pallas_call(
        _moba_body,
        grid=(batch, N_HEADS_MOBA // HEADS_PER_STEP),
        in_specs=[pl.BlockSpec(memory_space=pltpu.SMEM),
                  q_spec,
                  pl.BlockSpec((1, nb, MOBA_BLOCK, width), lambda b, g: (b, 0, 0, g)),
                  q_spec,
                  pl.BlockSpec((1, nb, width), lambda b, g: (b, 0, g)),
                  bias_spec, bias_spec],
        out_specs=q_spec,
        out_shape=jax.ShapeDtypeStruct((batch, nb, W_MOBA, MOBA_BLOCK), _F32),
        scratch_shapes=[pltpu.VMEM((HEADS_PER_STEP, nb, MOBA_BLOCK), _F32),
                        pltpu.VMEM((HEADS_PER_STEP, HEADS_PER_KEY_TILE * HEAD_DIM, MOBA_BLOCK), _BF16),
                        pltpu.VMEM((GROUPS_PER_ITER + PIPE_DEPTH, HEADS_PER_STEP, MOBA_BLOCK, MOBA_BLOCK), _F32)],
        compiler_params=_params(2),
        name="moba",
    )(rel_bias, qa_t, ka, va_t, kmean, bias_own, bias_prev)


def _swa_body(sinks_ref, q_ref, k_prev_ref, k_cur_ref, v_prev_ref, v_cur_ref, bias_ref, o_ref, s_ref):
    no_prev = jnp.where(pl.program_id(1) == 0, NEG, 0.0)
    window_row = lax.broadcasted_iota(jnp.int32, (2 * SWA_BLOCK, SWA_BLOCK), 0)
    first_window_mask = jnp.where(window_row < SWA_BLOCK, no_prev, 0.0)
    zeros = jnp.zeros((HEAD_DIM, SWA_BLOCK), _BF16)
    ones = jnp.ones((SUM_ROWS, 2 * SWA_BLOCK), _BF16)
    k_blocks = [k_prev_ref[0, 0]] + [k_cur_ref[0, t] for t in range(SWA_Q_PER_STEP)]
    v_blocks = [v_prev_ref[0, 0]] + [v_cur_ref[0, t] for t in range(SWA_Q_PER_STEP)]
    chains = [(t, kv) for t in range(SWA_Q_PER_STEP) for kv in range(N_KV_SWA)]

    for t, kv in chains:
        q_pad = jnp.concatenate(
            [jnp.concatenate(
                [q_ref[0, t, h * HEAD_DIM:(h + 1) * HEAD_DIM, :] if part == kv else zeros
                 for part in range(N_KV_SWA)], axis=0)
             for h in range(kv * SWA_GROUP, (kv + 1) * SWA_GROUP)], axis=1)
        window_keys = jnp.concatenate([k_blocks[t], k_blocks[t + 1]], axis=0)
        s = _dot(window_keys, q_pad)
        for slot in range(SWA_GROUP):
            cols = slice(slot * SWA_BLOCK, (slot + 1) * SWA_BLOCK)
            biased = s[:, cols] + bias_ref[kv * SWA_GROUP + slot]
            s_ref[t, kv, :, cols] = biased + first_window_mask if t == 0 else biased

    for t, kv in chains:
        kv_rows = slice(kv * HEAD_DIM, (kv + 1) * HEAD_DIM)
        window_values = jnp.concatenate(
            [jnp.concatenate([v_blocks[t][kv_rows, :], v_blocks[t + 1][kv_rows, :]], axis=1), ones], axis=0)
        p, sink_p = [], []
        for slot in range(SWA_GROUP):
            cols = slice(slot * SWA_BLOCK, (slot + 1) * SWA_BLOCK)
            sink = sinks_ref[kv * SWA_GROUP + slot] * LOG2E
            m = jnp.maximum(jnp.max(s_ref[t, kv, :, cols], axis=0, keepdims=True), sink)
            p.append(jnp.exp2(s_ref[t, kv, :, cols] - m).astype(_BF16))
            sink_p.append(jnp.exp2(sink - m))
        acc = _dot(window_values, jnp.concatenate(p, axis=1))
        for slot in range(SWA_GROUP):
            h = kv * SWA_GROUP + slot
            cols = slice(slot * SWA_BLOCK, (slot + 1) * SWA_BLOCK)
            l = acc[HEAD_DIM:HEAD_DIM + 1, cols] + sink_p[slot]
            o_ref[0, t, h * HEAD_DIM:(h + 1) * HEAD_DIM, :] = acc[:HEAD_DIM, cols] / l


def _swa(sinks, qs_t, ks, vs_t, bias_swa):
    batch, nb = qs_t.shape[0], qs_t.shape[1]

    def cur(rows, cols):
        return pl.BlockSpec((1, SWA_Q_PER_STEP, rows, cols), lambda b, n: (b, n, 0, 0))

    def prev(rows, cols):
        return pl.BlockSpec((1, 1, rows, cols),
                            lambda b, n: (b, jnp.maximum(n * SWA_Q_PER_STEP - 1, 0), 0, 0))

    return pl.pallas_call(
        _swa_body,
        grid=(batch, nb // SWA_Q_PER_STEP),
        in_specs=[pl.BlockSpec(memory_space=pltpu.SMEM),
                  cur(W_SWA, SWA_BLOCK),
                  prev(SWA_BLOCK, W_SWA_KV), cur(SWA_BLOCK, W_SWA_KV),
                  prev(W_SWA_KV, SWA_BLOCK), cur(W_SWA_KV, SWA_BLOCK),
                  _const_spec((N_HEADS_SWA, 2 * SWA_BLOCK, SWA_BLOCK))],
        out_specs=cur(W_SWA, SWA_BLOCK),
        out_shape=jax.ShapeDtypeStruct((batch, nb, W_SWA, SWA_BLOCK), _F32),
        scratch_shapes=[pltpu.VMEM((SWA_Q_PER_STEP, N_KV_SWA, 2 * SWA_BLOCK, SWA_GROUP * SWA_BLOCK), _F32)],
        compiler_params=_params(2),
        name="swa",
    )(sinks, qs_t, ks, ks, vs_t, vs_t, bias_swa)


def _group_norm_rows(o_ref, g_ref):
    rows = []
    for t in range(o_ref.shape[0]):
        o_t = o_ref[t]
        scale = lax.rsqrt(jnp.mean(o_t * o_t, axis=0, keepdims=True) + RMS_EPS)
        rows.append(((o_t * scale).T * g_ref[...]).astype(_BF16))
    return jnp.concatenate(rows, axis=0)


def _mix_ffn_body(x_ref, oa_ref, ob_ref, ga_ref, gb_ref, w_out_ref, mix_post_g_ref,
                  pre_g_ref, wg_hbm, wu_hbm, wd_hbm, post_g_ref, o_ref, h_ref, *weight_scratch):
    wg_ref, wu_ref, wd_ref = _ffn_weights((wg_hbm, wu_hbm, wd_hbm), weight_scratch)
    y = (_dot(_group_norm_rows(oa_ref, ga_ref), w_out_ref[:W_MOBA, :])
         + _dot(_group_norm_rows(ob_ref, gb_ref), w_out_ref[W_MOBA:, :]))
    x = x_ref[...] + _rms_rows(y, mix_post_g_ref[...])
    _ffn_tile(lambda rows: x[rows, :], pre_g_ref, wg_ref, wu_ref, wd_ref, post_g_ref, o_ref, h_ref)


def _mix_ffn(x2d, oa_t, ob_t, ga, gb, w_out, mix_post_g, ffn_params):
    n_tok = x2d.shape[0]
    return pl.pallas_call(
        _mix_ffn_body,
        grid=(n_tok // TOKEN_TILE,),
        in_specs=[_TOKEN_TILE_SPEC,
                  pl.BlockSpec((_MOBA_PER_TILE, W_MOBA, MOBA_BLOCK), lambda i: (i, 0, 0)),
                  pl.BlockSpec((_SWA_PER_TILE, W_SWA, SWA_BLOCK), lambda i: (i, 0, 0)),
                  _const_spec((1, W_MOBA)), _const_spec((1, W_SWA)),
                  _const_spec((W_MOBA + W_SWA, D_MODEL)),
                  _const_spec((1, D_MODEL))] + _FFN_WEIGHT_SPECS,
        out_specs=_TOKEN_TILE_SPEC,
        out_shape=jax.ShapeDtypeStruct((n_tok, D_MODEL), _F32),
        scratch_shapes=_FFN_SCRATCH,
        compiler_params=_params(1),
        name="mix_ffn",
    )(x2d, oa_t, ob_t, ga, gb, w_out, mix_post_g, *ffn_params)


def _row(v):
    return v.reshape(1, -1)


def _layer(x, ffn1, mix, ffn2, rel_bias, bias_tiles):
    batch, seq, _ = x.shape
    (mix_pre_g, w_in, moba_out_g, swa_sinks, swa_out_g, w_out, mix_post_g) = mix
    bias_own, bias_prev, bias_swa = bias_tiles

    def ffn_params(params):
        pre_g, w_gate, w_up, w_down, post_g = params
        return (_row(pre_g), w_gate, w_up, w_down, _row(post_g))

    x = _ffn(x.reshape(batch * seq, D_MODEL), ffn_params(ffn1))

    qa, ka, va, qs, ks, vs = jnp.split(
        w_in, [W_MOBA, 2 * W_MOBA, 3 * W_MOBA, 3 * W_MOBA + W_SWA, 3 * W_MOBA + W_SWA + W_SWA_KV], axis=1)
    w_rows = jnp.concatenate([ka, ks], axis=1).astype(_BF16)
    w_t = jnp.concatenate([qa, va, qs, vs], axis=1).T.astype(_BF16)
    qa_t, va_t, qs_t, vs_t, ka_b, ks_b, kmean = _in_proj(
        x.reshape(batch, seq, D_MODEL), _row(mix_pre_g), w_rows, w_t)

    oa_t = _moba(rel_bias, qa_t, ka_b, va_t, kmean.reshape(batch, seq // MOBA_BLOCK, W_MOBA),
                 bias_own, bias_prev)
    ob_t = _swa(swa_sinks, qs_t, ks_b, vs_t, bias_swa)

    x = _mix_ffn(x, oa_t.reshape(-1, W_MOBA, MOBA_BLOCK), ob_t.reshape(-1, W_SWA, SWA_BLOCK),
                 _row(moba_out_g), _row(swa_out_g), w_out.astype(_BF16), _row(mix_post_g),
                 ffn_params(ffn2))
    return x.reshape(batch, seq, D_MODEL)


def kernel(x, ffn1_pre_g, ffn1_w_gate, ffn1_w_up, ffn1_w_down, ffn1_post_g, mix_pre_g, w_in, rel_bias,
           moba_out_g, swa_sinks, swa_out_g, w_out, mix_post_g, ffn2_pre_g, ffn2_w_gate, ffn2_w_up,
           ffn2_w_down, ffn2_post_g):
    bias_tiles = _bias_tiles(rel_bias)
    for l in range(ffn1_pre_g.shape[0]):
        x = _layer(
            x,
            (ffn1_pre_g[l], ffn1_w_gate[l], ffn1_w_up[l], ffn1_w_down[l], ffn1_post_g[l]),
            (mix_pre_g[l], w_in[l], moba_out_g[l], swa_sinks[l], swa_out_g[l], w_out[l], mix_post_g[l]),
            (ffn2_pre_g[l], ffn2_w_gate[l], ffn2_w_up[l], ffn2_w_down[l], ffn2_post_g[l]),
            rel_bias, bias_tiles)
    return x
```

```python
import functools
import math

import jax
import jax.numpy as jnp
from jax import lax
from jax.experimental import pallas as pl
from jax.experimental.pallas import tpu as pltpu

D_MODEL = 1024
HEAD_DIM = 64
N_HEADS_MOBA = 8
N_HEADS_SWA = 8
N_KV_SWA = 2
SWA_GROUP = N_HEADS_SWA // N_KV_SWA
W_MOBA = N_HEADS_MOBA * HEAD_DIM
W_SWA = N_HEADS_SWA * HEAD_DIM
W_SWA_KV = N_KV_SWA * HEAD_DIM
MOBA_BLOCK = 256
MOBA_TOPK = 3
SWA_BLOCK = 128
SWA_WINDOW = 128
NUM_BUCKETS = 32
MAX_DISTANCE = 128
D_FF = 2816
RMS_EPS = 1e-6
FFN_RES_WEIGHT = 0.5
NEG = -1e30
QK_SCALE = HEAD_DIM ** -0.5
LOG2E = math.log2(math.e)
Q_FOLD = QK_SCALE * LOG2E
SUM_ROWS = 16

TOKEN_TILE = 1024
FF_CHUNK = 256
FFN_ROW_PARTS = 2
CAST_CHUNKS = 16
CAST_SLOTS = 4
HEADS_PER_STEP = 4
HEADS_PER_KEY_TILE = 2
PIPE_DEPTH = 2
GROUPS_PER_ITER = 4
SWA_Q_PER_STEP = 4
VMEM_LIMIT_BYTES = 58 * 1024 * 1024

_BF16 = jnp.bfloat16
_F32 = jnp.float32


def _dot(a, b):
    return jnp.dot(a, b, preferred_element_type=_F32)


def _dot_nt(a, b):
    return lax.dot_general(a, b, (((1,), (1,)), ((), ())), preferred_element_type=_F32)


def _rms_rows(x, g):
    return x * lax.rsqrt(jnp.mean(x * x, axis=-1, keepdims=True) + RMS_EPS) * g


def _const_spec(shape):
    return pl.BlockSpec(shape, lambda *_: (0,) * len(shape), pipeline_mode=pl.Buffered(1))


def _params(n_axes):
    return pltpu.CompilerParams(
        dimension_semantics=("arbitrary",) * n_axes, vmem_limit_bytes=VMEM_LIMIT_BYTES)


def _ffn_tile(x_rows, pre_g_ref, wg_ref, wu_ref, wd_ref, post_g_ref, o_ref, h_ref):
    part = TOKEN_TILE // FFN_ROW_PARTS
    for p in range(FFN_ROW_PARTS):
        rows = slice(p * part, (p + 1) * part)
        x = x_rows(rows)
        xn = _rms_rows(x, pre_g_ref[...]).astype(_BF16)
        for c in range(D_FF // FF_CHUNK):
            cols = slice(c * FF_CHUNK, (c + 1) * FF_CHUNK)
            gate = _dot(xn, wg_ref[:, cols])
            up = _dot(xn, wu_ref[:, cols])
            h_ref[rows, cols] = (jax.nn.silu(gate) * up).astype(_BF16)
        y = _dot(h_ref[rows, :], wd_ref[...])
        o_ref[rows, :] = x + FFN_RES_WEIGHT * _rms_rows(y, post_g_ref[...])


def _cast_weights(pairs, stage, sem):
    slots, rows = stage.shape[0], stage.shape[1]
    chunks = [(w_hbm, w_bf, k) for w_hbm, w_bf in pairs for k in range(w_hbm.shape[0] // rows)]

    def copy(c):
        w_hbm, _, k = chunks[c]
        return pltpu.make_async_copy(w_hbm.at[pl.ds(k * rows, rows), :], stage.at[c % slots], sem.at[c % slots])

    def prime():
        for c in range(min(slots, len(chunks))):
            copy(c).start()

    def drain():
        for c, (_, w_bf, k) in enumerate(chunks):
            copy(c).wait()
            w_bf[k * rows:(k + 1) * rows, :] = stage[c % slots].astype(_BF16)
            if c + slots < len(chunks):
                copy(c + slots).start()

    return prime, drain


def _ffn_weights(w_hbm, scratch):
    wg_bf, wu_bf, wd_bf, stage_in, stage_down, sem_in, sem_down = scratch

    @pl.when(pl.program_id(0) == 0)
    def _():
        prime_in, drain_in = _cast_weights([(w_hbm[0], wg_bf), (w_hbm[1], wu_bf)], stage_in, sem_in)
        prime_down, drain_down = _cast_weights([(w_hbm[2], wd_bf)], stage_down, sem_down)
        prime_in()
        prime_down()
        drain_in()
        drain_down()

    return wg_bf, wu_bf, wd_bf


def _ffn_body(x_ref, pre_g_ref, wg_hbm, wu_hbm, wd_hbm, post_g_ref, o_ref, h_ref, *weight_scratch):
    wg_ref, wu_ref, wd_ref = _ffn_weights((wg_hbm, wu_hbm, wd_hbm), weight_scratch)
    _ffn_tile(lambda rows: x_ref[rows, :], pre_g_ref, wg_ref, wu_ref, wd_ref, post_g_ref, o_ref, h_ref)


_TOKEN_TILE_SPEC = pl.BlockSpec((TOKEN_TILE, D_MODEL), lambda i: (i, 0))
_HBM_SPEC = pl.BlockSpec(memory_space=pl.ANY)
_FFN_WEIGHT_SPECS = [_const_spec((1, D_MODEL)), _HBM_SPEC, _HBM_SPEC, _HBM_SPEC, _const_spec((1, D_MODEL))]
_FFN_SCRATCH = [
    pltpu.VMEM((TOKEN_TILE, D_FF), _BF16),
    pltpu.VMEM((D_MODEL, D_FF), _BF16),
    pltpu.VMEM((D_MODEL, D_FF), _BF16),
    pltpu.VMEM((D_FF, D_MODEL), _BF16),
    pltpu.VMEM((CAST_SLOTS, D_MODEL // CAST_CHUNKS, D_FF), _F32),
    pltpu.VMEM((CAST_SLOTS, D_FF // CAST_CHUNKS, D_MODEL), _F32),
    pltpu.SemaphoreType.DMA((CAST_SLOTS,)),
    pltpu.SemaphoreType.DMA((CAST_SLOTS,)),
]


def _ffn(x2d, ffn_params):
    n_tok = x2d.shape[0]
    return pl.pallas_call(
        _ffn_body,
        grid=(n_tok // TOKEN_TILE,),
        in_specs=[_TOKEN_TILE_SPEC] + _FFN_WEIGHT_SPECS,
        out_specs=_TOKEN_TILE_SPEC,
        out_shape=jax.ShapeDtypeStruct((n_tok, D_MODEL), _F32),
        scratch_shapes=_FFN_SCRATCH,
        compiler_params=_params(1),
        name="ffn",
    )(x2d, *ffn_params)


_T_QA, _T_VA, _T_QS, _T_VS = 0, W_MOBA, 2 * W_MOBA, 2 * W_MOBA + W_SWA
_T_ROWS = 2 * W_MOBA + W_SWA + W_SWA_KV
_MOBA_PER_TILE = TOKEN_TILE // MOBA_BLOCK
_SWA_PER_TILE = TOKEN_TILE // SWA_BLOCK


def _in_proj_body(x_ref, g_ref, w_rows_ref, w_t_ref,
                  qa_ref, va_ref, qs_ref, vs_ref, ka_ref, ks_ref, kmean_ref):
    swa_per_moba = MOBA_BLOCK // SWA_BLOCK
    for t in range(_MOBA_PER_TILE):
        xn = _rms_rows(x_ref[0, t * MOBA_BLOCK:(t + 1) * MOBA_BLOCK, :], g_ref[...]).astype(_BF16)
        k_rows = _dot(xn, w_rows_ref[...])
        proj_t = _dot_nt(w_t_ref[...], xn)
        k_blk = k_rows[:, :W_MOBA]
        ka_ref[0, t] = k_blk.astype(_BF16)
        kmean_ref[0, t] = jnp.mean(k_blk, axis=0, keepdims=True)
        qa_ref[0, t] = (proj_t[_T_QA:_T_QA + W_MOBA, :] * Q_FOLD).astype(_BF16)
        va_ref[0, t] = proj_t[_T_VA:_T_VA + W_MOBA, :].astype(_BF16)
        for u in range(swa_per_moba):
            rows = slice(u * SWA_BLOCK, (u + 1) * SWA_BLOCK)
            ks_ref[0, t * swa_per_moba + u] = k_rows[rows, W_MOBA:].astype(_BF16)
            qs_ref[0, t * swa_per_moba + u] = (proj_t[_T_QS:_T_QS + W_SWA, rows] * Q_FOLD).astype(_BF16)
            vs_ref[0, t * swa_per_moba + u] = proj_t[_T_VS:_T_VS + W_SWA_KV, rows].astype(_BF16)


def _in_proj(x, g, w_rows, w_t):
    batch, seq, _ = x.shape
    nb_moba, nb_swa = seq // MOBA_BLOCK, seq // SWA_BLOCK

    def blocked(n_per_tile, rows, cols):
        return pl.BlockSpec((1, n_per_tile, rows, cols), lambda b, i: (b, i, 0, 0))

    return pl.pallas_call(
        _in_proj_body,
        grid=(batch, seq // TOKEN_TILE),
        in_specs=[pl.BlockSpec((1, TOKEN_TILE, D_MODEL), lambda b, i: (b, i, 0)),
                  _const_spec((1, D_MODEL)),
                  _const_spec((D_MODEL, W_MOBA + W_SWA_KV)),
                  _const_spec((_T_ROWS, D_MODEL))],
        out_specs=[blocked(_MOBA_PER_TILE, W_MOBA, MOBA_BLOCK),
                   blocked(_MOBA_PER_TILE, W_MOBA, MOBA_BLOCK),
                   blocked(_SWA_PER_TILE, W_SWA, SWA_BLOCK),
                   blocked(_SWA_PER_TILE, W_SWA_KV, SWA_BLOCK),
                   blocked(_MOBA_PER_TILE, MOBA_BLOCK, W_MOBA),
                   blocked(_SWA_PER_TILE, SWA_BLOCK, W_SWA_KV),
                   blocked(_MOBA_PER_TILE, 1, W_MOBA)],
        out_shape=[jax.ShapeDtypeStruct((batch, nb_moba, W_MOBA, MOBA_BLOCK), _BF16),
                   jax.ShapeDtypeStruct((batch, nb_moba, W_MOBA, MOBA_BLOCK), _BF16),
                   jax.ShapeDtypeStruct((batch, nb_swa, W_SWA, SWA_BLOCK), _BF16),
                   jax.ShapeDtypeStruct((batch, nb_swa, W_SWA_KV, SWA_BLOCK), _BF16),
                   jax.ShapeDtypeStruct((batch, nb_moba, MOBA_BLOCK, W_MOBA), _BF16),
                   jax.ShapeDtypeStruct((batch, nb_swa, SWA_BLOCK, W_SWA_KV), _BF16),
                   jax.ShapeDtypeStruct((batch, nb_moba, 1, W_MOBA), _F32)],
        compiler_params=_params(2),
        name="in_proj",
    )(x, g, w_rows, w_t)


def _t5_bucket(dist):
    n = jnp.maximum(dist, 0)
    max_exact = NUM_BUCKETS // 2
    nf = jnp.maximum(n, 1).astype(_F32)
    large = max_exact + jnp.floor(jnp.log(nf / max_exact) / math.log(MAX_DISTANCE / max_exact)
                                  * (NUM_BUCKETS - max_exact))
    large = jnp.minimum(large, float(NUM_BUCKETS - 1))
    return jnp.where(n < max_exact, n.astype(_F32), large)


def _bias_lookup(rel_bias_ref, bucket, head):
    val = jnp.full(bucket.shape, rel_bias_ref[NUM_BUCKETS - 1, head], _F32)
    for b in range(NUM_BUCKETS - 2, -1, -1):
        val = jnp.where(bucket == float(b), rel_bias_ref[b, head], val)
    return val


def _bias_tiles_body(rel_bias_ref, own_ref, prev_ref, swa_ref):
    assert MOBA_BLOCK == 2 * SWA_BLOCK and SWA_WINDOW == SWA_BLOCK and MAX_DISTANCE <= SWA_BLOCK
    h = pl.program_id(0)
    q = SWA_BLOCK
    key = lax.broadcasted_iota(jnp.int32, (q, q), 0)
    qry = lax.broadcasted_iota(jnp.int32, (q, q), 1)
    dist = qry - key

    def quadrants(head):
        near = jnp.where(dist >= 0, _bias_lookup(rel_bias_ref, _t5_bucket(dist), head) * LOG2E, NEG)
        nxt = _bias_lookup(rel_bias_ref, _t5_bucket(dist + q), head) * LOG2E
        return near, nxt

    near, nxt = quadrants(h)
    far = jnp.full((q, q), rel_bias_ref[NUM_BUCKETS - 1, h] * LOG2E, _F32)
    masked = jnp.full((q, q), NEG, _F32)
    own_ref[0, :q, :q] = near
    own_ref[0, :q, q:] = nxt
    own_ref[0, q:, :q] = masked
    own_ref[0, q:, q:] = near
    prev_ref[0, :q, :q] = far
    prev_ref[0, :q, q:] = far
    prev_ref[0, q:, :q] = nxt
    prev_ref[0, q:, q:] = far
    near, nxt = quadrants(h + N_HEADS_MOBA)
    swa_ref[0, :q, :] = jnp.where(dist < 0, nxt, NEG)
    swa_ref[0, q:, :] = near


def _bias_tiles(rel_bias):
    def per_head(rows, cols):
        return pl.BlockSpec((1, rows, cols), lambda h: (h, 0, 0))

    return pl.pallas_call(
        _bias_tiles_body,
        grid=(N_HEADS_MOBA,),
        in_specs=[pl.BlockSpec(memory_space=pltpu.SMEM)],
        out_specs=[per_head(MOBA_BLOCK, MOBA_BLOCK), per_head(MOBA_BLOCK, MOBA_BLOCK),
                   per_head(2 * SWA_BLOCK, SWA_BLOCK)],
        out_shape=[jax.ShapeDtypeStruct((N_HEADS_MOBA, MOBA_BLOCK, MOBA_BLOCK), _F32),
                   jax.ShapeDtypeStruct((N_HEADS_MOBA, MOBA_BLOCK, MOBA_BLOCK), _F32),
                   jax.ShapeDtypeStruct((N_HEADS_SWA, 2 * SWA_BLOCK, SWA_BLOCK), _F32)],
        compiler_params=_params(1),
        name="bias_tiles",
    )(rel_bias)


def _moba_body(rel_bias_ref, q_ref, k_ref, v_ref, kmean_ref, own_ref, prev_ref, o_ref,
               far_ref, qpad_ref, s_ref):
    g = pl.program_id(1)
    nb = k_ref.shape[1]
    key_lanes = HEADS_PER_KEY_TILE * HEAD_DIM
    q_row = lax.broadcasted_iota(jnp.int32, (key_lanes, MOBA_BLOCK), 0)
    km = kmean_ref[0]
    km_hi = km.astype(_BF16)
    km_lo = (km - km_hi.astype(_F32)).astype(_BF16)
    blk = lax.broadcasted_iota(jnp.int32, (nb, MOBA_BLOCK), 0)
    ones = jnp.ones((SUM_ROWS, MOBA_BLOCK), _BF16)
    ring_slots = range(GROUPS_PER_ITER)
    tail_slots = ring_slots[-PIPE_DEPTH:]
    near_slots = range(GROUPS_PER_ITER, GROUPS_PER_ITER + PIPE_DEPTH)
    assert PIPE_DEPTH == 2 and GROUPS_PER_ITER >= 2 * PIPE_DEPTH

    def key_tile(r):
        t, sub = divmod(r, HEADS_PER_KEY_TILE)
        return slice(t * key_lanes, (t + 1) * key_lanes), sub

    def prepare(i):
        past = blk < i
        own, prev = [], []
        for r in range(HEADS_PER_STEP):
            tile_lanes, sub = key_tile(r)
            q_t = q_ref[0, i, tile_lanes, :]
            q_pad = jnp.where((q_row >= sub * HEAD_DIM) & (q_row < (sub + 1) * HEAD_DIM), q_t, 0)
            qpad_ref[r] = q_pad

            gate = jnp.where(
                past, _dot(km_hi[:, tile_lanes], q_pad) + _dot(km_lo[:, tile_lanes], q_pad), NEG)
            sel = jnp.zeros(gate.shape, jnp.bool_)
            for _ in range(MOBA_TOPK):
                top = jnp.max(gate, axis=0, keepdims=True)
                first = jnp.min(jnp.where(gate == top, blk, nb), axis=0, keepdims=True)
                pick = blk == first
                sel = sel | pick
                gate = jnp.where(pick, -jnp.inf, gate)
            sel = sel & past

            far_bias = rel_bias_ref[NUM_BUCKETS - 1, g * HEADS_PER_STEP + r] * LOG2E
            far_ref[r] = jnp.where(sel & (blk < i - 1), far_bias, NEG)
            prev_neg = jnp.max(jnp.where(sel & (blk == i - 1), 0.0, NEG), axis=0, keepdims=True)
            own.append(own_ref[r])
            prev.append(prev_ref[r] + prev_neg)
        return own, prev

    def score(slot, j, tile_bias=None, row_bias=None):
        tops, shifts = [], []
        for r in range(HEADS_PER_STEP):
            s = _dot(k_ref[0, j, :, key_tile(r)[0]], qpad_ref[r])
            if tile_bias is not None:
                s = s + tile_bias[r]
            s_ref[slot, r] = s
            top = jnp.max(s, axis=0, keepdims=True)
            shift = jnp.zeros_like(top) if row_bias is None else row_bias[r]
            tops.append(top + shift)
            shifts.append(shift)
        return slot, j, tuple(tops), tuple(shifts)

    def absorb(carry, scored):
        slot, j, tops, shifts = scored
        out = []
        for r in range(HEADS_PER_STEP):
            lanes = slice(r * HEAD_DIM, (r + 1) * HEAD_DIM)
            m, acc = carry[r]
            m_new = jnp.maximum(m, tops[r])
            p = jnp.exp2(s_ref[slot, r] + (shifts[r] - m_new)).astype(_BF16)
            pv = _dot(jnp.concatenate([v_ref[0, j, lanes, :], ones], axis=0), p)
            out.append((m_new, jnp.exp2(m - m_new) * acc + pv))
        return tuple(out)

    def score_far(slot, grp):
        j = jnp.minimum(grp, nb - 1)
        return score(slot, j, row_bias=[far_ref[r, pl.ds(j, 1), :] for r in range(HEADS_PER_STEP)])

    def score_near(i):
        own, prev = prepare(i)
        return (score(near_slots[0], i, tile_bias=own)[2],
                score(near_slots[1], jnp.maximum(i - 1, 0), tile_bias=prev)[2])

    def start(i, near_tops):
        carry = tuple((jnp.full((1, MOBA_BLOCK), -jnp.inf, _F32),
                       jnp.zeros((HEAD_DIM + SUM_ROWS, MOBA_BLOCK), _F32)) for _ in range(HEADS_PER_STEP))
        near_js = (i, jnp.maximum(i - 1, 0))
        no_shift = tuple(jnp.zeros((1, MOBA_BLOCK), _F32) for _ in range(HEADS_PER_STEP))
        ring = {slot: (slot, j, tops, no_shift) for slot, j, tops in zip(near_slots, near_js, near_tops)}
        for c, slot in enumerate(tail_slots):
            ring[slot] = score_far(slot, c)
            carry = absorb(carry, ring[near_slots[c]])
        return carry, tuple(ring[slot][1:] for slot in tail_slots)

    def tile(i, state, last=False):
        def step(it, state):
            carry, pending = state
            ring = {slot: (slot,) + p for slot, p in zip(tail_slots, pending)}
            for k in ring_slots:
                scored = score_far(k, PIPE_DEPTH + it * GROUPS_PER_ITER + k)
                carry = absorb(carry, ring[(k - PIPE_DEPTH) % GROUPS_PER_ITER])
                ring[k] = scored
            return carry, tuple(ring[slot][1:] for slot in tail_slots)

        n_far = jnp.maximum(i - 1, 0)
        n_iters = (jnp.maximum(n_far - PIPE_DEPTH, 0) + GROUPS_PER_ITER - 1) // GROUPS_PER_ITER
        carry, pending = lax.fori_loop(0, n_iters, step, state)

        near_next = None if last else score_near(i + 1)
        for slot, p in zip(tail_slots, pending):
            carry = absorb(carry, (slot,) + p)
        for r in range(HEADS_PER_STEP):
            _, acc = carry[r]
            o_ref[0, i, r * HEAD_DIM:(r + 1) * HEAD_DIM, :] = acc[:HEAD_DIM] / acc[HEAD_DIM:HEAD_DIM + 1]
        return None if last else start(i + 1, near_next)

    state = lax.fori_loop(0, nb - 1, tile, start(0, score_near(0)))
    tile(nb - 1, state, last=True)


def _moba(rel_bias, qa_t, ka, va_t, kmean, bias_own, bias_prev):
    batch, nb = qa_t.shape[0], qa_t.shape[1]
    width = HEADS_PER_STEP * HEAD_DIM
    q_spec = pl.BlockSpec((1, nb, width, MOBA_BLOCK), lambda b, g: (b, 0, g, 0))
    bias_spec = pl.BlockSpec((HEADS_PER_STEP, MOBA_BLOCK, MOBA_BLOCK), lambda b, g: (g, 0, 0))
    return pl.pallas_call(
        _moba_body,
        grid=(batch, N_HEADS_MOBA // HEADS_PER_STEP),
        in_specs=[pl.BlockSpec(memory_space=pltpu.SMEM),
                  q_spec,
                  pl.BlockSpec((1, nb, MOBA_BLOCK, width), lambda b, g: (b, 0, 0, g)),
                  q_spec,
                  pl.BlockSpec((1, nb, width), lambda b, g: (b, 0, g)),
                  bias_spec, bias_spec],
        out_specs=q_spec,
        out_shape=jax.ShapeDtypeStruct((batch, nb, W_MOBA, MOBA_BLOCK), _F32),
        scratch_shapes=[pltpu.VMEM((HEADS_PER_STEP, nb, MOBA_BLOCK), _F32),
                        pltpu.VMEM((HEADS_PER_STEP, HEADS_PER_KEY_TILE * HEAD_DIM, MOBA_BLOCK), _BF16),
                        pltpu.VMEM((GROUPS_PER_ITER + PIPE_DEPTH, HEADS_PER_STEP, MOBA_BLOCK, MOBA_BLOCK), _F32)],
        compiler_params=_params(2),
        name="moba",
    )(rel_bias, qa_t, ka, va_t, kmean, bias_own, bias_prev)


def _swa_body(sinks_ref, q_ref, k_prev_ref, k_cur_ref, v_prev_ref, v_cur_ref, bias_ref, o_ref, s_ref):
    no_prev = jnp.where(pl.program_id(1) == 0, NEG, 0.0)
    window_row = lax.broadcasted_iota(jnp.int32, (2 * SWA_BLOCK, SWA_BLOCK), 0)
    first_window_mask = jnp.where(window_row < SWA_BLOCK, no_prev, 0.0)
    zeros = jnp.zeros((HEAD_DIM, SWA_BLOCK), _BF16)
    ones = jnp.ones((SUM_ROWS, 2 * SWA_BLOCK), _BF16)
    k_blocks = [k_prev_ref[0, 0]] + [k_cur_ref[0, t] for t in range(SWA_Q_PER_STEP)]
    v_blocks = [v_prev_ref[0, 0]] + [v_cur_ref[0, t] for t in range(SWA_Q_PER_STEP)]
    chains = [(t, kv) for t in range(SWA_Q_PER_STEP) for kv in range(N_KV_SWA)]

    for t, kv in chains:
        q_pad = jnp.concatenate(
            [jnp.concatenate(
                [q_ref[0, t, h * HEAD_DIM:(h + 1) * HEAD_DIM, :] if part == kv else zeros
                 for part in range(N_KV_SWA)], axis=0)
             for h in range(kv * SWA_GROUP, (kv + 1) * SWA_GROUP)], axis=1)
        window_keys = jnp.concatenate([k_blocks[t], k_blocks[t + 1]], axis=0)
        s = _dot(window_keys, q_pad)
        for slot in range(SWA_GROUP):
            cols = slice(slot * SWA_BLOCK, (slot + 1) * SWA_BLOCK)
            biased = s[:, cols] + bias_ref[kv * SWA_GROUP + slot]
            s_ref[t, kv, :, cols] = biased + first_window_mask if t == 0 else biased

    for t, kv in chains:
        kv_rows = slice(kv * HEAD_DIM, (kv + 1) * HEAD_DIM)
        window_values = jnp.concatenate(
            [jnp.concatenate([v_blocks[t][kv_rows, :], v_blocks[t + 1][kv_rows, :]], axis=1), ones], axis=0)
        p, sink_p = [], []
        for slot in range(SWA_GROUP):
            cols = slice(slot * SWA_BLOCK, (slot + 1) * SWA_BLOCK)
            sink = sinks_ref[kv * SWA_GROUP + slot] * LOG2E
            m = jnp.maximum(jnp.max(s_ref[t, kv, :, cols], axis=0, keepdims=True), sink)
            p.append(jnp.exp2(s_ref[t, kv, :, cols] - m).astype(_BF16))
            sink_p.append(jnp.exp2(sink - m))
        acc = _dot(window_values, jnp.concatenate(p, axis=1))
        for slot in range(SWA_GROUP):
            h = kv * SWA_GROUP + slot
            cols = slice(slot * SWA_BLOCK, (slot + 1) * SWA_BLOCK)
            l = acc[HEAD_DIM:HEAD_DIM + 1, cols] + sink_p[slot]
            o_ref[0, t, h * HEAD_DIM:(h + 1) * HEAD_DIM, :] = acc[:HEAD_DIM, cols] / l


def _swa(sinks, qs_t, ks, vs_t, bias_swa):
    batch, nb = qs_t.shape[0], qs_t.shape[1]

    def cur(rows, cols):
        return pl.BlockSpec((1, SWA_Q_PER_STEP, rows, cols), lambda b, n: (b, n, 0, 0))

    def prev(rows, cols):
        return pl.BlockSpec((1, 1, rows, cols),
                            lambda b, n: (b, jnp.maximum(n * SWA_Q_PER_STEP - 1, 0), 0, 0))

    return pl.pallas_call(
        _swa_body,
        grid=(batch, nb // SWA_Q_PER_STEP),
        in_specs=[pl.BlockSpec(memory_space=pltpu.SMEM),
                  cur(W_SWA, SWA_BLOCK),
                  prev(SWA_BLOCK, W_SWA_KV), cur(SWA_BLOCK, W_SWA_KV),
                  prev(W_SWA_KV, SWA_BLOCK), cur(W_SWA_KV, SWA_BLOCK),
                  _const_spec((N_HEADS_SWA, 2 * SWA_BLOCK, SWA_BLOCK))],
        out_specs=cur(W_SWA, SWA_BLOCK),
        out_shape=jax.ShapeDtypeStruct((batch, nb, W_SWA, SWA_BLOCK), _F32),
        scratch_shapes=[pltpu.VMEM((SWA_Q_PER_STEP, N_KV_SWA, 2 * SWA_BLOCK, SWA_GROUP * SWA_BLOCK), _F32)],
        compiler_params=_params(2),
        name="swa",
    )(sinks, qs_t, ks, ks, vs_t, vs_t, bias_swa)


def _group_norm_rows(o_ref, g_ref):
    rows = []
    for t in range(o_ref.shape[0]):
        o_t = o_ref[t]
        scale = lax.rsqrt(jnp.mean(o_t * o_t, axis=0, keepdims=True) + RMS_EPS)
        rows.append(((o_t * scale).T * g_ref[...]).astype(_BF16))
    return jnp.concatenate(rows, axis=0)


def _mix_ffn_body(x_ref, oa_ref, ob_ref, ga_ref, gb_ref, w_out_ref, mix_post_g_ref,
                  pre_g_ref, wg_hbm, wu_hbm, wd_hbm, post_g_ref, o_ref, h_ref, *weight_scratch):
    wg_ref, wu_ref, wd_ref = _ffn_weights((wg_hbm, wu_hbm, wd_hbm), weight_scratch)
    y = (_dot(_group_norm_rows(oa_ref, ga_ref), w_out_ref[:W_MOBA, :])
         + _dot(_group_norm_rows(ob_ref, gb_ref), w_out_ref[W_MOBA:, :]))
    x = x_ref[...] + _rms_rows(y, mix_post_g_ref[...])
    _ffn_tile(lambda rows: x[rows, :], pre_g_ref, wg_ref, wu_ref, wd_ref, post_g_ref, o_ref, h_ref)


def _mix_ffn(x2d, oa_t, ob_t, ga, gb, w_out, mix_post_g, ffn_params):
    n_tok = x2d.shape[0]
    return pl.pallas_call(
        _mix_ffn_body,
        grid=(n_tok // TOKEN_TILE,),
        in_specs=[_TOKEN_TILE_SPEC,
                  pl.BlockSpec((_MOBA_PER_TILE, W_MOBA, MOBA_BLOCK), lambda i: (i, 0, 0)),
                  pl.BlockSpec((_SWA_PER_TILE, W_SWA, SWA_BLOCK), lambda i: (i, 0, 0)),
                  _const_spec((1, W_MOBA)), _const_spec((1, W_SWA)),
                  _const_spec((W_MOBA + W_SWA, D_MODEL)),
                  _const_spec((1, D_MODEL))] + _FFN_WEIGHT_SPECS,
        out_specs=_TOKEN_TILE_SPEC,
        out_shape=jax.ShapeDtypeStruct((n_tok, D_MODEL), _F32),
        scratch_shapes=_FFN_SCRATCH,
        compiler_params=_params(1),
        name="mix_ffn",
    )(x2d, oa_t, ob_t, ga, gb, w_out, mix_post_g, *ffn_params)


def _row(v):
    return v.reshape(1, -1)


def _layer(x, ffn1, mix, ffn2, rel_bias, bias_tiles):
    batch, seq, _ = x.shape
    (mix_pre_g, w_in, moba_out_g, swa_sinks, swa_out_g, w_out, mix_post_g) = mix
    bias_own, bias_prev, bias_swa = bias_tiles

    def ffn_params(params):
        pre_g, w_gate, w_up, w_down, post_g = params
        return (_row(pre_g), w_gate, w_up, w_down, _row(post_g))

    x = _ffn(x.reshape(batch * seq, D_MODEL), ffn_params(ffn1))

    qa, ka, va, qs, ks, vs = jnp.split(
        w_in, [W_MOBA, 2 * W_MOBA, 3 * W_MOBA, 3 * W_MOBA + W_SWA, 3 * W_MOBA + W_SWA + W_SWA_KV], axis=1)
    w_rows = jnp.concatenate([ka, ks], axis=1).astype(_BF16)
    w_t = jnp.concatenate([qa, va, qs, vs], axis=1).T.astype(_BF16)
    qa_t, va_t, qs_t, vs_t, ka_b, ks_b, kmean = _in_proj(
        x.reshape(batch, seq, D_MODEL), _row(mix_pre_g), w_rows, w_t)

    oa_t = _moba(rel_bias, qa_t, ka_b, va_t, kmean.reshape(batch, seq // MOBA_BLOCK, W_MOBA),
                 bias_own, bias_prev)
    ob_t = _swa(swa_sinks, qs_t, ks_b, vs_t, bias_swa)

    x = _mix_ffn(x, oa_t.reshape(-1, W_MOBA, MOBA_BLOCK), ob_t.reshape(-1, W_SWA, SWA_BLOCK),
                 _row(moba_out_g), _row(swa_out_g), w_out.astype(_BF16), _row(mix_post_g),
                 ffn_params(ffn2))
    return x.reshape(batch, seq, D_MODEL)


def kernel(x, ffn1_pre_g, ffn1_w_gate, ffn1_w_up, ffn1_w_down, ffn1_post_g, mix_pre_g, w_in, rel_bias,
           moba_out_g, swa_sinks, swa_out_g, w_out, mix_post_g, ffn2_pre_g, ffn2_w_gate, ffn2_w_up,
           ffn2_w_down, ffn2_post_g):
    bias_tiles = _bias_tiles(rel_bias)
    for l in range(ffn1_pre_g.shape[0]):
        x = _layer(
            x,
            (ffn1_pre_g[l], ffn1_w_gate[l], ffn1_w_up[l], ffn1_w_down[l], ffn1_post_g[l]),
            (mix_pre_g[l], w_in[l], moba_out_g[l], swa_sinks[l], swa_out_g[l], w_out[l], mix_post_g[l]),
            (ffn2_pre_g[l], ffn2_w_gate[l], ffn2_w_up[l], ffn2_w_down[l], ffn2_post_g[l]),
            rel_bias, bias_tiles)
    return x
```

```python
import functools
import math

import jax
import jax.numpy as jnp
from jax import lax
from jax.experimental import pallas as pl
from jax.experimental.pallas import tpu as pltpu

D_MODEL = 1024
HEAD_DIM = 64
N_HEADS_MOBA = 8
N_HEADS_SWA = 8
N_KV_SWA = 2
SWA_GROUP = N_HEADS_SWA // N_KV_SWA
W_MOBA = N_HEADS_MOBA * HEAD_DIM
W_SWA = N_HEADS_SWA * HEAD_DIM
W_SWA_KV = N_KV_SWA * HEAD_DIM
MOBA_BLOCK = 256
MOBA_TOPK = 3
SWA_BLOCK = 128
SWA_WINDOW = 128
NUM_BUCKETS = 32
MAX_DISTANCE = 128
D_FF = 2816
RMS_EPS = 1e-6
FFN_RES_WEIGHT = 0.5
NEG = -1e30
QK_SCALE = HEAD_DIM ** -0.5
LOG2E = math.log2(math.e)
Q_FOLD = QK_SCALE * LOG2E
SUM_ROWS = 16

TOKEN_TILE = 1024
FF_CHUNK = 256
FFN_ROW_PARTS = 2
CAST_CHUNKS = 16
CAST_SLOTS = 4
HEADS_PER_STEP = 4
HEADS_PER_KEY_TILE = 2
PIPE_DEPTH = 2
GROUPS_PER_ITER = 4
SWA_Q_PER_STEP = 4
VMEM_LIMIT_BYTES = 58 * 1024 * 1024

_BF16 = jnp.bfloat16
_F32 = jnp.float32


def _dot(a, b):
    return jnp.dot(a, b, preferred_element_type=_F32)


def _dot_nt(a, b):
    return lax.dot_general(a, b, (((1,), (1,)), ((), ())), preferred_element_type=_F32)


def _rms_rows(x, g):
    return x * lax.rsqrt(jnp.mean(x * x, axis=-1, keepdims=True) + RMS_EPS) * g


def _const_spec(shape):
    return pl.BlockSpec(shape, lambda *_: (0,) * len(shape), pipeline_mode=pl.Buffered(1))


def _params(n_axes):
    return pltpu.CompilerParams(
        dimension_semantics=("arbitrary",) * n_axes, vmem_limit_bytes=VMEM_LIMIT_BYTES)


def _ffn_tile(x_rows, pre_g_ref, wg_ref, wu_ref, wd_ref, post_g_ref, o_ref, h_ref):
    part = TOKEN_TILE // FFN_ROW_PARTS
    for p in range(FFN_ROW_PARTS):
        rows = slice(p * part, (p + 1) * part)
        x = x_rows(rows)
        xn = _rms_rows(x, pre_g_ref[...]).astype(_BF16)
        for c in range(D_FF // FF_CHUNK):
            cols = slice(c * FF_CHUNK, (c + 1) * FF_CHUNK)
            gate = _dot(xn, wg_ref[:, cols])
            up = _dot(xn, wu_ref[:, cols])
            h_ref[rows, cols] = (jax.nn.silu(gate) * up).astype(_BF16)
        y = _dot(h_ref[rows, :], wd_ref[...])
        o_ref[rows, :] = x + FFN_RES_WEIGHT * _rms_rows(y, post_g_ref[...])


def _cast_weights(pairs, stage, sem):
    slots, rows = stage.shape[0], stage.shape[1]
    chunks = [(w_hbm, w_bf, k) for w_hbm, w_bf in pairs for k in range(w_hbm.shape[0] // rows)]

    def copy(c):
        w_hbm, _, k = chunks[c]
        return pltpu.make_async_copy(w_hbm.at[pl.ds(k * rows, rows), :], stage.at[c % slots], sem.at[c % slots])

    def prime():
        for c in range(min(slots, len(chunks))):
            copy(c).start()

    def drain():
        for c, (_, w_bf, k) in enumerate(chunks):
            copy(c).wait()
            w_bf[k * rows:(k + 1) * rows, :] = stage[c % slots].astype(_BF16)
            if c + slots < len(chunks):
                copy(c + slots).start()

    return prime, drain


def _ffn_weights(w_hbm, scratch):
    wg_bf, wu_bf, wd_bf, stage_in, stage_down, sem_in, sem_down = scratch

    @pl.when(pl.program_id(0) == 0)
    def _():
        prime_in, drain_in = _cast_weights([(w_hbm[0], wg_bf), (w_hbm[1], wu_bf)], stage_in, sem_in)
        prime_down, drain_down = _cast_weights([(w_hbm[2], wd_bf)], stage_down, sem_down)
        prime_in()
        prime_down()
        drain_in()
        drain_down()

    return wg_bf, wu_bf, wd_bf


def _ffn_body(x_ref, pre_g_ref, wg_hbm, wu_hbm, wd_hbm, post_g_ref, o_ref, h_ref, *weight_scratch):
    wg_ref, wu_ref, wd_ref = _ffn_weights((wg_hbm, wu_hbm, wd_hbm), weight_scratch)
    _ffn_tile(lambda rows: x_ref[rows, :], pre_g_ref, wg_ref, wu_ref, wd_ref, post_g_ref, o_ref, h_ref)


_TOKEN_TILE_SPEC = pl.BlockSpec((TOKEN_TILE, D_MODEL), lambda i: (i, 0))
_HBM_SPEC = pl.BlockSpec(memory_space=pl.ANY)
_FFN_WEIGHT_SPECS = [_const_spec((1, D_MODEL)), _HBM_SPEC, _HBM_SPEC, _HBM_SPEC, _const_spec((1, D_MODEL))]
_FFN_SCRATCH = [
    pltpu.VMEM((TOKEN_TILE, D_FF), _BF16),
    pltpu.VMEM((D_MODEL, D_FF), _BF16),
    pltpu.VMEM((D_MODEL, D_FF), _BF16),
    pltpu.VMEM((D_FF, D_MODEL), _BF16),
    pltpu.VMEM((CAST_SLOTS, D_MODEL // CAST_CHUNKS, D_FF), _F32),
    pltpu.VMEM((CAST_SLOTS, D_FF // CAST_CHUNKS, D_MODEL), _F32),
    pltpu.SemaphoreType.DMA((CAST_SLOTS,)),
    pltpu.SemaphoreType.DMA((CAST_SLOTS,)),
]


def _ffn(x2d, ffn_params):
    n_tok = x2d.shape[0]
    return pl.pallas_call(
        _ffn_body,
        grid=(n_tok // TOKEN_TILE,),
        in_specs=[_TOKEN_TILE_SPEC] + _FFN_WEIGHT_SPECS,
        out_specs=_TOKEN_TILE_SPEC,
        out_shape=jax.ShapeDtypeStruct((n_tok, D_MODEL), _F32),
        scratch_shapes=_FFN_SCRATCH,
        compiler_params=_params(1),
        name="ffn",
    )(x2d, *ffn_params)


_T_QA, _T_VA, _T_QS, _T_VS = 0, W_MOBA, 2 * W_MOBA, 2 * W_MOBA + W_SWA
_T_ROWS = 2 * W_MOBA + W_SWA + W_SWA_KV
_MOBA_PER_TILE = TOKEN_TILE // MOBA_BLOCK
_SWA_PER_TILE = TOKEN_TILE // SWA_BLOCK


def _in_proj_body(x_ref, g_ref, w_rows_ref, w_t_ref,
                  qa_ref, va_ref, qs_ref, vs_ref, ka_ref, ks_ref, kmean_ref):
    swa_per_moba = MOBA_BLOCK // SWA_BLOCK
    for t in range(_MOBA_PER_TILE):
        xn = _rms_rows(x_ref[0, t * MOBA_BLOCK:(t + 1) * MOBA_BLOCK, :], g_ref[...]).astype(_BF16)
        k_rows = _dot(xn, w_rows_ref[...])
        proj_t = _dot_nt(w_t_ref[...], xn)
        k_blk = k_rows[:, :W_MOBA]
        ka_ref[0, t] = k_blk.astype(_BF16)
        kmean_ref[0, t] = jnp.mean(k_blk, axis=0, keepdims=True)
        qa_ref[0, t] = (proj_t[_T_QA:_T_QA + W_MOBA, :] * Q_FOLD).astype(_BF16)
        va_ref[0, t] = proj_t[_T_VA:_T_VA + W_MOBA, :].astype(_BF16)
        for u in range(swa_per_moba):
            rows = slice(u * SWA_BLOCK, (u + 1) * SWA_BLOCK)
            ks_ref[0, t * swa_per_moba + u] = k_rows[rows, W_MOBA:].astype(_BF16)
            qs_ref[0, t * swa_per_moba + u] = (proj_t[_T_QS:_T_QS + W_SWA, rows] * Q_FOLD).astype(_BF16)
            vs_ref[0, t * swa_per_moba + u] = proj_t[_T_VS:_T_VS + W_SWA_KV, rows].astype(_BF16)


def _in_proj(x, g, w_rows, w_t):
    batch, seq, _ = x.shape
    nb_moba, nb_swa = seq // MOBA_BLOCK, seq // SWA_BLOCK

    def blocked(n_per_tile, rows, cols):
        return pl.BlockSpec((1, n_per_tile, rows, cols), lambda b, i: (b, i, 0, 0))

    return pl.pallas_call(
        _in_proj_body,
        grid=(batch, seq // TOKEN_TILE),
        in_specs=[pl.BlockSpec((1, TOKEN_TILE, D_MODEL), lambda b, i: (b, i, 0)),
                  _const_spec((1, D_MODEL)),
                  _const_spec((D_MODEL, W_MOBA + W_SWA_KV)),
                  _const_spec((_T_ROWS, D_MODEL))],
        out_specs=[blocked(_MOBA_PER_TILE, W_MOBA, MOBA_BLOCK),
                   blocked(_MOBA_PER_TILE, W_MOBA, MOBA_BLOCK),
                   blocked(_SWA_PER_TILE, W_SWA, SWA_BLOCK),
                   blocked(_SWA_PER_TILE, W_SWA_KV, SWA_BLOCK),
                   blocked(_MOBA_PER_TILE, MOBA_BLOCK, W_MOBA),
                   blocked(_SWA_PER_TILE, SWA_BLOCK, W_SWA_KV),
                   blocked(_MOBA_PER_TILE, 1, W_MOBA)],
        out_shape=[jax.ShapeDtypeStruct((batch, nb_moba, W_MOBA, MOBA_BLOCK), _BF16),
                   jax.ShapeDtypeStruct((batch, nb_moba, W_MOBA, MOBA_BLOCK), _BF16),
                   jax.ShapeDtypeStruct((batch, nb_swa, W_SWA, SWA_BLOCK), _BF16),
                   jax.ShapeDtypeStruct((batch, nb_swa, W_SWA_KV, SWA_BLOCK), _BF16),
                   jax.ShapeDtypeStruct((batch, nb_moba, MOBA_BLOCK, W_MOBA), _BF16),
                   jax.ShapeDtypeStruct((batch, nb_swa, SWA_BLOCK, W_SWA_KV), _BF16),
                   jax.ShapeDtypeStruct((batch, nb_moba, 1, W_MOBA), _F32)],
        compiler_params=_params(2),
        name="in_proj",
    )(x, g, w_rows, w_t)


def _t5_bucket(dist):
    n = jnp.maximum(dist, 0)
    max_exact = NUM_BUCKETS // 2
    nf = jnp.maximum(n, 1).astype(_F32)
    large = max_exact + jnp.floor(jnp.log(nf / max_exact) / math.log(MAX_DISTANCE / max_exact)
                                  * (NUM_BUCKETS - max_exact))
    large = jnp.minimum(large, float(NUM_BUCKETS - 1))
    return jnp.where(n < max_exact, n.astype(_F32), large)


def _bias_lookup(rel_bias_ref, bucket, head):
    val = jnp.full(bucket.shape, rel_bias_ref[NUM_BUCKETS - 1, head], _F32)
    for b in range(NUM_BUCKETS - 2, -1, -1):
        val = jnp.where(bucket == float(b), rel_bias_ref[b, head], val)
    return val


def _bias_tiles_body(rel_bias_ref, own_ref, prev_ref, swa_ref):
    assert MOBA_BLOCK == 2 * SWA_BLOCK and SWA_WINDOW == SWA_BLOCK and MAX_DISTANCE <= SWA_BLOCK
    h = pl.program_id(0)
    q = SWA_BLOCK
    key = lax.broadcasted_iota(jnp.int32, (q, q), 0)
    qry = lax.broadcasted_iota(jnp.int32, (q, q), 1)
    dist = qry - key

    def quadrants(head):
        near = jnp.where(dist >= 0, _bias_lookup(rel_bias_ref, _t5_bucket(dist), head) * LOG2E, NEG)
        nxt = _bias_lookup(rel_bias_ref, _t5_bucket(dist + q), head) * LOG2E
        return near, nxt

    near, nxt = quadrants(h)
    far = jnp.full((q, q), rel_bias_ref[NUM_BUCKETS - 1, h] * LOG2E, _F32)
    masked = jnp.full((q, q), NEG, _F32)
    own_ref[0, :q, :q] = near
    own_ref[0, :q, q:] = nxt
    own_ref[0, q:, :q] = masked
    own_ref[0, q:, q:] = near
    prev_ref[0, :q, :q] = far
    prev_ref[0, :q, q:] = far
    prev_ref[0, q:, :q] = nxt
    prev_ref[0, q:, q:] = far
    near, nxt = quadrants(h + N_HEADS_MOBA)
    swa_ref[0, :q, :] = jnp.where(dist < 0, nxt, NEG)
    swa_ref[0, q:, :] = near


def _bias_tiles(rel_bias):
    def per_head(rows, cols):
        return pl.BlockSpec((1, rows, cols), lambda h: (h, 0, 0))

    return pl.pallas_call(
        _bias_tiles_body,
        grid=(N_HEADS_MOBA,),
        in_specs=[pl.BlockSpec(memory_space=pltpu.SMEM)],
        out_specs=[per_head(MOBA_BLOCK, MOBA_BLOCK), per_head(MOBA_BLOCK, MOBA_BLOCK),
                   per_head(2 * SWA_BLOCK, SWA_BLOCK)],
        out_shape=[jax.ShapeDtypeStruct((N_HEADS_MOBA, MOBA_BLOCK, MOBA_BLOCK), _F32),
                   jax.ShapeDtypeStruct((N_HEADS_MOBA, MOBA_BLOCK, MOBA_BLOCK), _F32),
                   jax.ShapeDtypeStruct((N_HEADS_SWA, 2 * SWA_BLOCK, SWA_BLOCK), _F32)],
        compiler_params=_params(1),
        name="bias_tiles",
    )(rel_bias)


def _moba_body(rel_bias_ref, q_ref, k_ref, v_ref, kmean_ref, own_ref, prev_ref, o_ref,
               far_ref, qpad_ref, s_ref):
    g = pl.program_id(1)
    nb = k_ref.shape[1]
    key_lanes = HEADS_PER_KEY_TILE * HEAD_DIM
    q_row = lax.broadcasted_iota(jnp.int32, (key_lanes, MOBA_BLOCK), 0)
    km = kmean_ref[0]
    km_hi = km.astype(_BF16)
    km_lo = (km - km_hi.astype(_F32)).astype(_BF16)
    blk = lax.broadcasted_iota(jnp.int32, (nb, MOBA_BLOCK), 0)
    ones = jnp.ones((SUM_ROWS, MOBA_BLOCK), _BF16)
    ring_slots = range(GROUPS_PER_ITER)
    tail_slots = ring_slots[-PIPE_DEPTH:]
    near_slots = range(GROUPS_PER_ITER, GROUPS_PER_ITER + PIPE_DEPTH)
    assert PIPE_DEPTH == 2 and GROUPS_PER_ITER >= 2 * PIPE_DEPTH

    def key_tile(r):
        t, sub = divmod(r, HEADS_PER_KEY_TILE)
        return slice(t * key_lanes, (t + 1) * key_lanes), sub

    def prepare(i):
        past = blk < i
        own, prev = [], []
        for r in range(HEADS_PER_STEP):
            tile_lanes, sub = key_tile(r)
            q_t = q_ref[0, i, tile_lanes, :]
            q_pad = jnp.where((q_row >= sub * HEAD_DIM) & (q_row < (sub + 1) * HEAD_DIM), q_t, 0)
            qpad_ref[r] = q_pad

            gate = jnp.where(
                past, _dot(km_hi[:, tile_lanes], q_pad) + _dot(km_lo[:, tile_lanes], q_pad), NEG)
            sel = jnp.zeros(gate.shape, jnp.bool_)
            for _ in range(MOBA_TOPK):
                top = jnp.max(gate, axis=0, keepdims=True)
                first = jnp.min(jnp.where(gate == top, blk, nb), axis=0, keepdims=True)
                pick = blk == first
                sel = sel | pick
                gate = jnp.where(pick, -jnp.inf, gate)
            sel = sel & past

            far_bias = rel_bias_ref[NUM_BUCKETS - 1, g * HEADS_PER_STEP + r] * LOG2E
            far_ref[r] = jnp.where(sel & (blk < i - 1), far_bias, NEG)
            prev_neg = jnp.max(jnp.where(sel & (blk == i - 1), 0.0, NEG), axis=0, keepdims=True)
            own.append(own_ref[r])
            prev.append(prev_ref[r] + prev_neg)
        return own, prev

    def score(slot, j, tile_bias=None, row_bias=None):
        tops, shifts = [], []
        for r in range(HEADS_PER_STEP):
            s = _dot(k_ref[0, j, :, key_tile(r)[0]], qpad_ref[r])
            if tile_bias is not None:
                s = s + tile_bias[r]
            s_ref[slot, r] = s
            top = jnp.max(s, axis=0, keepdims=True)
            shift = jnp.zeros_like(top) if row_bias is None else row_bias[r]
            tops.append(top + shift)
            shifts.append(shift)
        return slot, j, tuple(tops), tuple(shifts)

    def absorb(carry, scored):
        slot, j, tops, shifts = scored
        out = []
        for r in range(HEADS_PER_STEP):
            lanes = slice(r * HEAD_DIM, (r + 1) * HEAD_DIM)
            m, acc = carry[r]
            m_new = jnp.maximum(m, tops[r])
            p = jnp.exp2((s_ref[slot, r] + (shifts[r] - m_new)).astype(_BF16))
            pv = _dot(jnp.concatenate([v_ref[0, j, lanes, :], ones], axis=0), p)
            out.append((m_new, jnp.exp2(m - m_new) * acc + pv))
        return tuple(out)

    def score_far(slot, grp):
        j = jnp.minimum(grp, nb - 1)
        return score(slot, j, row_bias=[far_ref[r, pl.ds(j, 1), :] for r in range(HEADS_PER_STEP)])

    def score_near(i):
        own, prev = prepare(i)
        return (score(near_slots[0], i, tile_bias=own)[2],
                score(near_slots[1], jnp.maximum(i - 1, 0), tile_bias=prev)[2])

    def start(i, near_tops):
        carry = tuple((jnp.full((1, MOBA_BLOCK), -jnp.inf, _F32),
                       jnp.zeros((HEAD_DIM + SUM_ROWS, MOBA_BLOCK), _F32)) for _ in range(HEADS_PER_STEP))
        near_js = (i, jnp.maximum(i - 1, 0))
        no_shift = tuple(jnp.zeros((1, MOBA_BLOCK), _F32) for _ in range(HEADS_PER_STEP))
        ring = {slot: (slot, j, tops, no_shift) for slot, j, tops in zip(near_slots, near_js, near_tops)}
        for c, slot in enumerate(tail_slots):
            ring[slot] = score_far(slot, c)
            carry = absorb(carry, ring[near_slots[c]])
        return carry, tuple(ring[slot][1:] for slot in tail_slots)

    def tile(i, state, last=False):
        def step(it, state):
            carry, pending = state
            ring = {slot: (slot,) + p for slot, p in zip(tail_slots, pending)}
            for k in ring_slots:
                scored = score_far(k, PIPE_DEPTH + it * GROUPS_PER_ITER + k)
                carry = absorb(carry, ring[(k - PIPE_DEPTH) % GROUPS_PER_ITER])
                ring[k] = scored
            return carry, tuple(ring[slot][1:] for slot in tail_slots)

        n_far = jnp.maximum(i - 1, 0)
        n_iters = (jnp.maximum(n_far - PIPE_DEPTH, 0) + GROUPS_PER_ITER - 1) // GROUPS_PER_ITER
        carry, pending = lax.fori_loop(0, n_iters, step, state)

        near_next = None if last else score_near(i + 1)
        for slot, p in zip(tail_slots, pending):
            carry = absorb(carry, (slot,) + p)
        for r in range(HEADS_PER_STEP):
            _, acc = carry[r]
            o_ref[0, i, r * HEAD_DIM:(r + 1) * HEAD_DIM, :] = acc[:HEAD_DIM] / acc[HEAD_DIM:HEAD_DIM + 1]
        return None if last else start(i + 1, near_next)

    state = lax.fori_loop(0, nb - 1, tile, start(0, score_near(0)))
    tile(nb - 1, state, last=True)


def _moba(rel_bias, qa_t, ka, va_t, kmean, bias_own, bias_prev):
    batch, nb = qa_t.shape[0], qa_t.shape[1]
    width = HEADS_PER_STEP * HEAD_DIM
    q_spec = pl.BlockSpec((1, nb, width, MOBA_BLOCK), lambda b, g: (b, 0, g, 0))
    bias_spec = pl.BlockSpec((HEADS_PER_STEP, MOBA_BLOCK, MOBA_BLOCK), lambda b, g: (g, 0, 0))
    return pl.pallas_call(
        _moba_body,
        grid=(batch, N_HEADS_MOBA // HEADS_PER_STEP),
        in_specs=[pl.BlockSpec(memory_space=pltpu.SMEM),
                  q_spec,
                  pl.BlockSpec((1, nb, MOBA_BLOCK, width), lambda b, g: (b, 0, 0, g)),
                  q_spec,
                  pl.BlockSpec((1, nb, width), lambda b, g: (b, 0, g)),
                  bias_spec, bias_spec],
        out_specs=q_spec,
        out_shape=jax.ShapeDtypeStruct((batch, nb, W_MOBA, MOBA_BLOCK), _F32),
        scratch_shapes=[pltpu.VMEM((HEADS_PER_STEP, nb, MOBA_BLOCK), _F32),
                        pltpu.VMEM((HEADS_PER_STEP, HEADS_PER_KEY_TILE * HEAD_DIM, MOBA_BLOCK), _BF16),
                        pltpu.VMEM((GROUPS_PER_ITER + PIPE_DEPTH, HEADS_PER_STEP, MOBA_BLOCK, MOBA_BLOCK), _F32)],
        compiler_params=_params(2),
        name="moba",
    )(rel_bias, qa_t, ka, va_t, kmean, bias_own, bias_prev)


def _swa_body(sinks_ref, q_ref, k_prev_ref, k_cur_ref, v_prev_ref, v_cur_ref, bias_ref, o_ref, s_ref):
    no_prev = jnp.where(pl.program_id(1) == 0, NEG, 0.0)
    window_row = lax.broadcasted_iota(jnp.int32, (2 * SWA_BLOCK, SWA_BLOCK), 0)
    first_window_mask = jnp.where(window_row < SWA_BLOCK, no_prev, 0.0)
    zeros = jnp.zeros((HEAD_DIM, SWA_BLOCK), _BF16)
    ones = jnp.ones((SUM_ROWS, 2 * SWA_BLOCK), _BF16)
    k_blocks = [k_prev_ref[0, 0]] + [k_cur_ref[0, t] for t in range(SWA_Q_PER_STEP)]
    v_blocks = [v_prev_ref[0, 0]] + [v_cur_ref[0, t] for t in range(SWA_Q_PER_STEP)]
    chains = [(t, kv) for t in range(SWA_Q_PER_STEP) for kv in range(N_KV_SWA)]

    for t, kv in chains:
        q_pad = jnp.concatenate(
            [jnp.concatenate(
                [q_ref[0, t, h * HEAD_DIM:(h + 1) * HEAD_DIM, :] if part == kv else zeros
                 for part in range(N_KV_SWA)], axis=0)
             for h in range(kv * SWA_GROUP, (kv + 1) * SWA_GROUP)], axis=1)
        window_keys = jnp.concatenate([k_blocks[t], k_blocks[t + 1]], axis=0)
        s = _dot(window_keys, q_pad)
        for slot in range(SWA_GROUP):
            cols = slice(slot * SWA_BLOCK, (slot + 1) * SWA_BLOCK)
            biased = s[:, cols] + bias_ref[kv * SWA_GROUP + slot]
            s_ref[t, kv, :, cols] = biased + first_window_mask if t == 0 else biased

    for t, kv in chains:
        kv_rows = slice(kv * HEAD_DIM, (kv + 1) * HEAD_DIM)
        window_values = jnp.concatenate(
            [jnp.concatenate([v_blocks[t][kv_rows, :], v_blocks[t + 1][kv_rows, :]], axis=1), ones], axis=0)
        p, sink_p = [], []
        for slot in range(SWA_GROUP):
            cols = slice(slot * SWA_BLOCK, (slot + 1) * SWA_BLOCK)
            sink = sinks_ref[kv * SWA_GROUP + slot] * LOG2E
            m = jnp.maximum(jnp.max(s_ref[t, kv, :, cols], axis=0, keepdims=True), sink)
            p.append(jnp.exp2((s_ref[t, kv, :, cols] - m).astype(_BF16)))
            sink_p.append(jnp.exp2(sink - m))
        acc = _dot(window_values, jnp.concatenate(p, axis=1))
        for slot in range(SWA_GROUP):
            h = kv * SWA_GROUP + slot
            cols = slice(slot * SWA_BLOCK, (slot + 1) * SWA_BLOCK)
            l = acc[HEAD_DIM:HEAD_DIM + 1, cols] + sink_p[slot]
            o_ref[0, t, h * HEAD_DIM:(h + 1) * HEAD_DIM, :] = acc[:HEAD_DIM, cols] / l


def _swa(sinks, qs_t, ks, vs_t, bias_swa):
    batch, nb = qs_t.shape[0], qs_t.shape[1]

    def cur(rows, cols):
        return pl.BlockSpec((1, SWA_Q_PER_STEP, rows, cols), lambda b, n: (b, n, 0, 0))

    def prev(rows, cols):
        return pl.BlockSpec((1, 1, rows, cols),
                            lambda b, n: (b, jnp.maximum(n * SWA_Q_PER_STEP - 1, 0), 0, 0))

    return pl.pallas_call(
        _swa_body,
        grid=(batch, nb // SWA_Q_PER_STEP),
        in_specs=[pl.BlockSpec(memory_space=pltpu.SMEM),
                  cur(W_SWA, SWA_BLOCK),
                  prev(SWA_BLOCK, W_SWA_KV), cur(SWA_BLOCK, W_SWA_KV),
                  prev(W_SWA_KV, SWA_BLOCK), cur(W_SWA_KV, SWA_BLOCK),
                  _const_spec((N_HEADS_SWA, 2 * SWA_BLOCK, SWA_BLOCK))],
        out_specs=cur(W_SWA, SWA_BLOCK),
        out_shape=jax.ShapeDtypeStruct((batch, nb, W_SWA, SWA_BLOCK), _F32),
        scratch_shapes=[pltpu.VMEM((SWA_Q_PER_STEP, N_KV_SWA, 2 * SWA_BLOCK, SWA_GROUP * SWA_BLOCK), _F32)],
        compiler_params=_params(2),
        name="swa",
    )(sinks, qs_t, ks, ks, vs_t, vs_t, bias_swa)


def _group_norm_rows(o_ref, g_ref):
    rows = []
    for t in range(o_ref.shape[0]):
        o_t = o_ref[t]
        scale = lax.rsqrt(jnp.mean(o_t * o_t, axis=0, keepdims=True) + RMS_EPS)
        rows.append(((o_t * scale).T * g_ref[...]).astype(_BF16))
    return jnp.concatenate(rows, axis=0)


def _mix_ffn_body(x_ref, oa_ref, ob_ref, ga_ref, gb_ref, w_out_ref, mix_post_g_ref,
                  pre_g_ref, wg_hbm, wu_hbm, wd_hbm, post_g_ref, o_ref, h_ref, *weight_scratch):
    wg_ref, wu_ref, wd_ref = _ffn_weights((wg_hbm, wu_hbm, wd_hbm), weight_scratch)
    y = (_dot(_group_norm_rows(oa_ref, ga_ref), w_out_ref[:W_MOBA, :])
         + _dot(_group_norm_rows(ob_ref, gb_ref), w_out_ref[W_MOBA:, :]))
    x = x_ref[...] + _rms_rows(y, mix_post_g_ref[...])
    _ffn_tile(lambda rows: x[rows, :], pre_g_ref, wg_ref, wu_ref, wd_ref, post_g_ref, o_ref, h_ref)


def _mix_ffn(x2d, oa_t, ob_t, ga, gb, w_out, mix_post_g, ffn_params):
    n_tok = x2d.shape[0]
    return pl.pallas_call(
        _mix_ffn_body,
        grid=(n_tok // TOKEN_TILE,),
        in_specs=[_TOKEN_TILE_SPEC,
                  pl.BlockSpec((_MOBA_PER_TILE, W_MOBA, MOBA_BLOCK), lambda i: (i, 0, 0)),
                  pl.BlockSpec((_SWA_PER_TILE, W_SWA, SWA_BLOCK), lambda i: (i, 0, 0)),
                  _const_spec((1, W_MOBA)), _const_spec((1, W_SWA)),
                  _const_spec((W_MOBA + W_SWA, D_MODEL)),
                  _const_spec((1, D_MODEL))] + _FFN_WEIGHT_SPECS,
        out_specs=_TOKEN_TILE_SPEC,
        out_shape=jax.ShapeDtypeStruct((n_tok, D_MODEL), _F32),
        scratch_shapes=_FFN_SCRATCH,
        compiler_params=_params(1),
        name="mix_ffn",
    )(x2d, oa_t, ob_t, ga, gb, w_out, mix_post_g, *ffn_params)


def _row(v):
    return v.reshape(1, -1)


def _layer(x, ffn1, mix, ffn2, rel_bias, bias_tiles):
    batch, seq, _ = x.shape
    (mix_pre_g, w_in, moba_out_g, swa_sinks, swa_out_g, w_out, mix_post_g) = mix
    bias_own, bias_prev, bias_swa = bias_tiles

    def ffn_params(params):
        pre_g, w_gate, w_up, w_down, post_g = params
        return (_row(pre_g), w_gate, w_up, w_down, _row(post_g))

    x = _ffn(x.reshape(batch * seq, D_MODEL), ffn_params(ffn1))

    qa, ka, va, qs, ks, vs = jnp.split(
        w_in, [W_MOBA, 2 * W_MOBA, 3 * W_MOBA, 3 * W_MOBA + W_SWA, 3 * W_MOBA + W_SWA + W_SWA_KV], axis=1)
    w_rows = jnp.concatenate([ka, ks], axis=1).astype(_BF16)
    w_t = jnp.concatenate([qa, va, qs, vs], axis=1).T.astype(_BF16)
    qa_t, va_t, qs_t, vs_t, ka_b, ks_b, kmean = _in_proj(
        x.reshape(batch, seq, D_MODEL), _row(mix_pre_g), w_rows, w_t)

    oa_t = _moba(rel_bias, qa_t, ka_b, va_t, kmean.reshape(batch, seq // MOBA_BLOCK, W_MOBA),
                 bias_own, bias_prev)
    ob_t = _swa(swa_sinks, qs_t, ks_b, vs_t, bias_swa)

    x = _mix_ffn(x, oa_t.reshape(-1, W_MOBA, MOBA_BLOCK), ob_t.reshape(-1, W_SWA, SWA_BLOCK),
                 _row(moba_out_g), _row(swa_out_g), w_out.astype(_BF16), _row(mix_post_g),
                 ffn_params(ffn2))
    return x.reshape(batch, seq, D_MODEL)


def kernel(x, ffn1_pre_g, ffn1_w_gate, ffn1_w_up, ffn1_w_down, ffn1_post_g, mix_pre_g, w_in, rel_bias,
           moba_out_g, swa_sinks, swa_out_g, w_out, mix_post_g, ffn2_pre_g, ffn2_w_gate, ffn2_w_up,
           ffn2_w_down, ffn2_post_g):
    bias_tiles = _bias_tiles(rel_bias)
    for l in range(ffn1_pre_g.shape[0]):
        x = _layer(
            x,
            (ffn1_pre_g[l], ffn1_w_gate[l], ffn1_w_up[l], ffn1_w_down[l], ffn1_post_g[l]),
            (mix_pre_g[l], w_in[l], moba_out_g[l], swa_sinks[l], swa_out_g[l], w_out[l], mix_post_g[l]),
            (ffn2_pre_g[l], ffn2_w_gate[l], ffn2_w_up[l], ffn2_w_down[l], ffn2_post_g[l]),
            rel_bias, bias_tiles)
    return x
```

```python
import math

import jax
import jax.numpy as jnp
from jax import lax
from jax.experimental import pallas as pl
from jax.experimental.pallas import tpu as pltpu

D_MODEL = 1024
HEAD_DIM = 64
N_HEADS_MOBA = 8
N_HEADS_SWA = 8
N_KV_SWA = 2
SWA_GROUP = N_HEADS_SWA // N_KV_SWA
W_MOBA = N_HEADS_MOBA * HEAD_DIM
W_SWA = N_HEADS_SWA * HEAD_DIM
W_SWA_KV = N_KV_SWA * HEAD_DIM
MOBA_BLOCK = 256
MOBA_TOPK = 3
SWA_BLOCK = 128
SWA_WINDOW = 128
NUM_BUCKETS = 32
MAX_DISTANCE = 128
D_FF = 2816
RMS_EPS = 1e-6
FFN_RES_WEIGHT = 0.5
NEG = -1e30
QK_SCALE = HEAD_DIM ** -0.5
LOG2E = math.log2(math.e)
Q_FOLD = QK_SCALE * LOG2E
SUM_ROWS = 16

TOKEN_TILE = 1024
FF_CHUNK = 256
FFN_ROW_PARTS = 2
CAST_CHUNKS = 16
CAST_SLOTS = 4
HEADS_PER_STEP = 4
HEADS_PER_KEY_TILE = 2
PIPE_DEPTH = 2
GROUPS_PER_ITER = 4
SWA_Q_PER_STEP = 4
VMEM_LIMIT_BYTES = 58 * 1024 * 1024

_BF16 = jnp.bfloat16
_F32 = jnp.float32


def _dot(a, b):
    return jnp.dot(a, b, preferred_element_type=_F32)


def _dot_nt(a, b):
    return lax.dot_general(a, b, (((1,), (1,)), ((), ())), preferred_element_type=_F32)


def _rms_rows(x, g):
    return x * lax.rsqrt(jnp.mean(x * x, axis=-1, keepdims=True) + RMS_EPS) * g


def _const_spec(shape):
    return pl.BlockSpec(shape, lambda *_: (0,) * len(shape), pipeline_mode=pl.Buffered(1))


def _params(n_axes):
    return pltpu.CompilerParams(
        dimension_semantics=("arbitrary",) * n_axes, vmem_limit_bytes=VMEM_LIMIT_BYTES)


def _ffn_tile(x_rows, pre_g_ref, wg_ref, wu_ref, wd_ref, post_g_ref, o_ref, h_ref):
    part = TOKEN_TILE // FFN_ROW_PARTS
    for p in range(FFN_ROW_PARTS):
        rows = slice(p * part, (p + 1) * part)
        x = x_rows(rows)
        xn = _rms_rows(x, pre_g_ref[...]).astype(_BF16)
        for c in range(D_FF // FF_CHUNK):
            cols = slice(c * FF_CHUNK, (c + 1) * FF_CHUNK)
            gate = _dot(xn, wg_ref[:, cols])
            up = _dot(xn, wu_ref[:, cols])
            h_ref[rows, cols] = (jax.nn.silu(gate) * up).astype(_BF16)
        y = _dot(h_ref[rows, :], wd_ref[...])
        o_ref[rows, :] = x + FFN_RES_WEIGHT * _rms_rows(y, post_g_ref[...])


def _cast_weights(pairs, stage, sem):
    slots, rows = stage.shape[0], stage.shape[1]
    chunks = [(w_hbm, w_bf, k) for w_hbm, w_bf in pairs for k in range(w_hbm.shape[0] // rows)]

    def copy(c):
        w_hbm, _, k = chunks[c]
        return pltpu.make_async_copy(w_hbm.at[pl.ds(k * rows, rows), :], stage.at[c % slots], sem.at[c % slots])

    def prime():
        for c in range(min(slots, len(chunks))):
            copy(c).start()

    def drain():
        for c, (_, w_bf, k) in enumerate(chunks):
            copy(c).wait()
            w_bf[k * rows:(k + 1) * rows, :] = stage[c % slots].astype(_BF16)
            if c + slots < len(chunks):
                copy(c + slots).start()

    return prime, drain


def _ffn_weights(w_hbm, scratch):
    wg_bf, wu_bf, wd_bf, stage_in, stage_down, sem_in, sem_down = scratch

    @pl.when(pl.program_id(0) == 0)
    def _():
        prime_in, drain_in = _cast_weights([(w_hbm[0], wg_bf), (w_hbm[1], wu_bf)], stage_in, sem_in)
        prime_down, drain_down = _cast_weights([(w_hbm[2], wd_bf)], stage_down, sem_down)
        prime_in()
        prime_down()
        drain_in()
        drain_down()

    return wg_bf, wu_bf, wd_bf


def _ffn_body(x_ref, pre_g_ref, wg_hbm, wu_hbm, wd_hbm, post_g_ref, o_ref, h_ref, *weight_scratch):
    wg_ref, wu_ref, wd_ref = _ffn_weights((wg_hbm, wu_hbm, wd_hbm), weight_scratch)
    _ffn_tile(lambda rows: x_ref[rows, :], pre_g_ref, wg_ref, wu_ref, wd_ref, post_g_ref, o_ref, h_ref)


_TOKEN_TILE_SPEC = pl.BlockSpec((TOKEN_TILE, D_MODEL), lambda i: (i, 0))
_HBM_SPEC = pl.BlockSpec(memory_space=pl.ANY)
_FFN_WEIGHT_SPECS = [_const_spec((1, D_MODEL)), _HBM_SPEC, _HBM_SPEC, _HBM_SPEC, _const_spec((1, D_MODEL))]
_FFN_SCRATCH = [
    pltpu.VMEM((TOKEN_TILE, D_FF), _BF16),
    pltpu.VMEM((D_MODEL, D_FF), _BF16),
    pltpu.VMEM((D_MODEL, D_FF), _BF16),
    pltpu.VMEM((D_FF, D_MODEL), _BF16),
    pltpu.VMEM((CAST_SLOTS, D_MODEL // CAST_CHUNKS, D_FF), _F32),
    pltpu.VMEM((CAST_SLOTS, D_FF // CAST_CHUNKS, D_MODEL), _F32),
    pltpu.SemaphoreType.DMA((CAST_SLOTS,)),
    pltpu.SemaphoreType.DMA((CAST_SLOTS,)),
]


def _ffn(x2d, ffn_params):
    n_tok = x2d.shape[0]
    return pl.pallas_call(
        _ffn_body,
        grid=(n_tok // TOKEN_TILE,),
        in_specs=[_TOKEN_TILE_SPEC] + _FFN_WEIGHT_SPECS,
        out_specs=_TOKEN_TILE_SPEC,
        out_shape=jax.ShapeDtypeStruct((n_tok, D_MODEL), _F32),
        scratch_shapes=_FFN_SCRATCH,
        compiler_params=_params(1),
        name="ffn",
    )(x2d, *ffn_params)


_T_QA, _T_VA, _T_QS, _T_VS = 0, W_MOBA, 2 * W_MOBA, 2 * W_MOBA + W_SWA
_T_ROWS = 2 * W_MOBA + W_SWA + W_SWA_KV
_MOBA_PER_TILE = TOKEN_TILE // MOBA_BLOCK
_SWA_PER_TILE = TOKEN_TILE // SWA_BLOCK


def _in_proj_body(x_ref, g_ref, w_rows_ref, w_t_ref,
                  qa_ref, va_ref, qs_ref, vs_ref, ka_ref, ks_ref, kmean_ref):
    swa_per_moba = MOBA_BLOCK // SWA_BLOCK
    for t in range(_MOBA_PER_TILE):
        xn = _rms_rows(x_ref[0, t * MOBA_BLOCK:(t + 1) * MOBA_BLOCK, :], g_ref[...]).astype(_BF16)
        k_rows = _dot(xn, w_rows_ref[...])
        proj_t = _dot_nt(w_t_ref[...], xn)
        k_blk = k_rows[:, :W_MOBA]
        ka_ref[0, t] = k_blk.astype(_BF16)
        kmean_ref[0, t] = jnp.mean(k_blk, axis=0, keepdims=True)
        qa_ref[0, t] = (proj_t[_T_QA:_T_QA + W_MOBA, :] * Q_FOLD).astype(_BF16)
        va_ref[0, t] = proj_t[_T_VA:_T_VA + W_MOBA, :].astype(_BF16)
        for u in range(swa_per_moba):
            rows = slice(u * SWA_BLOCK, (u + 1) * SWA_BLOCK)
            ks_ref[0, t * swa_per_moba + u] = k_rows[rows, W_MOBA:].astype(_BF16)
            qs_ref[0, t * swa_per_moba + u] = (proj_t[_T_QS:_T_QS + W_SWA, rows] * Q_FOLD).astype(_BF16)
            vs_ref[0, t * swa_per_moba + u] = proj_t[_T_VS:_T_VS + W_SWA_KV, rows].astype(_BF16)


def _in_proj(x, g, w_rows, w_t):
    batch, seq, _ = x.shape
    nb_moba, nb_swa = seq // MOBA_BLOCK, seq // SWA_BLOCK

    def blocked(n_per_tile, rows, cols):
        return pl.BlockSpec((1, n_per_tile, rows, cols), lambda b, i: (b, i, 0, 0))

    return pl.pallas_call(
        _in_proj_body,
        grid=(batch, seq // TOKEN_TILE),
        in_specs=[pl.BlockSpec((1, TOKEN_TILE, D_MODEL), lambda b, i: (b, i, 0)),
                  _const_spec((1, D_MODEL)),
                  _const_spec((D_MODEL, W_MOBA + W_SWA_KV)),
                  _const_spec((_T_ROWS, D_MODEL))],
        out_specs=[blocked(_MOBA_PER_TILE, W_MOBA, MOBA_BLOCK),
                   blocked(_MOBA_PER_TILE, W_MOBA, MOBA_BLOCK),
                   blocked(_SWA_PER_TILE, W_SWA, SWA_BLOCK),
                   blocked(_SWA_PER_TILE, W_SWA_KV, SWA_BLOCK),
                   blocked(_MOBA_PER_TILE, MOBA_BLOCK, W_MOBA),
                   blocked(_SWA_PER_TILE, SWA_BLOCK, W_SWA_KV),
                   blocked(_MOBA_PER_TILE, 1, W_MOBA)],
        out_shape=[jax.ShapeDtypeStruct((batch, nb_moba, W_MOBA, MOBA_BLOCK), _BF16),
                   jax.ShapeDtypeStruct((batch, nb_moba, W_MOBA, MOBA_BLOCK), _BF16),
                   jax.ShapeDtypeStruct((batch, nb_swa, W_SWA, SWA_BLOCK), _BF16),
                   jax.ShapeDtypeStruct((batch, nb_swa, W_SWA_KV, SWA_BLOCK), _BF16),
                   jax.ShapeDtypeStruct((batch, nb_moba, MOBA_BLOCK, W_MOBA), _BF16),
                   jax.ShapeDtypeStruct((batch, nb_swa, SWA_BLOCK, W_SWA_KV), _BF16),
                   jax.ShapeDtypeStruct((batch, nb_moba, 1, W_MOBA), _F32)],
        compiler_params=_params(2),
        name="in_proj",
    )(x, g, w_rows, w_t)


def _t5_bucket(dist):
    n = jnp.maximum(dist, 0)
    max_exact = NUM_BUCKETS // 2
    nf = jnp.maximum(n, 1).astype(_F32)
    large = max_exact + jnp.floor(jnp.log(nf / max_exact) / math.log(MAX_DISTANCE / max_exact)
                                  * (NUM_BUCKETS - max_exact))
    large = jnp.minimum(large, float(NUM_BUCKETS - 1))
    return jnp.where(n < max_exact, n.astype(_F32), large)


def _bias_lookup(rel_bias_ref, bucket, head):
    val = jnp.full(bucket.shape, rel_bias_ref[NUM_BUCKETS - 1, head], _F32)
    for b in range(NUM_BUCKETS - 2, -1, -1):
        val = jnp.where(bucket == float(b), rel_bias_ref[b, head], val)
    return val


def _bias_tiles_body(rel_bias_ref, own_ref, prev_ref, swa_ref):
    assert MOBA_BLOCK == 2 * SWA_BLOCK and SWA_WINDOW == SWA_BLOCK and MAX_DISTANCE <= SWA_BLOCK
    h = pl.program_id(0)
    q = SWA_BLOCK
    key = lax.broadcasted_iota(jnp.int32, (q, q), 0)
    qry = lax.broadcasted_iota(jnp.int32, (q, q), 1)
    dist = qry - key

    def quadrants(head):
        near = jnp.where(dist >= 0, _bias_lookup(rel_bias_ref, _t5_bucket(dist), head) * LOG2E, NEG)
        nxt = _bias_lookup(rel_bias_ref, _t5_bucket(dist + q), head) * LOG2E
        return near, nxt

    near, nxt = quadrants(h)
    far = jnp.full((q, q), rel_bias_ref[NUM_BUCKETS - 1, h] * LOG2E, _F32)
    masked = jnp.full((q, q), NEG, _F32)
    own_ref[0, :q, :q] = near
    own_ref[0, :q, q:] = nxt
    own_ref[0, q:, :q] = masked
    own_ref[0, q:, q:] = near
    prev_ref[0, :q, :q] = far
    prev_ref[0, :q, q:] = far
    prev_ref[0, q:, :q] = nxt
    prev_ref[0, q:, q:] = far
    near, nxt = quadrants(h + N_HEADS_MOBA)
    swa_ref[0, :q, :] = jnp.where(dist < 0, nxt, NEG)
    swa_ref[0, q:, :] = near


def _bias_tiles(rel_bias):
    def per_head(rows, cols):
        return pl.BlockSpec((1, rows, cols), lambda h: (h, 0, 0))

    return pl.pallas_call(
        _bias_tiles_body,
        grid=(N_HEADS_MOBA,),
        in_specs=[pl.BlockSpec(memory_space=pltpu.SMEM)],
        out_specs=[per_head(MOBA_BLOCK, MOBA_BLOCK), per_head(MOBA_BLOCK, MOBA_BLOCK),
                   per_head(2 * SWA_BLOCK, SWA_BLOCK)],
        out_shape=[jax.ShapeDtypeStruct((N_HEADS_MOBA, MOBA_BLOCK, MOBA_BLOCK), _F32),
                   jax.ShapeDtypeStruct((N_HEADS_MOBA, MOBA_BLOCK, MOBA_BLOCK), _F32),
                   jax.ShapeDtypeStruct((N_HEADS_SWA, 2 * SWA_BLOCK, SWA_BLOCK), _F32)],
        compiler_params=_params(1),
        name="bias_tiles",
    )(rel_bias)


def _moba_body(rel_bias_ref, q_ref, k_ref, v_ref, kmean_ref, own_ref, prev_ref, o_ref,
               far_ref, qpad_ref, s_ref):
    g = pl.program_id(1)
    nb = k_ref.shape[1]
    key_lanes = HEADS_PER_KEY_TILE * HEAD_DIM
    q_row = lax.broadcasted_iota(jnp.int32, (key_lanes, MOBA_BLOCK), 0)
    km = kmean_ref[0]
    km_hi = km.astype(_BF16)
    km_lo = (km - km_hi.astype(_F32)).astype(_BF16)
    blk = lax.broadcasted_iota(jnp.int32, (nb, MOBA_BLOCK), 0)
    ones = jnp.ones((SUM_ROWS, MOBA_BLOCK), _BF16)
    ring_slots = range(GROUPS_PER_ITER)
    tail_slots = ring_slots[-PIPE_DEPTH:]
    near_slots = range(GROUPS_PER_ITER, GROUPS_PER_ITER + PIPE_DEPTH)
    assert PIPE_DEPTH == 2 and GROUPS_PER_ITER >= 2 * PIPE_DEPTH

    def key_tile(r):
        t, sub = divmod(r, HEADS_PER_KEY_TILE)
        return slice(t * key_lanes, (t + 1) * key_lanes), sub

    def prepare(i):
        past = blk < i
        own, prev, prev_sel = [], [], []
        for r in range(HEADS_PER_STEP):
            tile_lanes, sub = key_tile(r)
            q_t = q_ref[0, i, tile_lanes, :]
            q_pad = jnp.where((q_row >= sub * HEAD_DIM) & (q_row < (sub + 1) * HEAD_DIM), q_t, 0)
            qpad_ref[r] = q_pad

            gate = jnp.where(
                past, _dot(km_hi[:, tile_lanes], q_pad) + _dot(km_lo[:, tile_lanes], q_pad), NEG)
            sel = jnp.zeros(gate.shape, jnp.bool_)
            for _ in range(MOBA_TOPK):
                top = jnp.max(gate, axis=0, keepdims=True)
                first = jnp.min(jnp.where(gate == top, blk, nb), axis=0, keepdims=True)
                pick = blk == first
                sel = sel | pick
                gate = jnp.where(pick, -jnp.inf, gate)
            sel = sel & past

            far_bias = rel_bias_ref[NUM_BUCKETS - 1, g * HEADS_PER_STEP + r] * LOG2E
            far_ref[r] = jnp.where(sel & (blk < i - 1), far_bias, NEG)
            prev_sel.append(jnp.max(jnp.where(sel & (blk == i - 1), 0.0, NEG), axis=0, keepdims=True))
            own.append(own_ref[r])
            prev.append(prev_ref[r])
        return own, prev, prev_sel

    def score(slot, j, tile_bias=None, row_bias=None):
        tops, shifts = [], []
        for r in range(HEADS_PER_STEP):
            s = _dot(k_ref[0, j, :, key_tile(r)[0]], qpad_ref[r])
            if tile_bias is not None:
                s = s + tile_bias[r]
            s_ref[slot, r] = s
            top = jnp.max(s, axis=0, keepdims=True)
            shift = jnp.zeros_like(top) if row_bias is None else row_bias[r]
            tops.append(top + shift)
            shifts.append(shift)
        return slot, j, tuple(tops), tuple(shifts)

    def absorb(carry, scored):
        slot, j, tops, shifts = scored
        out = []
        for r in range(HEADS_PER_STEP):
            lanes = slice(r * HEAD_DIM, (r + 1) * HEAD_DIM)
            m, acc = carry[r]
            m_new = jnp.maximum(m, tops[r])
            p = jnp.exp2(s_ref[slot, r] + (shifts[r] - m_new)).astype(_BF16)
            pv = _dot(jnp.concatenate([v_ref[0, j, lanes, :], ones], axis=0), p)
            out.append((m_new, jnp.exp2(m - m_new) * acc + pv))
        return tuple(out)

    def score_far(slot, grp):
        j = jnp.minimum(grp, nb - 1)
        return score(slot, j, row_bias=[far_ref[r, pl.ds(j, 1), :] for r in range(HEADS_PER_STEP)])

    def score_near(i):
        own, prev, prev_sel = prepare(i)
        return (score(near_slots[0], i, tile_bias=own)[2:],
                score(near_slots[1], jnp.maximum(i - 1, 0), tile_bias=prev, row_bias=prev_sel)[2:])

    def start(i, near):
        carry = tuple((jnp.full((1, MOBA_BLOCK), -jnp.inf, _F32),
                       jnp.zeros((HEAD_DIM + SUM_ROWS, MOBA_BLOCK), _F32)) for _ in range(HEADS_PER_STEP))
        near_js = (i, jnp.maximum(i - 1, 0))
        ring = {slot: (slot, j) + maxes_shifts for slot, j, maxes_shifts in zip(near_slots, near_js, near)}
        for c, slot in enumerate(tail_slots):
            ring[slot] = score_far(slot, c)
            carry = absorb(carry, ring[near_slots[c]])
        return carry, tuple(ring[slot][1:] for slot in tail_slots)

    def tile(i, state, last=False):
        def step(it, state):
            carry, pending = state
            ring = {slot: (slot,) + p for slot, p in zip(tail_slots, pending)}
            for k in ring_slots:
                scored = score_far(k, PIPE_DEPTH + it * GROUPS_PER_ITER + k)
                carry = absorb(carry, ring[(k - PIPE_DEPTH) % GROUPS_PER_ITER])
                ring[k] = scored
            return carry, tuple(ring[slot][1:] for slot in tail_slots)

        n_far = jnp.maximum(i - 1, 0)
        n_iters = (jnp.maximum(n_far - PIPE_DEPTH, 0) + GROUPS_PER_ITER - 1) // GROUPS_PER_ITER
        carry, pending = lax.fori_loop(0, n_iters, step, state)

        near_next = None if last else score_near(i + 1)
        for slot, p in zip(tail_slots, pending):
            carry = absorb(carry, (slot,) + p)
        for r in range(HEADS_PER_STEP):
            _, acc = carry[r]
            o_ref[0, i, r * HEAD_DIM:(r + 1) * HEAD_DIM, :] = acc[:HEAD_DIM] / acc[HEAD_DIM:HEAD_DIM + 1]
        return None if last else start(i + 1, near_next)

    state = lax.fori_loop(0, nb - 1, tile, start(0, score_near(0)))
    tile(nb - 1, state, last=True)


def _moba(rel_bias, qa_t, ka, va_t, kmean, bias_own, bias_prev):
    batch, nb = qa_t.shape[0], qa_t.shape[1]
    width = HEADS_PER_STEP * HEAD_DIM
    q_spec = pl.BlockSpec((1, nb, width, MOBA_BLOCK), lambda b, g: (b, 0, g, 0))
    bias_spec = pl.BlockSpec((HEADS_PER_STEP, MOBA_BLOCK, MOBA_BLOCK), lambda b, g: (g, 0, 0))
    return pl.pallas_call(
        _moba_body,
        grid=(batch, N_HEADS_MOBA // HEADS_PER_STEP),
        in_specs=[pl.BlockSpec(memory_space=pltpu.SMEM),
                  q_spec,
                  pl.BlockSpec((1, nb, MOBA_BLOCK, width), lambda b, g: (b, 0, 0, g)),
                  q_spec,
                  pl.BlockSpec((1, nb, width), lambda b, g: (b, 0, g)),
                  bias_spec, bias_spec],
        out_specs=q_spec,
        out_shape=jax.ShapeDtypeStruct((batch, nb, W_MOBA, MOBA_BLOCK), _F32),
        scratch_shapes=[pltpu.VMEM((HEADS_PER_STEP, nb, MOBA_BLOCK), _F32),
                        pltpu.VMEM((HEADS_PER_STEP, HEADS_PER_KEY_TILE * HEAD_DIM, MOBA_BLOCK), _BF16),
                        pltpu.VMEM((GROUPS_PER_ITER + PIPE_DEPTH, HEADS_PER_STEP, MOBA_BLOCK, MOBA_BLOCK), _F32)],
        compiler_params=_params(2),
        name="moba",
    )(rel_bias, qa_t, ka, va_t, kmean, bias_own, bias_prev)


def _swa_body(sinks_ref, q_ref, k_prev_ref, k_cur_ref, v_prev_ref, v_cur_ref, bias_ref, o_ref, s_ref):
    no_prev = jnp.where(pl.program_id(1) == 0, NEG, 0.0)
    window_row = lax.broadcasted_iota(jnp.int32, (2 * SWA_BLOCK, SWA_BLOCK), 0)
    first_window_mask = jnp.where(window_row < SWA_BLOCK, no_prev, 0.0)
    zeros = jnp.zeros((HEAD_DIM, SWA_BLOCK), _BF16)
    ones = jnp.ones((SUM_ROWS, 2 * SWA_BLOCK), _BF16)
    k_blocks = [k_prev_ref[0, 0]] + [k_cur_ref[0, t] for t in range(SWA_Q_PER_STEP)]
    v_blocks = [v_prev_ref[0, 0]] + [v_cur_ref[0, t] for t in range(SWA_Q_PER_STEP)]
    chains = [(t, kv) for t in range(SWA_Q_PER_STEP) for kv in range(N_KV_SWA)]

    for t, kv in chains:
        q_pad = jnp.concatenate(
            [jnp.concatenate(
                [q_ref[0, t, h * HEAD_DIM:(h + 1) * HEAD_DIM, :] if part == kv else zeros
                 for part in range(N_KV_SWA)], axis=0)
             for h in range(kv * SWA_GROUP, (kv + 1) * SWA_GROUP)], axis=1)
        window_keys = jnp.concatenate([k_blocks[t], k_blocks[t + 1]], axis=0)
        s = _dot(window_keys, q_pad)
        for slot in range(SWA_GROUP):
            cols = slice(slot * SWA_BLOCK, (slot + 1) * SWA_BLOCK)
            biased = s[:, cols] + bias_ref[kv * SWA_GROUP + slot]
            s_ref[t, kv, :, cols] = biased + first_window_mask if t == 0 else biased

    for t, kv in chains:
        kv_rows = slice(kv * HEAD_DIM, (kv + 1) * HEAD_DIM)
        window_values = jnp.concatenate(
            [jnp.concatenate([v_blocks[t][kv_rows, :], v_blocks[t + 1][kv_rows, :]], axis=1), ones], axis=0)
        p, sink_p = [], []
        for slot in range(SWA_GROUP):
            cols = slice(slot * SWA_BLOCK, (slot + 1) * SWA_BLOCK)
            sink = sinks_ref[kv * SWA_GROUP + slot] * LOG2E
            m = jnp.maximum(jnp.max(s_ref[t, kv, :, cols], axis=0, keepdims=True), sink)
            p.append(jnp.exp2(s_ref[t, kv, :, cols] - m).astype(_BF16))
            sink_p.append(jnp.exp2(sink - m))
        acc = _dot(window_values, jnp.concatenate(p, axis=1))
        for slot in range(SWA_GROUP):
            h = kv * SWA_GROUP + slot
            cols = slice(slot * SWA_BLOCK, (slot + 1) * SWA_BLOCK)
            l = acc[HEAD_DIM:HEAD_DIM + 1, cols] + sink_p[slot]
            o_ref[0, t, h * HEAD_DIM:(h + 1) * HEAD_DIM, :] = acc[:HEAD_DIM, cols] / l


def _swa(sinks, qs_t, ks, vs_t, bias_swa):
    batch, nb = qs_t.shape[0], qs_t.shape[1]

    def cur(rows, cols):
        return pl.BlockSpec((1, SWA_Q_PER_STEP, rows, cols), lambda b, n: (b, n, 0, 0))

    def prev(rows, cols):
        return pl.BlockSpec((1, 1, rows, cols),
                            lambda b, n: (b, jnp.maximum(n * SWA_Q_PER_STEP - 1, 0), 0, 0))

    return pl.pallas_call(
        _swa_body,
        grid=(batch, nb // SWA_Q_PER_STEP),
        in_specs=[pl.BlockSpec(memory_space=pltpu.SMEM),
                  cur(W_SWA, SWA_BLOCK),
                  prev(SWA_BLOCK, W_SWA_KV), cur(SWA_BLOCK, W_SWA_KV),
                  prev(W_SWA_KV, SWA_BLOCK), cur(W_SWA_KV, SWA_BLOCK),
                  _const_spec((N_HEADS_SWA, 2 * SWA_BLOCK, SWA_BLOCK))],
        out_specs=cur(W_SWA, SWA_BLOCK),
        out_shape=jax.ShapeDtypeStruct((batch, nb, W_SWA, SWA_BLOCK), _F32),
        scratch_shapes=[pltpu.VMEM((SWA_Q_PER_STEP, N_KV_SWA, 2 * SWA_BLOCK, SWA_GROUP * SWA_BLOCK), _F32)],
        compiler_params=_params(2),
        name="swa",
    )(sinks, qs_t, ks, ks, vs_t, vs_t, bias_swa)


def _group_norm_rows(o_ref, g_ref):
    rows = []
    for t in range(o_ref.shape[0]):
        o_t = o_ref[t]
        scale = lax.rsqrt(jnp.mean(o_t * o_t, axis=0, keepdims=True) + RMS_EPS)
        rows.append(((o_t * scale).T * g_ref[...]).astype(_BF16))
    return jnp.concatenate(rows, axis=0)


def _mix_ffn_body(x_ref, oa_ref, ob_ref, ga_ref, gb_ref, w_out_ref, mix_post_g_ref,
                  pre_g_ref, wg_hbm, wu_hbm, wd_hbm, post_g_ref, o_ref, h_ref, *weight_scratch):
    wg_ref, wu_ref, wd_ref = _ffn_weights((wg_hbm, wu_hbm, wd_hbm), weight_scratch)
    y = (_dot(_group_norm_rows(oa_ref, ga_ref), w_out_ref[:W_MOBA, :])
         + _dot(_group_norm_rows(ob_ref, gb_ref), w_out_ref[W_MOBA:, :]))
    x = x_ref[...] + _rms_rows(y, mix_post_g_ref[...])
    _ffn_tile(lambda rows: x[rows, :], pre_g_ref, wg_ref, wu_ref, wd_ref, post_g_ref, o_ref, h_ref)


def _mix_ffn(x2d, oa_t, ob_t, ga, gb, w_out, mix_post_g, ffn_params):
    n_tok = x2d.shape[0]
    return pl.pallas_call(
        _mix_ffn_body,
        grid=(n_tok // TOKEN_TILE,),
        in_specs=[_TOKEN_TILE_SPEC,
                  pl.BlockSpec((_MOBA_PER_TILE, W_MOBA, MOBA_BLOCK), lambda i: (i, 0, 0)),
                  pl.BlockSpec((_SWA_PER_TILE, W_SWA, SWA_BLOCK), lambda i: (i, 0, 0)),
                  _const_spec((1, W_MOBA)), _const_spec((1, W_SWA)),
                  _const_spec((W_MOBA + W_SWA, D_MODEL)),
                  _const_spec((1, D_MODEL))] + _FFN_WEIGHT_SPECS,
        out_specs=_TOKEN_TILE_SPEC,
        out_shape=jax.ShapeDtypeStruct((n_tok, D_MODEL), _F32),
        scratch_shapes=_FFN_SCRATCH,
        compiler_params=_params(1),
        name="mix_ffn",
    )(x2d, oa_t, ob_t, ga, gb, w_out, mix_post_g, *ffn_params)


def _row(v):
    return v.reshape(1, -1)


def _layer(x, ffn1, mix, ffn2, rel_bias, bias_tiles):
    batch, seq, _ = x.shape
    (mix_pre_g, w_in, moba_out_g, swa_sinks, swa_out_g, w_out, mix_post_g) = mix
    bias_own, bias_prev, bias_swa = bias_tiles

    def ffn_params(params):
        pre_g, w_gate, w_up, w_down, post_g = params
        return (_row(pre_g), w_gate, w_up, w_down, _row(post_g))

    x = _ffn(x.reshape(batch * seq, D_MODEL), ffn_params(ffn1))

    qa, ka, va, qs, ks, vs = jnp.split(
        w_in, [W_MOBA, 2 * W_MOBA, 3 * W_MOBA, 3 * W_MOBA + W_SWA, 3 * W_MOBA + W_SWA + W_SWA_KV], axis=1)
    w_rows = jnp.concatenate([ka, ks], axis=1).astype(_BF16)
    w_t = jnp.concatenate([qa, va, qs, vs], axis=1).astype(_BF16).T
    qa_t, va_t, qs_t, vs_t, ka_b, ks_b, kmean = _in_proj(
        x.reshape(batch, seq, D_MODEL), _row(mix_pre_g), w_rows, w_t)

    oa_t = _moba(rel_bias, qa_t, ka_b, va_t, kmean.reshape(batch, seq // MOBA_BLOCK, W_MOBA),
                 bias_own, bias_prev)
    ob_t = _swa(swa_sinks, qs_t, ks_b, vs_t, bias_swa)

    x = _mix_ffn(x, oa_t.reshape(-1, W_MOBA, MOBA_BLOCK), ob_t.reshape(-1, W_SWA, SWA_BLOCK),
                 _row(moba_out_g), _row(swa_out_g), w_out.astype(_BF16), _row(mix_post_g),
                 ffn_params(ffn2))
    return x.reshape(batch, seq, D_MODEL)


def kernel(x, ffn1_pre_g, ffn1_w_gate, ffn1_w_up, ffn1_w_down, ffn1_post_g, mix_pre_g, w_in, rel_bias,
           moba_out_g, swa_sinks, swa_out_g, w_out, mix_post_g, ffn2_pre_g, ffn2_w_gate, ffn2_w_up,
           ffn2_w_down, ffn2_post_g):
    bias_tiles = _bias_tiles(rel_bias)
    for l in range(ffn1_pre_g.shape[0]):
        x = _layer(
            x,
            (ffn1_pre_g[l], ffn1_w_gate[l], ffn1_w_up[l], ffn1_w_down[l], ffn1_post_g[l]),
            (mix_pre_g[l], w_in[l], moba_out_g[l], swa_sinks[l], swa_out_g[l], w_out[l], mix_post_g[l]),
            (ffn2_pre_g[l], ffn2_w_gate[l], ffn2_w_up[l], ffn2_w_down[l], ffn2_post_g[l]),
            rel_bias, bias_tiles)
    return x
```

```python
import math

import jax
import jax.numpy as jnp
from jax import lax
from jax.experimental import pallas as pl
from jax.experimental.pallas import tpu as pltpu

D_MODEL = 1024
HEAD_DIM = 64
N_HEADS_MOBA = 8
N_HEADS_SWA = 8
N_KV_SWA = 2
SWA_GROUP = N_HEADS_SWA // N_KV_SWA
W_MOBA = N_HEADS_MOBA * HEAD_DIM
W_SWA = N_HEADS_SWA * HEAD_DIM
W_SWA_KV = N_KV_SWA * HEAD_DIM
MOBA_BLOCK = 256
MOBA_TOPK = 3
SWA_BLOCK = 128
SWA_WINDOW = 128
NUM_BUCKETS = 32
MAX_DISTANCE = 128
D_FF = 2816
RMS_EPS = 1e-6
FFN_RES_WEIGHT = 0.5
NEG = -1e30
QK_SCALE = HEAD_DIM ** -0.5
LOG2E = math.log2(math.e)
Q_FOLD = QK_SCALE * LOG2E
SUM_ROWS = 16

TOKEN_TILE = 1024
FF_CHUNK = 256
FFN_ROW_PARTS = 2
CAST_CHUNKS = 16
CAST_SLOTS = 4
HEADS_PER_STEP = 4
HEADS_PER_KEY_TILE = 2
PIPE_DEPTH = 2
GROUPS_PER_ITER = 4
SWA_Q_PER_STEP = 4
VMEM_LIMIT_BYTES = 58 * 1024 * 1024

_BF16 = jnp.bfloat16
_F32 = jnp.float32


def _dot(a, b):
    return jnp.dot(a, b, preferred_element_type=_F32)


def _dot_nt(a, b):
    return lax.dot_general(a, b, (((1,), (1,)), ((), ())), preferred_element_type=_F32)


def _rms_rows(x, g):
    return x * lax.rsqrt(jnp.mean(x * x, axis=-1, keepdims=True) + RMS_EPS) * g


def _const_spec(shape):
    return pl.BlockSpec(shape, lambda *_: (0,) * len(shape), pipeline_mode=pl.Buffered(1))


def _params(n_axes):
    return pltpu.CompilerParams(
        dimension_semantics=("arbitrary",) * n_axes, vmem_limit_bytes=VMEM_LIMIT_BYTES)


def _ffn_tile(x_rows, pre_g_ref, wg_ref, wu_ref, wd_ref, post_g_ref, o_ref, h_ref):
    part = TOKEN_TILE // FFN_ROW_PARTS
    for p in range(FFN_ROW_PARTS):
        rows = slice(p * part, (p + 1) * part)
        x = x_rows(rows)
        xn = _rms_rows(x, pre_g_ref[...]).astype(_BF16)
        for c in range(D_FF // FF_CHUNK):
            cols = slice(c * FF_CHUNK, (c + 1) * FF_CHUNK)
            gate = _dot(xn, wg_ref[:, cols])
            up = _dot(xn, wu_ref[:, cols])
            h_ref[rows, cols] = (jax.nn.silu(gate) * up).astype(_BF16)
        y = _dot(h_ref[rows, :], wd_ref[...])
        o_ref[rows, :] = x + FFN_RES_WEIGHT * _rms_rows(y, post_g_ref[...])


def _cast_weights(pairs, stage, sem):
    slots, rows = stage.shape[0], stage.shape[1]
    chunks = [(w_hbm, w_bf, k) for w_hbm, w_bf in pairs for k in range(w_hbm.shape[0] // rows)]

    def copy(c):
        w_hbm, _, k = chunks[c]
        return pltpu.make_async_copy(w_hbm.at[pl.ds(k * rows, rows), :], stage.at[c % slots], sem.at[c % slots])

    def prime():
        for c in range(min(slots, len(chunks))):
            copy(c).start()

    def drain():
        for c, (_, w_bf, k) in enumerate(chunks):
            copy(c).wait()
            w_bf[k * rows:(k + 1) * rows, :] = stage[c % slots].astype(_BF16)
            if c + slots < len(chunks):
                copy(c + slots).start()

    return prime, drain


def _ffn_weights(w_hbm, scratch):
    wg_bf, wu_bf, wd_bf, stage_in, stage_down, sem_in, sem_down = scratch

    @pl.when(pl.program_id(0) == 0)
    def _():
        prime_in, drain_in = _cast_weights([(w_hbm[0], wg_bf), (w_hbm[1], wu_bf)], stage_in, sem_in)
        prime_down, drain_down = _cast_weights([(w_hbm[2], wd_bf)], stage_down, sem_down)
        prime_in()
        prime_down()
        drain_in()
        drain_down()

    return wg_bf, wu_bf, wd_bf


def _ffn_body(x_ref, pre_g_ref, wg_hbm, wu_hbm, wd_hbm, post_g_ref, o_ref, h_ref, *weight_scratch):
    wg_ref, wu_ref, wd_ref = _ffn_weights((wg_hbm, wu_hbm, wd_hbm), weight_scratch)
    _ffn_tile(lambda rows: x_ref[rows, :], pre_g_ref, wg_ref, wu_ref, wd_ref, post_g_ref, o_ref, h_ref)


_TOKEN_TILE_SPEC = pl.BlockSpec((TOKEN_TILE, D_MODEL), lambda i: (i, 0))
_HBM_SPEC = pl.BlockSpec(memory_space=pl.ANY)
_FFN_WEIGHT_SPECS = [_const_spec((1, D_MODEL)), _HBM_SPEC, _HBM_SPEC, _HBM_SPEC, _const_spec((1, D_MODEL))]
_FFN_SCRATCH = [
    pltpu.VMEM((TOKEN_TILE, D_FF), _BF16),
    pltpu.VMEM((D_MODEL, D_FF), _BF16),
    pltpu.VMEM((D_MODEL, D_FF), _BF16),
    pltpu.VMEM((D_FF, D_MODEL), _BF16),
    pltpu.VMEM((CAST_SLOTS, D_MODEL // CAST_CHUNKS, D_FF), _F32),
    pltpu.VMEM((CAST_SLOTS, D_FF // CAST_CHUNKS, D_MODEL), _F32),
    pltpu.SemaphoreType.DMA((CAST_SLOTS,)),
    pltpu.SemaphoreType.DMA((CAST_SLOTS,)),
]


def _ffn(x2d, ffn_params):
    n_tok = x2d.shape[0]
    return pl.pallas_call(
        _ffn_body,
        grid=(n_tok // TOKEN_TILE,),
        in_specs=[_TOKEN_TILE_SPEC] + _FFN_WEIGHT_SPECS,
        out_specs=_TOKEN_TILE_SPEC,
        out_shape=jax.ShapeDtypeStruct((n_tok, D_MODEL), _F32),
        scratch_shapes=_FFN_SCRATCH,
        compiler_params=_params(1),
        name="ffn",
    )(x2d, *ffn_params)


_T_QA, _T_VA, _T_QS, _T_VS = 0, W_MOBA, 2 * W_MOBA, 2 * W_MOBA + W_SWA
_T_ROWS = 2 * W_MOBA + W_SWA + W_SWA_KV
_MOBA_PER_TILE = TOKEN_TILE // MOBA_BLOCK
_SWA_PER_TILE = TOKEN_TILE // SWA_BLOCK


def _in_proj_body(x_ref, g_ref, w_rows_ref, w_t_ref,
                  qa_ref, va_ref, qs_ref, vs_ref, ka_ref, ks_ref, kmean_ref):
    swa_per_moba = MOBA_BLOCK // SWA_BLOCK
    for t in range(_MOBA_PER_TILE):
        xn = _rms_rows(x_ref[0, t * MOBA_BLOCK:(t + 1) * MOBA_BLOCK, :], g_ref[...]).astype(_BF16)
        k_rows = _dot(xn, w_rows_ref[...])
        proj_t = _dot_nt(w_t_ref[...], xn)
        k_blk = k_rows[:, :W_MOBA]
        ka_ref[0, t] = k_blk.astype(_BF16)
        kmean_ref[0, t] = jnp.mean(k_blk, axis=0, keepdims=True)
        qa_ref[0, t] = (proj_t[_T_QA:_T_QA + W_MOBA, :] * Q_FOLD).astype(_BF16)
        va_ref[0, t] = proj_t[_T_VA:_T_VA + W_MOBA, :].astype(_BF16)
        for u in range(swa_per_moba):
            rows = slice(u * SWA_BLOCK, (u + 1) * SWA_BLOCK)
            ks_ref[0, t * swa_per_moba + u] = k_rows[rows, W_MOBA:].astype(_BF16)
            qs_ref[0, t * swa_per_moba + u] = (proj_t[_T_QS:_T_QS + W_SWA, rows] * Q_FOLD).astype(_BF16)
            vs_ref[0, t * swa_per_moba + u] = proj_t[_T_VS:_T_VS + W_SWA_KV, rows].astype(_BF16)


def _in_proj(x, g, w_rows, w_t):
    batch, seq, _ = x.shape
    nb_moba, nb_swa = seq // MOBA_BLOCK, seq // SWA_BLOCK

    def blocked(n_per_tile, rows, cols):
        return pl.BlockSpec((1, n_per_tile, rows, cols), lambda b, i: (b, i, 0, 0))

    return pl.pallas_call(
        _in_proj_body,
        grid=(batch, seq // TOKEN_TILE),
        in_specs=[pl.BlockSpec((1, TOKEN_TILE, D_MODEL), lambda b, i: (b, i, 0)),
                  _const_spec((1, D_MODEL)),
                  _const_spec((D_MODEL, W_MOBA + W_SWA_KV)),
                  _const_spec((_T_ROWS, D_MODEL))],
        out_specs=[blocked(_MOBA_PER_TILE, W_MOBA, MOBA_BLOCK),
                   blocked(_MOBA_PER_TILE, W_MOBA, MOBA_BLOCK),
                   blocked(_SWA_PER_TILE, W_SWA, SWA_BLOCK),
                   blocked(_SWA_PER_TILE, W_SWA_KV, SWA_BLOCK),
                   blocked(_MOBA_PER_TILE, MOBA_BLOCK, W_MOBA),
                   blocked(_SWA_PER_TILE, SWA_BLOCK, W_SWA_KV),
                   blocked(_MOBA_PER_TILE, 1, W_MOBA)],
        out_shape=[jax.ShapeDtypeStruct((batch, nb_moba, W_MOBA, MOBA_BLOCK), _BF16),
                   jax.ShapeDtypeStruct((batch, nb_moba, W_MOBA, MOBA_BLOCK), _BF16),
                   jax.ShapeDtypeStruct((batch, nb_swa, W_SWA, SWA_BLOCK), _BF16),
                   jax.ShapeDtypeStruct((batch, nb_swa, W_SWA_KV, SWA_BLOCK), _BF16),
                   jax.ShapeDtypeStruct((batch, nb_moba, MOBA_BLOCK, W_MOBA), _BF16),
                   jax.ShapeDtypeStruct((batch, nb_swa, SWA_BLOCK, W_SWA_KV), _BF16),
                   jax.ShapeDtypeStruct((batch, nb_moba, 1, W_MOBA), _F32)],
        compiler_params=_params(2),
        name="in_proj",
    )(x, g, w_rows, w_t)


def _t5_bucket(dist):
    n = jnp.maximum(dist, 0)
    max_exact = NUM_BUCKETS // 2
    nf = jnp.maximum(n, 1).astype(_F32)
    large = max_exact + jnp.floor(jnp.log(nf / max_exact) / math.log(MAX_DISTANCE / max_exact)
                                  * (NUM_BUCKETS - max_exact))
    large = jnp.minimum(large, float(NUM_BUCKETS - 1))
    return jnp.where(n < max_exact, n.astype(_F32), large)


def _bias_lookup(rel_bias_ref, bucket, head):
    val = jnp.full(bucket.shape, rel_bias_ref[NUM_BUCKETS - 1, head], _F32)
    for b in range(NUM_BUCKETS - 2, -1, -1):
        val = jnp.where(bucket == float(b), rel_bias_ref[b, head], val)
    return val


def _bias_tiles_body(rel_bias_ref, own_ref, prev_ref, swa_ref):
    assert MOBA_BLOCK == 2 * SWA_BLOCK and SWA_WINDOW == SWA_BLOCK and MAX_DISTANCE <= SWA_BLOCK
    h = pl.program_id(0)
    q = SWA_BLOCK
    key = lax.broadcasted_iota(jnp.int32, (q, q), 0)
    qry = lax.broadcasted_iota(jnp.int32, (q, q), 1)
    dist = qry - key

    def quadrants(head):
        near = jnp.where(dist >= 0, _bias_lookup(rel_bias_ref, _t5_bucket(dist), head) * LOG2E, NEG)
        nxt = _bias_lookup(rel_bias_ref, _t5_bucket(dist + q), head) * LOG2E
        return near, nxt

    near, nxt = quadrants(h)
    far = jnp.full((q, q), rel_bias_ref[NUM_BUCKETS - 1, h] * LOG2E, _F32)
    masked = jnp.full((q, q), NEG, _F32)
    own_ref[0, :q, :q] = near
    own_ref[0, :q, q:] = nxt
    own_ref[0, q:, :q] = masked
    own_ref[0, q:, q:] = near
    prev_ref[0, :q, :q] = far
    prev_ref[0, :q, q:] = far
    prev_ref[0, q:, :q] = nxt
    prev_ref[0, q:, q:] = far
    near, nxt = quadrants(h + N_HEADS_MOBA)
    swa_ref[0, :q, :] = jnp.where(dist < 0, nxt, NEG)
    swa_ref[0, q:, :] = near


def _bias_tiles(rel_bias):
    def per_head(rows, cols):
        return pl.BlockSpec((1, rows, cols), lambda h: (h, 0, 0))

    return pl.pallas_call(
        _bias_tiles_body,
        grid=(N_HEADS_MOBA,),
        in_specs=[pl.BlockSpec(memory_space=pltpu.SMEM)],
        out_specs=[per_head(MOBA_BLOCK, MOBA_BLOCK), per_head(MOBA_BLOCK, MOBA_BLOCK),
                   per_head(2 * SWA_BLOCK, SWA_BLOCK)],
        out_shape=[jax.ShapeDtypeStruct((N_HEADS_MOBA, MOBA_BLOCK, MOBA_BLOCK), _F32),
                   jax.ShapeDtypeStruct((N_HEADS_MOBA, MOBA_BLOCK, MOBA_BLOCK), _F32),
                   jax.ShapeDtypeStruct((N_HEADS_SWA, 2 * SWA_BLOCK, SWA_BLOCK), _F32)],
        compiler_params=_params(1),
        name="bias_tiles",
    )(rel_bias)


def _moba_body(rel_bias_ref, q_ref, k_ref, v_ref, kmean_ref, own_ref, prev_ref, o_ref,
               far_ref, qpad_ref, s_ref, acc_ref):
    g = pl.program_id(1)
    nb = k_ref.shape[1]
    key_lanes = HEADS_PER_KEY_TILE * HEAD_DIM
    q_row = lax.broadcasted_iota(jnp.int32, (key_lanes, MOBA_BLOCK), 0)
    km = kmean_ref[0]
    km_hi = km.astype(_BF16)
    km_lo = (km - km_hi.astype(_F32)).astype(_BF16)
    blk = lax.broadcasted_iota(jnp.int32, (nb, MOBA_BLOCK), 0)
    ones = jnp.ones((SUM_ROWS, MOBA_BLOCK), _BF16)
    ring_slots = range(GROUPS_PER_ITER)
    tail_slots = ring_slots[-PIPE_DEPTH:]
    near_slots = range(GROUPS_PER_ITER, GROUPS_PER_ITER + PIPE_DEPTH)
    assert PIPE_DEPTH == 2 and GROUPS_PER_ITER >= 2 * PIPE_DEPTH

    def key_tile(r):
        t, sub = divmod(r, HEADS_PER_KEY_TILE)
        return slice(t * key_lanes, (t + 1) * key_lanes), sub

    def prepare(i):
        past = blk < i
        own, prev, prev_sel = [], [], []
        for r in range(HEADS_PER_STEP):
            tile_lanes, sub = key_tile(r)
            q_t = q_ref[0, i, tile_lanes, :]
            q_pad = jnp.where((q_row >= sub * HEAD_DIM) & (q_row < (sub + 1) * HEAD_DIM), q_t, 0)
            qpad_ref[r] = q_pad

            gate = jnp.where(
                past, _dot(km_hi[:, tile_lanes], q_pad) + _dot(km_lo[:, tile_lanes], q_pad), NEG)
            sel = jnp.zeros(gate.shape, jnp.bool_)
            for _ in range(MOBA_TOPK):
                top = jnp.max(gate, axis=0, keepdims=True)
                first = jnp.min(jnp.where(gate == top, blk, nb), axis=0, keepdims=True)
                pick = blk == first
                sel = sel | pick
                gate = jnp.where(pick, -jnp.inf, gate)
            sel = sel & past

            far_bias = rel_bias_ref[NUM_BUCKETS - 1, g * HEADS_PER_STEP + r] * LOG2E
            far_ref[r] = jnp.where(sel & (blk < i - 1), far_bias, NEG)
            prev_sel.append(jnp.max(jnp.where(sel & (blk == i - 1), 0.0, NEG), axis=0, keepdims=True))
            own.append(own_ref[r])
            prev.append(prev_ref[r])
        return own, prev, prev_sel

    def score(slot, j, tile_bias=None, row_bias=None):
        tops, shifts = [], []
        for r in range(HEADS_PER_STEP):
            s = _dot(k_ref[0, j, :, key_tile(r)[0]], qpad_ref[r])
            if tile_bias is not None:
                s = s + tile_bias[r]
            s_ref[slot, r] = s
            top = jnp.max(s, axis=0, keepdims=True)
            shift = jnp.zeros_like(top) if row_bias is None else row_bias[r]
            tops.append(top + shift)
            shifts.append(shift)
        return slot, j, tuple(tops), tuple(shifts)

    def absorb(carry, scored):
        slot, j, tops, shifts = scored
        out = []
        for r in range(HEADS_PER_STEP):
            lanes = slice(r * HEAD_DIM, (r + 1) * HEAD_DIM)
            m = carry[r]
            m_new = jnp.maximum(m, tops[r])
            p = jnp.exp2(s_ref[slot, r] + (shifts[r] - m_new)).astype(_BF16)
            pv = _dot(jnp.concatenate([v_ref[0, j, lanes, :], ones], axis=0), p)
            acc_ref[r] = jnp.exp2(m - m_new) * acc_ref[r] + pv
            out.append(m_new)
        return tuple(out)

    def score_far(slot, grp):
        j = jnp.minimum(grp, nb - 1)
        return score(slot, j, row_bias=[far_ref[r, pl.ds(j, 1), :] for r in range(HEADS_PER_STEP)])

    def score_near(i):
        own, prev, prev_sel = prepare(i)
        return (score(near_slots[0], i, tile_bias=own)[2:],
                score(near_slots[1], jnp.maximum(i - 1, 0), tile_bias=prev, row_bias=prev_sel)[2:])

    def start(i, near):
        carry = tuple(jnp.full((1, MOBA_BLOCK), -jnp.inf, _F32) for _ in range(HEADS_PER_STEP))
        acc_ref[...] = jnp.zeros(acc_ref.shape, _F32)
        near_js = (i, jnp.maximum(i - 1, 0))
        ring = {slot: (slot, j) + maxes_shifts for slot, j, maxes_shifts in zip(near_slots, near_js, near)}
        for c, slot in enumerate(tail_slots):
            ring[slot] = score_far(slot, c)
            carry = absorb(carry, ring[near_slots[c]])
        return carry, tuple(ring[slot][1:] for slot in tail_slots)

    def tile(i, state, last=False):
        def step(it, state):
            carry, pending = state
            ring = {slot: (slot,) + p for slot, p in zip(tail_slots, pending)}
            for k in ring_slots:
                scored = score_far(k, PIPE_DEPTH + it * GROUPS_PER_ITER + k)
                carry = absorb(carry, ring[(k - PIPE_DEPTH) % GROUPS_PER_ITER])
                ring[k] = scored
            return carry, tuple(ring[slot][1:] for slot in tail_slots)

        n_far = jnp.maximum(i - 1, 0)
        n_iters = (jnp.maximum(n_far - PIPE_DEPTH, 0) + GROUPS_PER_ITER - 1) // GROUPS_PER_ITER
        carry, pending = lax.fori_loop(0, n_iters, step, state)

        near_next = None if last else score_near(i + 1)
        for slot, p in zip(tail_slots, pending):
            carry = absorb(carry, (slot,) + p)
        for r in range(HEADS_PER_STEP):
            acc = acc_ref[r]
            o_ref[0, i, r * HEAD_DIM:(r + 1) * HEAD_DIM, :] = acc[:HEAD_DIM] / acc[HEAD_DIM:HEAD_DIM + 1]
        return None if last else start(i + 1, near_next)

    state = lax.fori_loop(0, nb - 1, tile, start(0, score_near(0)))
    tile(nb - 1, state, last=True)


def _moba(rel_bias, qa_t, ka, va_t, kmean, bias_own, bias_prev):
    batch, nb = qa_t.shape[0], qa_t.shape[1]
    width = HEADS_PER_STEP * HEAD_DIM
    q_spec = pl.BlockSpec((1, nb, width, MOBA_BLOCK), lambda b, g: (b, 0, g, 0))
    bias_spec = pl.BlockSpec((HEADS_PER_STEP, MOBA_BLOCK, MOBA_BLOCK), lambda b, g: (g, 0, 0))
    return pl.pallas_call(
        _moba_body,
        grid=(batch, N_HEADS_MOBA // HEADS_PER_STEP),
        in_specs=[pl.BlockSpec(memory_space=pltpu.SMEM),
                  q_spec,
                  pl.BlockSpec((1, nb, MOBA_BLOCK, width), lambda b, g: (b, 0, 0, g)),
                  q_spec,
                  pl.BlockSpec((1, nb, width), lambda b, g: (b, 0, g)),
                  bias_spec, bias_spec],
        out_specs=q_spec,
        out_shape=jax.ShapeDtypeStruct((batch, nb, W_MOBA, MOBA_BLOCK), _F32),
        scratch_shapes=[pltpu.VMEM((HEADS_PER_STEP, nb, MOBA_BLOCK), _F32),
                        pltpu.VMEM((HEADS_PER_STEP, HEADS_PER_KEY_TILE * HEAD_DIM, MOBA_BLOCK), _BF16),
                        pltpu.VMEM((GROUPS_PER_ITER + PIPE_DEPTH, HEADS_PER_STEP, MOBA_BLOCK, MOBA_BLOCK), _F32),
                        pltpu.VMEM((HEADS_PER_STEP, HEAD_DIM + SUM_ROWS, MOBA_BLOCK), _F32)],
        compiler_params=_params(2),
        name="moba",
    )(rel_bias, qa_t, ka, va_t, kmean, bias_own, bias_prev)


def _swa_body(sinks_ref, q_ref, k_prev_ref, k_cur_ref, v_prev_ref, v_cur_ref, bias_ref, o_ref, s_ref):
    no_prev = jnp.where(pl.program_id(1) == 0, NEG, 0.0)
    window_row = lax.broadcasted_iota(jnp.int32, (2 * SWA_BLOCK, SWA_BLOCK), 0)
    first_window_mask = jnp.where(window_row < SWA_BLOCK, no_prev, 0.0)
    zeros = jnp.zeros((HEAD_DIM, SWA_BLOCK), _BF16)
    ones = jnp.ones((SUM_ROWS, 2 * SWA_BLOCK), _BF16)
    k_blocks = [k_prev_ref[0, 0]] + [k_cur_ref[0, t] for t in range(SWA_Q_PER_STEP)]
    v_blocks = [v_prev_ref[0, 0]] + [v_cur_ref[0, t] for t in range(SWA_Q_PER_STEP)]
    chains = [(t, kv) for t in range(SWA_Q_PER_STEP) for kv in range(N_KV_SWA)]

    for t, kv in chains:
        q_pad = jnp.concatenate(
            [jnp.concatenate(
                [q_ref[0, t, h * HEAD_DIM:(h + 1) * HEAD_DIM, :] if part == kv else zeros
                 for part in range(N_KV_SWA)], axis=0)
             for h in range(kv * SWA_GROUP, (kv + 1) * SWA_GROUP)], axis=1)
        window_keys = jnp.concatenate([k_blocks[t], k_blocks[t + 1]], axis=0)
        s = _dot(window_keys, q_pad)
        for slot in range(SWA_GROUP):
            cols = slice(slot * SWA_BLOCK, (slot + 1) * SWA_BLOCK)
            biased = s[:, cols] + bias_ref[kv * SWA_GROUP + slot]
            s_ref[t, kv, :, cols] = biased + first_window_mask if t == 0 else biased

    for t, kv in chains:
        kv_rows = slice(kv * HEAD_DIM, (kv + 1) * HEAD_DIM)
        window_values = jnp.concatenate(
            [jnp.concatenate([v_blocks[t][kv_rows, :], v_blocks[t + 1][kv_rows, :]], axis=1), ones], axis=0)
        p, sink_p = [], []
        for slot in range(SWA_GROUP):
            cols = slice(slot * SWA_BLOCK, (slot + 1) * SWA_BLOCK)
            sink = sinks_ref[kv * SWA_GROUP + slot] * LOG2E
            m = jnp.maximum(jnp.max(s_ref[t, kv, :, cols], axis=0, keepdims=True), sink)
            p.append(jnp.exp2(s_ref[t, kv, :, cols] - m).astype(_BF16))
            sink_p.append(jnp.exp2(sink - m))
        acc = _dot(window_values, jnp.concatenate(p, axis=1))
        for slot in range(SWA_GROUP):
            h = kv * SWA_GROUP + slot
            cols = slice(slot * SWA_BLOCK, (slot + 1) * SWA_BLOCK)
            l = acc[HEAD_DIM:HEAD_DIM + 1, cols] + sink_p[slot]
            o_ref[0, t, h * HEAD_DIM:(h + 1) * HEAD_DIM, :] = acc[:HEAD_DIM, cols] / l


def _swa(sinks, qs_t, ks, vs_t, bias_swa):
    batch, nb = qs_t.shape[0], qs_t.shape[1]

    def cur(rows, cols):
        return pl.BlockSpec((1, SWA_Q_PER_STEP, rows, cols), lambda b, n: (b, n, 0, 0))

    def prev(rows, cols):
        return pl.BlockSpec((1, 1, rows, cols),
                            lambda b, n: (b, jnp.maximum(n * SWA_Q_PER_STEP - 1, 0), 0, 0))

    return pl.pallas_call(
        _swa_body,
        grid=(batch, nb // SWA_Q_PER_STEP),
        in_specs=[pl.BlockSpec(memory_space=pltpu.SMEM),
                  cur(W_SWA, SWA_BLOCK),
                  prev(SWA_BLOCK, W_SWA_KV), cur(SWA_BLOCK, W_SWA_KV),
                  prev(W_SWA_KV, SWA_BLOCK), cur(W_SWA_KV, SWA_BLOCK),
                  _const_spec((N_HEADS_SWA, 2 * SWA_BLOCK, SWA_BLOCK))],
        out_specs=cur(W_SWA, SWA_BLOCK),
        out_shape=jax.ShapeDtypeStruct((batch, nb, W_SWA, SWA_BLOCK), _F32),
        scratch_shapes=[pltpu.VMEM((SWA_Q_PER_STEP, N_KV_SWA, 2 * SWA_BLOCK, SWA_GROUP * SWA_BLOCK), _F32)],
        compiler_params=_params(2),
        name="swa",
    )(sinks, qs_t, ks, ks, vs_t, vs_t, bias_swa)


def _group_norm_rows(o_ref, g_ref):
    rows = []
    for t in range(o_ref.shape[0]):
        o_t = o_ref[t]
        scale = lax.rsqrt(jnp.mean(o_t * o_t, axis=0, keepdims=True) + RMS_EPS)
        rows.append(((o_t * scale).T * g_ref[...]).astype(_BF16))
    return jnp.concatenate(rows, axis=0)


def _mix_ffn_body(x_ref, oa_ref, ob_ref, ga_ref, gb_ref, w_out_ref, mix_post_g_ref,
                  pre_g_ref, wg_hbm, wu_hbm, wd_hbm, post_g_ref, o_ref, h_ref, *weight_scratch):
    wg_ref, wu_ref, wd_ref = _ffn_weights((wg_hbm, wu_hbm, wd_hbm), weight_scratch)
    y = (_dot(_group_norm_rows(oa_ref, ga_ref), w_out_ref[:W_MOBA, :])
         + _dot(_group_norm_rows(ob_ref, gb_ref), w_out_ref[W_MOBA:, :]))
    x = x_ref[...] + _rms_rows(y, mix_post_g_ref[...])
    _ffn_tile(lambda rows: x[rows, :], pre_g_ref, wg_ref, wu_ref, wd_ref, post_g_ref, o_ref, h_ref)


def _mix_ffn(x2d, oa_t, ob_t, ga, gb, w_out, mix_post_g, ffn_params):
    n_tok = x2d.shape[0]
    return pl.pallas_call(
        _mix_ffn_body,
        grid=(n_tok // TOKEN_TILE,),
        in_specs=[_TOKEN_TILE_SPEC,
                  pl.BlockSpec((_MOBA_PER_TILE, W_MOBA, MOBA_BLOCK), lambda i: (i, 0, 0)),
                  pl.BlockSpec((_SWA_PER_TILE, W_SWA, SWA_BLOCK), lambda i: (i, 0, 0)),
                  _const_spec((1, W_MOBA)), _const_spec((1, W_SWA)),
                  _const_spec((W_MOBA + W_SWA, D_MODEL)),
                  _const_spec((1, D_MODEL))] + _FFN_WEIGHT_SPECS,
        out_specs=_TOKEN_TILE_SPEC,
        out_shape=jax.ShapeDtypeStruct((n_tok, D_MODEL), _F32),
        scratch_shapes=_FFN_SCRATCH,
        compiler_params=_params(1),
        name="mix_ffn",
    )(x2d, oa_t, ob_t, ga, gb, w_out, mix_post_g, *ffn_params)


def _row(v):
    return v.reshape(1, -1)


def _layer(x, ffn1, mix, ffn2, rel_bias, bias_tiles):
    batch, seq, _ = x.shape
    (mix_pre_g, w_in, moba_out_g, swa_sinks, swa_out_g, w_out, mix_post_g) = mix
    bias_own, bias_prev, bias_swa = bias_tiles

    def ffn_params(params):
        pre_g, w_gate, w_up, w_down, post_g = params
        return (_row(pre_g), w_gate, w_up, w_down, _row(post_g))

    x = _ffn(x.reshape(batch * seq, D_MODEL), ffn_params(ffn1))

    qa, ka, va, qs, ks, vs = jnp.split(
        w_in, [W_MOBA, 2 * W_MOBA, 3 * W_MOBA, 3 * W_MOBA + W_SWA, 3 * W_MOBA + W_SWA + W_SWA_KV], axis=1)
    w_rows = jnp.concatenate([ka, ks], axis=1).astype(_BF16)
    w_t = jnp.concatenate([qa, va, qs, vs], axis=1).astype(_BF16).T
    qa_t, va_t, qs_t, vs_t, ka_b, ks_b, kmean = _in_proj(
        x.reshape(batch, seq, D_MODEL), _row(mix_pre_g), w_rows, w_t)

    oa_t = _moba(rel_bias, qa_t, ka_b, va_t, kmean.reshape(batch, seq // MOBA_BLOCK, W_MOBA),
                 bias_own, bias_prev)
    ob_t = _swa(swa_sinks, qs_t, ks_b, vs_t, bias_swa)

    x = _mix_ffn(x, oa_t.reshape(-1, W_MOBA, MOBA_BLOCK), ob_t.reshape(-1, W_SWA, SWA_BLOCK),
                 _row(moba_out_g), _row(swa_out_g), w_out.astype(_BF16), _row(mix_post_g),
                 ffn_params(ffn2))
    return x.reshape(batch, seq, D_MODEL)


def kernel(x, ffn1_pre_g, ffn1_w_gate, ffn1_w_up, ffn1_w_down, ffn1_post_g, mix_pre_g, w_in, rel_bias,
           moba_out_g, swa_sinks, swa_out_g, w_out, mix_post_g, ffn2_pre_g, ffn2_w_gate, ffn2_w_up,
           ffn2_w_down, ffn2_post_g):
    bias_tiles = _bias_tiles(rel_bias)
    for l in range(ffn1_pre_g.shape[0]):
        x = _layer(
            x,
            (ffn1_pre_g[l], ffn1_w_gate[l], ffn1_w_up[l], ffn1_w_down[l], ffn1_post_g[l]),
            (mix_pre_g[l], w_in[l], moba_out_g[l], swa_sinks[l], swa_out_g[l], w_out[l], mix_post_g[l]),
            (ffn2_pre_g[l], ffn2_w_gate[l], ffn2_w_up[l], ffn2_w_down[l], ffn2_post_g[l]),
            rel_bias, bias_tiles)
    return x
```

```python
import math

import jax
import jax.numpy as jnp
from jax import lax
from jax.experimental import pallas as pl
from jax.experimental.pallas import tpu as pltpu

D_MODEL = 1024
HEAD_DIM = 64
N_HEADS_MOBA = 8
N_HEADS_SWA = 8
N_KV_SWA = 2
SWA_GROUP = N_HEADS_SWA // N_KV_SWA
W_MOBA = N_HEADS_MOBA * HEAD_DIM
W_SWA = N_HEADS_SWA * HEAD_DIM
W_SWA_KV = N_KV_SWA * HEAD_DIM
MOBA_BLOCK = 256
MOBA_TOPK = 3
SWA_BLOCK = 128
SWA_WINDOW = 128
NUM_BUCKETS = 32
MAX_DISTANCE = 128
D_FF = 2816
RMS_EPS = 1e-6
FFN_RES_WEIGHT = 0.5
NEG = -1e30
QK_SCALE = HEAD_DIM ** -0.5
LOG2E = math.log2(math.e)
Q_FOLD = QK_SCALE * LOG2E
SUM_ROWS = 16

TOKEN_TILE = 1024
FF_CHUNK = 256
FFN_ROW_PARTS = 2
CAST_CHUNKS = 16
CAST_SLOTS = 4
HEADS_PER_STEP = 4
HEADS_PER_KEY_TILE = 2
PIPE_DEPTH = 2
GROUPS_PER_ITER = 4
SWA_Q_PER_STEP = 4
VMEM_LIMIT_BYTES = 58 * 1024 * 1024

_BF16 = jnp.bfloat16
_F32 = jnp.float32


def _dot(a, b):
    return jnp.dot(a, b, preferred_element_type=_F32)


def _dot_nt(a, b):
    return lax.dot_general(a, b, (((1,), (1,)), ((), ())), preferred_element_type=_F32)


def _rms_rows(x, g):
    return x * lax.rsqrt(jnp.mean(x * x, axis=-1, keepdims=True) + RMS_EPS) * g


def _const_spec(shape):
    return pl.BlockSpec(shape, lambda *_: (0,) * len(shape), pipeline_mode=pl.Buffered(1))


def _params(n_axes):
    return pltpu.CompilerParams(
        dimension_semantics=("arbitrary",) * n_axes, vmem_limit_bytes=VMEM_LIMIT_BYTES)


def _ffn_tile(x_rows, pre_g_ref, wg_ref, wu_ref, wd_ref, post_g_ref, o_ref, h_ref):
    part = TOKEN_TILE // FFN_ROW_PARTS
    for p in range(FFN_ROW_PARTS):
        rows = slice(p * part, (p + 1) * part)
        x = x_rows(rows)
        xn = _rms_rows(x, pre_g_ref[...]).astype(_BF16)
        for c in range(D_FF // FF_CHUNK):
            cols = slice(c * FF_CHUNK, (c + 1) * FF_CHUNK)
            gate = _dot(xn, wg_ref[:, cols])
            up = _dot(xn, wu_ref[:, cols])
            h_ref[rows, cols] = (jax.nn.silu(gate) * up).astype(_BF16)
        y = _dot(h_ref[rows, :], wd_ref[...])
        o_ref[rows, :] = x + FFN_RES_WEIGHT * _rms_rows(y, post_g_ref[...])


def _cast_weights(pairs, stage, sem):
    slots, rows = stage.shape[0], stage.shape[1]
    chunks = [(w_hbm, w_bf, k) for w_hbm, w_bf in pairs for k in range(w_hbm.shape[0] // rows)]

    def copy(c):
        w_hbm, _, k = chunks[c]
        return pltpu.make_async_copy(w_hbm.at[pl.ds(k * rows, rows), :], stage.at[c % slots], sem.at[c % slots])

    def prime():
        for c in range(min(slots, len(chunks))):
            copy(c).start()

    def drain():
        for c, (_, w_bf, k) in enumerate(chunks):
            copy(c).wait()
            w_bf[k * rows:(k + 1) * rows, :] = stage[c % slots].astype(_BF16)
            if c + slots < len(chunks):
                copy(c + slots).start()

    return prime, drain


def _ffn_weights(w_hbm, scratch):
    wg_bf, wu_bf, wd_bf, stage_in, stage_down, sem_in, sem_down = scratch

    @pl.when(pl.program_id(0) == 0)
    def _():
        prime_in, drain_in = _cast_weights([(w_hbm[0], wg_bf), (w_hbm[1], wu_bf)], stage_in, sem_in)
        prime_down, drain_down = _cast_weights([(w_hbm[2], wd_bf)], stage_down, sem_down)
        prime_in()
        prime_down()
        drain_in()
        drain_down()

    return wg_bf, wu_bf, wd_bf


def _ffn_body(x_ref, pre_g_ref, wg_hbm, wu_hbm, wd_hbm, post_g_ref, o_ref, h_ref, *weight_scratch):
    wg_ref, wu_ref, wd_ref = _ffn_weights((wg_hbm, wu_hbm, wd_hbm), weight_scratch)
    _ffn_tile(lambda rows: x_ref[rows, :], pre_g_ref, wg_ref, wu_ref, wd_ref, post_g_ref, o_ref, h_ref)


_TOKEN_TILE_SPEC = pl.BlockSpec((TOKEN_TILE, D_MODEL), lambda i: (i, 0))
_HBM_SPEC = pl.BlockSpec(memory_space=pl.ANY)
_FFN_WEIGHT_SPECS = [_const_spec((1, D_MODEL)), _HBM_SPEC, _HBM_SPEC, _HBM_SPEC, _const_spec((1, D_MODEL))]
_FFN_SCRATCH = [
    pltpu.VMEM((TOKEN_TILE, D_FF), _BF16),
    pltpu.VMEM((D_MODEL, D_FF), _BF16),
    pltpu.VMEM((D_MODEL, D_FF), _BF16),
    pltpu.VMEM((D_FF, D_MODEL), _BF16),
    pltpu.VMEM((CAST_SLOTS, D_MODEL // CAST_CHUNKS, D_FF), _F32),
    pltpu.VMEM((CAST_SLOTS, D_FF // CAST_CHUNKS, D_MODEL), _F32),
    pltpu.SemaphoreType.DMA((CAST_SLOTS,)),
    pltpu.SemaphoreType.DMA((CAST_SLOTS,)),
]


def _ffn(x2d, ffn_params):
    n_tok = x2d.shape[0]
    return pl.pallas_call(
        _ffn_body,
        grid=(n_tok // TOKEN_TILE,),
        in_specs=[_TOKEN_TILE_SPEC] + _FFN_WEIGHT_SPECS,
        out_specs=_TOKEN_TILE_SPEC,
        out_shape=jax.ShapeDtypeStruct((n_tok, D_MODEL), _F32),
        scratch_shapes=_FFN_SCRATCH,
        compiler_params=_params(1),
        name="ffn",
    )(x2d, *ffn_params)


_T_QA, _T_VA, _T_QS, _T_VS = 0, W_MOBA, 2 * W_MOBA, 2 * W_MOBA + W_SWA
_T_ROWS = 2 * W_MOBA + W_SWA + W_SWA_KV
_MOBA_PER_TILE = TOKEN_TILE // MOBA_BLOCK
_SWA_PER_TILE = TOKEN_TILE // SWA_BLOCK


def _in_proj_body(x_ref, g_ref, w_rows_ref, w_t_ref,
                  qa_ref, va_ref, qs_ref, vs_ref, ka_ref, ks_ref, kmean_ref):
    swa_per_moba = MOBA_BLOCK // SWA_BLOCK
    for t in range(_MOBA_PER_TILE):
        xn = _rms_rows(x_ref[0, t * MOBA_BLOCK:(t + 1) * MOBA_BLOCK, :], g_ref[...]).astype(_BF16)
        k_rows = _dot(xn, w_rows_ref[...])
        proj_t = _dot_nt(w_t_ref[...], xn)
        k_blk = k_rows[:, :W_MOBA]
        ka_ref[0, t] = k_blk.astype(_BF16)
        kmean_ref[0, t] = jnp.mean(k_blk, axis=0, keepdims=True)
        qa_ref[0, t] = (proj_t[_T_QA:_T_QA + W_MOBA, :] * Q_FOLD).astype(_BF16)
        va_ref[0, t] = proj_t[_T_VA:_T_VA + W_MOBA, :].astype(_BF16)
        for u in range(swa_per_moba):
            rows = slice(u * SWA_BLOCK, (u + 1) * SWA_BLOCK)
            ks_ref[0, t * swa_per_moba + u] = k_rows[rows, W_MOBA:].astype(_BF16)
            qs_ref[0, t * swa_per_moba + u] = (proj_t[_T_QS:_T_QS + W_SWA, rows] * Q_FOLD).astype(_BF16)
            vs_ref[0, t * swa_per_moba + u] = proj_t[_T_VS:_T_VS + W_SWA_KV, rows].astype(_BF16)


def _in_proj(x, g, w_rows, w_t):
    batch, seq, _ = x.shape
    nb_moba, nb_swa = seq // MOBA_BLOCK, seq // SWA_BLOCK

    def blocked(n_per_tile, rows, cols):
        return pl.BlockSpec((1, n_per_tile, rows, cols), lambda b, i: (b, i, 0, 0))

    return pl.pallas_call(
        _in_proj_body,
        grid=(batch, seq // TOKEN_TILE),
        in_specs=[pl.BlockSpec((1, TOKEN_TILE, D_MODEL), lambda b, i: (b, i, 0)),
                  _const_spec((1, D_MODEL)),
                  _const_spec((D_MODEL, W_MOBA + W_SWA_KV)),
                  _const_spec((_T_ROWS, D_MODEL))],
        out_specs=[blocked(_MOBA_PER_TILE, W_MOBA, MOBA_BLOCK),
                   blocked(_MOBA_PER_TILE, W_MOBA, MOBA_BLOCK),
                   blocked(_SWA_PER_TILE, W_SWA, SWA_BLOCK),
                   blocked(_SWA_PER_TILE, W_SWA_KV, SWA_BLOCK),
                   blocked(_MOBA_PER_TILE, MOBA_BLOCK, W_MOBA),
                   blocked(_SWA_PER_TILE, SWA_BLOCK, W_SWA_KV),
                   blocked(_MOBA_PER_TILE, 1, W_MOBA)],
        out_shape=[jax.ShapeDtypeStruct((batch, nb_moba, W_MOBA, MOBA_BLOCK), _BF16),
                   jax.ShapeDtypeStruct((batch, nb_moba, W_MOBA, MOBA_BLOCK), _BF16),
                   jax.ShapeDtypeStruct((batch, nb_swa, W_SWA, SWA_BLOCK), _BF16),
                   jax.ShapeDtypeStruct((batch, nb_swa, W_SWA_KV, SWA_BLOCK), _BF16),
                   jax.ShapeDtypeStruct((batch, nb_moba, MOBA_BLOCK, W_MOBA), _BF16),
                   jax.ShapeDtypeStruct((batch, nb_swa, SWA_BLOCK, W_SWA_KV), _BF16),
                   jax.ShapeDtypeStruct((batch, nb_moba, 1, W_MOBA), _F32)],
        compiler_params=_params(2),
        name="in_proj",
    )(x, g, w_rows, w_t)


def _t5_bucket(dist):
    n = jnp.maximum(dist, 0)
    max_exact = NUM_BUCKETS // 2
    nf = jnp.maximum(n, 1).astype(_F32)
    large = max_exact + jnp.floor(jnp.log(nf / max_exact) / math.log(MAX_DISTANCE / max_exact)
                                  * (NUM_BUCKETS - max_exact))
    large = jnp.minimum(large, float(NUM_BUCKETS - 1))
    return jnp.where(n < max_exact, n.astype(_F32), large)


def _bias_lookup(rel_bias_ref, bucket, head):
    val = jnp.full(bucket.shape, rel_bias_ref[NUM_BUCKETS - 1, head], _F32)
    for b in range(NUM_BUCKETS - 2, -1, -1):
        val = jnp.where(bucket == float(b), rel_bias_ref[b, head], val)
    return val


def _bias_tiles_body(rel_bias_ref, own_ref, prev_ref, swa_ref):
    assert MOBA_BLOCK == 2 * SWA_BLOCK and SWA_WINDOW == SWA_BLOCK and MAX_DISTANCE <= SWA_BLOCK
    h = pl.program_id(0)
    q = SWA_BLOCK
    key = lax.broadcasted_iota(jnp.int32, (q, q), 0)
    qry = lax.broadcasted_iota(jnp.int32, (q, q), 1)
    dist = qry - key

    def quadrants(head):
        near = jnp.where(dist >= 0, _bias_lookup(rel_bias_ref, _t5_bucket(dist), head) * LOG2E, NEG)
        nxt = _bias_lookup(rel_bias_ref, _t5_bucket(dist + q), head) * LOG2E
        return near, nxt

    near, nxt = quadrants(h)
    far = jnp.full((q, q), rel_bias_ref[NUM_BUCKETS - 1, h] * LOG2E, _F32)
    masked = jnp.full((q, q), NEG, _F32)
    own_ref[0, :q, :q] = near
    own_ref[0, :q, q:] = nxt
    own_ref[0, q:, :q] = masked
    own_ref[0, q:, q:] = near
    prev_ref[0, :q, :q] = far
    prev_ref[0, :q, q:] = far
    prev_ref[0, q:, :q] = nxt
    prev_ref[0, q:, q:] = far
    near, nxt = quadrants(h + N_HEADS_MOBA)
    swa_ref[0, :q, :] = jnp.where(dist < 0, nxt, NEG)
    swa_ref[0, q:, :] = near


def _bias_tiles(rel_bias):
    def per_head(rows, cols):
        return pl.BlockSpec((1, rows, cols), lambda h: (h, 0, 0))

    return pl.pallas_call(
        _bias_tiles_body,
        grid=(N_HEADS_MOBA,),
        in_specs=[pl.BlockSpec(memory_space=pltpu.SMEM)],
        out_specs=[per_head(MOBA_BLOCK, MOBA_BLOCK), per_head(MOBA_BLOCK, MOBA_BLOCK),
                   per_head(2 * SWA_BLOCK, SWA_BLOCK)],
        out_shape=[jax.ShapeDtypeStruct((N_HEADS_MOBA, MOBA_BLOCK, MOBA_BLOCK), _F32),
                   jax.ShapeDtypeStruct((N_HEADS_MOBA, MOBA_BLOCK, MOBA_BLOCK), _F32),
                   jax.ShapeDtypeStruct((N_HEADS_SWA, 2 * SWA_BLOCK, SWA_BLOCK), _F32)],
        compiler_params=_params(1),
        name="bias_tiles",
    )(rel_bias)


def _moba_body(rel_bias_ref, q_ref, k_ref, v_ref, kmean_ref, own_ref, prev_ref, o_ref,
               far_ref, qpad_ref, s_ref, stat_ref, acc_ref, m_ref):
    g = pl.program_id(1)
    nb = k_ref.shape[1]
    key_lanes = HEADS_PER_KEY_TILE * HEAD_DIM
    q_row = lax.broadcasted_iota(jnp.int32, (key_lanes, MOBA_BLOCK), 0)
    km = kmean_ref[0]
    km_hi = km.astype(_BF16)
    km_lo = (km - km_hi.astype(_F32)).astype(_BF16)
    blk = lax.broadcasted_iota(jnp.int32, (nb, MOBA_BLOCK), 0)
    ones = jnp.ones((SUM_ROWS, MOBA_BLOCK), _BF16)
    ring_slots = range(GROUPS_PER_ITER)
    tail_slots = ring_slots[-PIPE_DEPTH:]
    near_slots = range(GROUPS_PER_ITER, GROUPS_PER_ITER + PIPE_DEPTH)
    assert PIPE_DEPTH == 2 and GROUPS_PER_ITER >= 2 * PIPE_DEPTH

    def key_tile(r):
        t, sub = divmod(r, HEADS_PER_KEY_TILE)
        return slice(t * key_lanes, (t + 1) * key_lanes), sub

    def prepare(i):
        past = blk < i
        own, prev, prev_sel = [], [], []
        for r in range(HEADS_PER_STEP):
            tile_lanes, sub = key_tile(r)
            q_t = q_ref[0, i, tile_lanes, :]
            q_pad = jnp.where((q_row >= sub * HEAD_DIM) & (q_row < (sub + 1) * HEAD_DIM), q_t, 0)
            qpad_ref[r] = q_pad

            gate = jnp.where(
                past, _dot(km_hi[:, tile_lanes], q_pad) + _dot(km_lo[:, tile_lanes], q_pad), NEG)
            sel = jnp.zeros(gate.shape, jnp.bool_)
            for _ in range(MOBA_TOPK):
                top = jnp.max(gate, axis=0, keepdims=True)
                first = jnp.min(jnp.where(gate == top, blk, nb), axis=0, keepdims=True)
                pick = blk == first
                sel = sel | pick
                gate = jnp.where(pick, -jnp.inf, gate)
            sel = sel & past

            far_bias = rel_bias_ref[NUM_BUCKETS - 1, g * HEADS_PER_STEP + r] * LOG2E
            far_ref[r] = jnp.where(sel & (blk < i - 1), far_bias, NEG)
            prev_sel.append(jnp.max(jnp.where(sel & (blk == i - 1), 0.0, NEG), axis=0, keepdims=True))
            own.append(own_ref[r])
            prev.append(prev_ref[r])
        return own, prev, prev_sel

    def score(slot, j, tile_bias=None, row_bias=None):
        for r in range(HEADS_PER_STEP):
            s = _dot(k_ref[0, j, :, key_tile(r)[0]], qpad_ref[r])
            if tile_bias is not None:
                s = s + tile_bias[r]
            s_ref[slot, r] = s
            top = jnp.max(s, axis=0, keepdims=True)
            shift = jnp.zeros_like(top) if row_bias is None else row_bias[r]
            stat_ref[slot, 0, r] = top + shift
            stat_ref[slot, 1, r] = shift

    def absorb(slot, j):
        for r in range(HEADS_PER_STEP):
            lanes = slice(r * HEAD_DIM, (r + 1) * HEAD_DIM)
            m = m_ref[r]
            m_new = jnp.maximum(m, stat_ref[slot, 0, r])
            p = jnp.exp2(s_ref[slot, r] + (stat_ref[slot, 1, r] - m_new)).astype(_BF16)
            pv = _dot(jnp.concatenate([v_ref[0, j, lanes, :], ones], axis=0), p)
            acc_ref[r] = jnp.exp2(m - m_new) * acc_ref[r] + pv
            m_ref[r] = m_new

    def far_block(grp):
        return jnp.minimum(grp, nb - 1)

    def score_far(slot, grp):
        j = far_block(grp)
        score(slot, j, row_bias=[far_ref[r, pl.ds(j, 1), :] for r in range(HEADS_PER_STEP)])

    def score_near(i):
        own, prev, prev_sel = prepare(i)
        score(near_slots[0], i, tile_bias=own)
        score(near_slots[1], jnp.maximum(i - 1, 0), tile_bias=prev, row_bias=prev_sel)

    def start(i):
        m_ref[...] = jnp.full(m_ref.shape, -jnp.inf, _F32)
        acc_ref[...] = jnp.zeros(acc_ref.shape, _F32)
        near_js = (i, jnp.maximum(i - 1, 0))
        for c, slot in enumerate(tail_slots):
            score_far(slot, c)
            absorb(near_slots[c], near_js[c])

    def tile(i, last=False):
        def step(it, _):
            for k in ring_slots:
                group = it * GROUPS_PER_ITER + k
                score_far(k, group + PIPE_DEPTH)
                absorb((k - PIPE_DEPTH) % GROUPS_PER_ITER, far_block(group))
            return 0

        n_far = jnp.maximum(i - 1, 0)
        n_iters = (jnp.maximum(n_far - PIPE_DEPTH, 0) + GROUPS_PER_ITER - 1) // GROUPS_PER_ITER
        lax.fori_loop(0, n_iters, step, 0)

        if not last:
            score_near(i + 1)
        for c, slot in enumerate(tail_slots):
            absorb(slot, far_block(n_iters * GROUPS_PER_ITER + c))
        for r in range(HEADS_PER_STEP):
            acc = acc_ref[r]
            o_ref[0, i, r * HEAD_DIM:(r + 1) * HEAD_DIM, :] = acc[:HEAD_DIM] / acc[HEAD_DIM:HEAD_DIM + 1]
        if not last:
            start(i + 1)

    score_near(0)
    start(0)
    lax.fori_loop(0, nb - 1, lambda i, _: tile(i) or 0, 0)
    tile(nb - 1, last=True)


def _moba(rel_bias, qa_t, ka, va_t, kmean, bias_own, bias_prev):
    batch, nb = qa_t.shape[0], qa_t.shape[1]
    width = HEADS_PER_STEP * HEAD_DIM
    q_spec = pl.BlockSpec((1, nb, width, MOBA_BLOCK), lambda b, g: (b, 0, g, 0))
    bias_spec = pl.BlockSpec((HEADS_PER_STEP, MOBA_BLOCK, MOBA_BLOCK), lambda b, g: (g, 0, 0))
    return pl.pallas_call(
        _moba_body,
        grid=(batch, N_HEADS_MOBA // HEADS_PER_STEP),
        in_specs=[pl.BlockSpec(memory_space=pltpu.SMEM),
                  q_spec,
                  pl.BlockSpec((1, nb, MOBA_BLOCK, width), lambda b, g: (b, 0, 0, g)),
                  q_spec,
                  pl.BlockSpec((1, nb, width), lambda b, g: (b, 0, g)),
                  bias_spec, bias_spec],
        out_specs=q_spec,
        out_shape=jax.ShapeDtypeStruct((batch, nb, W_MOBA, MOBA_BLOCK), _F32),
        scratch_shapes=[pltpu.VMEM((HEADS_PER_STEP, nb, MOBA_BLOCK), _F32),
                        pltpu.VMEM((HEADS_PER_STEP, HEADS_PER_KEY_TILE * HEAD_DIM, MOBA_BLOCK), _BF16),
                        pltpu.VMEM((GROUPS_PER_ITER + PIPE_DEPTH, HEADS_PER_STEP, MOBA_BLOCK, MOBA_BLOCK), _F32),
                        pltpu.VMEM((GROUPS_PER_ITER + PIPE_DEPTH, 2, HEADS_PER_STEP, 1, MOBA_BLOCK), _F32),
                        pltpu.VMEM((HEADS_PER_STEP, HEAD_DIM + SUM_ROWS, MOBA_BLOCK), _F32),
                        pltpu.VMEM((HEADS_PER_STEP, 1, MOBA_BLOCK), _F32)],
        compiler_params=_params(2),
        name="moba",
    )(rel_bias, qa_t, ka, va_t, kmean, bias_own, bias_prev)


def _swa_body(sinks_ref, q_ref, k_prev_ref, k_cur_ref, v_prev_ref, v_cur_ref, bias_ref, o_ref, s_ref):
    no_prev = jnp.where(pl.program_id(1) == 0, NEG, 0.0)
    window_row = lax.broadcasted_iota(jnp.int32, (2 * SWA_BLOCK, SWA_BLOCK), 0)
    first_window_mask = jnp.where(window_row < SWA_BLOCK, no_prev, 0.0)
    zeros = jnp.zeros((HEAD_DIM, SWA_BLOCK), _BF16)
    ones = jnp.ones((SUM_ROWS, 2 * SWA_BLOCK), _BF16)
    k_blocks = [k_prev_ref[0, 0]] + [k_cur_ref[0, t] for t in range(SWA_Q_PER_STEP)]
    v_blocks = [v_prev_ref[0, 0]] + [v_cur_ref[0, t] for t in range(SWA_Q_PER_STEP)]
    chains = [(t, kv) for t in range(SWA_Q_PER_STEP) for kv in range(N_KV_SWA)]

    for t, kv in chains:
        q_pad = jnp.concatenate(
            [jnp.concatenate(
                [q_ref[0, t, h * HEAD_DIM:(h + 1) * HEAD_DIM, :] if part == kv else zeros
                 for part in range(N_KV_SWA)], axis=0)
             for h in range(kv * SWA_GROUP, (kv + 1) * SWA_GROUP)], axis=1)
        window_keys = jnp.concatenate([k_blocks[t], k_blocks[t + 1]], axis=0)
        s = _dot(window_keys, q_pad)
        for slot in range(SWA_GROUP):
            cols = slice(slot * SWA_BLOCK, (slot + 1) * SWA_BLOCK)
            biased = s[:, cols] + bias_ref[kv * SWA_GROUP + slot]
            s_ref[t, kv, :, cols] = biased + first_window_mask if t == 0 else biased

    for t, kv in chains:
        kv_rows = slice(kv * HEAD_DIM, (kv + 1) * HEAD_DIM)
        window_values = jnp.concatenate(
            [jnp.concatenate([v_blocks[t][kv_rows, :], v_blocks[t + 1][kv_rows, :]], axis=1), ones], axis=0)
        p, sink_p = [], []
        for slot in range(SWA_GROUP):
            cols = slice(slot * SWA_BLOCK, (slot + 1) * SWA_BLOCK)
            sink = sinks_ref[kv * SWA_GROUP + slot] * LOG2E
            m = jnp.maximum(jnp.max(s_ref[t, kv, :, cols], axis=0, keepdims=True), sink)
            p.append(jnp.exp2(s_ref[t, kv, :, cols] - m).astype(_BF16))
            sink_p.append(jnp.exp2(sink - m))
        acc = _dot(window_values, jnp.concatenate(p, axis=1))
        for slot in range(SWA_GROUP):
            h = kv * SWA_GROUP + slot
            cols = slice(slot * SWA_BLOCK, (slot + 1) * SWA_BLOCK)
            l = acc[HEAD_DIM:HEAD_DIM + 1, cols] + sink_p[slot]
            o_ref[0, t, h * HEAD_DIM:(h + 1) * HEAD_DIM, :] = acc[:HEAD_DIM, cols] / l


def _swa(sinks, qs_t, ks, vs_t, bias_swa):
    batch, nb = qs_t.shape[0], qs_t.shape[1]

    def cur(rows, cols):
        return pl.BlockSpec((1, SWA_Q_PER_STEP, rows, cols), lambda b, n: (b, n, 0, 0))

    def prev(rows, cols):
        return pl.BlockSpec((1, 1, rows, cols),
                            lambda b, n: (b, jnp.maximum(n * SWA_Q_PER_STEP - 1, 0), 0, 0))

    return pl.pallas_call(
        _swa_body,
        grid=(batch, nb // SWA_Q_PER_STEP),
        in_specs=[pl.BlockSpec(memory_space=pltpu.SMEM),
                  cur(W_SWA, SWA_BLOCK),
                  prev(SWA_BLOCK, W_SWA_KV), cur(SWA_BLOCK, W_SWA_KV),
                  prev(W_SWA_KV, SWA_BLOCK), cur(W_SWA_KV, SWA_BLOCK),
                  _const_spec((N_HEADS_SWA, 2 * SWA_BLOCK, SWA_BLOCK))],
        out_specs=cur(W_SWA, SWA_BLOCK),
        out_shape=jax.ShapeDtypeStruct((batch, nb, W_SWA, SWA_BLOCK), _F32),
        scratch_shapes=[pltpu.VMEM((SWA_Q_PER_STEP, N_KV_SWA, 2 * SWA_BLOCK, SWA_GROUP * SWA_BLOCK), _F32)],
        compiler_params=_params(2),
        name="swa",
    )(sinks, qs_t, ks, ks, vs_t, vs_t, bias_swa)


def _group_norm_rows(o_ref, g_ref):
    rows = []
    for t in range(o_ref.shape[0]):
        o_t = o_ref[t]
        scale = lax.rsqrt(jnp.mean(o_t * o_t, axis=0, keepdims=True) + RMS_EPS)
        rows.append(((o_t * scale).T * g_ref[...]).astype(_BF16))
    return jnp.concatenate(rows, axis=0)


def _mix_ffn_body(x_ref, oa_ref, ob_ref, ga_ref, gb_ref, w_out_ref, mix_post_g_ref,
                  pre_g_ref, wg_hbm, wu_hbm, wd_hbm, post_g_ref, o_ref, h_ref, *weight_scratch):
    wg_ref, wu_ref, wd_ref = _ffn_weights((wg_hbm, wu_hbm, wd_hbm), weight_scratch)
    y = (_dot(_group_norm_rows(oa_ref, ga_ref), w_out_ref[:W_MOBA, :])
         + _dot(_group_norm_rows(ob_ref, gb_ref), w_out_ref[W_MOBA:, :]))
    x = x_ref[...] + _rms_rows(y, mix_post_g_ref[...])
    _ffn_tile(lambda rows: x[rows, :], pre_g_ref, wg_ref, wu_ref, wd_ref, post_g_ref, o_ref, h_ref)


def _mix_ffn(x2d, oa_t, ob_t, ga, gb, w_out, mix_post_g, ffn_params):
    n_tok = x2d.shape[0]
    return pl.pallas_call(
        _mix_ffn_body,
        grid=(n_tok // TOKEN_TILE,),
        in_specs=[_TOKEN_TILE_SPEC,
                  pl.BlockSpec((_MOBA_PER_TILE, W_MOBA, MOBA_BLOCK), lambda i: (i, 0, 0)),
                  pl.BlockSpec((_SWA_PER_TILE, W_SWA, SWA_BLOCK), lambda i: (i, 0, 0)),
                  _const_spec((1, W_MOBA)), _const_spec((1, W_SWA)),
                  _const_spec((W_MOBA + W_SWA, D_MODEL)),
                  _const_spec((1, D_MODEL))] + _FFN_WEIGHT_SPECS,
        out_specs=_TOKEN_TILE_SPEC,
        out_shape=jax.ShapeDtypeStruct((n_tok, D_MODEL), _F32),
        scratch_shapes=_FFN_SCRATCH,
        compiler_params=_params(1),
        name="mix_ffn",
    )(x2d, oa_t, ob_t, ga, gb, w_out, mix_post_g, *ffn_params)


def _row(v):
    return v.reshape(1, -1)


def _layer(x, ffn1, mix, ffn2, rel_bias, bias_tiles):
    batch, seq, _ = x.shape
    (mix_pre_g, w_in, moba_out_g, swa_sinks, swa_out_g, w_out, mix_post_g) = mix
    bias_own, bias_prev, bias_swa = bias_tiles

    def ffn_params(params):
        pre_g, w_gate, w_up, w_down, post_g = params
        return (_row(pre_g), w_gate, w_up, w_down, _row(post_g))

    x = _ffn(x.reshape(batch * seq, D_MODEL), ffn_params(ffn1))

    qa, ka, va, qs, ks, vs = jnp.split(
        w_in, [W_MOBA, 2 * W_MOBA, 3 * W_MOBA, 3 * W_MOBA + W_SWA, 3 * W_MOBA + W_SWA + W_SWA_KV], axis=1)
    w_rows = jnp.concatenate([ka, ks], axis=1).astype(_BF16)
    w_t = jnp.concatenate([qa, va, qs, vs], axis=1).astype(_BF16).T
    qa_t, va_t, qs_t, vs_t, ka_b, ks_b, kmean = _in_proj(
        x.reshape(batch, seq, D_MODEL), _row(mix_pre_g), w_rows, w_t)

    oa_t = _moba(rel_bias, qa_t, ka_b, va_t, kmean.reshape(batch, seq // MOBA_BLOCK, W_MOBA),
                 bias_own, bias_prev)
    ob_t = _swa(swa_sinks, qs_t, ks_b, vs_t, bias_swa)

    x = _mix_ffn(x, oa_t.reshape(-1, W_MOBA, MOBA_BLOCK), ob_t.reshape(-1, W_SWA, SWA_BLOCK),
                 _row(moba_out_g), _row(swa_out_g), w_out.astype(_BF16), _row(mix_post_g),
                 ffn_params(ffn2))
    return x.reshape(batch, seq, D_MODEL)


def kernel(x, ffn1_pre_g, ffn1_w_gate, ffn1_w_up, ffn1_w_down, ffn1_post_g, mix_pre_g, w_in, rel_bias,
           moba_out_g, swa_sinks, swa_out_g, w_out, mix_post_g, ffn2_pre_g, ffn2_w_gate, ffn2_w_up,
           ffn2_w_down, ffn2_post_g):
    bias_tiles = _bias_tiles(rel_bias)
    for l in range(ffn1_pre_g.shape[0]):
        x = _layer(
            x,
            (ffn1_pre_g[l], ffn1_w_gate[l], ffn1_w_up[l], ffn1_w_down[l], ffn1_post_g[l]),
            (mix_pre_g[l], w_in[l], moba_out_g[l], swa_sinks[l], swa_out_g[l], w_out[l], mix_post_g[l]),
            (ffn2_pre_g[l], ffn2_w_gate[l], ffn2_w_up[l], ffn2_w_down[l], ffn2_post_g[l]),
            rel_bias, bias_tiles)
    return x
```

```python
import math

import jax
import jax.numpy as jnp
from jax import lax
from jax.experimental import pallas as pl
from jax.experimental.pallas import tpu as pltpu

D_MODEL = 1024
HEAD_DIM = 64
N_HEADS_MOBA = 8
N_HEADS_SWA = 8
N_KV_SWA = 2
SWA_GROUP = N_HEADS_SWA // N_KV_SWA
W_MOBA = N_HEADS_MOBA * HEAD_DIM
W_SWA = N_HEADS_SWA * HEAD_DIM
W_SWA_KV = N_KV_SWA * HEAD_DIM
MOBA_BLOCK = 256
MOBA_TOPK = 3
SWA_BLOCK = 128
SWA_WINDOW = 128
NUM_BUCKETS = 32
MAX_DISTANCE = 128
D_FF = 2816
RMS_EPS = 1e-6
FFN_RES_WEIGHT = 0.5
NEG = -1e30
QK_SCALE = HEAD_DIM ** -0.5
LOG2E = math.log2(math.e)
Q_FOLD = QK_SCALE * LOG2E
SUM_ROWS = 16

TOKEN_TILE = 1024
FF_CHUNK = 256
FFN_ROW_PARTS = 2
CAST_CHUNKS = 16
CAST_SLOTS = 4
HEADS_PER_STEP = 4
HEADS_PER_KEY_TILE = 2
PIPE_DEPTH = 2
GROUPS_PER_ITER = 4
SWA_Q_PER_STEP = 4
VMEM_LIMIT_BYTES = 58 * 1024 * 1024

_BF16 = jnp.bfloat16
_F32 = jnp.float32


def _dot(a, b):
    return jnp.dot(a, b, preferred_element_type=_F32)


def _dot_nt(a, b):
    return lax.dot_general(a, b, (((1,), (1,)), ((), ())), preferred_element_type=_F32)


def _rms_rows(x, g):
    return x * lax.rsqrt(jnp.mean(x * x, axis=-1, keepdims=True) + RMS_EPS) * g


def _const_spec(shape):
    return pl.BlockSpec(shape, lambda *_: (0,) * len(shape), pipeline_mode=pl.Buffered(1))


def _params(n_axes):
    return pltpu.CompilerParams(
        dimension_semantics=("arbitrary",) * n_axes, vmem_limit_bytes=VMEM_LIMIT_BYTES)


def _ffn_tile(x_rows, pre_g_ref, wg_ref, wu_ref, wd_ref, post_g_ref, o_ref, h_ref):
    part = TOKEN_TILE // FFN_ROW_PARTS
    for p in range(FFN_ROW_PARTS):
        rows = slice(p * part, (p + 1) * part)
        x = x_rows(rows)
        xn = _rms_rows(x, pre_g_ref[...]).astype(_BF16)
        for c in range(D_FF // FF_CHUNK):
            cols = slice(c * FF_CHUNK, (c + 1) * FF_CHUNK)
            gate = _dot(xn, wg_ref[:, cols])
            up = _dot(xn, wu_ref[:, cols])
            h_ref[rows, cols] = (jax.nn.silu(gate) * up).astype(_BF16)
        y = _dot(h_ref[rows, :], wd_ref[...])
        o_ref[rows, :] = x + FFN_RES_WEIGHT * _rms_rows(y, post_g_ref[...])


def _cast_weights(pairs, stage, sem):
    slots, rows = stage.shape[0], stage.shape[1]
    chunks = [(w_hbm, w_bf, k) for w_hbm, w_bf in pairs for k in range(w_hbm.shape[0] // rows)]

    def copy(c):
        w_hbm, _, k = chunks[c]
        return pltpu.make_async_copy(w_hbm.at[pl.ds(k * rows, rows), :], stage.at[c % slots], sem.at[c % slots])

    def prime():
        for c in range(min(slots, len(chunks))):
            copy(c).start()

    def drain():
        for c, (_, w_bf, k) in enumerate(chunks):
            copy(c).wait()
            w_bf[k * rows:(k + 1) * rows, :] = stage[c % slots].astype(_BF16)
            if c + slots < len(chunks):
                copy(c + slots).start()

    return prime, drain


def _ffn_weights(w_hbm, scratch):
    wg_bf, wu_bf, wd_bf, stage_in, stage_down, sem_in, sem_down = scratch

    @pl.when(pl.program_id(0) == 0)
    def _():
        prime_in, drain_in = _cast_weights([(w_hbm[0], wg_bf), (w_hbm[1], wu_bf)], stage_in, sem_in)
        prime_down, drain_down = _cast_weights([(w_hbm[2], wd_bf)], stage_down, sem_down)
        prime_in()
        prime_down()
        drain_in()
        drain_down()

    return wg_bf, wu_bf, wd_bf


def _ffn_body(x_ref, pre_g_ref, wg_hbm, wu_hbm, wd_hbm, post_g_ref, o_ref, h_ref, *weight_scratch):
    wg_ref, wu_ref, wd_ref = _ffn_weights((wg_hbm, wu_hbm, wd_hbm), weight_scratch)
    _ffn_tile(lambda rows: x_ref[rows, :], pre_g_ref, wg_ref, wu_ref, wd_ref, post_g_ref, o_ref, h_ref)


_TOKEN_TILE_SPEC = pl.BlockSpec((TOKEN_TILE, D_MODEL), lambda i: (i, 0))
_HBM_SPEC = pl.BlockSpec(memory_space=pl.ANY)
_FFN_WEIGHT_SPECS = [_const_spec((1, D_MODEL)), _HBM_SPEC, _HBM_SPEC, _HBM_SPEC, _const_spec((1, D_MODEL))]
_FFN_SCRATCH = [
    pltpu.VMEM((TOKEN_TILE, D_FF), _BF16),
    pltpu.VMEM((D_MODEL, D_FF), _BF16),
    pltpu.VMEM((D_MODEL, D_FF), _BF16),
    pltpu.VMEM((D_FF, D_MODEL), _BF16),
    pltpu.VMEM((CAST_SLOTS, D_MODEL // CAST_CHUNKS, D_FF), _F32),
    pltpu.VMEM((CAST_SLOTS, D_FF // CAST_CHUNKS, D_MODEL), _F32),
    pltpu.SemaphoreType.DMA((CAST_SLOTS,)),
    pltpu.SemaphoreType.DMA((CAST_SLOTS,)),
]


def _ffn(x2d, ffn_params):
    n_tok = x2d.shape[0]
    return pl.pallas_call(
        _ffn_body,
        grid=(n_tok // TOKEN_TILE,),
        in_specs=[_TOKEN_TILE_SPEC] + _FFN_WEIGHT_SPECS,
        out_specs=_TOKEN_TILE_SPEC,
        out_shape=jax.ShapeDtypeStruct((n_tok, D_MODEL), _F32),
        scratch_shapes=_FFN_SCRATCH,
        compiler_params=_params(1),
        name="ffn",
    )(x2d, *ffn_params)


_T_QA, _T_VA, _T_QS, _T_VS = 0, W_MOBA, 2 * W_MOBA, 2 * W_MOBA + W_SWA
_T_ROWS = 2 * W_MOBA + W_SWA + W_SWA_KV
_MOBA_PER_TILE = TOKEN_TILE // MOBA_BLOCK
_SWA_PER_TILE = TOKEN_TILE // SWA_BLOCK


def _in_proj_body(x_ref, g_ref, w_rows_ref, w_t_ref,
                  qa_ref, va_ref, qs_ref, vs_ref, ka_ref, ks_ref, kmean_ref):
    swa_per_moba = MOBA_BLOCK // SWA_BLOCK
    for t in range(_MOBA_PER_TILE):
        xn = _rms_rows(x_ref[0, t * MOBA_BLOCK:(t + 1) * MOBA_BLOCK, :], g_ref[...]).astype(_BF16)
        k_rows = _dot(xn, w_rows_ref[...])
        proj_t = _dot_nt(w_t_ref[...], xn)
        k_blk = k_rows[:, :W_MOBA]
        ka_ref[0, t] = k_blk.astype(_BF16)
        kmean_ref[0, t] = jnp.mean(k_blk, axis=0, keepdims=True)
        qa_ref[0, t] = (proj_t[_T_QA:_T_QA + W_MOBA, :] * Q_FOLD).astype(_BF16)
        va_ref[0, t] = proj_t[_T_VA:_T_VA + W_MOBA, :].astype(_BF16)
        for u in range(swa_per_moba):
            rows = slice(u * SWA_BLOCK, (u + 1) * SWA_BLOCK)
            ks_ref[0, t * swa_per_moba + u] = k_rows[rows, W_MOBA:].astype(_BF16)
            qs_ref[0, t * swa_per_moba + u] = (proj_t[_T_QS:_T_QS + W_SWA, rows] * Q_FOLD).astype(_BF16)
            vs_ref[0, t * swa_per_moba + u] = proj_t[_T_VS:_T_VS + W_SWA_KV, rows].astype(_BF16)


def _in_proj(x, g, w_rows, w_t):
    batch, seq, _ = x.shape
    nb_moba, nb_swa = seq // MOBA_BLOCK, seq // SWA_BLOCK

    def blocked(n_per_tile, rows, cols):
        return pl.BlockSpec((1, n_per_tile, rows, cols), lambda b, i: (b, i, 0, 0))

    return pl.pallas_call(
        _in_proj_body,
        grid=(batch, seq // TOKEN_TILE),
        in_specs=[pl.BlockSpec((1, TOKEN_TILE, D_MODEL), lambda b, i: (b, i, 0)),
                  _const_spec((1, D_MODEL)),
                  _const_spec((D_MODEL, W_MOBA + W_SWA_KV)),
                  _const_spec((_T_ROWS, D_MODEL))],
        out_specs=[blocked(_MOBA_PER_TILE, W_MOBA, MOBA_BLOCK),
                   blocked(_MOBA_PER_TILE, W_MOBA, MOBA_BLOCK),
                   blocked(_SWA_PER_TILE, W_SWA, SWA_BLOCK),
                   blocked(_SWA_PER_TILE, W_SWA_KV, SWA_BLOCK),
                   blocked(_MOBA_PER_TILE, MOBA_BLOCK, W_MOBA),
                   blocked(_SWA_PER_TILE, SWA_BLOCK, W_SWA_KV),
                   blocked(_MOBA_PER_TILE, 1, W_MOBA)],
        out_shape=[jax.ShapeDtypeStruct((batch, nb_moba, W_MOBA, MOBA_BLOCK), _BF16),
                   jax.ShapeDtypeStruct((batch, nb_moba, W_MOBA, MOBA_BLOCK), _BF16),
                   jax.ShapeDtypeStruct((batch, nb_swa, W_SWA, SWA_BLOCK), _BF16),
                   jax.ShapeDtypeStruct((batch, nb_swa, W_SWA_KV, SWA_BLOCK), _BF16),
                   jax.ShapeDtypeStruct((batch, nb_moba, MOBA_BLOCK, W_MOBA), _BF16),
                   jax.ShapeDtypeStruct((batch, nb_swa, SWA_BLOCK, W_SWA_KV), _BF16),
                   jax.ShapeDtypeStruct((batch, nb_moba, 1, W_MOBA), _F32)],
        compiler_params=_params(2),
        name="in_proj",
    )(x, g, w_rows, w_t)


def _t5_bucket(dist):
    n = jnp.maximum(dist, 0)
    max_exact = NUM_BUCKETS // 2
    nf = jnp.maximum(n, 1).astype(_F32)
    large = max_exact + jnp.floor(jnp.log(nf / max_exact) / math.log(MAX_DISTANCE / max_exact)
                                  * (NUM_BUCKETS - max_exact))
    large = jnp.minimum(large, float(NUM_BUCKETS - 1))
    return jnp.where(n < max_exact, n.astype(_F32), large)


def _bias_lookup(rel_bias_ref, bucket, head):
    val = jnp.full(bucket.shape, rel_bias_ref[NUM_BUCKETS - 1, head], _F32)
    for b in range(NUM_BUCKETS - 2, -1, -1):
        val = jnp.where(bucket == float(b), rel_bias_ref[b, head], val)
    return val


def _bias_tiles_body(rel_bias_ref, own_ref, prev_ref, swa_ref):
    assert MOBA_BLOCK == 2 * SWA_BLOCK and SWA_WINDOW == SWA_BLOCK and MAX_DISTANCE <= SWA_BLOCK
    h = pl.program_id(0)
    q = SWA_BLOCK
    key = lax.broadcasted_iota(jnp.int32, (q, q), 0)
    qry = lax.broadcasted_iota(jnp.int32, (q, q), 1)
    dist = qry - key

    def quadrants(head):
        near = jnp.where(dist >= 0, _bias_lookup(rel_bias_ref, _t5_bucket(dist), head) * LOG2E, NEG)
        nxt = _bias_lookup(rel_bias_ref, _t5_bucket(dist + q), head) * LOG2E
        return near, nxt

    near, nxt = quadrants(h)
    far = jnp.full((q, q), rel_bias_ref[NUM_BUCKETS - 1, h] * LOG2E, _F32)
    masked = jnp.full((q, q), NEG, _F32)
    own_ref[0, :q, :q] = near
    own_ref[0, :q, q:] = nxt
    own_ref[0, q:, :q] = masked
    own_ref[0, q:, q:] = near
    prev_ref[0, :q, :q] = far
    prev_ref[0, :q, q:] = far
    prev_ref[0, q:, :q] = nxt
    prev_ref[0, q:, q:] = far
    near, nxt = quadrants(h + N_HEADS_MOBA)
    swa_ref[0, :q, :] = jnp.where(dist < 0, nxt, NEG)
    swa_ref[0, q:, :] = near


def _bias_tiles(rel_bias):
    def per_head(rows, cols):
        return pl.BlockSpec((1, rows, cols), lambda h: (h, 0, 0))

    return pl.pallas_call(
        _bias_tiles_body,
        grid=(N_HEADS_MOBA,),
        in_specs=[pl.BlockSpec(memory_space=pltpu.SMEM)],
        out_specs=[per_head(MOBA_BLOCK, MOBA_BLOCK), per_head(MOBA_BLOCK, MOBA_BLOCK),
                   per_head(2 * SWA_BLOCK, SWA_BLOCK)],
        out_shape=[jax.ShapeDtypeStruct((N_HEADS_MOBA, MOBA_BLOCK, MOBA_BLOCK), _F32),
                   jax.ShapeDtypeStruct((N_HEADS_MOBA, MOBA_BLOCK, MOBA_BLOCK), _F32),
                   jax.ShapeDtypeStruct((N_HEADS_SWA, 2 * SWA_BLOCK, SWA_BLOCK), _F32)],
        compiler_params=_params(1),
        name="bias_tiles",
    )(rel_bias)


def _moba_body(rel_bias_ref, q_ref, k_ref, v_ref, kmean_ref, own_ref, prev_ref, o_ref,
               far_ref, qpad_ref, s_ref, acc_ref):
    g = pl.program_id(1)
    nb = k_ref.shape[1]
    key_lanes = HEADS_PER_KEY_TILE * HEAD_DIM
    q_row = lax.broadcasted_iota(jnp.int32, (key_lanes, MOBA_BLOCK), 0)
    km = kmean_ref[0]
    km_hi = km.astype(_BF16)
    km_lo = (km - km_hi.astype(_F32)).astype(_BF16)
    blk = lax.broadcasted_iota(jnp.int32, (nb, MOBA_BLOCK), 0)
    ones = jnp.ones((SUM_ROWS, MOBA_BLOCK), _BF16)
    ring_slots = range(GROUPS_PER_ITER)
    tail_slots = ring_slots[-PIPE_DEPTH:]
    near_slots = range(GROUPS_PER_ITER, GROUPS_PER_ITER + PIPE_DEPTH)
    assert PIPE_DEPTH == 2 and GROUPS_PER_ITER >= 2 * PIPE_DEPTH

    def key_tile(r):
        t, sub = divmod(r, HEADS_PER_KEY_TILE)
        return slice(t * key_lanes, (t + 1) * key_lanes), sub

    def prepare(i):
        past = blk < i
        own, prev, prev_sel = [], [], []
        for r in range(HEADS_PER_STEP):
            tile_lanes, sub = key_tile(r)
            q_t = q_ref[0, i, tile_lanes, :]
            q_pad = jnp.where((q_row >= sub * HEAD_DIM) & (q_row < (sub + 1) * HEAD_DIM), q_t, 0)
            qpad_ref[r] = q_pad

            gate = jnp.where(
                past, _dot(km_hi[:, tile_lanes], q_pad) + _dot(km_lo[:, tile_lanes], q_pad), NEG)
            sel = jnp.zeros(gate.shape, jnp.bool_)
            for _ in range(MOBA_TOPK):
                top = jnp.max(gate, axis=0, keepdims=True)
                first = jnp.min(jnp.where(gate == top, blk, nb), axis=0, keepdims=True)
                pick = blk == first
                sel = sel | pick
                gate = jnp.where(pick, -jnp.inf, gate)
            sel = sel & past

            far_bias = rel_bias_ref[NUM_BUCKETS - 1, g * HEADS_PER_STEP + r] * LOG2E
            far_ref[r] = jnp.where(sel & (blk < i - 1), far_bias, NEG)
            prev_sel.append(jnp.max(jnp.where(sel & (blk == i - 1), 0.0, NEG), axis=0, keepdims=True))
            own.append(own_ref[r])
            prev.append(prev_ref[r])
        return own, prev, prev_sel

    def score(slot, j, tile_bias=None, row_bias=None):
        tops, shifts = [], []
        for r in range(HEADS_PER_STEP):
            s = _dot(k_ref[0, j, :, key_tile(r)[0]], qpad_ref[r])
            if tile_bias is not None:
                s = s + tile_bias[r]
            s_ref[slot, r] = s
            top = jnp.max(s, axis=0, keepdims=True)
            shift = jnp.zeros_like(top) if row_bias is None else row_bias[r]
            tops.append(top + shift)
            shifts.append(shift)
        return slot, j, tuple(tops), tuple(shifts)

    def absorb(carry, scored):
        slot, j, tops, shifts = scored
        out = []
        for r in range(HEADS_PER_STEP):
            lanes = slice(r * HEAD_DIM, (r + 1) * HEAD_DIM)
            m = carry[r]
            m_new = jnp.maximum(m, tops[r])
            p = jnp.exp2(s_ref[slot, r] + (shifts[r] - m_new)).astype(_BF16)
            pv = _dot(jnp.concatenate([v_ref[0, j, lanes, :], ones], axis=0), p)
            acc_ref[r] = jnp.exp2(m - m_new) * acc_ref[r] + pv
            out.append(m_new)
        return tuple(out)

    def score_far(slot, grp):
        j = jnp.minimum(grp, nb - 1)
        return score(slot, j, row_bias=[far_ref[r, pl.ds(j, 1), :] for r in range(HEADS_PER_STEP)])

    def score_near(i):
        own, prev, prev_sel = prepare(i)
        return (score(near_slots[0], i, tile_bias=own)[2:],
                score(near_slots[1], jnp.maximum(i - 1, 0), tile_bias=prev, row_bias=prev_sel)[2:])

    def start(i, near):
        carry = tuple(jnp.full((1, MOBA_BLOCK), -jnp.inf, _F32) for _ in range(HEADS_PER_STEP))
        acc_ref[...] = jnp.zeros(acc_ref.shape, _F32)
        near_js = (i, jnp.maximum(i - 1, 0))
        ring = {slot: (slot, j) + maxes_shifts for slot, j, maxes_shifts in zip(near_slots, near_js, near)}
        for c, slot in enumerate(tail_slots):
            ring[slot] = score_far(slot, c)
            carry = absorb(carry, ring[near_slots[c]])
        return carry, tuple(ring[slot][1:] for slot in tail_slots)

    def tile(i, phase, state, last=False):
        def ring_steps(first_group, count, carry, pending):
            ring = {slot: (slot,) + p for slot, p in zip(tail_slots, pending)}
            for k in range(count):
                scored = score_far(k, first_group + k)
                carry = absorb(carry, ring[(k - PIPE_DEPTH) % GROUPS_PER_ITER])
                ring[k] = scored
            return carry, ring

        def step(it, state):
            carry, ring = ring_steps(PIPE_DEPTH + it * GROUPS_PER_ITER, GROUPS_PER_ITER, *state)
            return carry, tuple(ring[slot][1:] for slot in tail_slots)

        leftover = (phase - 1 - PIPE_DEPTH) % GROUPS_PER_ITER
        n_iters = jnp.maximum(i - 1 - PIPE_DEPTH, 0) // GROUPS_PER_ITER
        carry, pending = lax.fori_loop(0, n_iters, step, state)
        carry, ring = ring_steps(PIPE_DEPTH + n_iters * GROUPS_PER_ITER, leftover, carry, pending)

        near_next = None if last else score_near(i + 1)
        for c in range(PIPE_DEPTH):
            carry = absorb(carry, ring[(leftover - PIPE_DEPTH + c) % GROUPS_PER_ITER])
        for r in range(HEADS_PER_STEP):
            acc = acc_ref[r]
            o_ref[0, i, r * HEAD_DIM:(r + 1) * HEAD_DIM, :] = acc[:HEAD_DIM] / acc[HEAD_DIM:HEAD_DIM + 1]
        return None if last else start(i + 1, near_next)

    def tiles(a, state, last=False):
        for phase in range(GROUPS_PER_ITER):
            state = tile(a * GROUPS_PER_ITER + phase, phase, state,
                         last=last and phase == GROUPS_PER_ITER - 1)
        return state

    assert nb % GROUPS_PER_ITER == 0
    state = lax.fori_loop(0, nb // GROUPS_PER_ITER - 1, tiles, start(0, score_near(0)))
    tiles(nb // GROUPS_PER_ITER - 1, state, last=True)


def _moba(rel_bias, qa_t, ka, va_t, kmean, bias_own, bias_prev):
    batch, nb = qa_t.shape[0], qa_t.shape[1]
    width = HEADS_PER_STEP * HEAD_DIM
    q_spec = pl.BlockSpec((1, nb, width, MOBA_BLOCK), lambda b, g: (b, 0, g, 0))
    bias_spec = pl.BlockSpec((HEADS_PER_STEP, MOBA_BLOCK, MOBA_BLOCK), lambda b, g: (g, 0, 0))
    return pl.pallas_call(
        _moba_body,
        grid=(batch, N_HEADS_MOBA // HEADS_PER_STEP),
        in_specs=[pl.BlockSpec(memory_space=pltpu.SMEM),
                  q_spec,
                  pl.BlockSpec((1, nb, MOBA_BLOCK, width), lambda b, g: (b, 0, 0, g)),
                  q_spec,
                  pl.BlockSpec((1, nb, width), lambda b, g: (b, 0, g)),
                  bias_spec, bias_spec],
        out_specs=q_spec,
        out_shape=jax.ShapeDtypeStruct((batch, nb, W_MOBA, MOBA_BLOCK), _F32),
        scratch_shapes=[pltpu.VMEM((HEADS_PER_STEP, nb, MOBA_BLOCK), _F32),
                        pltpu.VMEM((HEADS_PER_STEP, HEADS_PER_KEY_TILE * HEAD_DIM, MOBA_BLOCK), _BF16),
                        pltpu.VMEM((GROUPS_PER_ITER + PIPE_DEPTH, HEADS_PER_STEP, MOBA_BLOCK, MOBA_BLOCK), _F32),
                        pltpu.VMEM((HEADS_PER_STEP, HEAD_DIM + SUM_ROWS, MOBA_BLOCK), _F32)],
        compiler_params=_params(2),
        name="moba",
    )(rel_bias, qa_t, ka, va_t, kmean, bias_own, bias_prev)


def _swa_body(sinks_ref, q_ref, k_prev_ref, k_cur_ref, v_prev_ref, v_cur_ref, bias_ref, o_ref, s_ref):
    no_prev = jnp.where(pl.program_id(1) == 0, NEG, 0.0)
    window_row = lax.broadcasted_iota(jnp.int32, (2 * SWA_BLOCK, SWA_BLOCK), 0)
    first_window_mask = jnp.where(window_row < SWA_BLOCK, no_prev, 0.0)
    zeros = jnp.zeros((HEAD_DIM, SWA_BLOCK), _BF16)
    ones = jnp.ones((SUM_ROWS, 2 * SWA_BLOCK), _BF16)
    k_blocks = [k_prev_ref[0, 0]] + [k_cur_ref[0, t] for t in range(SWA_Q_PER_STEP)]
    v_blocks = [v_prev_ref[0, 0]] + [v_cur_ref[0, t] for t in range(SWA_Q_PER_STEP)]
    chains = [(t, kv) for t in range(SWA_Q_PER_STEP) for kv in range(N_KV_SWA)]

    for t, kv in chains:
        q_pad = jnp.concatenate(
            [jnp.concatenate(
                [q_ref[0, t, h * HEAD_DIM:(h + 1) * HEAD_DIM, :] if part == kv else zeros
                 for part in range(N_KV_SWA)], axis=0)
             for h in range(kv * SWA_GROUP, (kv + 1) * SWA_GROUP)], axis=1)
        window_keys = jnp.concatenate([k_blocks[t], k_blocks[t + 1]], axis=0)
        s = _dot(window_keys, q_pad)
        for slot in range(SWA_GROUP):
            cols = slice(slot * SWA_BLOCK, (slot + 1) * SWA_BLOCK)
            biased = s[:, cols] + bias_ref[kv * SWA_GROUP + slot]
            s_ref[t, kv, :, cols] = biased + first_window_mask if t == 0 else biased

    for t, kv in chains:
        kv_rows = slice(kv * HEAD_DIM, (kv + 1) * HEAD_DIM)
        window_values = jnp.concatenate(
            [jnp.concatenate([v_blocks[t][kv_rows, :], v_blocks[t + 1][kv_rows, :]], axis=1), ones], axis=0)
        p, sink_p = [], []
        for slot in range(SWA_GROUP):
            cols = slice(slot * SWA_BLOCK, (slot + 1) * SWA_BLOCK)
            sink = sinks_ref[kv * SWA_GROUP + slot] * LOG2E
            m = jnp.maximum(jnp.max(s_ref[t, kv, :, cols], axis=0, keepdims=True), sink)
            p.append(jnp.exp2(s_ref[t, kv, :, cols] - m).astype(_BF16))
            sink_p.append(jnp.exp2(sink - m))
        acc = _dot(window_values, jnp.concatenate(p, axis=1))
        for slot in range(SWA_GROUP):
            h = kv * SWA_GROUP + slot
            cols = slice(slot * SWA_BLOCK, (slot + 1) * SWA_BLOCK)
            l = acc[HEAD_DIM:HEAD_DIM + 1, cols] + sink_p[slot]
            o_ref[0, t, h * HEAD_DIM:(h + 1) * HEAD_DIM, :] = acc[:HEAD_DIM, cols] / l


def _swa(sinks, qs_t, ks, vs_t, bias_swa):
    batch, nb = qs_t.shape[0], qs_t.shape[1]

    def cur(rows, cols):
        return pl.BlockSpec((1, SWA_Q_PER_STEP, rows, cols), lambda b, n: (b, n, 0, 0))

    def prev(rows, cols):
        return pl.BlockSpec((1, 1, rows, cols),
                            lambda b, n: (b, jnp.maximum(n * SWA_Q_PER_STEP - 1, 0), 0, 0))

    return pl.pallas_call(
        _swa_body,
        grid=(batch, nb // SWA_Q_PER_STEP),
        in_specs=[pl.BlockSpec(memory_space=pltpu.SMEM),
                  cur(W_SWA, SWA_BLOCK),
                  prev(SWA_BLOCK, W_SWA_KV), cur(SWA_BLOCK, W_SWA_KV),
                  prev(W_SWA_KV, SWA_BLOCK), cur(W_SWA_KV, SWA_BLOCK),
                  _const_spec((N_HEADS_SWA, 2 * SWA_BLOCK, SWA_BLOCK))],
        out_specs=cur(W_SWA, SWA_BLOCK),
        out_shape=jax.ShapeDtypeStruct((batch, nb, W_SWA, SWA_BLOCK), _F32),
        scratch_shapes=[pltpu.VMEM((SWA_Q_PER_STEP, N_KV_SWA, 2 * SWA_BLOCK, SWA_GROUP * SWA_BLOCK), _F32)],
        compiler_params=_params(2),
        name="swa",
    )(sinks, qs_t, ks, ks, vs_t, vs_t, bias_swa)


def _group_norm_rows(o_ref, g_ref):
    rows = []
    for t in range(o_ref.shape[0]):
        o_t = o_ref[t]
        scale = lax.rsqrt(jnp.mean(o_t * o_t, axis=0, keepdims=True) + RMS_EPS)
        rows.append(((o_t * scale).T * g_ref[...]).astype(_BF16))
    return jnp.concatenate(rows, axis=0)


def _mix_ffn_body(x_ref, oa_ref, ob_ref, ga_ref, gb_ref, w_out_ref, mix_post_g_ref,
                  pre_g_ref, wg_hbm, wu_hbm, wd_hbm, post_g_ref, o_ref, h_ref, *weight_scratch):
    wg_ref, wu_ref, wd_ref = _ffn_weights((wg_hbm, wu_hbm, wd_hbm), weight_scratch)
    y = (_dot(_group_norm_rows(oa_ref, ga_ref), w_out_ref[:W_MOBA, :])
         + _dot(_group_norm_rows(ob_ref, gb_ref), w_out_ref[W_MOBA:, :]))
    x = x_ref[...] + _rms_rows(y, mix_post_g_ref[...])
    _ffn_tile(lambda rows: x[rows, :], pre_g_ref, wg_ref, wu_ref, wd_ref, post_g_ref, o_ref, h_ref)


def _mix_ffn(x2d, oa_t, ob_t, ga, gb, w_out, mix_post_g, ffn_params):
    n_tok = x2d.shape[0]
    return pl.pallas_call(
        _mix_ffn_body,
        grid=(n_tok // TOKEN_TILE,),
        in_specs=[_TOKEN_TILE_SPEC,
                  pl.BlockSpec((_MOBA_PER_TILE, W_MOBA, MOBA_BLOCK), lambda i: (i, 0, 0)),
                  pl.BlockSpec((_SWA_PER_TILE, W_SWA, SWA_BLOCK), lambda i: (i, 0, 0)),
                  _const_spec((1, W_MOBA)), _const_spec((1, W_SWA)),
                  _const_spec((W_MOBA + W_SWA, D_MODEL)),
                  _const_spec((1, D_MODEL))] + _FFN_WEIGHT_SPECS,
        out_specs=_TOKEN_TILE_SPEC,
        out_shape=jax.ShapeDtypeStruct((n_tok, D_MODEL), _F32),
        scratch_shapes=_FFN_SCRATCH,
        compiler_params=_params(1),
        name="mix_ffn",
    )(x2d, oa_t, ob_t, ga, gb, w_out, mix_post_g, *ffn_params)


def _row(v):
    return v.reshape(1, -1)


def _layer(x, ffn1, mix, ffn2, rel_bias, bias_tiles):
    batch, seq, _ = x.shape
    (mix_pre_g, w_in, moba_out_g, swa_sinks, swa_out_g, w_out, mix_post_g) = mix
    bias_own, bias_prev, bias_swa = bias_tiles

    def ffn_params(params):
        pre_g, w_gate, w_up, w_down, post_g = params
        return (_row(pre_g), w_gate, w_up, w_down, _row(post_g))

    x = _ffn(x.reshape(batch * seq, D_MODEL), ffn_params(ffn1))

    qa, ka, va, qs, ks, vs = jnp.split(
        w_in, [W_MOBA, 2 * W_MOBA, 3 * W_MOBA, 3 * W_MOBA + W_SWA, 3 * W_MOBA + W_SWA + W_SWA_KV], axis=1)
    w_rows = jnp.concatenate([ka, ks], axis=1).astype(_BF16)
    w_t = jnp.concatenate([qa, va, qs, vs], axis=1).astype(_BF16).T
    qa_t, va_t, qs_t, vs_t, ka_b, ks_b, kmean = _in_proj(
        x.reshape(batch, seq, D_MODEL), _row(mix_pre_g), w_rows, w_t)

    oa_t = _moba(rel_bias, qa_t, ka_b, va_t, kmean.reshape(batch, seq // MOBA_BLOCK, W_MOBA),
                 bias_own, bias_prev)
    ob_t = _swa(swa_sinks, qs_t, ks_b, vs_t, bias_swa)

    x = _mix_ffn(x, oa_t.reshape(-1, W_MOBA, MOBA_BLOCK), ob_t.reshape(-1, W_SWA, SWA_BLOCK),
                 _row(moba_out_g), _row(swa_out_g), w_out.astype(_BF16), _row(mix_post_g),
                 ffn_params(ffn2))
    return x.reshape(batch, seq, D_MODEL)


def kernel(x, ffn1_pre_g, ffn1_w_gate, ffn1_w_up, ffn1_w_down, ffn1_post_g, mix_pre_g, w_in, rel_bias,
           moba_out_g, swa_sinks, swa_out_g, w_out, mix_post_g, ffn2_pre_g, ffn2_w_gate, ffn2_w_up,
           ffn2_w_down, ffn2_post_g):
    bias_tiles = _bias_tiles(rel_bias)
    for l in range(ffn1_pre_g.shape[0]):
        x = _layer(
            x,
            (ffn1_pre_g[l], ffn1_w_gate[l], ffn1_w_up[l], ffn1_w_down[l], ffn1_post_g[l]),
            (mix_pre_g[l], w_in[l], moba_out_g[l], swa_sinks[l], swa_out_g[l], w_out[l], mix_post_g[l]),
            (ffn2_pre_g[l], ffn2_w_gate[l], ffn2_w_up[l], ffn2_w_down[l], ffn2_post_g[l]),
            rel_bias, bias_tiles)
    return x
```

```python
import math

import jax
import jax.numpy as jnp
from jax import lax
from jax.experimental import pallas as pl
from jax.experimental.pallas import tpu as pltpu

D_MODEL = 1024
HEAD_DIM = 64
N_HEADS_MOBA = 8
N_HEADS_SWA = 8
N_KV_SWA = 2
SWA_GROUP = N_HEADS_SWA // N_KV_SWA
W_MOBA = N_HEADS_MOBA * HEAD_DIM
W_SWA = N_HEADS_SWA * HEAD_DIM
W_SWA_KV = N_KV_SWA * HEAD_DIM
MOBA_BLOCK = 256
MOBA_TOPK = 3
SWA_BLOCK = 128
SWA_WINDOW = 128
NUM_BUCKETS = 32
MAX_DISTANCE = 128
D_FF = 2816
RMS_EPS = 1e-6
FFN_RES_WEIGHT = 0.5
NEG = -1e30
QK_SCALE = HEAD_DIM ** -0.5
LOG2E = math.log2(math.e)
Q_FOLD = QK_SCALE * LOG2E
SUM_ROWS = 16

TOKEN_TILE = 1024
FF_CHUNK = 256
FFN_ROW_PARTS = 2
CAST_CHUNKS = 16
CAST_SLOTS = 4
HEADS_PER_STEP = 4
HEADS_PER_KEY_TILE = 2
PIPE_DEPTH = 2
GROUPS_PER_ITER = 4
RING_UNROLL = 2
SWA_Q_PER_STEP = 4
VMEM_LIMIT_BYTES = 58 * 1024 * 1024

_BF16 = jnp.bfloat16
_F32 = jnp.float32


def _dot(a, b):
    return jnp.dot(a, b, preferred_element_type=_F32)


def _dot_nt(a, b):
    return lax.dot_general(a, b, (((1,), (1,)), ((), ())), preferred_element_type=_F32)


def _rms_rows(x, g):
    return x * lax.rsqrt(jnp.mean(x * x, axis=-1, keepdims=True) + RMS_EPS) * g


def _const_spec(shape):
    return pl.BlockSpec(shape, lambda *_: (0,) * len(shape), pipeline_mode=pl.Buffered(1))


def _params(n_axes):
    return pltpu.CompilerParams(
        dimension_semantics=("arbitrary",) * n_axes, vmem_limit_bytes=VMEM_LIMIT_BYTES)


def _ffn_tile(x_rows, pre_g_ref, wg_ref, wu_ref, wd_ref, post_g_ref, o_ref, h_ref):
    part = TOKEN_TILE // FFN_ROW_PARTS
    for p in range(FFN_ROW_PARTS):
        rows = slice(p * part, (p + 1) * part)
        x = x_rows(rows)
        xn = _rms_rows(x, pre_g_ref[...]).astype(_BF16)
        for c in range(D_FF // FF_CHUNK):
            cols = slice(c * FF_CHUNK, (c + 1) * FF_CHUNK)
            gate = _dot(xn, wg_ref[:, cols])
            up = _dot(xn, wu_ref[:, cols])
            h_ref[rows, cols] = (jax.nn.silu(gate) * up).astype(_BF16)
        y = _dot(h_ref[rows, :], wd_ref[...])
        o_ref[rows, :] = x + FFN_RES_WEIGHT * _rms_rows(y, post_g_ref[...])


def _cast_weights(pairs, stage, sem):
    slots, rows = stage.shape[0], stage.shape[1]
    chunks = [(w_hbm, w_bf, k) for w_hbm, w_bf in pairs for k in range(w_hbm.shape[0] // rows)]

    def copy(c):
        w_hbm, _, k = chunks[c]
        return pltpu.make_async_copy(w_hbm.at[pl.ds(k * rows, rows), :], stage.at[c % slots], sem.at[c % slots])

    def prime():
        for c in range(min(slots, len(chunks))):
            copy(c).start()

    def drain():
        for c, (_, w_bf, k) in enumerate(chunks):
            copy(c).wait()
            w_bf[k * rows:(k + 1) * rows, :] = stage[c % slots].astype(_BF16)
            if c + slots < len(chunks):
                copy(c + slots).start()

    return prime, drain


def _ffn_weights(w_hbm, scratch):
    wg_bf, wu_bf, wd_bf, stage_in, stage_down, sem_in, sem_down = scratch

    @pl.when(pl.program_id(0) == 0)
    def _():
        prime_in, drain_in = _cast_weights([(w_hbm[0], wg_bf), (w_hbm[1], wu_bf)], stage_in, sem_in)
        prime_down, drain_down = _cast_weights([(w_hbm[2], wd_bf)], stage_down, sem_down)
        prime_in()
        prime_down()
        drain_in()
        drain_down()

    return wg_bf, wu_bf, wd_bf


def _ffn_body(x_ref, pre_g_ref, wg_hbm, wu_hbm, wd_hbm, post_g_ref, o_ref, h_ref, *weight_scratch):
    wg_ref, wu_ref, wd_ref = _ffn_weights((wg_hbm, wu_hbm, wd_hbm), weight_scratch)
    _ffn_tile(lambda rows: x_ref[rows, :], pre_g_ref, wg_ref, wu_ref, wd_ref, post_g_ref, o_ref, h_ref)


_TOKEN_TILE_SPEC = pl.BlockSpec((TOKEN_TILE, D_MODEL), lambda i: (i, 0))
_HBM_SPEC = pl.BlockSpec(memory_space=pl.ANY)
_FFN_WEIGHT_SPECS = [_const_spec((1, D_MODEL)), _HBM_SPEC, _HBM_SPEC, _HBM_SPEC, _const_spec((1, D_MODEL))]
_FFN_SCRATCH = [
    pltpu.VMEM((TOKEN_TILE, D_FF), _BF16),
    pltpu.VMEM((D_MODEL, D_FF), _BF16),
    pltpu.VMEM((D_MODEL, D_FF), _BF16),
    pltpu.VMEM((D_FF, D_MODEL), _BF16),
    pltpu.VMEM((CAST_SLOTS, D_MODEL // CAST_CHUNKS, D_FF), _F32),
    pltpu.VMEM((CAST_SLOTS, D_FF // CAST_CHUNKS, D_MODEL), _F32),
    pltpu.SemaphoreType.DMA((CAST_SLOTS,)),
    pltpu.SemaphoreType.DMA((CAST_SLOTS,)),
]


def _ffn(x2d, ffn_params):
    n_tok = x2d.shape[0]
    return pl.pallas_call(
        _ffn_body,
        grid=(n_tok // TOKEN_TILE,),
        in_specs=[_TOKEN_TILE_SPEC] + _FFN_WEIGHT_SPECS,
        out_specs=_TOKEN_TILE_SPEC,
        out_shape=jax.ShapeDtypeStruct((n_tok, D_MODEL), _F32),
        scratch_shapes=_FFN_SCRATCH,
        compiler_params=_params(1),
        name="ffn",
    )(x2d, *ffn_params)


_T_QA, _T_VA, _T_QS, _T_VS = 0, W_MOBA, 2 * W_MOBA, 2 * W_MOBA + W_SWA
_T_ROWS = 2 * W_MOBA + W_SWA + W_SWA_KV
_MOBA_PER_TILE = TOKEN_TILE // MOBA_BLOCK
_SWA_PER_TILE = TOKEN_TILE // SWA_BLOCK


def _in_proj_body(x_ref, g_ref, w_rows_ref, w_t_ref,
                  qa_ref, va_ref, qs_ref, vs_ref, ka_ref, ks_ref, kmean_ref):
    swa_per_moba = MOBA_BLOCK // SWA_BLOCK
    for t in range(_MOBA_PER_TILE):
        xn = _rms_rows(x_ref[0, t * MOBA_BLOCK:(t + 1) * MOBA_BLOCK, :], g_ref[...]).astype(_BF16)
        k_rows = _dot(xn, w_rows_ref[...])
        proj_t = _dot_nt(w_t_ref[...], xn)
        k_blk = k_rows[:, :W_MOBA]
        ka_ref[0, t] = k_blk.astype(_BF16)
        kmean_ref[0, t] = jnp.mean(k_blk, axis=0, keepdims=True)
        qa_ref[0, t] = (proj_t[_T_QA:_T_QA + W_MOBA, :] * Q_FOLD).astype(_BF16)
        va_ref[0, t] = proj_t[_T_VA:_T_VA + W_MOBA, :].astype(_BF16)
        for u in range(swa_per_moba):
            rows = slice(u * SWA_BLOCK, (u + 1) * SWA_BLOCK)
            ks_ref[0, t * swa_per_moba + u] = k_rows[rows, W_MOBA:].astype(_BF16)
            qs_ref[0, t * swa_per_moba + u] = (proj_t[_T_QS:_T_QS + W_SWA, rows] * Q_FOLD).astype(_BF16)
            vs_ref[0, t * swa_per_moba + u] = proj_t[_T_VS:_T_VS + W_SWA_KV, rows].astype(_BF16)


def _in_proj(x, g, w_rows, w_t):
    batch, seq, _ = x.shape
    nb_moba, nb_swa = seq // MOBA_BLOCK, seq // SWA_BLOCK

    def blocked(n_per_tile, rows, cols):
        return pl.BlockSpec((1, n_per_tile, rows, cols), lambda b, i: (b, i, 0, 0))

    return pl.pallas_call(
        _in_proj_body,
        grid=(batch, seq // TOKEN_TILE),
        in_specs=[pl.BlockSpec((1, TOKEN_TILE, D_MODEL), lambda b, i: (b, i, 0)),
                  _const_spec((1, D_MODEL)),
                  _const_spec((D_MODEL, W_MOBA + W_SWA_KV)),
                  _const_spec((_T_ROWS, D_MODEL))],
        out_specs=[blocked(_MOBA_PER_TILE, W_MOBA, MOBA_BLOCK),
                   blocked(_MOBA_PER_TILE, W_MOBA, MOBA_BLOCK),
                   blocked(_SWA_PER_TILE, W_SWA, SWA_BLOCK),
                   blocked(_SWA_PER_TILE, W_SWA_KV, SWA_BLOCK),
                   blocked(_MOBA_PER_TILE, MOBA_BLOCK, W_MOBA),
                   blocked(_SWA_PER_TILE, SWA_BLOCK, W_SWA_KV),
                   blocked(_MOBA_PER_TILE, 1, W_MOBA)],
        out_shape=[jax.ShapeDtypeStruct((batch, nb_moba, W_MOBA, MOBA_BLOCK), _BF16),
                   jax.ShapeDtypeStruct((batch, nb_moba, W_MOBA, MOBA_BLOCK), _BF16),
                   jax.ShapeDtypeStruct((batch, nb_swa, W_SWA, SWA_BLOCK), _BF16),
                   jax.ShapeDtypeStruct((batch, nb_swa, W_SWA_KV, SWA_BLOCK), _BF16),
                   jax.ShapeDtypeStruct((batch, nb_moba, MOBA_BLOCK, W_MOBA), _BF16),
                   jax.ShapeDtypeStruct((batch, nb_swa, SWA_BLOCK, W_SWA_KV), _BF16),
                   jax.ShapeDtypeStruct((batch, nb_moba, 1, W_MOBA), _F32)],
        compiler_params=_params(2),
        name="in_proj",
    )(x, g, w_rows, w_t)


def _t5_bucket(dist):
    n = jnp.maximum(dist, 0)
    max_exact = NUM_BUCKETS // 2
    nf = jnp.maximum(n, 1).astype(_F32)
    large = max_exact + jnp.floor(jnp.log(nf / max_exact) / math.log(MAX_DISTANCE / max_exact)
                                  * (NUM_BUCKETS - max_exact))
    large = jnp.minimum(large, float(NUM_BUCKETS - 1))
    return jnp.where(n < max_exact, n.astype(_F32), large)


def _bias_lookup(rel_bias_ref, bucket, head):
    val = jnp.full(bucket.shape, rel_bias_ref[NUM_BUCKETS - 1, head], _F32)
    for b in range(NUM_BUCKETS - 2, -1, -1):
        val = jnp.where(bucket == float(b), rel_bias_ref[b, head], val)
    return val


def _bias_tiles_body(rel_bias_ref, own_ref, prev_ref, swa_ref):
    assert MOBA_BLOCK == 2 * SWA_BLOCK and SWA_WINDOW == SWA_BLOCK and MAX_DISTANCE <= SWA_BLOCK
    h = pl.program_id(0)
    q = SWA_BLOCK
    key = lax.broadcasted_iota(jnp.int32, (q, q), 0)
    qry = lax.broadcasted_iota(jnp.int32, (q, q), 1)
    dist = qry - key

    def quadrants(head):
        near = jnp.where(dist >= 0, _bias_lookup(rel_bias_ref, _t5_bucket(dist), head) * LOG2E, NEG)
        nxt = _bias_lookup(rel_bias_ref, _t5_bucket(dist + q), head) * LOG2E
        return near, nxt

    near, nxt = quadrants(h)
    far = jnp.full((q, q), rel_bias_ref[NUM_BUCKETS - 1, h] * LOG2E, _F32)
    masked = jnp.full((q, q), NEG, _F32)
    own_ref[0, :q, :q] = near
    own_ref[0, :q, q:] = nxt
    own_ref[0, q:, :q] = masked
    own_ref[0, q:, q:] = near
    prev_ref[0, :q, :q] = far
    prev_ref[0, :q, q:] = far
    prev_ref[0, q:, :q] = nxt
    prev_ref[0, q:, q:] = far
    near, nxt = quadrants(h + N_HEADS_MOBA)
    swa_ref[0, :q, :] = jnp.where(dist < 0, nxt, NEG)
    swa_ref[0, q:, :] = near


def _bias_tiles(rel_bias):
    def per_head(rows, cols):
        return pl.BlockSpec((1, rows, cols), lambda h: (h, 0, 0))

    return pl.pallas_call(
        _bias_tiles_body,
        grid=(N_HEADS_MOBA,),
        in_specs=[pl.BlockSpec(memory_space=pltpu.SMEM)],
        out_specs=[per_head(MOBA_BLOCK, MOBA_BLOCK), per_head(MOBA_BLOCK, MOBA_BLOCK),
                   per_head(2 * SWA_BLOCK, SWA_BLOCK)],
        out_shape=[jax.ShapeDtypeStruct((N_HEADS_MOBA, MOBA_BLOCK, MOBA_BLOCK), _F32),
                   jax.ShapeDtypeStruct((N_HEADS_MOBA, MOBA_BLOCK, MOBA_BLOCK), _F32),
                   jax.ShapeDtypeStruct((N_HEADS_SWA, 2 * SWA_BLOCK, SWA_BLOCK), _F32)],
        compiler_params=_params(1),
        name="bias_tiles",
    )(rel_bias)


def _moba_body(rel_bias_ref, q_ref, k_ref, v_ref, kmean_ref, own_ref, prev_ref, o_ref,
               far_ref, qpad_ref, s_ref, acc_ref):
    g = pl.program_id(1)
    nb = k_ref.shape[1]
    key_lanes = HEADS_PER_KEY_TILE * HEAD_DIM
    q_row = lax.broadcasted_iota(jnp.int32, (key_lanes, MOBA_BLOCK), 0)
    km = kmean_ref[0]
    km_hi = km.astype(_BF16)
    km_lo = (km - km_hi.astype(_F32)).astype(_BF16)
    blk = lax.broadcasted_iota(jnp.int32, (nb, MOBA_BLOCK), 0)
    ones = jnp.ones((SUM_ROWS, MOBA_BLOCK), _BF16)
    ring_slots = range(GROUPS_PER_ITER)
    tail_slots = ring_slots[-PIPE_DEPTH:]
    near_slots = range(GROUPS_PER_ITER, GROUPS_PER_ITER + PIPE_DEPTH)
    assert PIPE_DEPTH == 2 and GROUPS_PER_ITER >= 2 * PIPE_DEPTH

    def key_tile(r):
        t, sub = divmod(r, HEADS_PER_KEY_TILE)
        return slice(t * key_lanes, (t + 1) * key_lanes), sub

    def prepare(i):
        past = blk < i
        own, prev, prev_sel = [], [], []
        for r in range(HEADS_PER_STEP):
            tile_lanes, sub = key_tile(r)
            q_t = q_ref[0, i, tile_lanes, :]
            q_pad = jnp.where((q_row >= sub * HEAD_DIM) & (q_row < (sub + 1) * HEAD_DIM), q_t, 0)
            qpad_ref[r] = q_pad

            gate = jnp.where(
                past, _dot(km_hi[:, tile_lanes], q_pad) + _dot(km_lo[:, tile_lanes], q_pad), NEG)
            sel = jnp.zeros(gate.shape, jnp.bool_)
            for _ in range(MOBA_TOPK):
                top = jnp.max(gate, axis=0, keepdims=True)
                first = jnp.min(jnp.where(gate == top, blk, nb), axis=0, keepdims=True)
                pick = blk == first
                sel = sel | pick
                gate = jnp.where(pick, -jnp.inf, gate)
            sel = sel & past

            far_bias = rel_bias_ref[NUM_BUCKETS - 1, g * HEADS_PER_STEP + r] * LOG2E
            far_ref[r] = jnp.where(sel & (blk < i - 1), far_bias, NEG)
            prev_sel.append(jnp.max(jnp.where(sel & (blk == i - 1), 0.0, NEG), axis=0, keepdims=True))
            own.append(own_ref[r])
            prev.append(prev_ref[r])
        return own, prev, prev_sel

    def score(slot, j, tile_bias=None, row_bias=None):
        tops, shifts = [], []
        for r in range(HEADS_PER_STEP):
            s = _dot(k_ref[0, j, :, key_tile(r)[0]], qpad_ref[r])
            if tile_bias is not None:
                s = s + tile_bias[r]
            s_ref[slot, r] = s
            top = jnp.max(s, axis=0, keepdims=True)
            shift = jnp.zeros_like(top) if row_bias is None else row_bias[r]
            tops.append(top + shift)
            shifts.append(shift)
        return slot, j, tuple(tops), tuple(shifts)

    def absorb(carry, scored):
        slot, j, tops, shifts = scored
        out = []
        for r in range(HEADS_PER_STEP):
            lanes = slice(r * HEAD_DIM, (r + 1) * HEAD_DIM)
            m = carry[r]
            m_new = jnp.maximum(m, tops[r])
            p = jnp.exp2(s_ref[slot, r] + (shifts[r] - m_new)).astype(_BF16)
            pv = _dot(jnp.concatenate([v_ref[0, j, lanes, :], ones], axis=0), p)
            acc_ref[r] = jnp.exp2(m - m_new) * acc_ref[r] + pv
            out.append(m_new)
        return tuple(out)

    def score_far(slot, grp):
        j = jnp.minimum(grp, nb - 1)
        return score(slot, j, row_bias=[far_ref[r, pl.ds(j, 1), :] for r in range(HEADS_PER_STEP)])

    def score_near(i):
        own, prev, prev_sel = prepare(i)
        return (score(near_slots[0], i, tile_bias=own)[2:],
                score(near_slots[1], jnp.maximum(i - 1, 0), tile_bias=prev, row_bias=prev_sel)[2:])

    def start(i, near):
        carry = tuple(jnp.full((1, MOBA_BLOCK), -jnp.inf, _F32) for _ in range(HEADS_PER_STEP))
        acc_ref[...] = jnp.zeros(acc_ref.shape, _F32)
        near_js = (i, jnp.maximum(i - 1, 0))
        ring = {slot: (slot, j) + maxes_shifts for slot, j, maxes_shifts in zip(near_slots, near_js, near)}
        for c, slot in enumerate(tail_slots):
            ring[slot] = score_far(slot, c)
            carry = absorb(carry, ring[near_slots[c]])
        return carry, tuple(ring[slot][1:] for slot in tail_slots)

    def tile(i, phase, state, last=False):
        def ring_steps(first_group, count, carry, pending):
            ring = {slot: (slot,) + p for slot, p in zip(tail_slots, pending)}
            for k in range(count):
                scored = score_far(k % GROUPS_PER_ITER, first_group + k)
                carry = absorb(carry, ring[(k - PIPE_DEPTH) % GROUPS_PER_ITER])
                ring[k % GROUPS_PER_ITER] = scored
            return carry, ring

        def ring_loop(first_group, turns, trips, state):
            def step(it, state):
                groups = turns * GROUPS_PER_ITER
                carry, ring = ring_steps(first_group + it * groups, groups, *state)
                return carry, tuple(ring[slot][1:] for slot in tail_slots)
            return lax.fori_loop(0, trips, step, state)

        leftover = (phase - 1 - PIPE_DEPTH) % GROUPS_PER_ITER
        n_turns = jnp.maximum(i - 1 - PIPE_DEPTH, 0) // GROUPS_PER_ITER
        n_long = n_turns // RING_UNROLL
        done = n_long * RING_UNROLL
        state = ring_loop(PIPE_DEPTH, RING_UNROLL, n_long, state)
        state = ring_loop(PIPE_DEPTH + done * GROUPS_PER_ITER, 1, n_turns - done, state)
        carry, ring = ring_steps(PIPE_DEPTH + n_turns * GROUPS_PER_ITER, leftover, *state)

        near_next = None if last else score_near(i + 1)
        for c in range(PIPE_DEPTH):
            carry = absorb(carry, ring[(leftover - PIPE_DEPTH + c) % GROUPS_PER_ITER])
        for r in range(HEADS_PER_STEP):
            acc = acc_ref[r]
            o_ref[0, i, r * HEAD_DIM:(r + 1) * HEAD_DIM, :] = acc[:HEAD_DIM] / acc[HEAD_DIM:HEAD_DIM + 1]
        return None if last else start(i + 1, near_next)

    def tiles(a, state, last=False):
        for phase in range(GROUPS_PER_ITER):
            state = tile(a * GROUPS_PER_ITER + phase, phase, state,
                         last=last and phase == GROUPS_PER_ITER - 1)
        return state

    assert nb % GROUPS_PER_ITER == 0
    state = lax.fori_loop(0, nb // GROUPS_PER_ITER - 1, tiles, start(0, score_near(0)))
    tiles(nb // GROUPS_PER_ITER - 1, state, last=True)


def _moba(rel_bias, qa_t, ka, va_t, kmean, bias_own, bias_prev):
    batch, nb = qa_t.shape[0], qa_t.shape[1]
    width = HEADS_PER_STEP * HEAD_DIM
    q_spec = pl.BlockSpec((1, nb, width, MOBA_BLOCK), lambda b, g: (b, 0, g, 0))
    bias_spec = pl.BlockSpec((HEADS_PER_STEP, MOBA_BLOCK, MOBA_BLOCK), lambda b, g: (g, 0, 0))
    return pl.pallas_call(
        _moba_body,
        grid=(batch, N_HEADS_MOBA // HEADS_PER_STEP),
        in_specs=[pl.BlockSpec(memory_space=pltpu.SMEM),
                  q_spec,
                  pl.BlockSpec((1, nb, MOBA_BLOCK, width), lambda b, g: (b, 0, 0, g)),
                  q_spec,
                  pl.BlockSpec((1, nb, width), lambda b, g: (b, 0, g)),
                  bias_spec, bias_spec],
        out_specs=q_spec,
        out_shape=jax.ShapeDtypeStruct((batch, nb, W_MOBA, MOBA_BLOCK), _F32),
        scratch_shapes=[pltpu.VMEM((HEADS_PER_STEP, nb, MOBA_BLOCK), _F32),
                        pltpu.VMEM((HEADS_PER_STEP, HEADS_PER_KEY_TILE * HEAD_DIM, MOBA_BLOCK), _BF16),
                        pltpu.VMEM((GROUPS_PER_ITER + PIPE_DEPTH, HEADS_PER_STEP, MOBA_BLOCK, MOBA_BLOCK), _F32),
                        pltpu.VMEM((HEADS_PER_STEP, HEAD_DIM + SUM_ROWS, MOBA_BLOCK), _F32)],
        compiler_params=_params(2),
        name="moba",
    )(rel_bias, qa_t, ka, va_t, kmean, bias_own, bias_prev)


def _swa_body(sinks_ref, q_ref, k_prev_ref, k_cur_ref, v_prev_ref, v_cur_ref, bias_ref, o_ref, s_ref):
    no_prev = jnp.where(pl.program_id(1) == 0, NEG, 0.0)
    window_row = lax.broadcasted_iota(jnp.int32, (2 * SWA_BLOCK, SWA_BLOCK), 0)
    first_window_mask = jnp.where(window_row < SWA_BLOCK, no_prev, 0.0)
    zeros = jnp.zeros((HEAD_DIM, SWA_BLOCK), _BF16)
    ones = jnp.ones((SUM_ROWS, 2 * SWA_BLOCK), _BF16)
    k_blocks = [k_prev_ref[0, 0]] + [k_cur_ref[0, t] for t in range(SWA_Q_PER_STEP)]
    v_blocks = [v_prev_ref[0, 0]] + [v_cur_ref[0, t] for t in range(SWA_Q_PER_STEP)]
    chains = [(t, kv) for t in range(SWA_Q_PER_STEP) for kv in range(N_KV_SWA)]

    for t, kv in chains:
        q_pad = jnp.concatenate(
            [jnp.concatenate(
                [q_ref[0, t, h * HEAD_DIM:(h + 1) * HEAD_DIM, :] if part == kv else zeros
                 for part in range(N_KV_SWA)], axis=0)
             for h in range(kv * SWA_GROUP, (kv + 1) * SWA_GROUP)], axis=1)
        window_keys = jnp.concatenate([k_blocks[t], k_blocks[t + 1]], axis=0)
        s = _dot(window_keys, q_pad)
        for slot in range(SWA_GROUP):
            cols = slice(slot * SWA_BLOCK, (slot + 1) * SWA_BLOCK)
            biased = s[:, cols] + bias_ref[kv * SWA_GROUP + slot]
            s_ref[t, kv, :, cols] = biased + first_window_mask if t == 0 else biased

    for t, kv in chains:
        kv_rows = slice(kv * HEAD_DIM, (kv + 1) * HEAD_DIM)
        window_values = jnp.concatenate(
            [jnp.concatenate([v_blocks[t][kv_rows, :], v_blocks[t + 1][kv_rows, :]], axis=1), ones], axis=0)
        p, sink_p = [], []
        for slot in range(SWA_GROUP):
            cols = slice(slot * SWA_BLOCK, (slot + 1) * SWA_BLOCK)
            sink = sinks_ref[kv * SWA_GROUP + slot] * LOG2E
            m = jnp.maximum(jnp.max(s_ref[t, kv, :, cols], axis=0, keepdims=True), sink)
            p.append(jnp.exp2(s_ref[t, kv, :, cols] - m).astype(_BF16))
            sink_p.append(jnp.exp2(sink - m))
        acc = _dot(window_values, jnp.concatenate(p, axis=1))
        for slot in range(SWA_GROUP):
            h = kv * SWA_GROUP + slot
            cols = slice(slot * SWA_BLOCK, (slot + 1) * SWA_BLOCK)
            l = acc[HEAD_DIM:HEAD_DIM + 1, cols] + sink_p[slot]
            o_ref[0, t, h * HEAD_DIM:(h + 1) * HEAD_DIM, :] = acc[:HEAD_DIM, cols] / l


def _swa(sinks, qs_t, ks, vs_t, bias_swa):
    batch, nb = qs_t.shape[0], qs_t.shape[1]

    def cur(rows, cols):
        return pl.BlockSpec((1, SWA_Q_PER_STEP, rows, cols), lambda b, n: (b, n, 0, 0))

    def prev(rows, cols):
        return pl.BlockSpec((1, 1, rows, cols),
                            lambda b, n: (b, jnp.maximum(n * SWA_Q_PER_STEP - 1, 0), 0, 0))

    return pl.pallas_call(
        _swa_body,
        grid=(batch, nb // SWA_Q_PER_STEP),
        in_specs=[pl.BlockSpec(memory_space=pltpu.SMEM),
                  cur(W_SWA, SWA_BLOCK),
                  prev(SWA_BLOCK, W_SWA_KV), cur(SWA_BLOCK, W_SWA_KV),
                  prev(W_SWA_KV, SWA_BLOCK), cur(W_SWA_KV, SWA_BLOCK),
                  _const_spec((N_HEADS_SWA, 2 * SWA_BLOCK, SWA_BLOCK))],
        out_specs=cur(W_SWA, SWA_BLOCK),
        out_shape=jax.ShapeDtypeStruct((batch, nb, W_SWA, SWA_BLOCK), _F32),
        scratch_shapes=[pltpu.VMEM((SWA_Q_PER_STEP, N_KV_SWA, 2 * SWA_BLOCK, SWA_GROUP * SWA_BLOCK), _F32)],
        compiler_params=_params(2),
        name="swa",
    )(sinks, qs_t, ks, ks, vs_t, vs_t, bias_swa)


def _group_norm_rows(o_ref, g_ref):
    rows = []
    for t in range(o_ref.shape[0]):
        o_t = o_ref[t]
        scale = lax.rsqrt(jnp.mean(o_t * o_t, axis=0, keepdims=True) + RMS_EPS)
        rows.append(((o_t * scale).T * g_ref[...]).astype(_BF16))
    return jnp.concatenate(rows, axis=0)


def _mix_ffn_body(x_ref, oa_ref, ob_ref, ga_ref, gb_ref, w_out_ref, mix_post_g_ref,
                  pre_g_ref, wg_hbm, wu_hbm, wd_hbm, post_g_ref, o_ref, h_ref, *weight_scratch):
    wg_ref, wu_ref, wd_ref = _ffn_weights((wg_hbm, wu_hbm, wd_hbm), weight_scratch)
    y = (_dot(_group_norm_rows(oa_ref, ga_ref), w_out_ref[:W_MOBA, :])
         + _dot(_group_norm_rows(ob_ref, gb_ref), w_out_ref[W_MOBA:, :]))
    x = x_ref[...] + _rms_rows(y, mix_post_g_ref[...])
    _ffn_tile(lambda rows: x[rows, :], pre_g_ref, wg_ref, wu_ref, wd_ref, post_g_ref, o_ref, h_ref)


def _mix_ffn(x2d, oa_t, ob_t, ga, gb, w_out, mix_post_g, ffn_params):
    n_tok = x2d.shape[0]
    return pl.pallas_call(
        _mix_ffn_body,
        grid=(n_tok // TOKEN_TILE,),
        in_specs=[_TOKEN_TILE_SPEC,
                  pl.BlockSpec((_MOBA_PER_TILE, W_MOBA, MOBA_BLOCK), lambda i: (i, 0, 0)),
                  pl.BlockSpec((_SWA_PER_TILE, W_SWA, SWA_BLOCK), lambda i: (i, 0, 0)),
                  _const_spec((1, W_MOBA)), _const_spec((1, W_SWA)),
                  _const_spec((W_MOBA + W_SWA, D_MODEL)),
                  _const_spec((1, D_MODEL))] + _FFN_WEIGHT_SPECS,
        out_specs=_TOKEN_TILE_SPEC,
        out_shape=jax.ShapeDtypeStruct((n_tok, D_MODEL), _F32),
        scratch_shapes=_FFN_SCRATCH,
        compiler_params=_params(1),
        name="mix_ffn",
    )(x2d, oa_t, ob_t, ga, gb, w_out, mix_post_g, *ffn_params)


def _row(v):
    return v.reshape(1, -1)


def _layer(x, ffn1, mix, ffn2, rel_bias, bias_tiles):
    batch, seq, _ = x.shape
    (mix_pre_g, w_in, moba_out_g, swa_sinks, swa_out_g, w_out, mix_post_g) = mix
    bias_own, bias_prev, bias_swa = bias_tiles

    def ffn_params(params):
        pre_g, w_gate, w_up, w_down, post_g = params
        return (_row(pre_g), w_gate, w_up, w_down, _row(post_g))

    x = _ffn(x.reshape(batch * seq, D_MODEL), ffn_params(ffn1))

    qa, ka, va, qs, ks, vs = jnp.split(
        w_in, [W_MOBA, 2 * W_MOBA, 3 * W_MOBA, 3 * W_MOBA + W_SWA, 3 * W_MOBA + W_SWA + W_SWA_KV], axis=1)
    w_rows = jnp.concatenate([ka, ks], axis=1).astype(_BF16)
    w_t = jnp.concatenate([qa, va, qs, vs], axis=1).astype(_BF16).T
    qa_t, va_t, qs_t, vs_t, ka_b, ks_b, kmean = _in_proj(
        x.reshape(batch, seq, D_MODEL), _row(mix_pre_g), w_rows, w_t)

    oa_t = _moba(rel_bias, qa_t, ka_b, va_t, kmean.reshape(batch, seq // MOBA_BLOCK, W_MOBA),
                 bias_own, bias_prev)
    ob_t = _swa(swa_sinks, qs_t, ks_b, vs_t, bias_swa)

    x = _mix_ffn(x, oa_t.reshape(-1, W_MOBA, MOBA_BLOCK), ob_t.reshape(-1, W_SWA, SWA_BLOCK),
                 _row(moba_out_g), _row(swa_out_g), w_out.astype(_BF16), _row(mix_post_g),
                 ffn_params(ffn2))
    return x.reshape(batch, seq, D_MODEL)


def kernel(x, ffn1_pre_g, ffn1_w_gate, ffn1_w_up, ffn1_w_down, ffn1_post_g, mix_pre_g, w_in, rel_bias,
           moba_out_g, swa_sinks, swa_out_g, w_out, mix_post_g, ffn2_pre_g, ffn2_w_gate, ffn2_w_up,
           ffn2_w_down, ffn2_post_g):
    bias_tiles = _bias_tiles(rel_bias)
    for l in range(ffn1_pre_g.shape[0]):
        x = _layer(
            x,
            (ffn1_pre_g[l], ffn1_w_gate[l], ffn1_w_up[l], ffn1_w_down[l], ffn1_post_g[l]),
            (mix_pre_g[l], w_in[l], moba_out_g[l], swa_sinks[l], swa_out_g[l], w_out[l], mix_post_g[l]),
            (ffn2_pre_g[l], ffn2_w_gate[l], ffn2_w_up[l], ffn2_w_down[l], ffn2_post_g[l]),
            rel_bias, bias_tiles)
    return x
```

```python
import math

import jax
import jax.numpy as jnp
from jax import lax
from jax.experimental import pallas as pl
from jax.experimental.pallas import tpu as pltpu

D_MODEL = 1024
HEAD_DIM = 64
N_HEADS_MOBA = 8
N_HEADS_SWA = 8
N_KV_SWA = 2
SWA_GROUP = N_HEADS_SWA // N_KV_SWA
W_MOBA = N_HEADS_MOBA * HEAD_DIM
W_SWA = N_HEADS_SWA * HEAD_DIM
W_SWA_KV = N_KV_SWA * HEAD_DIM
MOBA_BLOCK = 256
MOBA_TOPK = 3
SWA_BLOCK = 128
SWA_WINDOW = 128
NUM_BUCKETS = 32
MAX_DISTANCE = 128
D_FF = 2816
RMS_EPS = 1e-6
FFN_RES_WEIGHT = 0.5
NEG = -1e30
QK_SCALE = HEAD_DIM ** -0.5
LOG2E = math.log2(math.e)
Q_FOLD = QK_SCALE * LOG2E
SUM_ROWS = 16

TOKEN_TILE = 1024
FF_CHUNK = 256
FFN_ROW_PARTS = 2
CAST_CHUNKS = 16
CAST_SLOTS = 4
HEADS_PER_STEP = 4
HEADS_PER_KEY_TILE = 2
PIPE_DEPTH = 2
GROUPS_PER_ITER = 4
RING_UNROLL = 2
SWA_Q_PER_STEP = 4
VMEM_LIMIT_BYTES = 58 * 1024 * 1024

_BF16 = jnp.bfloat16
_F32 = jnp.float32


def _dot(a, b):
    return jnp.dot(a, b, preferred_element_type=_F32)


def _dot_nt(a, b):
    return lax.dot_general(a, b, (((1,), (1,)), ((), ())), preferred_element_type=_F32)


def _rms_rows(x, g):
    return x * lax.rsqrt(jnp.mean(x * x, axis=-1, keepdims=True) + RMS_EPS) * g


def _const_spec(shape):
    return pl.BlockSpec(shape, lambda *_: (0,) * len(shape), pipeline_mode=pl.Buffered(1))


def _params(n_axes):
    return pltpu.CompilerParams(
        dimension_semantics=("arbitrary",) * n_axes, vmem_limit_bytes=VMEM_LIMIT_BYTES)


def _ffn_tile(x_rows, pre_g_ref, wg_ref, wu_ref, wd_ref, post_g_ref, o_ref, h_ref):
    part = TOKEN_TILE // FFN_ROW_PARTS
    for p in range(FFN_ROW_PARTS):
        rows = slice(p * part, (p + 1) * part)
        x = x_rows(rows)
        xn = _rms_rows(x, pre_g_ref[...]).astype(_BF16)
        for c in range(D_FF // FF_CHUNK):
            cols = slice(c * FF_CHUNK, (c + 1) * FF_CHUNK)
            gate = _dot(xn, wg_ref[:, cols])
            up = _dot(xn, wu_ref[:, cols])
            h_ref[rows, cols] = (jax.nn.silu(gate) * up).astype(_BF16)
        y = _dot(h_ref[rows, :], wd_ref[...])
        o_ref[rows, :] = x + FFN_RES_WEIGHT * _rms_rows(y, post_g_ref[...])


def _cast_weights(pairs, stage, sem):
    slots, rows = stage.shape[0], stage.shape[1]
    chunks = [(w_hbm, w_bf, k) for w_hbm, w_bf in pairs for k in range(w_hbm.shape[0] // rows)]

    def copy(c):
        w_hbm, _, k = chunks[c]
        return pltpu.make_async_copy(w_hbm.at[pl.ds(k * rows, rows), :], stage.at[c % slots], sem.at[c % slots])

    def prime():
        for c in range(min(slots, len(chunks))):
            copy(c).start()

    def drain():
        for c, (_, w_bf, k) in enumerate(chunks):
            copy(c).wait()
            w_bf[k * rows:(k + 1) * rows, :] = stage[c % slots].astype(_BF16)
            if c + slots < len(chunks):
                copy(c + slots).start()

    return prime, drain


def _ffn_weights(w_hbm, scratch):
    wg_bf, wu_bf, wd_bf, stage_in, stage_down, sem_in, sem_down = scratch

    @pl.when(pl.program_id(0) == 0)
    def _():
        prime_in, drain_in = _cast_weights([(w_hbm[0], wg_bf), (w_hbm[1], wu_bf)], stage_in, sem_in)
        prime_down, drain_down = _cast_weights([(w_hbm[2], wd_bf)], stage_down, sem_down)
        prime_in()
        prime_down()
        drain_in()
        drain_down()

    return wg_bf, wu_bf, wd_bf


def _ffn_body(x_ref, pre_g_ref, wg_hbm, wu_hbm, wd_hbm, post_g_ref, o_ref, h_ref, *weight_scratch):
    wg_ref, wu_ref, wd_ref = _ffn_weights((wg_hbm, wu_hbm, wd_hbm), weight_scratch)
    _ffn_tile(lambda rows: x_ref[rows, :], pre_g_ref, wg_ref, wu_ref, wd_ref, post_g_ref, o_ref, h_ref)


_TOKEN_TILE_SPEC = pl.BlockSpec((TOKEN_TILE, D_MODEL), lambda i: (i, 0))
_HBM_SPEC = pl.BlockSpec(memory_space=pl.ANY)
_FFN_WEIGHT_SPECS = [_const_spec((1, D_MODEL)), _HBM_SPEC, _HBM_SPEC, _HBM_SPEC, _const_spec((1, D_MODEL))]
_FFN_SCRATCH = [
    pltpu.VMEM((TOKEN_TILE, D_FF), _BF16),
    pltpu.VMEM((D_MODEL, D_FF), _BF16),
    pltpu.VMEM((D_MODEL, D_FF), _BF16),
    pltpu.VMEM((D_FF, D_MODEL), _BF16),
    pltpu.VMEM((CAST_SLOTS, D_MODEL // CAST_CHUNKS, D_FF), _F32),
    pltpu.VMEM((CAST_SLOTS, D_FF // CAST_CHUNKS, D_MODEL), _F32),
    pltpu.SemaphoreType.DMA((CAST_SLOTS,)),
    pltpu.SemaphoreType.DMA((CAST_SLOTS,)),
]


def _ffn(x2d, ffn_params):
    n_tok = x2d.shape[0]
    return pl.pallas_call(
        _ffn_body,
        grid=(n_tok // TOKEN_TILE,),
        in_specs=[_TOKEN_TILE_SPEC] + _FFN_WEIGHT_SPECS,
        out_specs=_TOKEN_TILE_SPEC,
        out_shape=jax.ShapeDtypeStruct((n_tok, D_MODEL), _F32),
        scratch_shapes=_FFN_SCRATCH,
        compiler_params=_params(1),
        name="ffn",
    )(x2d, *ffn_params)


_T_QA, _T_VA, _T_QS, _T_VS = 0, W_MOBA, 2 * W_MOBA, 2 * W_MOBA + W_SWA
_T_ROWS = 2 * W_MOBA + W_SWA + W_SWA_KV
_MOBA_PER_TILE = TOKEN_TILE // MOBA_BLOCK
_SWA_PER_TILE = TOKEN_TILE // SWA_BLOCK


def _in_proj_body(x_ref, g_ref, w_rows_ref, w_t_ref,
                  qa_ref, va_ref, qs_ref, vs_ref, ka_ref, ks_ref, kmean_ref):
    swa_per_moba = MOBA_BLOCK // SWA_BLOCK
    for t in range(_MOBA_PER_TILE):
        xn = _rms_rows(x_ref[0, t * MOBA_BLOCK:(t + 1) * MOBA_BLOCK, :], g_ref[...]).astype(_BF16)
        k_rows = _dot(xn, w_rows_ref[...])
        proj_t = _dot_nt(w_t_ref[...], xn)
        k_blk = k_rows[:, :W_MOBA]
        ka_ref[0, t] = k_blk.astype(_BF16)
        kmean_ref[0, t] = jnp.mean(k_blk, axis=0, keepdims=True)
        qa_ref[0, t] = (proj_t[_T_QA:_T_QA + W_MOBA, :] * Q_FOLD).astype(_BF16)
        va_ref[0, t] = proj_t[_T_VA:_T_VA + W_MOBA, :].astype(_BF16)
        for u in range(swa_per_moba):
            rows = slice(u * SWA_BLOCK, (u + 1) * SWA_BLOCK)
            ks_ref[0, t * swa_per_moba + u] = k_rows[rows, W_MOBA:].astype(_BF16)
            qs_ref[0, t * swa_per_moba + u] = (proj_t[_T_QS:_T_QS + W_SWA, rows] * Q_FOLD).astype(_BF16)
            vs_ref[0, t * swa_per_moba + u] = proj_t[_T_VS:_T_VS + W_SWA_KV, rows].astype(_BF16)


def _in_proj(x, g, w_rows, w_t):
    batch, seq, _ = x.shape
    nb_moba, nb_swa = seq // MOBA_BLOCK, seq // SWA_BLOCK

    def blocked(n_per_tile, rows, cols):
        return pl.BlockSpec((1, n_per_tile, rows, cols), lambda b, i: (b, i, 0, 0))

    return pl.pallas_call(
        _in_proj_body,
        grid=(batch, seq // TOKEN_TILE),
        in_specs=[pl.BlockSpec((1, TOKEN_TILE, D_MODEL), lambda b, i: (b, i, 0)),
                  _const_spec((1, D_MODEL)),
                  _const_spec((D_MODEL, W_MOBA + W_SWA_KV)),
                  _const_spec((_T_ROWS, D_MODEL))],
        out_specs=[blocked(_MOBA_PER_TILE, W_MOBA, MOBA_BLOCK),
                   blocked(_MOBA_PER_TILE, W_MOBA, MOBA_BLOCK),
                   blocked(_SWA_PER_TILE, W_SWA, SWA_BLOCK),
                   blocked(_SWA_PER_TILE, W_SWA_KV, SWA_BLOCK),
                   blocked(_MOBA_PER_TILE, MOBA_BLOCK, W_MOBA),
                   blocked(_SWA_PER_TILE, SWA_BLOCK, W_SWA_KV),
                   blocked(_MOBA_PER_TILE, 1, W_MOBA)],
        out_shape=[jax.ShapeDtypeStruct((batch, nb_moba, W_MOBA, MOBA_BLOCK), _BF16),
                   jax.ShapeDtypeStruct((batch, nb_moba, W_MOBA, MOBA_BLOCK), _BF16),
                   jax.ShapeDtypeStruct((batch, nb_swa, W_SWA, SWA_BLOCK), _BF16),
                   jax.ShapeDtypeStruct((batch, nb_swa, W_SWA_KV, SWA_BLOCK), _BF16),
                   jax.ShapeDtypeStruct((batch, nb_moba, MOBA_BLOCK, W_MOBA), _BF16),
                   jax.ShapeDtypeStruct((batch, nb_swa, SWA_BLOCK, W_SWA_KV), _BF16),
                   jax.ShapeDtypeStruct((batch, nb_moba, 1, W_MOBA), _F32)],
        compiler_params=_params(2),
        name="in_proj",
    )(x, g, w_rows, w_t)


def _t5_bucket(dist):
    n = jnp.maximum(dist, 0)
    max_exact = NUM_BUCKETS // 2
    nf = jnp.maximum(n, 1).astype(_F32)
    large = max_exact + jnp.floor(jnp.log(nf / max_exact) / math.log(MAX_DISTANCE / max_exact)
                                  * (NUM_BUCKETS - max_exact))
    large = jnp.minimum(large, float(NUM_BUCKETS - 1))
    return jnp.where(n < max_exact, n.astype(_F32), large)


def _bias_lookup(rel_bias_ref, bucket, head):
    val = jnp.full(bucket.shape, rel_bias_ref[NUM_BUCKETS - 1, head], _F32)
    for b in range(NUM_BUCKETS - 2, -1, -1):
        val = jnp.where(bucket == float(b), rel_bias_ref[b, head], val)
    return val


def _bias_tiles_body(rel_bias_ref, own_ref, prev_ref, swa_ref):
    assert MOBA_BLOCK == 2 * SWA_BLOCK and SWA_WINDOW == SWA_BLOCK and MAX_DISTANCE <= SWA_BLOCK
    h = pl.program_id(0)
    q = SWA_BLOCK
    key = lax.broadcasted_iota(jnp.int32, (q, q), 0)
    qry = lax.broadcasted_iota(jnp.int32, (q, q), 1)
    dist = qry - key

    def quadrants(head):
        near = jnp.where(dist >= 0, _bias_lookup(rel_bias_ref, _t5_bucket(dist), head) * LOG2E, NEG)
        nxt = _bias_lookup(rel_bias_ref, _t5_bucket(dist + q), head) * LOG2E
        return near, nxt

    near, nxt = quadrants(h)
    far = jnp.full((q, q), rel_bias_ref[NUM_BUCKETS - 1, h] * LOG2E, _F32)
    masked = jnp.full((q, q), NEG, _F32)
    own_ref[0, :q, :q] = near
    own_ref[0, :q, q:] = nxt
    own_ref[0, q:, :q] = masked
    own_ref[0, q:, q:] = near
    prev_ref[0, :q, :q] = far
    prev_ref[0, :q, q:] = far
    prev_ref[0, q:, :q] = nxt
    prev_ref[0, q:, q:] = far
    near, nxt = quadrants(h + N_HEADS_MOBA)
    swa_ref[0, :q, :] = jnp.where(dist < 0, nxt, NEG)
    swa_ref[0, q:, :] = near


def _bias_tiles(rel_bias):
    def per_head(rows, cols):
        return pl.BlockSpec((1, rows, cols), lambda h: (h, 0, 0))

    return pl.pallas_call(
        _bias_tiles_body,
        grid=(N_HEADS_MOBA,),
        in_specs=[pl.BlockSpec(memory_space=pltpu.SMEM)],
        out_specs=[per_head(MOBA_BLOCK, MOBA_BLOCK), per_head(MOBA_BLOCK, MOBA_BLOCK),
                   per_head(2 * SWA_BLOCK, SWA_BLOCK)],
        out_shape=[jax.ShapeDtypeStruct((N_HEADS_MOBA, MOBA_BLOCK, MOBA_BLOCK), _F32),
                   jax.ShapeDtypeStruct((N_HEADS_MOBA, MOBA_BLOCK, MOBA_BLOCK), _F32),
                   jax.ShapeDtypeStruct((N_HEADS_SWA, 2 * SWA_BLOCK, SWA_BLOCK), _F32)],
        compiler_params=_params(1),
        name="bias_tiles",
    )(rel_bias)


def _moba_body(rel_bias_ref, q_ref, k_ref, v_ref, kmean_ref, own_ref, prev_ref, o_ref,
               far_ref, qpad_ref, s_ref, acc_ref):
    g = pl.program_id(1)
    nb = k_ref.shape[1]
    key_lanes = HEADS_PER_KEY_TILE * HEAD_DIM
    ring_slots = range(GROUPS_PER_ITER)
    tail_slots = ring_slots[-PIPE_DEPTH:]
    near_slots = range(GROUPS_PER_ITER, GROUPS_PER_ITER + PIPE_DEPTH)
    assert PIPE_DEPTH == 2 and GROUPS_PER_ITER >= 2 * PIPE_DEPTH

    def key_tile(r):
        t, sub = divmod(r, HEADS_PER_KEY_TILE)
        return slice(t * key_lanes, (t + 1) * key_lanes), sub

    def prepare(i):
        q_row = lax.broadcasted_iota(jnp.int32, (key_lanes, MOBA_BLOCK), 0)
        blk = lax.broadcasted_iota(jnp.int32, (nb, MOBA_BLOCK), 0)
        past = blk < i
        own, prev, prev_sel = [], [], []
        for r in range(HEADS_PER_STEP):
            tile_lanes, sub = key_tile(r)
            km = kmean_ref[0, :, tile_lanes]
            km_hi = km.astype(_BF16)
            km_lo = (km - km_hi.astype(_F32)).astype(_BF16)
            q_t = q_ref[0, i, tile_lanes, :]
            q_pad = jnp.where((q_row >= sub * HEAD_DIM) & (q_row < (sub + 1) * HEAD_DIM), q_t, 0)
            qpad_ref[r] = q_pad

            gate = jnp.where(
                past, _dot(km_hi, q_pad) + _dot(km_lo, q_pad), NEG)
            sel = jnp.zeros(gate.shape, jnp.bool_)
            for _ in range(MOBA_TOPK):
                top = jnp.max(gate, axis=0, keepdims=True)
                first = jnp.min(jnp.where(gate == top, blk, nb), axis=0, keepdims=True)
                pick = blk == first
                sel = sel | pick
                gate = jnp.where(pick, -jnp.inf, gate)
            sel = sel & past

            far_bias = rel_bias_ref[NUM_BUCKETS - 1, g * HEADS_PER_STEP + r] * LOG2E
            far_ref[r] = jnp.where(sel & (blk < i - 1), far_bias, NEG)
            prev_sel.append(jnp.max(jnp.where(sel & (blk == i - 1), 0.0, NEG), axis=0, keepdims=True))
            own.append(own_ref[r])
            prev.append(prev_ref[r])
        return own, prev, prev_sel

    def score(slot, j, tile_bias=None, row_bias=None):
        tops, shifts = [], []
        for r in range(HEADS_PER_STEP):
            s = _dot(k_ref[0, j, :, key_tile(r)[0]], qpad_ref[r])
            if tile_bias is not None:
                s = s + tile_bias[r]
            s_ref[slot, r] = s
            top = jnp.max(s, axis=0, keepdims=True)
            shift = jnp.zeros_like(top) if row_bias is None else row_bias[r]
            tops.append(top + shift)
            shifts.append(shift)
        return slot, j, tuple(tops), tuple(shifts)

    def absorb(carry, scored):
        slot, j, tops, shifts = scored
        out = []
        for r in range(HEADS_PER_STEP):
            lanes = slice(r * HEAD_DIM, (r + 1) * HEAD_DIM)
            m = carry[r]
            m_new = jnp.maximum(m, tops[r])
            p = jnp.exp2(s_ref[slot, r] + (shifts[r] - m_new)).astype(_BF16)
            ones = jnp.ones((SUM_ROWS, MOBA_BLOCK), _BF16)
            pv = _dot(jnp.concatenate([v_ref[0, j, lanes, :], ones], axis=0), p)
            acc_ref[r] = jnp.exp2(m - m_new) * acc_ref[r] + pv
            out.append(m_new)
        return tuple(out)

    def score_far(slot, grp):
        j = jnp.minimum(grp, nb - 1)
        return score(slot, j, row_bias=[far_ref[r, pl.ds(j, 1), :] for r in range(HEADS_PER_STEP)])

    def score_near(i):
        own, prev, prev_sel = prepare(i)
        return (score(near_slots[0], i, tile_bias=own)[2:],
                score(near_slots[1], jnp.maximum(i - 1, 0), tile_bias=prev, row_bias=prev_sel)[2:])

    def start(i, near):
        carry = tuple(jnp.full((1, MOBA_BLOCK), -jnp.inf, _F32) for _ in range(HEADS_PER_STEP))
        acc_ref[...] = jnp.zeros(acc_ref.shape, _F32)
        near_js = (i, jnp.maximum(i - 1, 0))
        ring = {slot: (slot, j) + maxes_shifts for slot, j, maxes_shifts in zip(near_slots, near_js, near)}
        for c, slot in enumerate(tail_slots):
            ring[slot] = score_far(slot, c)
            carry = absorb(carry, ring[near_slots[c]])
        return carry, tuple(ring[slot][1:] for slot in tail_slots)

    def tile(i, phase, state, last=False):
        def ring_steps(first_group, count, carry, pending):
            ring = {slot: (slot,) + p for slot, p in zip(tail_slots, pending)}
            for k in range(count):
                scored = score_far(k % GROUPS_PER_ITER, first_group + k)
                carry = absorb(carry, ring[(k - PIPE_DEPTH) % GROUPS_PER_ITER])
                ring[k % GROUPS_PER_ITER] = scored
            return carry, ring

        def ring_loop(first_group, turns, trips, state):
            def step(it, state):
                groups = turns * GROUPS_PER_ITER
                carry, ring = ring_steps(first_group + it * groups, groups, *state)
                return carry, tuple(ring[slot][1:] for slot in tail_slots)
            return lax.fori_loop(0, trips, step, state)

        leftover = (phase - 1 - PIPE_DEPTH) % GROUPS_PER_ITER
        n_turns = jnp.maximum(i - 1 - PIPE_DEPTH, 0) // GROUPS_PER_ITER
        n_long = n_turns // RING_UNROLL
        done = n_long * RING_UNROLL
        state = ring_loop(PIPE_DEPTH, RING_UNROLL, n_long, state)
        state = ring_loop(PIPE_DEPTH + done * GROUPS_PER_ITER, 1, n_turns - done, state)
        carry, ring = ring_steps(PIPE_DEPTH + n_turns * GROUPS_PER_ITER, leftover, *state)

        near_next = None if last else score_near(i + 1)
        for c in range(PIPE_DEPTH):
            carry = absorb(carry, ring[(leftover - PIPE_DEPTH + c) % GROUPS_PER_ITER])
        for r in range(HEADS_PER_STEP):
            acc = acc_ref[r]
            o_ref[0, i, r * HEAD_DIM:(r + 1) * HEAD_DIM, :] = acc[:HEAD_DIM] / acc[HEAD_DIM:HEAD_DIM + 1]
        return None if last else start(i + 1, near_next)

    def tiles(a, state, last=False):
        for phase in range(GROUPS_PER_ITER):
            state = tile(a * GROUPS_PER_ITER + phase, phase, state,
                         last=last and phase == GROUPS_PER_ITER - 1)
        return state

    assert nb % GROUPS_PER_ITER == 0
    state = lax.fori_loop(0, nb // GROUPS_PER_ITER - 1, tiles, start(0, score_near(0)))
    tiles(nb // GROUPS_PER_ITER - 1, state, last=True)


def _moba(rel_bias, qa_t, ka, va_t, kmean, bias_own, bias_prev):
    batch, nb = qa_t.shape[0], qa_t.shape[1]
    width = HEADS_PER_STEP * HEAD_DIM
    q_spec = pl.BlockSpec((1, nb, width, MOBA_BLOCK), lambda b, g: (b, 0, g, 0))
    bias_spec = pl.BlockSpec((HEADS_PER_STEP, MOBA_BLOCK, MOBA_BLOCK), lambda b, g: (g, 0, 0))
    return pl.pallas_call(
        _moba_body,
        grid=(batch, N_HEADS_MOBA // HEADS_PER_STEP),
        in_specs=[pl.BlockSpec(memory_space=pltpu.SMEM),
                  q_spec,
                  pl.BlockSpec((1, nb, MOBA_BLOCK, width), lambda b, g: (b, 0, 0, g)),
                  q_spec,
                  pl.BlockSpec((1, nb, width), lambda b, g: (b, 0, g)),
                  bias_spec, bias_spec],
        out_specs=q_spec,
        out_shape=jax.ShapeDtypeStruct((batch, nb, W_MOBA, MOBA_BLOCK), _F32),
        scratch_shapes=[pltpu.VMEM((HEADS_PER_STEP, nb, MOBA_BLOCK), _F32),
                        pltpu.VMEM((HEADS_PER_STEP, HEADS_PER_KEY_TILE * HEAD_DIM, MOBA_BLOCK), _BF16),
                        pltpu.VMEM((GROUPS_PER_ITER + PIPE_DEPTH, HEADS_PER_STEP, MOBA_BLOCK, MOBA_BLOCK), _F32),
                        pltpu.VMEM((HEADS_PER_STEP, HEAD_DIM + SUM_ROWS, MOBA_BLOCK), _F32)],
        compiler_params=_params(2),
        name="moba",
    )(rel_bias, qa_t, ka, va_t, kmean, bias_own, bias_prev)


def _swa_body(sinks_ref, q_ref, k_prev_ref, k_cur_ref, v_prev_ref, v_cur_ref, bias_ref, o_ref, s_ref):
    no_prev = jnp.where(pl.program_id(1) == 0, NEG, 0.0)
    window_row = lax.broadcasted_iota(jnp.int32, (2 * SWA_BLOCK, SWA_BLOCK), 0)
    first_window_mask = jnp.where(window_row < SWA_BLOCK, no_prev, 0.0)
    zeros = jnp.zeros((HEAD_DIM, SWA_BLOCK), _BF16)
    ones = jnp.ones((SUM_ROWS, 2 * SWA_BLOCK), _BF16)
    k_blocks = [k_prev_ref[0, 0]] + [k_cur_ref[0, t] for t in range(SWA_Q_PER_STEP)]
    v_blocks = [v_prev_ref[0, 0]] + [v_cur_ref[0, t] for t in range(SWA_Q_PER_STEP)]
    chains = [(t, kv) for t in range(SWA_Q_PER_STEP) for kv in range(N_KV_SWA)]

    for t, kv in chains:
        q_pad = jnp.concatenate(
            [jnp.concatenate(
                [q_ref[0, t, h * HEAD_DIM:(h + 1) * HEAD_DIM, :] if part == kv else zeros
                 for part in range(N_KV_SWA)], axis=0)
             for h in range(kv * SWA_GROUP, (kv + 1) * SWA_GROUP)], axis=1)
        window_keys = jnp.concatenate([k_blocks[t], k_blocks[t + 1]], axis=0)
        s = _dot(window_keys, q_pad)
        for slot in range(SWA_GROUP):
            cols = slice(slot * SWA_BLOCK, (slot + 1) * SWA_BLOCK)
            biased = s[:, cols] + bias_ref[kv * SWA_GROUP + slot]
            s_ref[t, kv, :, cols] = biased + first_window_mask if t == 0 else biased

    for t, kv in chains:
        kv_rows = slice(kv * HEAD_DIM, (kv + 1) * HEAD_DIM)
        window_values = jnp.concatenate(
            [jnp.concatenate([v_blocks[t][kv_rows, :], v_blocks[t + 1][kv_rows, :]], axis=1), ones], axis=0)
        p, sink_p = [], []
        for slot in range(SWA_GROUP):
            cols = slice(slot * SWA_BLOCK, (slot + 1) * SWA_BLOCK)
            sink = sinks_ref[kv * SWA_GROUP + slot] * LOG2E
            m = jnp.maximum(jnp.max(s_ref[t, kv, :, cols], axis=0, keepdims=True), sink)
            p.append(jnp.exp2(s_ref[t, kv, :, cols] - m).astype(_BF16))
            sink_p.append(jnp.exp2(sink - m))
        acc = _dot(window_values, jnp.concatenate(p, axis=1))
        for slot in range(SWA_GROUP):
            h = kv * SWA_GROUP + slot
            cols = slice(slot * SWA_BLOCK, (slot + 1) * SWA_BLOCK)
            l = acc[HEAD_DIM:HEAD_DIM + 1, cols] + sink_p[slot]
            o_ref[0, t, h * HEAD_DIM:(h + 1) * HEAD_DIM, :] = acc[:HEAD_DIM, cols] / l


def _swa(sinks, qs_t, ks, vs_t, bias_swa):
    batch, nb = qs_t.shape[0], qs_t.shape[1]

    def cur(rows, cols):
        return pl.BlockSpec((1, SWA_Q_PER_STEP, rows, cols), lambda b, n: (b, n, 0, 0))

    def prev(rows, cols):
        return pl.BlockSpec((1, 1, rows, cols),
                            lambda b, n: (b, jnp.maximum(n * SWA_Q_PER_STEP - 1, 0), 0, 0))

    return pl.pallas_call(
        _swa_body,
        grid=(batch, nb // SWA_Q_PER_STEP),
        in_specs=[pl.BlockSpec(memory_space=pltpu.SMEM),
                  cur(W_SWA, SWA_BLOCK),
                  prev(SWA_BLOCK, W_SWA_KV), cur(SWA_BLOCK, W_SWA_KV),
                  prev(W_SWA_KV, SWA_BLOCK), cur(W_SWA_KV, SWA_BLOCK),
                  _const_spec((N_HEADS_SWA, 2 * SWA_BLOCK, SWA_BLOCK))],
        out_specs=cur(W_SWA, SWA_BLOCK),
        out_shape=jax.ShapeDtypeStruct((batch, nb, W_SWA, SWA_BLOCK), _F32),
        scratch_shapes=[pltpu.VMEM((SWA_Q_PER_STEP, N_KV_SWA, 2 * SWA_BLOCK, SWA_GROUP * SWA_BLOCK), _F32)],
        compiler_params=_params(2),
        name="swa",
    )(sinks, qs_t, ks, ks, vs_t, vs_t, bias_swa)


def _group_norm_rows(o_ref, g_ref):
    rows = []
    for t in range(o_ref.shape[0]):
        o_t = o_ref[t]
        scale = lax.rsqrt(jnp.mean(o_t * o_t, axis=0, keepdims=True) + RMS_EPS)
        rows.append(((o_t * scale).T * g_ref[...]).astype(_BF16))
    return jnp.concatenate(rows, axis=0)


def _mix_ffn_body(x_ref, oa_ref, ob_ref, ga_ref, gb_ref, w_out_ref, mix_post_g_ref,
                  pre_g_ref, wg_hbm, wu_hbm, wd_hbm, post_g_ref, o_ref, h_ref, *weight_scratch):
    wg_ref, wu_ref, wd_ref = _ffn_weights((wg_hbm, wu_hbm, wd_hbm), weight_scratch)
    y = (_dot(_group_norm_rows(oa_ref, ga_ref), w_out_ref[:W_MOBA, :])
         + _dot(_group_norm_rows(ob_ref, gb_ref), w_out_ref[W_MOBA:, :]))
    x = x_ref[...] + _rms_rows(y, mix_post_g_ref[...])
    _ffn_tile(lambda rows: x[rows, :], pre_g_ref, wg_ref, wu_ref, wd_ref, post_g_ref, o_ref, h_ref)


def _mix_ffn(x2d, oa_t, ob_t, ga, gb, w_out, mix_post_g, ffn_params):
    n_tok = x2d.shape[0]
    return pl.pallas_call(
        _mix_ffn_body,
        grid=(n_tok // TOKEN_TILE,),
        in_specs=[_TOKEN_TILE_SPEC,
                  pl.BlockSpec((_MOBA_PER_TILE, W_MOBA, MOBA_BLOCK), lambda i: (i, 0, 0)),
                  pl.BlockSpec((_SWA_PER_TILE, W_SWA, SWA_BLOCK), lambda i: (i, 0, 0)),
                  _const_spec((1, W_MOBA)), _const_spec((1, W_SWA)),
                  _const_spec((W_MOBA + W_SWA, D_MODEL)),
                  _const_spec((1, D_MODEL))] + _FFN_WEIGHT_SPECS,
        out_specs=_TOKEN_TILE_SPEC,
        out_shape=jax.ShapeDtypeStruct((n_tok, D_MODEL), _F32),
        scratch_shapes=_FFN_SCRATCH,
        compiler_params=_params(1),
        name="mix_ffn",
    )(x2d, oa_t, ob_t, ga, gb, w_out, mix_post_g, *ffn_params)


def _row(v):
    return v.reshape(1, -1)


def _layer(x, ffn1, mix, ffn2, rel_bias, bias_tiles):
    batch, seq, _ = x.shape
    (mix_pre_g, w_in, moba_out_g, swa_sinks, swa_out_g, w_out, mix_post_g) = mix
    bias_own, bias_prev, bias_swa = bias_tiles

    def ffn_params(params):
        pre_g, w_gate, w_up, w_down, post_g = params
        return (_row(pre_g), w_gate, w_up, w_down, _row(post_g))

    x = _ffn(x.reshape(batch * seq, D_MODEL), ffn_params(ffn1))

    qa, ka, va, qs, ks, vs = jnp.split(
        w_in, [W_MOBA, 2 * W_MOBA, 3 * W_MOBA, 3 * W_MOBA + W_SWA, 3 * W_MOBA + W_SWA + W_SWA_KV], axis=1)
    w_rows = jnp.concatenate([ka, ks], axis=1).astype(_BF16)
    w_t = jnp.concatenate([qa, va, qs, vs], axis=1).astype(_BF16).T
    qa_t, va_t, qs_t, vs_t, ka_b, ks_b, kmean = _in_proj(
        x.reshape(batch, seq, D_MODEL), _row(mix_pre_g), w_rows, w_t)

    oa_t = _moba(rel_bias, qa_t, ka_b, va_t, kmean.reshape(batch, seq // MOBA_BLOCK, W_MOBA),
                 bias_own, bias_prev)
    ob_t = _swa(swa_sinks, qs_t, ks_b, vs_t, bias_swa)

    x = _mix_ffn(x, oa_t.reshape(-1, W_MOBA, MOBA_BLOCK), ob_t.reshape(-1, W_SWA, SWA_BLOCK),
                 _row(moba_out_g), _row(swa_out_g), w_out.astype(_BF16), _row(mix_post_g),
                 ffn_params(ffn2))
    return x.reshape(batch, seq, D_MODEL)


def kernel(x, ffn1_pre_g, ffn1_w_gate, ffn1_w_up, ffn1_w_down, ffn1_post_g, mix_pre_g, w_in, rel_bias,
           moba_out_g, swa_sinks, swa_out_g, w_out, mix_post_g, ffn2_pre_g, ffn2_w_gate, ffn2_w_up,
           ffn2_w_down, ffn2_post_g):
    bias_tiles = _bias_tiles(rel_bias)
    for l in range(ffn1_pre_g.shape[0]):
        x = _layer(
            x,
            (ffn1_pre_g[l], ffn1_w_gate[l], ffn1_w_up[l], ffn1_w_down[l], ffn1_post_g[l]),
            (mix_pre_g[l], w_in[l], moba_out_g[l], swa_sinks[l], swa_out_g[l], w_out[l], mix_post_g[l]),
            (ffn2_pre_g[l], ffn2_w_gate[l], ffn2_w_up[l], ffn2_w_down[l], ffn2_post_g[l]),
            rel_bias, bias_tiles)
    return x
```

```python
import math

import jax
import jax.numpy as jnp
from jax import lax
from jax.experimental import pallas as pl
from jax.experimental.pallas import tpu as pltpu

D_MODEL = 1024
HEAD_DIM = 64
N_HEADS_MOBA = 8
N_HEADS_SWA = 8
N_KV_SWA = 2
SWA_GROUP = N_HEADS_SWA // N_KV_SWA
W_MOBA = N_HEADS_MOBA * HEAD_DIM
W_SWA = N_HEADS_SWA * HEAD_DIM
W_SWA_KV = N_KV_SWA * HEAD_DIM
MOBA_BLOCK = 256
MOBA_TOPK = 3
SWA_BLOCK = 128
SWA_WINDOW = 128
NUM_BUCKETS = 32
MAX_DISTANCE = 128
D_FF = 2816
RMS_EPS = 1e-6
FFN_RES_WEIGHT = 0.5
NEG = -1e30
QK_SCALE = HEAD_DIM ** -0.5
LOG2E = math.log2(math.e)
Q_FOLD = QK_SCALE * LOG2E
SUM_ROWS = 16

TOKEN_TILE = 1024
FF_CHUNK = 256
FFN_ROW_PARTS = 2
CAST_CHUNKS = 16
CAST_SLOTS = 4
HEADS_PER_STEP = 4
HEADS_PER_KEY_TILE = 2
PIPE_DEPTH = 2
GROUPS_PER_ITER = 4
RING_UNROLL = 2
SWA_Q_PER_STEP = 4
VMEM_LIMIT_BYTES = 58 * 1024 * 1024

_BF16 = jnp.bfloat16
_F32 = jnp.float32


def _dot(a, b):
    return jnp.dot(a, b, preferred_element_type=_F32)


def _dot_nt(a, b):
    return lax.dot_general(a, b, (((1,), (1,)), ((), ())), preferred_element_type=_F32)


def _rms_rows(x, g):
    return x * lax.rsqrt(jnp.mean(x * x, axis=-1, keepdims=True) + RMS_EPS) * g


def _const_spec(shape):
    return pl.BlockSpec(shape, lambda *_: (0,) * len(shape), pipeline_mode=pl.Buffered(1))


def _params(n_axes):
    return pltpu.CompilerParams(
        dimension_semantics=("arbitrary",) * n_axes, vmem_limit_bytes=VMEM_LIMIT_BYTES)


def _ffn_tile(x_rows, pre_g_ref, wg_ref, wu_ref, wd_ref, post_g_ref, o_ref, h_ref):
    part = TOKEN_TILE // FFN_ROW_PARTS
    for p in range(FFN_ROW_PARTS):
        rows = slice(p * part, (p + 1) * part)
        x = x_rows(rows)
        xn = _rms_rows(x, pre_g_ref[...]).astype(_BF16)
        for c in range(D_FF // FF_CHUNK):
            cols = slice(c * FF_CHUNK, (c + 1) * FF_CHUNK)
            gate = _dot(xn, wg_ref[:, cols])
            up = _dot(xn, wu_ref[:, cols])
            h_ref[rows, cols] = (jax.nn.silu(gate) * up).astype(_BF16)
        y = _dot(h_ref[rows, :], wd_ref[...])
        o_ref[rows, :] = x + FFN_RES_WEIGHT * _rms_rows(y, post_g_ref[...])


def _cast_weights(pairs, stage, sem):
    slots, rows = stage.shape[0], stage.shape[1]
    chunks = [(w_hbm, w_bf, k) for w_hbm, w_bf in pairs for k in range(w_hbm.shape[0] // rows)]

    def copy(c):
        w_hbm, _, k = chunks[c]
        return pltpu.make_async_copy(w_hbm.at[pl.ds(k * rows, rows), :], stage.at[c % slots], sem.at[c % slots])

    def prime():
        for c in range(min(slots, len(chunks))):
            copy(c).start()

    def drain():
        for c, (_, w_bf, k) in enumerate(chunks):
            copy(c).wait()
            w_bf[k * rows:(k + 1) * rows, :] = stage[c % slots].astype(_BF16)
            if c + slots < len(chunks):
                copy(c + slots).start()

    return prime, drain


def _ffn_weights(w_hbm, scratch):
    wg_bf, wu_bf, wd_bf, stage_in, stage_down, sem_in, sem_down = scratch

    @pl.when(pl.program_id(0) == 0)
    def _():
        prime_in, drain_in = _cast_weights([(w_hbm[0], wg_bf), (w_hbm[1], wu_bf)], stage_in, sem_in)
        prime_down, drain_down = _cast_weights([(w_hbm[2], wd_bf)], stage_down, sem_down)
        prime_in()
        prime_down()
        drain_in()
        drain_down()

    return wg_bf, wu_bf, wd_bf


def _ffn_body(x_ref, pre_g_ref, wg_hbm, wu_hbm, wd_hbm, post_g_ref, o_ref, h_ref, *weight_scratch):
    wg_ref, wu_ref, wd_ref = _ffn_weights((wg_hbm, wu_hbm, wd_hbm), weight_scratch)
    _ffn_tile(lambda rows: x_ref[rows, :], pre_g_ref, wg_ref, wu_ref, wd_ref, post_g_ref, o_ref, h_ref)


_TOKEN_TILE_SPEC = pl.BlockSpec((TOKEN_TILE, D_MODEL), lambda i: (i, 0))
_HBM_SPEC = pl.BlockSpec(memory_space=pl.ANY)
_FFN_WEIGHT_SPECS = [_const_spec((1, D_MODEL)), _HBM_SPEC, _HBM_SPEC, _HBM_SPEC, _const_spec((1, D_MODEL))]
_FFN_SCRATCH = [
    pltpu.VMEM((TOKEN_TILE, D_FF), _BF16),
    pltpu.VMEM((D_MODEL, D_FF), _BF16),
    pltpu.VMEM((D_MODEL, D_FF), _BF16),
    pltpu.VMEM((D_FF, D_MODEL), _BF16),
    pltpu.VMEM((CAST_SLOTS, D_MODEL // CAST_CHUNKS, D_FF), _F32),
    pltpu.VMEM((CAST_SLOTS, D_FF // CAST_CHUNKS, D_MODEL), _F32),
    pltpu.SemaphoreType.DMA((CAST_SLOTS,)),
    pltpu.SemaphoreType.DMA((CAST_SLOTS,)),
]


def _ffn(x2d, ffn_params):
    n_tok = x2d.shape[0]
    return pl.pallas_call(
        _ffn_body,
        grid=(n_tok // TOKEN_TILE,),
        in_specs=[_TOKEN_TILE_SPEC] + _FFN_WEIGHT_SPECS,
        out_specs=_TOKEN_TILE_SPEC,
        out_shape=jax.ShapeDtypeStruct((n_tok, D_MODEL), _F32),
        scratch_shapes=_FFN_SCRATCH,
        compiler_params=_params(1),
        name="ffn",
    )(x2d, *ffn_params)


_T_QA, _T_VA, _T_QS, _T_VS = 0, W_MOBA, 2 * W_MOBA, 2 * W_MOBA + W_SWA
_T_ROWS = 2 * W_MOBA + W_SWA + W_SWA_KV
_MOBA_PER_TILE = TOKEN_TILE // MOBA_BLOCK
_SWA_PER_TILE = TOKEN_TILE // SWA_BLOCK


def _in_proj_body(x_ref, g_ref, w_rows_ref, w_t_ref,
                  qa_ref, va_ref, qs_ref, vs_ref, ka_ref, ks_ref, kmean_ref):
    swa_per_moba = MOBA_BLOCK // SWA_BLOCK
    for t in range(_MOBA_PER_TILE):
        xn = _rms_rows(x_ref[0, t * MOBA_BLOCK:(t + 1) * MOBA_BLOCK, :], g_ref[...]).astype(_BF16)
        k_rows = _dot(xn, w_rows_ref[...])
        proj_t = _dot_nt(w_t_ref[...], xn)
        k_blk = k_rows[:, :W_MOBA]
        ka_ref[0, t] = k_blk.astype(_BF16)
        kmean_ref[0, t] = jnp.mean(k_blk, axis=0, keepdims=True)
        qa_ref[0, t] = (proj_t[_T_QA:_T_QA + W_MOBA, :] * Q_FOLD).astype(_BF16)
        va_ref[0, t] = proj_t[_T_VA:_T_VA + W_MOBA, :].astype(_BF16)
        for u in range(swa_per_moba):
            rows = slice(u * SWA_BLOCK, (u + 1) * SWA_BLOCK)
            ks_ref[0, t * swa_per_moba + u] = k_rows[rows, W_MOBA:].astype(_BF16)
            qs_ref[0, t * swa_per_moba + u] = (proj_t[_T_QS:_T_QS + W_SWA, rows] * Q_FOLD).astype(_BF16)
            vs_ref[0, t * swa_per_moba + u] = proj_t[_T_VS:_T_VS + W_SWA_KV, rows].astype(_BF16)


def _in_proj(x, g, w_rows, w_t):
    batch, seq, _ = x.shape
    nb_moba, nb_swa = seq // MOBA_BLOCK, seq // SWA_BLOCK

    def blocked(n_per_tile, rows, cols):
        return pl.BlockSpec((1, n_per_tile, rows, cols), lambda b, i: (b, i, 0, 0))

    return pl.pallas_call(
        _in_proj_body,
        grid=(batch, seq // TOKEN_TILE),
        in_specs=[pl.BlockSpec((1, TOKEN_TILE, D_MODEL), lambda b, i: (b, i, 0)),
                  _const_spec((1, D_MODEL)),
                  _const_spec((D_MODEL, W_MOBA + W_SWA_KV)),
                  _const_spec((_T_ROWS, D_MODEL))],
        out_specs=[blocked(_MOBA_PER_TILE, W_MOBA, MOBA_BLOCK),
                   blocked(_MOBA_PER_TILE, W_MOBA, MOBA_BLOCK),
                   blocked(_SWA_PER_TILE, W_SWA, SWA_BLOCK),
                   blocked(_SWA_PER_TILE, W_SWA_KV, SWA_BLOCK),
                   blocked(_MOBA_PER_TILE, MOBA_BLOCK, W_MOBA),
                   blocked(_SWA_PER_TILE, SWA_BLOCK, W_SWA_KV),
                   blocked(_MOBA_PER_TILE, 1, W_MOBA)],
        out_shape=[jax.ShapeDtypeStruct((batch, nb_moba, W_MOBA, MOBA_BLOCK), _BF16),
                   jax.ShapeDtypeStruct((batch, nb_moba, W_MOBA, MOBA_BLOCK), _BF16),
                   jax.ShapeDtypeStruct((batch, nb_swa, W_SWA, SWA_BLOCK), _BF16),
                   jax.ShapeDtypeStruct((batch, nb_swa, W_SWA_KV, SWA_BLOCK), _BF16),
                   jax.ShapeDtypeStruct((batch, nb_moba, MOBA_BLOCK, W_MOBA), _BF16),
                   jax.ShapeDtypeStruct((batch, nb_swa, SWA_BLOCK, W_SWA_KV), _BF16),
                   jax.ShapeDtypeStruct((batch, nb_moba, 1, W_MOBA), _F32)],
        compiler_params=_params(2),
        name="in_proj",
    )(x, g, w_rows, w_t)


def _t5_bucket(dist):
    n = jnp.maximum(dist, 0)
    max_exact = NUM_BUCKETS // 2
    nf = jnp.maximum(n, 1).astype(_F32)
    large = max_exact + jnp.floor(jnp.log(nf / max_exact) / math.log(MAX_DISTANCE / max_exact)
                                  * (NUM_BUCKETS - max_exact))
    large = jnp.minimum(large, float(NUM_BUCKETS - 1))
    return jnp.where(n < max_exact, n.astype(_F32), large)


def _bias_lookup(rel_bias_ref, bucket, head):
    val = jnp.full(bucket.shape, rel_bias_ref[NUM_BUCKETS - 1, head], _F32)
    for b in range(NUM_BUCKETS - 2, -1, -1):
        val = jnp.where(bucket == float(b), rel_bias_ref[b, head], val)
    return val


def _bias_tiles_body(rel_bias_ref, own_ref, prev_ref, swa_ref):
    assert MOBA_BLOCK == 2 * SWA_BLOCK and SWA_WINDOW == SWA_BLOCK and MAX_DISTANCE <= SWA_BLOCK
    h = pl.program_id(0)
    q = SWA_BLOCK
    key = lax.broadcasted_iota(jnp.int32, (q, q), 0)
    qry = lax.broadcasted_iota(jnp.int32, (q, q), 1)
    dist = qry - key

    def quadrants(head):
        near = jnp.where(dist >= 0, _bias_lookup(rel_bias_ref, _t5_bucket(dist), head) * LOG2E, NEG)
        nxt = _bias_lookup(rel_bias_ref, _t5_bucket(dist + q), head) * LOG2E
        return near, nxt

    near, nxt = quadrants(h)
    far = jnp.full((q, q), rel_bias_ref[NUM_BUCKETS - 1, h] * LOG2E, _F32)
    masked = jnp.full((q, q), NEG, _F32)
    own_ref[0, :q, :q] = near
    own_ref[0, :q, q:] = nxt
    own_ref[0, q:, :q] = masked
    own_ref[0, q:, q:] = near
    prev_ref[0, :q, :q] = far
    prev_ref[0, :q, q:] = far
    prev_ref[0, q:, :q] = nxt
    prev_ref[0, q:, q:] = far
    near, nxt = quadrants(h + N_HEADS_MOBA)
    swa_ref[0, :q, :] = jnp.where(dist < 0, nxt, NEG)
    swa_ref[0, q:, :] = near


def _bias_tiles(rel_bias):
    def per_head(rows, cols):
        return pl.BlockSpec((1, rows, cols), lambda h: (h, 0, 0))

    return pl.pallas_call(
        _bias_tiles_body,
        grid=(N_HEADS_MOBA,),
        in_specs=[pl.BlockSpec(memory_space=pltpu.SMEM)],
        out_specs=[per_head(MOBA_BLOCK, MOBA_BLOCK), per_head(MOBA_BLOCK, MOBA_BLOCK),
                   per_head(2 * SWA_BLOCK, SWA_BLOCK)],
        out_shape=[jax.ShapeDtypeStruct((N_HEADS_MOBA, MOBA_BLOCK, MOBA_BLOCK), _F32),
                   jax.ShapeDtypeStruct((N_HEADS_MOBA, MOBA_BLOCK, MOBA_BLOCK), _F32),
                   jax.ShapeDtypeStruct((N_HEADS_SWA, 2 * SWA_BLOCK, SWA_BLOCK), _F32)],
        compiler_params=_params(1),
        name="bias_tiles",
    )(rel_bias)


def _moba_body(rel_bias_ref, q_ref, k_ref, v_ref, kmean_ref, own_ref, prev_ref, o_ref,
               far_ref, qpad_ref, s_ref, acc_ref):
    g = pl.program_id(1)
    nb = k_ref.shape[1]
    key_lanes = HEADS_PER_KEY_TILE * HEAD_DIM
    q_row = lax.broadcasted_iota(jnp.int32, (key_lanes, MOBA_BLOCK), 0)
    km = kmean_ref[0]
    km_hi = km.astype(_BF16)
    km_lo = (km - km_hi.astype(_F32)).astype(_BF16)
    blk = lax.broadcasted_iota(jnp.int32, (nb, MOBA_BLOCK), 0)
    ones = jnp.ones((SUM_ROWS, MOBA_BLOCK), _BF16)
    ring_slots = range(GROUPS_PER_ITER)
    tail_slots = ring_slots[-PIPE_DEPTH:]
    near_slots = range(GROUPS_PER_ITER, GROUPS_PER_ITER + PIPE_DEPTH)
    assert PIPE_DEPTH == 2 and GROUPS_PER_ITER >= 2 * PIPE_DEPTH

    def key_tile(r):
        t, sub = divmod(r, HEADS_PER_KEY_TILE)
        return slice(t * key_lanes, (t + 1) * key_lanes), sub

    def prepare(i):
        past = blk < i
        own, prev, prev_sel = [], [], []
        for r in range(HEADS_PER_STEP):
            tile_lanes, sub = key_tile(r)
            q_t = q_ref[0, i, tile_lanes, :]
            q_pad = jnp.where((q_row >= sub * HEAD_DIM) & (q_row < (sub + 1) * HEAD_DIM), q_t, 0)
            qpad_ref[r] = q_pad

            gate = jnp.where(
                past, _dot(km_hi[:, tile_lanes], q_pad) + _dot(km_lo[:, tile_lanes], q_pad), NEG)
            sel = jnp.zeros(gate.shape, jnp.bool_)
            for _ in range(MOBA_TOPK):
                top = jnp.max(gate, axis=0, keepdims=True)
                first = jnp.min(jnp.where(gate == top, blk, nb), axis=0, keepdims=True)
                pick = blk == first
                sel = sel | pick
                gate = jnp.where(pick, -jnp.inf, gate)
            sel = sel & past

            far_bias = rel_bias_ref[NUM_BUCKETS - 1, g * HEADS_PER_STEP + r] * LOG2E
            far_ref[r] = jnp.where(sel & (blk < i - 1), far_bias, NEG)
            prev_sel.append(jnp.max(jnp.where(sel & (blk == i - 1), 0.0, NEG), axis=0, keepdims=True))
            own.append(own_ref[r])
            prev.append(prev_ref[r])
        return own, prev, prev_sel

    def score(slot, j, tile_bias=None, row_bias=None):
        tops, shifts = [], []
        for r in range(HEADS_PER_STEP):
            s = _dot(k_ref[0, j, :, key_tile(r)[0]], qpad_ref[r])
            if tile_bias is not None:
                s = s + tile_bias[r]
            s_ref[slot, r] = s
            top = jnp.max(s, axis=0, keepdims=True)
            shift = jnp.zeros_like(top) if row_bias is None else row_bias[r]
            tops.append(top + shift)
            shifts.append(shift)
        return slot, j, tuple(tops), tuple(shifts)

    def absorb(carry, blocks):
        out = []
        for r in range(HEADS_PER_STEP):
            lanes = slice(r * HEAD_DIM, (r + 1) * HEAD_DIM)
            m = carry[r]
            m_new = m
            for _, _, tops, _ in blocks:
                m_new = jnp.maximum(m_new, tops[r])
            p = jnp.concatenate([jnp.exp2(s_ref[slot, r] + (shifts[r] - m_new)).astype(_BF16)
                                 for slot, _, _, shifts in blocks], axis=0)
            v_ext = jnp.concatenate([jnp.concatenate([v_ref[0, j, lanes, :], ones], axis=0)
                                     for _, j, _, _ in blocks], axis=1)
            acc_ref[r] = jnp.exp2(m - m_new) * acc_ref[r] + _dot(v_ext, p)
            out.append(m_new)
        return tuple(out)

    def score_far(slot, grp):
        j = jnp.minimum(grp, nb - 1)
        return score(slot, j, row_bias=[far_ref[r, pl.ds(j, 1), :] for r in range(HEADS_PER_STEP)])

    def score_near(i):
        own, prev, prev_sel = prepare(i)
        return (score(near_slots[0], i, tile_bias=own)[2:],
                score(near_slots[1], jnp.maximum(i - 1, 0), tile_bias=prev, row_bias=prev_sel)[2:])

    def start(i, near):
        carry = tuple(jnp.full((1, MOBA_BLOCK), -jnp.inf, _F32) for _ in range(HEADS_PER_STEP))
        acc_ref[...] = jnp.zeros(acc_ref.shape, _F32)
        near_js = (i, jnp.maximum(i - 1, 0))
        ring = {slot: (slot, j) + maxes_shifts for slot, j, maxes_shifts in zip(near_slots, near_js, near)}
        for c, slot in enumerate(tail_slots):
            ring[slot] = score_far(slot, c)
        carry = absorb(carry, [ring[slot] for slot in near_slots])
        return carry, tuple(ring[slot][1:] for slot in tail_slots)

    def tile(i, phase, state, last=False):
        def ring_steps(first_group, count, carry, pending):
            ring = {slot: (slot,) + p for slot, p in zip(tail_slots, pending)}
            for k0 in range(0, count, PIPE_DEPTH):
                ks = range(k0, min(k0 + PIPE_DEPTH, count))
                scored = [score_far(k % GROUPS_PER_ITER, first_group + k) for k in ks]
                carry = absorb(carry, [ring[(k - PIPE_DEPTH) % GROUPS_PER_ITER] for k in ks])
                ring.update({k % GROUPS_PER_ITER: s for k, s in zip(ks, scored)})
            return carry, ring

        def ring_loop(first_group, turns, trips, state):
            def step(it, state):
                groups = turns * GROUPS_PER_ITER
                carry, ring = ring_steps(first_group + it * groups, groups, *state)
                return carry, tuple(ring[slot][1:] for slot in tail_slots)
            return lax.fori_loop(0, trips, step, state)

        leftover = (phase - 1 - PIPE_DEPTH) % GROUPS_PER_ITER
        n_turns = jnp.maximum(i - 1 - PIPE_DEPTH, 0) // GROUPS_PER_ITER
        n_long = n_turns // RING_UNROLL
        done = n_long * RING_UNROLL
        state = ring_loop(PIPE_DEPTH, RING_UNROLL, n_long, state)
        state = ring_loop(PIPE_DEPTH + done * GROUPS_PER_ITER, 1, n_turns - done, state)
        carry, ring = ring_steps(PIPE_DEPTH + n_turns * GROUPS_PER_ITER, leftover, *state)

        near_next = None if last else score_near(i + 1)
        carry = absorb(carry, [ring[(leftover - PIPE_DEPTH + c) % GROUPS_PER_ITER] for c in range(PIPE_DEPTH)])
        for r in range(HEADS_PER_STEP):
            acc = acc_ref[r]
            o_ref[0, i, r * HEAD_DIM:(r + 1) * HEAD_DIM, :] = acc[:HEAD_DIM] / acc[HEAD_DIM:HEAD_DIM + 1]
        return None if last else start(i + 1, near_next)

    def tiles(a, state, last=False):
        for phase in range(GROUPS_PER_ITER):
            state = tile(a * GROUPS_PER_ITER + phase, phase, state,
                         last=last and phase == GROUPS_PER_ITER - 1)
        return state

    assert nb % GROUPS_PER_ITER == 0
    state = lax.fori_loop(0, nb // GROUPS_PER_ITER - 1, tiles, start(0, score_near(0)))
    tiles(nb // GROUPS_PER_ITER - 1, state, last=True)


def _moba(rel_bias, qa_t, ka, va_t, kmean, bias_own, bias_prev):
    batch, nb = qa_t.shape[0], qa_t.shape[1]
    width = HEADS_PER_STEP * HEAD_DIM
    q_spec = pl.BlockSpec((1, nb, width, MOBA_BLOCK), lambda b, g: (b, 0, g, 0))
    bias_spec = pl.BlockSpec((HEADS_PER_STEP, MOBA_BLOCK, MOBA_BLOCK), lambda b, g: (g, 0, 0))
    return pl.pallas_call(
        _moba_body,
        grid=(batch, N_HEADS_MOBA // HEADS_PER_STEP),
        in_specs=[pl.BlockSpec(memory_space=pltpu.SMEM),
                  q_spec,
                  pl.BlockSpec((1, nb, MOBA_BLOCK, width), lambda b, g: (b, 0, 0, g)),
                  q_spec,
                  pl.BlockSpec((1, nb, width), lambda b, g: (b, 0, g)),
                  bias_spec, bias_spec],
        out_specs=q_spec,
        out_shape=jax.ShapeDtypeStruct((batch, nb, W_MOBA, MOBA_BLOCK), _F32),
        scratch_shapes=[pltpu.VMEM((HEADS_PER_STEP, nb, MOBA_BLOCK), _F32),
                        pltpu.VMEM((HEADS_PER_STEP, HEADS_PER_KEY_TILE * HEAD_DIM, MOBA_BLOCK), _BF16),
                        pltpu.VMEM((GROUPS_PER_ITER + PIPE_DEPTH, HEADS_PER_STEP, MOBA_BLOCK, MOBA_BLOCK), _F32),
                        pltpu.VMEM((HEADS_PER_STEP, HEAD_DIM + SUM_ROWS, MOBA_BLOCK), _F32)],
        compiler_params=_params(2),
        name="moba",
    )(rel_bias, qa_t, ka, va_t, kmean, bias_own, bias_prev)


def _swa_body(sinks_ref, q_ref, k_prev_ref, k_cur_ref, v_prev_ref, v_cur_ref, bias_ref, o_ref, s_ref):
    no_prev = jnp.where(pl.program_id(1) == 0, NEG, 0.0)
    window_row = lax.broadcasted_iota(jnp.int32, (2 * SWA_BLOCK, SWA_BLOCK), 0)
    first_window_mask = jnp.where(window_row < SWA_BLOCK, no_prev, 0.0)
    zeros = jnp.zeros((HEAD_DIM, SWA_BLOCK), _BF16)
    ones = jnp.ones((SUM_ROWS, 2 * SWA_BLOCK), _BF16)
    k_blocks = [k_prev_ref[0, 0]] + [k_cur_ref[0, t] for t in range(SWA_Q_PER_STEP)]
    v_blocks = [v_prev_ref[0, 0]] + [v_cur_ref[0, t] for t in range(SWA_Q_PER_STEP)]
    chains = [(t, kv) for t in range(SWA_Q_PER_STEP) for kv in range(N_KV_SWA)]

    for t, kv in chains:
        q_pad = jnp.concatenate(
            [jnp.concatenate(
                [q_ref[0, t, h * HEAD_DIM:(h + 1) * HEAD_DIM, :] if part == kv else zeros
                 for part in range(N_KV_SWA)], axis=0)
             for h in range(kv * SWA_GROUP, (kv + 1) * SWA_GROUP)], axis=1)
        window_keys = jnp.concatenate([k_blocks[t], k_blocks[t + 1]], axis=0)
        s = _dot(window_keys, q_pad)
        for slot in range(SWA_GROUP):
            cols = slice(slot * SWA_BLOCK, (slot + 1) * SWA_BLOCK)
            biased = s[:, cols] + bias_ref[kv * SWA_GROUP + slot]
            s_ref[t, kv, :, cols] = biased + first_window_mask if t == 0 else biased

    for t, kv in chains:
        kv_rows = slice(kv * HEAD_DIM, (kv + 1) * HEAD_DIM)
        window_values = jnp.concatenate(
            [jnp.concatenate([v_blocks[t][kv_rows, :], v_blocks[t + 1][kv_rows, :]], axis=1), ones], axis=0)
        p, sink_p = [], []
        for slot in range(SWA_GROUP):
            cols = slice(slot * SWA_BLOCK, (slot + 1) * SWA_BLOCK)
            sink = sinks_ref[kv * SWA_GROUP + slot] * LOG2E
            m = jnp.maximum(jnp.max(s_ref[t, kv, :, cols], axis=0, keepdims=True), sink)
            p.append(jnp.exp2(s_ref[t, kv, :, cols] - m).astype(_BF16))
            sink_p.append(jnp.exp2(sink - m))
        acc = _dot(window_values, jnp.concatenate(p, axis=1))
        for slot in range(SWA_GROUP):
            h = kv * SWA_GROUP + slot
            cols = slice(slot * SWA_BLOCK, (slot + 1) * SWA_BLOCK)
            l = acc[HEAD_DIM:HEAD_DIM + 1, cols] + sink_p[slot]
            o_ref[0, t, h * HEAD_DIM:(h + 1) * HEAD_DIM, :] = acc[:HEAD_DIM, cols] / l


def _swa(sinks, qs_t, ks, vs_t, bias_swa):
    batch, nb = qs_t.shape[0], qs_t.shape[1]

    def cur(rows, cols):
        return pl.BlockSpec((1, SWA_Q_PER_STEP, rows, cols), lambda b, n: (b, n, 0, 0))

    def prev(rows, cols):
        return pl.BlockSpec((1, 1, rows, cols),
                            lambda b, n: (b, jnp.maximum(n * SWA_Q_PER_STEP - 1, 0), 0, 0))

    return pl.pallas_call(
        _swa_body,
        grid=(batch, nb // SWA_Q_PER_STEP),
        in_specs=[pl.BlockSpec(memory_space=pltpu.SMEM),
                  cur(W_SWA, SWA_BLOCK),
                  prev(SWA_BLOCK, W_SWA_KV), cur(SWA_BLOCK, W_SWA_KV),
                  prev(W_SWA_KV, SWA_BLOCK), cur(W_SWA_KV, SWA_BLOCK),
                  _const_spec((N_HEADS_SWA, 2 * SWA_BLOCK, SWA_BLOCK))],
        out_specs=cur(W_SWA, SWA_BLOCK),
        out_shape=jax.ShapeDtypeStruct((batch, nb, W_SWA, SWA_BLOCK), _F32),
        scratch_shapes=[pltpu.VMEM((SWA_Q_PER_STEP, N_KV_SWA, 2 * SWA_BLOCK, SWA_GROUP * SWA_BLOCK), _F32)],
        compiler_params=_params(2),
        name="swa",
    )(sinks, qs_t, ks, ks, vs_t, vs_t, bias_swa)


def _group_norm_rows(o_ref, g_ref):
    rows = []
    for t in range(o_ref.shape[0]):
        o_t = o_ref[t]
        scale = lax.rsqrt(jnp.mean(o_t * o_t, axis=0, keepdims=True) + RMS_EPS)
        rows.append(((o_t * scale).T * g_ref[...]).astype(_BF16))
    return jnp.concatenate(rows, axis=0)


def _mix_ffn_body(x_ref, oa_ref, ob_ref, ga_ref, gb_ref, w_out_ref, mix_post_g_ref,
                  pre_g_ref, wg_hbm, wu_hbm, wd_hbm, post_g_ref, o_ref, h_ref, *weight_scratch):
    wg_ref, wu_ref, wd_ref = _ffn_weights((wg_hbm, wu_hbm, wd_hbm), weight_scratch)
    y = (_dot(_group_norm_rows(oa_ref, ga_ref), w_out_ref[:W_MOBA, :])
         + _dot(_group_norm_rows(ob_ref, gb_ref), w_out_ref[W_MOBA:, :]))
    x = x_ref[...] + _rms_rows(y, mix_post_g_ref[...])
    _ffn_tile(lambda rows: x[rows, :], pre_g_ref, wg_ref, wu_ref, wd_ref, post_g_ref, o_ref, h_ref)


def _mix_ffn(x2d, oa_t, ob_t, ga, gb, w_out, mix_post_g, ffn_params):
    n_tok = x2d.shape[0]
    return pl.pallas_call(
        _mix_ffn_body,
        grid=(n_tok // TOKEN_TILE,),
        in_specs=[_TOKEN_TILE_SPEC,
                  pl.BlockSpec((_MOBA_PER_TILE, W_MOBA, MOBA_BLOCK), lambda i: (i, 0, 0)),
                  pl.BlockSpec((_SWA_PER_TILE, W_SWA, SWA_BLOCK), lambda i: (i, 0, 0)),
                  _const_spec((1, W_MOBA)), _const_spec((1, W_SWA)),
                  _const_spec((W_MOBA + W_SWA, D_MODEL)),
                  _const_spec((1, D_MODEL))] + _FFN_WEIGHT_SPECS,
        out_specs=_TOKEN_TILE_SPEC,
        out_shape=jax.ShapeDtypeStruct((n_tok, D_MODEL), _F32),
        scratch_shapes=_FFN_SCRATCH,
        compiler_params=_params(1),
        name="mix_ffn",
    )(x2d, oa_t, ob_t, ga, gb, w_out, mix_post_g, *ffn_params)


def _row(v):
    return v.reshape(1, -1)


def _layer(x, ffn1, mix, ffn2, rel_bias, bias_tiles):
    batch, seq, _ = x.shape
    (mix_pre_g, w_in, moba_out_g, swa_sinks, swa_out_g, w_out, mix_post_g) = mix
    bias_own, bias_prev, bias_swa = bias_tiles

    def ffn_params(params):
        pre_g, w_gate, w_up, w_down, post_g = params
        return (_row(pre_g), w_gate, w_up, w_down, _row(post_g))

    x = _ffn(x.reshape(batch * seq, D_MODEL), ffn_params(ffn1))

    qa, ka, va, qs, ks, vs = jnp.split(
        w_in, [W_MOBA, 2 * W_MOBA, 3 * W_MOBA, 3 * W_MOBA + W_SWA, 3 * W_MOBA + W_SWA + W_SWA_KV], axis=1)
    w_rows = jnp.concatenate([ka, ks], axis=1).astype(_BF16)
    w_t = jnp.concatenate([qa, va, qs, vs], axis=1).astype(_BF16).T
    qa_t, va_t, qs_t, vs_t, ka_b, ks_b, kmean = _in_proj(
        x.reshape(batch, seq, D_MODEL), _row(mix_pre_g), w_rows, w_t)

    oa_t = _moba(rel_bias, qa_t, ka_b, va_t, kmean.reshape(batch, seq // MOBA_BLOCK, W_MOBA),
                 bias_own, bias_prev)
    ob_t = _swa(swa_sinks, qs_t, ks_b, vs_t, bias_swa)

    x = _mix_ffn(x, oa_t.reshape(-1, W_MOBA, MOBA_BLOCK), ob_t.reshape(-1, W_SWA, SWA_BLOCK),
                 _row(moba_out_g), _row(swa_out_g), w_out.astype(_BF16), _row(mix_post_g),
                 ffn_params(ffn2))
    return x.reshape(batch, seq, D_MODEL)


def kernel(x, ffn1_pre_g, ffn1_w_gate, ffn1_w_up, ffn1_w_down, ffn1_post_g, mix_pre_g, w_in, rel_bias,
           moba_out_g, swa_sinks, swa_out_g, w_out, mix_post_g, ffn2_pre_g, ffn2_w_gate, ffn2_w_up,
           ffn2_w_down, ffn2_post_g):
    bias_tiles = _bias_tiles(rel_bias)
    for l in range(ffn1_pre_g.shape[0]):
        x = _layer(
            x,
            (ffn1_pre_g[l], ffn1_w_gate[l], ffn1_w_up[l], ffn1_w_down[l], ffn1_post_g[l]),
            (mix_pre_g[l], w_in[l], moba_out_g[l], swa_sinks[l], swa_out_g[l], w_out[l], mix_post_g[l]),
            (ffn2_pre_g[l], ffn2_w_gate[l], ffn2_w_up[l], ffn2_w_down[l], ffn2_post_g[l]),
            rel_bias, bias_tiles)
    return x
```

```python
import math

import jax
import jax.numpy as jnp
from jax import lax
from jax.experimental import pallas as pl
from jax.experimental.pallas import tpu as pltpu

D_MODEL = 1024
HEAD_DIM = 64
N_HEADS_MOBA = 8
N_HEADS_SWA = 8
N_KV_SWA = 2
SWA_GROUP = N_HEADS_SWA // N_KV_SWA
W_MOBA = N_HEADS_MOBA * HEAD_DIM
W_SWA = N_HEADS_SWA * HEAD_DIM
W_SWA_KV = N_KV_SWA * HEAD_DIM
MOBA_BLOCK = 256
MOBA_TOPK = 3
SWA_BLOCK = 128
SWA_WINDOW = 128
NUM_BUCKETS = 32
MAX_DISTANCE = 128
D_FF = 2816
RMS_EPS = 1e-6
FFN_RES_WEIGHT = 0.5
NEG = -1e30
QK_SCALE = HEAD_DIM ** -0.5
LOG2E = math.log2(math.e)
Q_FOLD = QK_SCALE * LOG2E
SUM_ROWS = 16

TOKEN_TILE = 1024
FF_CHUNK = 256
FFN_ROW_PARTS = 2
CAST_CHUNKS = 16
CAST_SLOTS = 4
HEADS_PER_STEP = 4
HEADS_PER_KEY_TILE = 2
PIPE_DEPTH = 2
GROUPS_PER_ITER = 4
RING_UNROLL = 2
SWA_Q_PER_STEP = 4
VMEM_LIMIT_BYTES = 58 * 1024 * 1024

_BF16 = jnp.bfloat16
_F32 = jnp.float32


def _dot(a, b):
    return jnp.dot(a, b, preferred_element_type=_F32)


def _dot_nt(a, b):
    return lax.dot_general(a, b, (((1,), (1,)), ((), ())), preferred_element_type=_F32)


def _rms_rows(x, g):
    return x * lax.rsqrt(jnp.mean(x * x, axis=-1, keepdims=True) + RMS_EPS) * g


def _const_spec(shape):
    return pl.BlockSpec(shape, lambda *_: (0,) * len(shape), pipeline_mode=pl.Buffered(1))


def _params(n_axes):
    return pltpu.CompilerParams(
        dimension_semantics=("arbitrary",) * n_axes, vmem_limit_bytes=VMEM_LIMIT_BYTES)


def _ffn_tile(x_rows, pre_g_ref, wg_ref, wu_ref, wd_ref, post_g_ref, o_ref, h_ref):
    part = TOKEN_TILE // FFN_ROW_PARTS
    for p in range(FFN_ROW_PARTS):
        rows = slice(p * part, (p + 1) * part)
        x = x_rows(rows)
        xn = _rms_rows(x, pre_g_ref[...]).astype(_BF16)
        for c in range(D_FF // FF_CHUNK):
            cols = slice(c * FF_CHUNK, (c + 1) * FF_CHUNK)
            gate = _dot(xn, wg_ref[:, cols])
            up = _dot(xn, wu_ref[:, cols])
            h_ref[rows, cols] = (jax.nn.silu(gate) * up).astype(_BF16)
        y = _dot(h_ref[rows, :], wd_ref[...])
        o_ref[rows, :] = x + FFN_RES_WEIGHT * _rms_rows(y, post_g_ref[...])


def _cast_weights(pairs, stage, sem):
    slots, rows = stage.shape[0], stage.shape[1]
    chunks = [(w_hbm, w_bf, k) for w_hbm, w_bf in pairs for k in range(w_hbm.shape[0] // rows)]

    def copy(c):
        w_hbm, _, k = chunks[c]
        return pltpu.make_async_copy(w_hbm.at[pl.ds(k * rows, rows), :], stage.at[c % slots], sem.at[c % slots])

    def prime():
        for c in range(min(slots, len(chunks))):
            copy(c).start()

    def drain():
        for c, (_, w_bf, k) in enumerate(chunks):
            copy(c).wait()
            w_bf[k * rows:(k + 1) * rows, :] = stage[c % slots].astype(_BF16)
            if c + slots < len(chunks):
                copy(c + slots).start()

    return prime, drain


def _ffn_weights(w_hbm, scratch):
    wg_bf, wu_bf, wd_bf, stage_in, stage_down, sem_in, sem_down = scratch

    @pl.when(pl.program_id(0) == 0)
    def _():
        prime_in, drain_in = _cast_weights([(w_hbm[0], wg_bf), (w_hbm[1], wu_bf)], stage_in, sem_in)
        prime_down, drain_down = _cast_weights([(w_hbm[2], wd_bf)], stage_down, sem_down)
        prime_in()
        prime_down()
        drain_in()
        drain_down()

    return wg_bf, wu_bf, wd_bf


def _ffn_body(x_ref, pre_g_ref, wg_hbm, wu_hbm, wd_hbm, post_g_ref, o_ref, h_ref, *weight_scratch):
    wg_ref, wu_ref, wd_ref = _ffn_weights((wg_hbm, wu_hbm, wd_hbm), weight_scratch)
    _ffn_tile(lambda rows: x_ref[rows, :], pre_g_ref, wg_ref, wu_ref, wd_ref, post_g_ref, o_ref, h_ref)


_TOKEN_TILE_SPEC = pl.BlockSpec((TOKEN_TILE, D_MODEL), lambda i: (i, 0))
_HBM_SPEC = pl.BlockSpec(memory_space=pl.ANY)
_FFN_WEIGHT_SPECS = [_const_spec((1, D_MODEL)), _HBM_SPEC, _HBM_SPEC, _HBM_SPEC, _const_spec((1, D_MODEL))]
_FFN_SCRATCH = [
    pltpu.VMEM((TOKEN_TILE, D_FF), _BF16),
    pltpu.VMEM((D_MODEL, D_FF), _BF16),
    pltpu.VMEM((D_MODEL, D_FF), _BF16),
    pltpu.VMEM((D_FF, D_MODEL), _BF16),
    pltpu.VMEM((CAST_SLOTS, D_MODEL // CAST_CHUNKS, D_FF), _F32),
    pltpu.VMEM((CAST_SLOTS, D_FF // CAST_CHUNKS, D_MODEL), _F32),
    pltpu.SemaphoreType.DMA((CAST_SLOTS,)),
    pltpu.SemaphoreType.DMA((CAST_SLOTS,)),
]


def _ffn(x2d, ffn_params):
    n_tok = x2d.shape[0]
    return pl.pallas_call(
        _ffn_body,
        grid=(n_tok // TOKEN_TILE,),
        in_specs=[_TOKEN_TILE_SPEC] + _FFN_WEIGHT_SPECS,
        out_specs=_TOKEN_TILE_SPEC,
        out_shape=jax.ShapeDtypeStruct((n_tok, D_MODEL), _F32),
        scratch_shapes=_FFN_SCRATCH,
        compiler_params=_params(1),
        name="ffn",
    )(x2d, *ffn_params)


_T_QA, _T_VA, _T_QS, _T_VS = 0, W_MOBA, 2 * W_MOBA, 2 * W_MOBA + W_SWA
_T_ROWS = 2 * W_MOBA + W_SWA + W_SWA_KV
_MOBA_PER_TILE = TOKEN_TILE // MOBA_BLOCK
_SWA_PER_TILE = TOKEN_TILE // SWA_BLOCK


def _in_proj_body(x_ref, g_ref, w_rows_ref, w_t_ref,
                  qa_ref, va_ref, qs_ref, vs_ref, ka_ref, ks_ref, kmean_ref):
    swa_per_moba = MOBA_BLOCK // SWA_BLOCK
    for t in range(_MOBA_PER_TILE):
        xn = _rms_rows(x_ref[0, t * MOBA_BLOCK:(t + 1) * MOBA_BLOCK, :], g_ref[...]).astype(_BF16)
        k_rows = _dot(xn, w_rows_ref[...])
        proj_t = _dot_nt(w_t_ref[...], xn)
        k_blk = k_rows[:, :W_MOBA]
        ka_ref[0, t] = k_blk.astype(_BF16)
        kmean_ref[0, t] = jnp.mean(k_blk, axis=0, keepdims=True)
        qa_ref[0, t] = (proj_t[_T_QA:_T_QA + W_MOBA, :] * Q_FOLD).astype(_BF16)
        va_ref[0, t] = proj_t[_T_VA:_T_VA + W_MOBA, :].astype(_BF16)
        for u in range(swa_per_moba):
            rows = slice(u * SWA_BLOCK, (u + 1) * SWA_BLOCK)
            ks_ref[0, t * swa_per_moba + u] = k_rows[rows, W_MOBA:].astype(_BF16)
            qs_ref[0, t * swa_per_moba + u] = (proj_t[_T_QS:_T_QS + W_SWA, rows] * Q_FOLD).astype(_BF16)
            vs_ref[0, t * swa_per_moba + u] = proj_t[_T_VS:_T_VS + W_SWA_KV, rows].astype(_BF16)


def _in_proj(x, g, w_rows, w_t):
    batch, seq, _ = x.shape
    nb_moba, nb_swa = seq // MOBA_BLOCK, seq // SWA_BLOCK

    def blocked(n_per_tile, rows, cols):
        return pl.BlockSpec((1, n_per_tile, rows, cols), lambda b, i: (b, i, 0, 0))

    return pl.pallas_call(
        _in_proj_body,
        grid=(batch, seq // TOKEN_TILE),
        in_specs=[pl.BlockSpec((1, TOKEN_TILE, D_MODEL), lambda b, i: (b, i, 0)),
                  _const_spec((1, D_MODEL)),
                  _const_spec((D_MODEL, W_MOBA + W_SWA_KV)),
                  _const_spec((_T_ROWS, D_MODEL))],
        out_specs=[blocked(_MOBA_PER_TILE, W_MOBA, MOBA_BLOCK),
                   blocked(_MOBA_PER_TILE, W_MOBA, MOBA_BLOCK),
                   blocked(_SWA_PER_TILE, W_SWA, SWA_BLOCK),
                   blocked(_SWA_PER_TILE, W_SWA_KV, SWA_BLOCK),
                   blocked(_MOBA_PER_TILE, MOBA_BLOCK, W_MOBA),
                   blocked(_SWA_PER_TILE, SWA_BLOCK, W_SWA_KV),
                   blocked(_MOBA_PER_TILE, 1, W_MOBA)],
        out_shape=[jax.ShapeDtypeStruct((batch, nb_moba, W_MOBA, MOBA_BLOCK), _BF16),
                   jax.ShapeDtypeStruct((batch, nb_moba, W_MOBA, MOBA_BLOCK), _BF16),
                   jax.ShapeDtypeStruct((batch, nb_swa, W_SWA, SWA_BLOCK), _BF16),
                   jax.ShapeDtypeStruct((batch, nb_swa, W_SWA_KV, SWA_BLOCK), _BF16),
                   jax.ShapeDtypeStruct((batch, nb_moba, MOBA_BLOCK, W_MOBA), _BF16),
                   jax.ShapeDtypeStruct((batch, nb_swa, SWA_BLOCK, W_SWA_KV), _BF16),
                   jax.ShapeDtypeStruct((batch, nb_moba, 1, W_MOBA), _F32)],
        compiler_params=_params(2),
        name="in_proj",
    )(x, g, w_rows, w_t)


def _t5_bucket(dist):
    n = jnp.maximum(dist, 0)
    max_exact = NUM_BUCKETS // 2
    nf = jnp.maximum(n, 1).astype(_F32)
    large = max_exact + jnp.floor(jnp.log(nf / max_exact) / math.log(MAX_DISTANCE / max_exact)
                                  * (NUM_BUCKETS - max_exact))
    large = jnp.minimum(large, float(NUM_BUCKETS - 1))
    return jnp.where(n < max_exact, n.astype(_F32), large)


def _bias_lookup(rel_bias_ref, bucket, head):
    val = jnp.full(bucket.shape, rel_bias_ref[NUM_BUCKETS - 1, head], _F32)
    for b in range(NUM_BUCKETS - 2, -1, -1):
        val = jnp.where(bucket == float(b), rel_bias_ref[b, head], val)
    return val


def _bias_tiles_body(rel_bias_ref, own_ref, prev_ref, swa_ref):
    assert MOBA_BLOCK == 2 * SWA_BLOCK and SWA_WINDOW == SWA_BLOCK and MAX_DISTANCE <= SWA_BLOCK
    h = pl.program_id(0)
    q = SWA_BLOCK
    key = lax.broadcasted_iota(jnp.int32, (q, q), 0)
    qry = lax.broadcasted_iota(jnp.int32, (q, q), 1)
    dist = qry - key

    def quadrants(head):
        near = jnp.where(dist >= 0, _bias_lookup(rel_bias_ref, _t5_bucket(dist), head) * LOG2E, NEG)
        nxt = _bias_lookup(rel_bias_ref, _t5_bucket(dist + q), head) * LOG2E
        return near, nxt

    near, nxt = quadrants(h)
    far = jnp.full((q, q), rel_bias_ref[NUM_BUCKETS - 1, h] * LOG2E, _F32)
    masked = jnp.full((q, q), NEG, _F32)
    own_ref[0, :q, :q] = near
    own_ref[0, :q, q:] = nxt
    own_ref[0, q:, :q] = masked
    own_ref[0, q:, q:] = near
    prev_ref[0, :q, :q] = far
    prev_ref[0, :q, q:] = far
    prev_ref[0, q:, :q] = nxt
    prev_ref[0, q:, q:] = far
    near, nxt = quadrants(h + N_HEADS_MOBA)
    swa_ref[0, :q, :] = jnp.where(dist < 0, nxt, NEG)
    swa_ref[0, q:, :] = near


def _bias_tiles(rel_bias):
    def per_head(rows, cols):
        return pl.BlockSpec((1, rows, cols), lambda h: (h, 0, 0))

    return pl.pallas_call(
        _bias_tiles_body,
        grid=(N_HEADS_MOBA,),
        in_specs=[pl.BlockSpec(memory_space=pltpu.SMEM)],
        out_specs=[per_head(MOBA_BLOCK, MOBA_BLOCK), per_head(MOBA_BLOCK, MOBA_BLOCK),
                   per_head(2 * SWA_BLOCK, SWA_BLOCK)],
        out_shape=[jax.ShapeDtypeStruct((N_HEADS_MOBA, MOBA_BLOCK, MOBA_BLOCK), _F32),
                   jax.ShapeDtypeStruct((N_HEADS_MOBA, MOBA_BLOCK, MOBA_BLOCK), _F32),
                   jax.ShapeDtypeStruct((N_HEADS_SWA, 2 * SWA_BLOCK, SWA_BLOCK), _F32)],
        compiler_params=_params(1),
        name="bias_tiles",
    )(rel_bias)


def _moba_body(rel_bias_ref, q_ref, k_ref, v_ref, kmean_ref, own_ref, prev_ref, o_ref,
               far_ref, qpad_ref, s_ref, acc_ref):
    g = pl.program_id(1)
    nb = k_ref.shape[1]
    key_lanes = HEADS_PER_KEY_TILE * HEAD_DIM
    q_row = lax.broadcasted_iota(jnp.int32, (key_lanes, MOBA_BLOCK), 0)
    km = kmean_ref[0]
    km_hi = km.astype(_BF16)
    km_lo = (km - km_hi.astype(_F32)).astype(_BF16)
    blk = lax.broadcasted_iota(jnp.int32, (nb, MOBA_BLOCK), 0)
    ones = jnp.ones((SUM_ROWS, MOBA_BLOCK), _BF16)
    ring_slots = range(GROUPS_PER_ITER)
    tail_slots = ring_slots[-PIPE_DEPTH:]
    near_slots = range(GROUPS_PER_ITER, GROUPS_PER_ITER + PIPE_DEPTH)
    assert PIPE_DEPTH == 2 and GROUPS_PER_ITER >= 2 * PIPE_DEPTH

    def key_tile(r):
        t, sub = divmod(r, HEADS_PER_KEY_TILE)
        return slice(t * key_lanes, (t + 1) * key_lanes), sub

    def prepare(i):
        past = blk < i
        own, prev, prev_sel = [], [], []
        for r in range(HEADS_PER_STEP):
            tile_lanes, sub = key_tile(r)
            q_t = q_ref[0, i, tile_lanes, :]
            q_pad = jnp.where((q_row >= sub * HEAD_DIM) & (q_row < (sub + 1) * HEAD_DIM), q_t, 0)
            qpad_ref[r] = q_pad

            gate = jnp.where(
                past, _dot(km_hi[:, tile_lanes], q_pad) + _dot(km_lo[:, tile_lanes], q_pad), NEG)
            sel = jnp.zeros(gate.shape, jnp.bool_)
            for _ in range(MOBA_TOPK):
                top = jnp.max(gate, axis=0, keepdims=True)
                first = jnp.min(jnp.where(gate == top, blk, nb), axis=0, keepdims=True)
                pick = blk == first
                sel = sel | pick
                gate = jnp.where(pick, -jnp.inf, gate)
            sel = sel & past

            far_bias = rel_bias_ref[NUM_BUCKETS - 1, g * HEADS_PER_STEP + r] * LOG2E
            far_ref[r] = jnp.where(sel & (blk < i - 1), far_bias, NEG)
            prev_sel.append(jnp.max(jnp.where(sel & (blk == i - 1), 0.0, NEG), axis=0, keepdims=True))
            own.append(own_ref[r])
            prev.append(prev_ref[r])
        return own, prev, prev_sel

    all_heads = range(HEADS_PER_STEP)
    head_parts = (all_heads[:HEADS_PER_KEY_TILE], all_heads[HEADS_PER_KEY_TILE:])

    def score(slot, j, tile_bias=None, row_bias=None, heads=all_heads):
        tops, shifts = {}, {}
        for r in heads:
            s = _dot(k_ref[0, j, :, key_tile(r)[0]], qpad_ref[r])
            if tile_bias is not None:
                s = s + tile_bias[r]
            s_ref[slot, r] = s
            top = jnp.max(s, axis=0, keepdims=True)
            shift = jnp.zeros_like(top) if row_bias is None else row_bias[r]
            tops[r] = top + shift
            shifts[r] = shift
        return slot, j, tops, shifts

    def absorb(carry, scored, heads=all_heads):
        slot, j, tops, shifts = scored
        out = list(carry)
        for r in heads:
            lanes = slice(r * HEAD_DIM, (r + 1) * HEAD_DIM)
            m = carry[r]
            m_new = jnp.maximum(m, tops[r])
            p = jnp.exp2(s_ref[slot, r] + (shifts[r] - m_new)).astype(_BF16)
            pv = _dot(jnp.concatenate([v_ref[0, j, lanes, :], ones], axis=0), p)
            acc_ref[r] = jnp.exp2(m - m_new) * acc_ref[r] + pv
            out[r] = m_new
        return tuple(out)

    def score_far(slot, grp, heads=all_heads):
        j = jnp.minimum(grp, nb - 1)
        return score(slot, j, row_bias={r: far_ref[r, pl.ds(j, 1), :] for r in heads}, heads=heads)

    def score_near(i):
        own, prev, prev_sel = prepare(i)
        return (score(near_slots[0], i, tile_bias=own)[2:],
                score(near_slots[1], jnp.maximum(i - 1, 0), tile_bias=prev, row_bias=prev_sel)[2:])

    def start(i, near):
        carry = tuple(jnp.full((1, MOBA_BLOCK), -jnp.inf, _F32) for _ in range(HEADS_PER_STEP))
        acc_ref[...] = jnp.zeros(acc_ref.shape, _F32)
        near_js = (i, jnp.maximum(i - 1, 0))
        ring = {slot: (slot, j) + maxes_shifts for slot, j, maxes_shifts in zip(near_slots, near_js, near)}
        for c, slot in enumerate(tail_slots):
            ring[slot] = score_far(slot, c)
            carry = absorb(carry, ring[near_slots[c]])
        return carry, tuple(ring[slot][1:] for slot in tail_slots)

    def tile(i, phase, state, last=False):
        def ring_steps(first_group, count, carry, pending):
            ring = {slot: (slot,) + p for slot, p in zip(tail_slots, pending)}
            for k in range(count):
                tops, shifts = {}, {}
                for heads in head_parts:
                    slot, j, part_tops, part_shifts = score_far(k % GROUPS_PER_ITER, first_group + k, heads)
                    carry = absorb(carry, ring[(k - PIPE_DEPTH) % GROUPS_PER_ITER], heads)
                    tops.update(part_tops)
                    shifts.update(part_shifts)
                ring[k % GROUPS_PER_ITER] = (slot, j, tops, shifts)
            return carry, ring

        def ring_loop(first_group, turns, trips, state):
            def step(it, state):
                groups = turns * GROUPS_PER_ITER
                carry, ring = ring_steps(first_group + it * groups, groups, *state)
                return carry, tuple(ring[slot][1:] for slot in tail_slots)
            return lax.fori_loop(0, trips, step, state)

        leftover = (phase - 1 - PIPE_DEPTH) % GROUPS_PER_ITER
        n_turns = jnp.maximum(i - 1 - PIPE_DEPTH, 0) // GROUPS_PER_ITER
        n_long = n_turns // RING_UNROLL
        done = n_long * RING_UNROLL
        state = ring_loop(PIPE_DEPTH, RING_UNROLL, n_long, state)
        state = ring_loop(PIPE_DEPTH + done * GROUPS_PER_ITER, 1, n_turns - done, state)
        carry, ring = ring_steps(PIPE_DEPTH + n_turns * GROUPS_PER_ITER, leftover, *state)

        near_next = None if last else score_near(i + 1)
        for c in range(PIPE_DEPTH):
            carry = absorb(carry, ring[(leftover - PIPE_DEPTH + c) % GROUPS_PER_ITER])
        for r in range(HEADS_PER_STEP):
            acc = acc_ref[r]
            o_ref[0, i, r * HEAD_DIM:(r + 1) * HEAD_DIM, :] = acc[:HEAD_DIM] / acc[HEAD_DIM:HEAD_DIM + 1]
        return None if last else start(i + 1, near_next)

    def tiles(a, state, last=False):
        for phase in range(GROUPS_PER_ITER):
            state = tile(a * GROUPS_PER_ITER + phase, phase, state,
                         last=last and phase == GROUPS_PER_ITER - 1)
        return state

    assert nb % GROUPS_PER_ITER == 0
    state = lax.fori_loop(0, nb // GROUPS_PER_ITER - 1, tiles, start(0, score_near(0)))
    tiles(nb // GROUPS_PER_ITER - 1, state, last=True)


def _moba(rel_bias, qa_t, ka, va_t, kmean, bias_own, bias_prev):
    batch, nb = qa_t.shape[0], qa_t.shape[1]
    width = HEADS_PER_STEP * HEAD_DIM
    q_spec = pl.BlockSpec((1, nb, width, MOBA_BLOCK), lambda b, g: (b, 0, g, 0))
    bias_spec = pl.BlockSpec((HEADS_PER_STEP, MOBA_BLOCK, MOBA_BLOCK), lambda b, g: (g, 0, 0))
    return pl.pallas_call(
        _moba_body,
        grid=(batch, N_HEADS_MOBA // HEADS_PER_STEP),
        in_specs=[pl.BlockSpec(memory_space=pltpu.SMEM),
                  q_spec,
                  pl.BlockSpec((1, nb, MOBA_BLOCK, width), lambda b, g: (b, 0, 0, g)),
                  q_spec,
                  pl.BlockSpec((1, nb, width), lambda b, g: (b, 0, g)),
                  bias_spec, bias_spec],
        out_specs=q_spec,
        out_shape=jax.ShapeDtypeStruct((batch, nb, W_MOBA, MOBA_BLOCK), _F32),
        scratch_shapes=[pltpu.VMEM((HEADS_PER_STEP, nb, MOBA_BLOCK), _F32),
                        pltpu.VMEM((HEADS_PER_STEP, HEADS_PER_KEY_TILE * HEAD_DIM, MOBA_BLOCK), _BF16),
                        pltpu.VMEM((GROUPS_PER_ITER + PIPE_DEPTH, HEADS_PER_STEP, MOBA_BLOCK, MOBA_BLOCK), _F32),
                        pltpu.VMEM((HEADS_PER_STEP, HEAD_DIM + SUM_ROWS, MOBA_BLOCK), _F32)],
        compiler_params=_params(2),
        name="moba",
    )(rel_bias, qa_t, ka, va_t, kmean, bias_own, bias_prev)


def _swa_body(sinks_ref, q_ref, k_prev_ref, k_cur_ref, v_prev_ref, v_cur_ref, bias_ref, o_ref, s_ref):
    no_prev = jnp.where(pl.program_id(1) == 0, NEG, 0.0)
    window_row = lax.broadcasted_iota(jnp.int32, (2 * SWA_BLOCK, SWA_BLOCK), 0)
    first_window_mask = jnp.where(window_row < SWA_BLOCK, no_prev, 0.0)
    zeros = jnp.zeros((HEAD_DIM, SWA_BLOCK), _BF16)
    ones = jnp.ones((SUM_ROWS, 2 * SWA_BLOCK), _BF16)
    k_blocks = [k_prev_ref[0, 0]] + [k_cur_ref[0, t] for t in range(SWA_Q_PER_STEP)]
    v_blocks = [v_prev_ref[0, 0]] + [v_cur_ref[0, t] for t in range(SWA_Q_PER_STEP)]
    chains = [(t, kv) for t in range(SWA_Q_PER_STEP) for kv in range(N_KV_SWA)]

    for t, kv in chains:
        q_pad = jnp.concatenate(
            [jnp.concatenate(
                [q_ref[0, t, h * HEAD_DIM:(h + 1) * HEAD_DIM, :] if part == kv else zeros
                 for part in range(N_KV_SWA)], axis=0)
             for h in range(kv * SWA_GROUP, (kv + 1) * SWA_GROUP)], axis=1)
        window_keys = jnp.concatenate([k_blocks[t], k_blocks[t + 1]], axis=0)
        s = _dot(window_keys, q_pad)
        for slot in range(SWA_GROUP):
            cols = slice(slot * SWA_BLOCK, (slot + 1) * SWA_BLOCK)
            biased = s[:, cols] + bias_ref[kv * SWA_GROUP + slot]
            s_ref[t, kv, :, cols] = biased + first_window_mask if t == 0 else biased

    for t, kv in chains:
        kv_rows = slice(kv * HEAD_DIM, (kv + 1) * HEAD_DIM)
        window_values = jnp.concatenate(
            [jnp.concatenate([v_blocks[t][kv_rows, :], v_blocks[t + 1][kv_rows, :]], axis=1), ones], axis=0)
        p, sink_p = [], []
        for slot in range(SWA_GROUP):
            cols = slice(slot * SWA_BLOCK, (slot + 1) * SWA_BLOCK)
            sink = sinks_ref[kv * SWA_GROUP + slot] * LOG2E
            m = jnp.maximum(jnp.max(s_ref[t, kv, :, cols], axis=0, keepdims=True), sink)
            p.append(jnp.exp2(s_ref[t, kv, :, cols] - m).astype(_BF16))
            sink_p.append(jnp.exp2(sink - m))
        acc = _dot(window_values, jnp.concatenate(p, axis=1))
        for slot in range(SWA_GROUP):
            h = kv * SWA_GROUP + slot
            cols = slice(slot * SWA_BLOCK, (slot + 1) * SWA_BLOCK)
            l = acc[HEAD_DIM:HEAD_DIM + 1, cols] + sink_p[slot]
            o_ref[0, t, h * HEAD_DIM:(h + 1) * HEAD_DIM, :] = acc[:HEAD_DIM, cols] / l


def _swa(sinks, qs_t, ks, vs_t, bias_swa):
    batch, nb = qs_t.shape[0], qs_t.shape[1]

    def cur(rows, cols):
        return pl.BlockSpec((1, SWA_Q_PER_STEP, rows, cols), lambda b, n: (b, n, 0, 0))

    def prev(rows, cols):
        return pl.BlockSpec((1, 1, rows, cols),
                            lambda b, n: (b, jnp.maximum(n * SWA_Q_PER_STEP - 1, 0), 0, 0))

    return pl.pallas_call(
        _swa_body,
        grid=(batch, nb // SWA_Q_PER_STEP),
        in_specs=[pl.BlockSpec(memory_space=pltpu.SMEM),
                  cur(W_SWA, SWA_BLOCK),
                  prev(SWA_BLOCK, W_SWA_KV), cur(SWA_BLOCK, W_SWA_KV),
                  prev(W_SWA_KV, SWA_BLOCK), cur(W_SWA_KV, SWA_BLOCK),
                  _const_spec((N_HEADS_SWA, 2 * SWA_BLOCK, SWA_BLOCK))],
        out_specs=cur(W_SWA, SWA_BLOCK),
        out_shape=jax.ShapeDtypeStruct((batch, nb, W_SWA, SWA_BLOCK), _F32),
        scratch_shapes=[pltpu.VMEM((SWA_Q_PER_STEP, N_KV_SWA, 2 * SWA_BLOCK, SWA_GROUP * SWA_BLOCK), _F32)],
        compiler_params=_params(2),
        name="swa",
    )(sinks, qs_t, ks, ks, vs_t, vs_t, bias_swa)


def _group_norm_rows(o_ref, g_ref):
    rows = []
    for t in range(o_ref.shape[0]):
        o_t = o_ref[t]
        scale = lax.rsqrt(jnp.mean(o_t * o_t, axis=0, keepdims=True) + RMS_EPS)
        rows.append(((o_t * scale).T * g_ref[...]).astype(_BF16))
    return jnp.concatenate(rows, axis=0)


def _mix_ffn_body(x_ref, oa_ref, ob_ref, ga_ref, gb_ref, w_out_ref, mix_post_g_ref,
                  pre_g_ref, wg_hbm, wu_hbm, wd_hbm, post_g_ref, o_ref, h_ref, *weight_scratch):
    wg_ref, wu_ref, wd_ref = _ffn_weights((wg_hbm, wu_hbm, wd_hbm), weight_scratch)
    y = (_dot(_group_norm_rows(oa_ref, ga_ref), w_out_ref[:W_MOBA, :])
         + _dot(_group_norm_rows(ob_ref, gb_ref), w_out_ref[W_MOBA:, :]))
    x = x_ref[...] + _rms_rows(y, mix_post_g_ref[...])
    _ffn_tile(lambda rows: x[rows, :], pre_g_ref, wg_ref, wu_ref, wd_ref, post_g_ref, o_ref, h_ref)


def _mix_ffn(x2d, oa_t, ob_t, ga, gb, w_out, mix_post_g, ffn_params):
    n_tok = x2d.shape[0]
    return pl.pallas_call(
        _mix_ffn_body,
        grid=(n_tok // TOKEN_TILE,),
        in_specs=[_TOKEN_TILE_SPEC,
                  pl.BlockSpec((_MOBA_PER_TILE, W_MOBA, MOBA_BLOCK), lambda i: (i, 0, 0)),
                  pl.BlockSpec((_SWA_PER_TILE, W_SWA, SWA_BLOCK), lambda i: (i, 0, 0)),
                  _const_spec((1, W_MOBA)), _const_spec((1, W_SWA)),
                  _const_spec((W_MOBA + W_SWA, D_MODEL)),
                  _const_spec((1, D_MODEL))] + _FFN_WEIGHT_SPECS,
        out_specs=_TOKEN_TILE_SPEC,
        out_shape=jax.ShapeDtypeStruct((n_tok, D_MODEL), _F32),
        scratch_shapes=_FFN_SCRATCH,
        compiler_params=_params(1),
        name="mix_ffn",
    )(x2d, oa_t, ob_t, ga, gb, w_out, mix_post_g, *ffn_params)


def _row(v):
    return v.reshape(1, -1)


def _layer(x, ffn1, mix, ffn2, rel_bias, bias_tiles):
    batch, seq, _ = x.shape
    (mix_pre_g, w_in, moba_out_g, swa_sinks, swa_out_g, w_out, mix_post_g) = mix
    bias_own, bias_prev, bias_swa = bias_tiles

    def ffn_params(params):
        pre_g, w_gate, w_up, w_down, post_g = params
        return (_row(pre_g), w_gate, w_up, w_down, _row(post_g))

    x = _ffn(x.reshape(batch * seq, D_MODEL), ffn_params(ffn1))

    qa, ka, va, qs, ks, vs = jnp.split(
        w_in, [W_MOBA, 2 * W_MOBA, 3 * W_MOBA, 3 * W_MOBA + W_SWA, 3 * W_MOBA + W_SWA + W_SWA_KV], axis=1)
    w_rows = jnp.concatenate([ka, ks], axis=1).astype(_BF16)
    w_t = jnp.concatenate([qa, va, qs, vs], axis=1).astype(_BF16).T
    qa_t, va_t, qs_t, vs_t, ka_b, ks_b, kmean = _in_proj(
        x.reshape(batch, seq, D_MODEL), _row(mix_pre_g), w_rows, w_t)

    oa_t = _moba(rel_bias, qa_t, ka_b, va_t, kmean.reshape(batch, seq // MOBA_BLOCK, W_MOBA),
                 bias_own, bias_prev)
    ob_t = _swa(swa_sinks, qs_t, ks_b, vs_t, bias_swa)

    x = _mix_ffn(x, oa_t.reshape(-1, W_MOBA, MOBA_BLOCK), ob_t.reshape(-1, W_SWA, SWA_BLOCK),
                 _row(moba_out_g), _row(swa_out_g), w_out.astype(_BF16), _row(mix_post_g),
                 ffn_params(ffn2))
    return x.reshape(batch, seq, D_MODEL)


def kernel(x, ffn1_pre_g, ffn1_w_gate, ffn1_w_up, ffn1_w_down, ffn1_post_g, mix_pre_g, w_in, rel_bias,
           moba_out_g, swa_sinks, swa_out_g, w_out, mix_post_g, ffn2_pre_g, ffn2_w_gate, ffn2_w_up,
           ffn2_w_down, ffn2_post_g):
    bias_tiles = _bias_tiles(rel_bias)
    for l in range(ffn1_pre_g.shape[0]):
        x = _layer(
            x,
            (ffn1_pre_g[l], ffn1_w_gate[l], ffn1_w_up[l], ffn1_w_down[l], ffn1_post_g[l]),
            (mix_pre_g[l], w_in[l], moba_out_g[l], swa_sinks[l], swa_out_g[l], w_out[l], mix_post_g[l]),
            (ffn2_pre_g[l], ffn2_w_gate[l], ffn2_w_up[l], ffn2_w_down[l], ffn2_post_g[l]),
            rel_bias, bias_tiles)
    return x
```

```python
import math

import jax
import jax.numpy as jnp
from jax import lax
from jax.experimental import pallas as pl
from jax.experimental.pallas import tpu as pltpu

D_MODEL = 1024
HEAD_DIM = 64
N_HEADS_MOBA = 8
N_HEADS_SWA = 8
N_KV_SWA = 2
SWA_GROUP = N_HEADS_SWA // N_KV_SWA
W_MOBA = N_HEADS_MOBA * HEAD_DIM
W_SWA = N_HEADS_SWA * HEAD_DIM
W_SWA_KV = N_KV_SWA * HEAD_DIM
MOBA_BLOCK = 256
MOBA_TOPK = 3
SWA_BLOCK = 128
SWA_WINDOW = 128
NUM_BUCKETS = 32
MAX_DISTANCE = 128
D_FF = 2816
RMS_EPS = 1e-6
FFN_RES_WEIGHT = 0.5
NEG = -1e30
QK_SCALE = HEAD_DIM ** -0.5
LOG2E = math.log2(math.e)
Q_FOLD = QK_SCALE * LOG2E
SUM_ROWS = 16

TOKEN_TILE = 1024
FF_CHUNK = 256
FFN_ROW_PARTS = 2
CAST_CHUNKS = 16
CAST_SLOTS = 4
HEADS_PER_STEP = 4
HEADS_PER_KEY_TILE = 2
PIPE_DEPTH = 2
GROUPS_PER_ITER = 4
RING_UNROLL = 2
SWA_Q_PER_STEP = 4
VMEM_LIMIT_BYTES = 58 * 1024 * 1024

_BF16 = jnp.bfloat16
_F32 = jnp.float32


def _dot(a, b):
    return jnp.dot(a, b, preferred_element_type=_F32)


def _dot_nt(a, b):
    return lax.dot_general(a, b, (((1,), (1,)), ((), ())), preferred_element_type=_F32)


def _rms_rows(x, g):
    return x * lax.rsqrt(jnp.mean(x * x, axis=-1, keepdims=True) + RMS_EPS) * g


def _const_spec(shape):
    return pl.BlockSpec(shape, lambda *_: (0,) * len(shape), pipeline_mode=pl.Buffered(1))


def _params(n_axes):
    return pltpu.CompilerParams(
        dimension_semantics=("arbitrary",) * n_axes, vmem_limit_bytes=VMEM_LIMIT_BYTES)


def _ffn_tile(x_rows, pre_g_ref, wg_ref, wu_ref, wd_ref, post_g_ref, o_ref, h_ref):
    part = TOKEN_TILE // FFN_ROW_PARTS
    for p in range(FFN_ROW_PARTS):
        rows = slice(p * part, (p + 1) * part)
        x = x_rows(rows)
        xn = _rms_rows(x, pre_g_ref[...]).astype(_BF16)
        for c in range(D_FF // FF_CHUNK):
            cols = slice(c * FF_CHUNK, (c + 1) * FF_CHUNK)
            gate = _dot(xn, wg_ref[:, cols])
            up = _dot(xn, wu_ref[:, cols])
            h_ref[rows, cols] = (jax.nn.silu(gate) * up).astype(_BF16)
        y = _dot(h_ref[rows, :], wd_ref[...])
        o_ref[rows, :] = x + FFN_RES_WEIGHT * _rms_rows(y, post_g_ref[...])


def _cast_weights(pairs, stage, sem):
    slots, rows = stage.shape[0], stage.shape[1]
    chunks = [(w_hbm, w_bf, k) for w_hbm, w_bf in pairs for k in range(w_hbm.shape[0] // rows)]

    def copy(c):
        w_hbm, _, k = chunks[c]
        return pltpu.make_async_copy(w_hbm.at[pl.ds(k * rows, rows), :], stage.at[c % slots], sem.at[c % slots])

    def prime():
        for c in range(min(slots, len(chunks))):
            copy(c).start()

    def drain():
        for c, (_, w_bf, k) in enumerate(chunks):
            copy(c).wait()
            w_bf[k * rows:(k + 1) * rows, :] = stage[c % slots].astype(_BF16)
            if c + slots < len(chunks):
                copy(c + slots).start()

    return prime, drain


def _ffn_weights(w_hbm, scratch):
    wg_bf, wu_bf, wd_bf, stage_in, stage_down, sem_in, sem_down = scratch

    @pl.when(pl.program_id(0) == 0)
    def _():
        prime_in, drain_in = _cast_weights([(w_hbm[0], wg_bf), (w_hbm[1], wu_bf)], stage_in, sem_in)
        prime_down, drain_down = _cast_weights([(w_hbm[2], wd_bf)], stage_down, sem_down)
        prime_in()
        prime_down()
        drain_in()
        drain_down()

    return wg_bf, wu_bf, wd_bf


def _ffn_body(x_ref, pre_g_ref, wg_hbm, wu_hbm, wd_hbm, post_g_ref, o_ref, h_ref, *weight_scratch):
    wg_ref, wu_ref, wd_ref = _ffn_weights((wg_hbm, wu_hbm, wd_hbm), weight_scratch)
    _ffn_tile(lambda rows: x_ref[rows, :], pre_g_ref, wg_ref, wu_ref, wd_ref, post_g_ref, o_ref, h_ref)


_TOKEN_TILE_SPEC = pl.BlockSpec((TOKEN_TILE, D_MODEL), lambda i: (i, 0))
_HBM_SPEC = pl.BlockSpec(memory_space=pl.ANY)
_FFN_WEIGHT_SPECS = [_const_spec((1, D_MODEL)), _HBM_SPEC, _HBM_SPEC, _HBM_SPEC, _const_spec((1, D_MODEL))]
_FFN_SCRATCH = [
    pltpu.VMEM((TOKEN_TILE, D_FF), _BF16),
    pltpu.VMEM((D_MODEL, D_FF), _BF16),
    pltpu.VMEM((D_MODEL, D_FF), _BF16),
    pltpu.VMEM((D_FF, D_MODEL), _BF16),
    pltpu.VMEM((CAST_SLOTS, D_MODEL // CAST_CHUNKS, D_FF), _F32),
    pltpu.VMEM((CAST_SLOTS, D_FF // CAST_CHUNKS, D_MODEL), _F32),
    pltpu.SemaphoreType.DMA((CAST_SLOTS,)),
    pltpu.SemaphoreType.DMA((CAST_SLOTS,)),
]


def _ffn(x2d, ffn_params):
    n_tok = x2d.shape[0]
    return pl.pallas_call(
        _ffn_body,
        grid=(n_tok // TOKEN_TILE,),
        in_specs=[_TOKEN_TILE_SPEC] + _FFN_WEIGHT_SPECS,
        out_specs=_TOKEN_TILE_SPEC,
        out_shape=jax.ShapeDtypeStruct((n_tok, D_MODEL), _F32),
        scratch_shapes=_FFN_SCRATCH,
        compiler_params=_params(1),
        name="ffn",
    )(x2d, *ffn_params)


_T_QA, _T_VA, _T_QS, _T_VS = 0, W_MOBA, 2 * W_MOBA, 2 * W_MOBA + W_SWA
_T_ROWS = 2 * W_MOBA + W_SWA + W_SWA_KV
_MOBA_PER_TILE = TOKEN_TILE // MOBA_BLOCK
_SWA_PER_TILE = TOKEN_TILE // SWA_BLOCK


def _in_proj_body(x_ref, g_ref, w_rows_ref, w_t_ref,
                  qa_ref, va_ref, qs_ref, vs_ref, ka_ref, ks_ref, kmean_ref):
    swa_per_moba = MOBA_BLOCK // SWA_BLOCK
    for t in range(_MOBA_PER_TILE):
        xn = _rms_rows(x_ref[0, t * MOBA_BLOCK:(t + 1) * MOBA_BLOCK, :], g_ref[...]).astype(_BF16)
        k_rows = _dot(xn, w_rows_ref[...])
        proj_t = _dot_nt(w_t_ref[...], xn)
        k_blk = k_rows[:, :W_MOBA]
        ka_ref[0, t] = k_blk.astype(_BF16)
        kmean_ref[0, t] = jnp.mean(k_blk, axis=0, keepdims=True)
        qa_ref[0, t] = (proj_t[_T_QA:_T_QA + W_MOBA, :] * Q_FOLD).astype(_BF16)
        va_ref[0, t] = proj_t[_T_VA:_T_VA + W_MOBA, :].astype(_BF16)
        for u in range(swa_per_moba):
            rows = slice(u * SWA_BLOCK, (u + 1) * SWA_BLOCK)
            ks_ref[0, t * swa_per_moba + u] = k_rows[rows, W_MOBA:].astype(_BF16)
            qs_ref[0, t * swa_per_moba + u] = (proj_t[_T_QS:_T_QS + W_SWA, rows] * Q_FOLD).astype(_BF16)
            vs_ref[0, t * swa_per_moba + u] = proj_t[_T_VS:_T_VS + W_SWA_KV, rows].astype(_BF16)


def _in_proj(x, g, w_rows, w_t):
    batch, seq, _ = x.shape
    nb_moba, nb_swa = seq // MOBA_BLOCK, seq // SWA_BLOCK

    def blocked(n_per_tile, rows, cols):
        return pl.BlockSpec((1, n_per_tile, rows, cols), lambda b, i: (b, i, 0, 0))

    return pl.pallas_call(
        _in_proj_body,
        grid=(batch, seq // TOKEN_TILE),
        in_specs=[pl.BlockSpec((1, TOKEN_TILE, D_MODEL), lambda b, i: (b, i, 0)),
                  _const_spec((1, D_MODEL)),
                  _const_spec((D_MODEL, W_MOBA + W_SWA_KV)),
                  _const_spec((_T_ROWS, D_MODEL))],
        out_specs=[blocked(_MOBA_PER_TILE, W_MOBA, MOBA_BLOCK),
                   blocked(_MOBA_PER_TILE, W_MOBA, MOBA_BLOCK),
                   blocked(_SWA_PER_TILE, W_SWA, SWA_BLOCK),
                   blocked(_SWA_PER_TILE, W_SWA_KV, SWA_BLOCK),
                   blocked(_MOBA_PER_TILE, MOBA_BLOCK, W_MOBA),
                   blocked(_SWA_PER_TILE, SWA_BLOCK, W_SWA_KV),
                   blocked(_MOBA_PER_TILE, 1, W_MOBA)],
        out_shape=[jax.ShapeDtypeStruct((batch, nb_moba, W_MOBA, MOBA_BLOCK), _BF16),
                   jax.ShapeDtypeStruct((batch, nb_moba, W_MOBA, MOBA_BLOCK), _BF16),
                   jax.ShapeDtypeStruct((batch, nb_swa, W_SWA, SWA_BLOCK), _BF16),
                   jax.ShapeDtypeStruct((batch, nb_swa, W_SWA_KV, SWA_BLOCK), _BF16),
                   jax.ShapeDtypeStruct((batch, nb_moba, MOBA_BLOCK, W_MOBA), _BF16),
                   jax.ShapeDtypeStruct((batch, nb_swa, SWA_BLOCK, W_SWA_KV), _BF16),
                   jax.ShapeDtypeStruct((batch, nb_moba, 1, W_MOBA), _F32)],
        compiler_params=_params(2),
        name="in_proj",
    )(x, g, w_rows, w_t)


def _t5_bucket(dist):
    n = jnp.maximum(dist, 0)
    max_exact = NUM_BUCKETS // 2
    nf = jnp.maximum(n, 1).astype(_F32)
    large = max_exact + jnp.floor(jnp.log(nf / max_exact) / math.log(MAX_DISTANCE / max_exact)
                                  * (NUM_BUCKETS - max_exact))
    large = jnp.minimum(large, float(NUM_BUCKETS - 1))
    return jnp.where(n < max_exact, n.astype(_F32), large)


def _bias_lookup(rel_bias_ref, bucket, head):
    val = jnp.full(bucket.shape, rel_bias_ref[NUM_BUCKETS - 1, head], _F32)
    for b in range(NUM_BUCKETS - 2, -1, -1):
        val = jnp.where(bucket == float(b), rel_bias_ref[b, head], val)
    return val


def _bias_tiles_body(rel_bias_ref, own_ref, prev_ref, swa_ref):
    assert MOBA_BLOCK == 2 * SWA_BLOCK and SWA_WINDOW == SWA_BLOCK and MAX_DISTANCE <= SWA_BLOCK
    h = pl.program_id(0)
    q = SWA_BLOCK
    key = lax.broadcasted_iota(jnp.int32, (q, q), 0)
    qry = lax.broadcasted_iota(jnp.int32, (q, q), 1)
    dist = qry - key

    def quadrants(head):
        near = jnp.where(dist >= 0, _bias_lookup(rel_bias_ref, _t5_bucket(dist), head) * LOG2E, NEG)
        nxt = _bias_lookup(rel_bias_ref, _t5_bucket(dist + q), head) * LOG2E
        return near, nxt

    near, nxt = quadrants(h)
    far = jnp.full((q, q), rel_bias_ref[NUM_BUCKETS - 1, h] * LOG2E, _F32)
    masked = jnp.full((q, q), NEG, _F32)
    own_ref[0, :q, :q] = near
    own_ref[0, :q, q:] = nxt
    own_ref[0, q:, :q] = masked
    own_ref[0, q:, q:] = near
    prev_ref[0, :q, :q] = far
    prev_ref[0, :q, q:] = far
    prev_ref[0, q:, :q] = nxt
    prev_ref[0, q:, q:] = far
    near, nxt = quadrants(h + N_HEADS_MOBA)
    swa_ref[0, :q, :] = jnp.where(dist < 0, nxt, NEG)
    swa_ref[0, q:, :] = near


def _bias_tiles(rel_bias):
    def per_head(rows, cols):
        return pl.BlockSpec((1, rows, cols), lambda h: (h, 0, 0))

    return pl.pallas_call(
        _bias_tiles_body,
        grid=(N_HEADS_MOBA,),
        in_specs=[pl.BlockSpec(memory_space=pltpu.SMEM)],
        out_specs=[per_head(MOBA_BLOCK, MOBA_BLOCK), per_head(MOBA_BLOCK, MOBA_BLOCK),
                   per_head(2 * SWA_BLOCK, SWA_BLOCK)],
        out_shape=[jax.ShapeDtypeStruct((N_HEADS_MOBA, MOBA_BLOCK, MOBA_BLOCK), _F32),
                   jax.ShapeDtypeStruct((N_HEADS_MOBA, MOBA_BLOCK, MOBA_BLOCK), _F32),
                   jax.ShapeDtypeStruct((N_HEADS_SWA, 2 * SWA_BLOCK, SWA_BLOCK), _F32)],
        compiler_params=_params(1),
        name="bias_tiles",
    )(rel_bias)


def _moba_body(rel_bias_ref, q_ref, k_ref, v_ref, kmean_ref, own_ref, prev_ref, o_ref,
               far_ref, qpad_ref, s_ref, acc_ref):
    g = pl.program_id(1)
    nb = k_ref.shape[1]
    key_lanes = HEADS_PER_KEY_TILE * HEAD_DIM
    q_row = lax.broadcasted_iota(jnp.int32, (key_lanes, MOBA_BLOCK), 0)
    km = kmean_ref[0]
    km_hi = km.astype(_BF16)
    km_lo = (km - km_hi.astype(_F32)).astype(_BF16)
    blk = lax.broadcasted_iota(jnp.int32, (nb, MOBA_BLOCK), 0)
    ones = jnp.ones((SUM_ROWS, MOBA_BLOCK), _BF16)
    ring_slots = range(GROUPS_PER_ITER)
    tail_slots = ring_slots[-PIPE_DEPTH:]
    near_slots = range(GROUPS_PER_ITER, GROUPS_PER_ITER + PIPE_DEPTH)
    assert PIPE_DEPTH == 2 and GROUPS_PER_ITER >= 2 * PIPE_DEPTH

    def key_tile(r):
        t, sub = divmod(r, HEADS_PER_KEY_TILE)
        return slice(t * key_lanes, (t + 1) * key_lanes), sub

    def prepare(i):
        past = blk < i
        own, prev, prev_sel = [], [], []
        for r in range(HEADS_PER_STEP):
            tile_lanes, sub = key_tile(r)
            q_t = q_ref[0, i, tile_lanes, :]
            q_pad = jnp.where((q_row >= sub * HEAD_DIM) & (q_row < (sub + 1) * HEAD_DIM), q_t, 0)
            qpad_ref[r] = q_pad

            gate = jnp.where(
                past, _dot(km_hi[:, tile_lanes], q_pad) + _dot(km_lo[:, tile_lanes], q_pad), NEG)
            sel = jnp.zeros(gate.shape, jnp.bool_)
            for _ in range(MOBA_TOPK):
                top = jnp.max(gate, axis=0, keepdims=True)
                first = jnp.min(jnp.where(gate == top, blk, nb), axis=0, keepdims=True)
                pick = blk == first
                sel = sel | pick
                gate = jnp.where(pick, -jnp.inf, gate)
            sel = sel & past

            far_bias = rel_bias_ref[NUM_BUCKETS - 1, g * HEADS_PER_STEP + r] * LOG2E
            far_ref[r] = jnp.where(sel & (blk < i - 1), far_bias, NEG)
            prev_sel.append(jnp.max(jnp.where(sel & (blk == i - 1), 0.0, NEG), axis=0, keepdims=True))
            own.append(own_ref[r])
            prev.append(prev_ref[r])
        return own, prev, prev_sel

    all_heads = range(HEADS_PER_STEP)
    head_parts = tuple(all_heads[r:r + 1] for r in all_heads)

    def score(slot, j, tile_bias=None, row_bias=None, heads=all_heads):
        tops, shifts = {}, {}
        for r in heads:
            s = _dot(k_ref[0, j, :, key_tile(r)[0]], qpad_ref[r])
            if tile_bias is not None:
                s = s + tile_bias[r]
            s_ref[slot, r] = s
            top = jnp.max(s, axis=0, keepdims=True)
            shift = jnp.zeros_like(top) if row_bias is None else row_bias[r]
            tops[r] = top + shift
            shifts[r] = shift
        return slot, j, tops, shifts

    def absorb(carry, scored, heads=all_heads):
        slot, j, tops, shifts = scored
        out = list(carry)
        for r in heads:
            lanes = slice(r * HEAD_DIM, (r + 1) * HEAD_DIM)
            m = carry[r]
            m_new = jnp.maximum(m, tops[r])
            p = jnp.exp2(s_ref[slot, r] + (shifts[r] - m_new)).astype(_BF16)
            pv = _dot(jnp.concatenate([v_ref[0, j, lanes, :], ones], axis=0), p)
            acc_ref[r] = jnp.exp2(m - m_new) * acc_ref[r] + pv
            out[r] = m_new
        return tuple(out)

    def score_far(slot, grp, heads=all_heads):
        j = jnp.minimum(grp, nb - 1)
        return score(slot, j, row_bias={r: far_ref[r, pl.ds(j, 1), :] for r in heads}, heads=heads)

    def score_near(i):
        own, prev, prev_sel = prepare(i)
        return (score(near_slots[0], i, tile_bias=own)[2:],
                score(near_slots[1], jnp.maximum(i - 1, 0), tile_bias=prev, row_bias=prev_sel)[2:])

    def start(i, near):
        carry = tuple(jnp.full((1, MOBA_BLOCK), -jnp.inf, _F32) for _ in range(HEADS_PER_STEP))
        acc_ref[...] = jnp.zeros(acc_ref.shape, _F32)
        near_js = (i, jnp.maximum(i - 1, 0))
        ring = {slot: (slot, j) + maxes_shifts for slot, j, maxes_shifts in zip(near_slots, near_js, near)}
        for c, slot in enumerate(tail_slots):
            ring[slot] = score_far(slot, c)
            carry = absorb(carry, ring[near_slots[c]])
        return carry, tuple(ring[slot][1:] for slot in tail_slots)

    def tile(i, phase, state, last=False):
        def ring_steps(first_group, count, carry, pending):
            ring = {slot: (slot,) + p for slot, p in zip(tail_slots, pending)}
            for k in range(count):
                tops, shifts = {}, {}
                for heads in head_parts:
                    slot, j, part_tops, part_shifts = score_far(k % GROUPS_PER_ITER, first_group + k, heads)
                    carry = absorb(carry, ring[(k - PIPE_DEPTH) % GROUPS_PER_ITER], heads)
                    tops.update(part_tops)
                    shifts.update(part_shifts)
                ring[k % GROUPS_PER_ITER] = (slot, j, tops, shifts)
            return carry, ring

        def ring_loop(first_group, turns, trips, state):
            def step(it, state):
                groups = turns * GROUPS_PER_ITER
                carry, ring = ring_steps(first_group + it * groups, groups, *state)
                return carry, tuple(ring[slot][1:] for slot in tail_slots)
            return lax.fori_loop(0, trips, step, state)

        leftover = (phase - 1 - PIPE_DEPTH) % GROUPS_PER_ITER
        n_turns = jnp.maximum(i - 1 - PIPE_DEPTH, 0) // GROUPS_PER_ITER
        n_long = n_turns // RING_UNROLL
        done = n_long * RING_UNROLL
        state = ring_loop(PIPE_DEPTH, RING_UNROLL, n_long, state)
        state = ring_loop(PIPE_DEPTH + done * GROUPS_PER_ITER, 1, n_turns - done, state)
        carry, ring = ring_steps(PIPE_DEPTH + n_turns * GROUPS_PER_ITER, leftover, *state)

        near_next = None if last else score_near(i + 1)
        for c in range(PIPE_DEPTH):
            carry = absorb(carry, ring[(leftover - PIPE_DEPTH + c) % GROUPS_PER_ITER])
        for r in range(HEADS_PER_STEP):
            acc = acc_ref[r]
            o_ref[0, i, r * HEAD_DIM:(r + 1) * HEAD_DIM, :] = acc[:HEAD_DIM] / acc[HEAD_DIM:HEAD_DIM + 1]
        return None if last else start(i + 1, near_next)

    def tiles(a, state, last=False):
        for phase in range(GROUPS_PER_ITER):
            state = tile(a * GROUPS_PER_ITER + phase, phase, state,
                         last=last and phase == GROUPS_PER_ITER - 1)
        return state

    assert nb % GROUPS_PER_ITER == 0
    state = lax.fori_loop(0, nb // GROUPS_PER_ITER - 1, tiles, start(0, score_near(0)))
    tiles(nb // GROUPS_PER_ITER - 1, state, last=True)


def _moba(rel_bias, qa_t, ka, va_t, kmean, bias_own, bias_prev):
    batch, nb = qa_t.shape[0], qa_t.shape[1]
    width = HEADS_PER_STEP * HEAD_DIM
    q_spec = pl.BlockSpec((1, nb, width, MOBA_BLOCK), lambda b, g: (b, 0, g, 0))
    bias_spec = pl.BlockSpec((HEADS_PER_STEP, MOBA_BLOCK, MOBA_BLOCK), lambda b, g: (g, 0, 0))
    return pl.pallas_call(
        _moba_body,
        grid=(batch, N_HEADS_MOBA // HEADS_PER_STEP),
        in_specs=[pl.BlockSpec(memory_space=pltpu.SMEM),
                  q_spec,
                  pl.BlockSpec((1, nb, MOBA_BLOCK, width), lambda b, g: (b, 0, 0, g)),
                  q_spec,
                  pl.BlockSpec((1, nb, width), lambda b, g: (b, 0, g)),
                  bias_spec, bias_spec],
        out_specs=q_spec,
        out_shape=jax.ShapeDtypeStruct((batch, nb, W_MOBA, MOBA_BLOCK), _F32),
        scratch_shapes=[pltpu.VMEM((HEADS_PER_STEP, nb, MOBA_BLOCK), _F32),
                        pltpu.VMEM((HEADS_PER_STEP, HEADS_PER_KEY_TILE * HEAD_DIM, MOBA_BLOCK), _BF16),
                        pltpu.VMEM((GROUPS_PER_ITER + PIPE_DEPTH, HEADS_PER_STEP, MOBA_BLOCK, MOBA_BLOCK), _F32),
                        pltpu.VMEM((HEADS_PER_STEP, HEAD_DIM + SUM_ROWS, MOBA_BLOCK), _F32)],
        compiler_params=_params(2),
        name="moba",
    )(rel_bias, qa_t, ka, va_t, kmean, bias_own, bias_prev)


def _swa_body(sinks_ref, q_ref, k_prev_ref, k_cur_ref, v_prev_ref, v_cur_ref, bias_ref, o_ref, s_ref):
    no_prev = jnp.where(pl.program_id(1) == 0, NEG, 0.0)
    window_row = lax.broadcasted_iota(jnp.int32, (2 * SWA_BLOCK, SWA_BLOCK), 0)
    first_window_mask = jnp.where(window_row < SWA_BLOCK, no_prev, 0.0)
    zeros = jnp.zeros((HEAD_DIM, SWA_BLOCK), _BF16)
    ones = jnp.ones((SUM_ROWS, 2 * SWA_BLOCK), _BF16)
    k_blocks = [k_prev_ref[0, 0]] + [k_cur_ref[0, t] for t in range(SWA_Q_PER_STEP)]
    v_blocks = [v_prev_ref[0, 0]] + [v_cur_ref[0, t] for t in range(SWA_Q_PER_STEP)]
    chains = [(t, kv) for t in range(SWA_Q_PER_STEP) for kv in range(N_KV_SWA)]

    for t, kv in chains:
        q_pad = jnp.concatenate(
            [jnp.concatenate(
                [q_ref[0, t, h * HEAD_DIM:(h + 1) * HEAD_DIM, :] if part == kv else zeros
                 for part in range(N_KV_SWA)], axis=0)
             for h in range(kv * SWA_GROUP, (kv + 1) * SWA_GROUP)], axis=1)
        window_keys = jnp.concatenate([k_blocks[t], k_blocks[t + 1]], axis=0)
        s = _dot(window_keys, q_pad)
        for slot in range(SWA_GROUP):
            cols = slice(slot * SWA_BLOCK, (slot + 1) * SWA_BLOCK)
            biased = s[:, cols] + bias_ref[kv * SWA_GROUP + slot]
            s_ref[t, kv, :, cols] = biased + first_window_mask if t == 0 else biased

    for t, kv in chains:
        kv_rows = slice(kv * HEAD_DIM, (kv + 1) * HEAD_DIM)
        window_values = jnp.concatenate(
            [jnp.concatenate([v_blocks[t][kv_rows, :], v_blocks[t + 1][kv_rows, :]], axis=1), ones], axis=0)
        p, sink_p = [], []
        for slot in range(SWA_GROUP):
            cols = slice(slot * SWA_BLOCK, (slot + 1) * SWA_BLOCK)
            sink = sinks_ref[kv * SWA_GROUP + slot] * LOG2E
            m = jnp.maximum(jnp.max(s_ref[t, kv, :, cols], axis=0, keepdims=True), sink)
            p.append(jnp.exp2(s_ref[t, kv, :, cols] - m).astype(_BF16))
            sink_p.append(jnp.exp2(sink - m))
        acc = _dot(window_values, jnp.concatenate(p, axis=1))
        for slot in range(SWA_GROUP):
            h = kv * SWA_GROUP + slot
            cols = slice(slot * SWA_BLOCK, (slot + 1) * SWA_BLOCK)
            l = acc[HEAD_DIM:HEAD_DIM + 1, cols] + sink_p[slot]
            o_ref[0, t, h * HEAD_DIM:(h + 1) * HEAD_DIM, :] = acc[:HEAD_DIM, cols] / l


def _swa(sinks, qs_t, ks, vs_t, bias_swa):
    batch, nb = qs_t.shape[0], qs_t.shape[1]

    def cur(rows, cols):
        return pl.BlockSpec((1, SWA_Q_PER_STEP, rows, cols), lambda b, n: (b, n, 0, 0))

    def prev(rows, cols):
        return pl.BlockSpec((1, 1, rows, cols),
                            lambda b, n: (b, jnp.maximum(n * SWA_Q_PER_STEP - 1, 0), 0, 0))

    return pl.pallas_call(
        _swa_body,
        grid=(batch, nb // SWA_Q_PER_STEP),
        in_specs=[pl.BlockSpec(memory_space=pltpu.SMEM),
                  cur(W_SWA, SWA_BLOCK),
                  prev(SWA_BLOCK, W_SWA_KV), cur(SWA_BLOCK, W_SWA_KV),
                  prev(W_SWA_KV, SWA_BLOCK), cur(W_SWA_KV, SWA_BLOCK),
                  _const_spec((N_HEADS_SWA, 2 * SWA_BLOCK, SWA_BLOCK))],
        out_specs=cur(W_SWA, SWA_BLOCK),
        out_shape=jax.ShapeDtypeStruct((batch, nb, W_SWA, SWA_BLOCK), _F32),
        scratch_shapes=[pltpu.VMEM((SWA_Q_PER_STEP, N_KV_SWA, 2 * SWA_BLOCK, SWA_GROUP * SWA_BLOCK), _F32)],
        compiler_params=_params(2),
        name="swa",
    )(sinks, qs_t, ks, ks, vs_t, vs_t, bias_swa)


def _group_norm_rows(o_ref, g_ref):
    rows = []
    for t in range(o_ref.shape[0]):
        o_t = o_ref[t]
        scale = lax.rsqrt(jnp.mean(o_t * o_t, axis=0, keepdims=True) + RMS_EPS)
        rows.append(((o_t * scale).T * g_ref[...]).astype(_BF16))
    return jnp.concatenate(rows, axis=0)


def _mix_ffn_body(x_ref, oa_ref, ob_ref, ga_ref, gb_ref, w_out_ref, mix_post_g_ref,
                  pre_g_ref, wg_hbm, wu_hbm, wd_hbm, post_g_ref, o_ref, h_ref, *weight_scratch):
    wg_ref, wu_ref, wd_ref = _ffn_weights((wg_hbm, wu_hbm, wd_hbm), weight_scratch)
    y = (_dot(_group_norm_rows(oa_ref, ga_ref), w_out_ref[:W_MOBA, :])
         + _dot(_group_norm_rows(ob_ref, gb_ref), w_out_ref[W_MOBA:, :]))
    x = x_ref[...] + _rms_rows(y, mix_post_g_ref[...])
    _ffn_tile(lambda rows: x[rows, :], pre_g_ref, wg_ref, wu_ref, wd_ref, post_g_ref, o_ref, h_ref)


def _mix_ffn(x2d, oa_t, ob_t, ga, gb, w_out, mix_post_g, ffn_params):
    n_tok = x2d.shape[0]
    return pl.pallas_call(
        _mix_ffn_body,
        grid=(n_tok // TOKEN_TILE,),
        in_specs=[_TOKEN_TILE_SPEC,
                  pl.BlockSpec((_MOBA_PER_TILE, W_MOBA, MOBA_BLOCK), lambda i: (i, 0, 0)),
                  pl.BlockSpec((_SWA_PER_TILE, W_SWA, SWA_BLOCK), lambda i: (i, 0, 0)),
                  _const_spec((1, W_MOBA)), _const_spec((1, W_SWA)),
                  _const_spec((W_MOBA + W_SWA, D_MODEL)),
                  _const_spec((1, D_MODEL))] + _FFN_WEIGHT_SPECS,
        out_specs=_TOKEN_TILE_SPEC,
        out_shape=jax.ShapeDtypeStruct((n_tok, D_MODEL), _F32),
        scratch_shapes=_FFN_SCRATCH,
        compiler_params=_params(1),
        name="mix_ffn",
    )(x2d, oa_t, ob_t, ga, gb, w_out, mix_post_g, *ffn_params)


def _row(v):
    return v.reshape(1, -1)


def _layer(x, ffn1, mix, ffn2, rel_bias, bias_tiles):
    batch, seq, _ = x.shape
    (mix_pre_g, w_in, moba_out_g, swa_sinks, swa_out_g, w_out, mix_post_g) = mix
    bias_own, bias_prev, bias_swa = bias_tiles

    def ffn_params(params):
        pre_g, w_gate, w_up, w_down, post_g = params
        return (_row(pre_g), w_gate, w_up, w_down, _row(post_g))

    x = _ffn(x.reshape(batch * seq, D_MODEL), ffn_params(ffn1))

    qa, ka, va, qs, ks, vs = jnp.split(
        w_in, [W_MOBA, 2 * W_MOBA, 3 * W_MOBA, 3 * W_MOBA + W_SWA, 3 * W_MOBA + W_SWA + W_SWA_KV], axis=1)
    w_rows = jnp.concatenate([ka, ks], axis=1).astype(_BF16)
    w_t = jnp.concatenate([qa, va, qs, vs], axis=1).astype(_BF16).T
    qa_t, va_t, qs_t, vs_t, ka_b, ks_b, kmean = _in_proj(
        x.reshape(batch, seq, D_MODEL), _row(mix_pre_g), w_rows, w_t)

    oa_t = _moba(rel_bias, qa_t, ka_b, va_t, kmean.reshape(batch, seq // MOBA_BLOCK, W_MOBA),
                 bias_own, bias_prev)
    ob_t = _swa(swa_sinks, qs_t, ks_b, vs_t, bias_swa)

    x = _mix_ffn(x, oa_t.reshape(-1, W_MOBA, MOBA_BLOCK), ob_t.reshape(-1, W_SWA, SWA_BLOCK),
                 _row(moba_out_g), _row(swa_out_g), w_out.astype(_BF16), _row(mix_post_g),
                 ffn_params(ffn2))
    return x.reshape(batch, seq, D_MODEL)


def kernel(x, ffn1_pre_g, ffn1_w_gate, ffn1_w_up, ffn1_w_down, ffn1_post_g, mix_pre_g, w_in, rel_bias,
           moba_out_g, swa_sinks, swa_out_g, w_out, mix_post_g, ffn2_pre_g, ffn2_w_gate, ffn2_w_up,
           ffn2_w_down, ffn2_post_g):
    bias_tiles = _bias_tiles(rel_bias)
    for l in range(ffn1_pre_g.shape[0]):
        x = _layer(
            x,
            (ffn1_pre_g[l], ffn1_w_gate[l], ffn1_w_up[l], ffn1_w_down[l], ffn1_post_g[l]),
            (mix_pre_g[l], w_in[l], moba_out_g[l], swa_sinks[l], swa_out_g[l], w_out[l], mix_post_g[l]),
            (ffn2_pre_g[l], ffn2_w_gate[l], ffn2_w_up[l], ffn2_w_down[l], ffn2_post_g[l]),
            rel_bias, bias_tiles)
    return x
```
